```python
import math
import jax
import jax.numpy as jnp
from jax import lax
import numpy as np

D_MODEL = 1024
BATCH = 2
SEQ = 8192
DEPTH = 2
DEC_BATCH = 32
DEC_SEQ = 1
PAST_LEN = 8192
PAGE_SIZE = 128

HEAD_DIM = 64
SSD_INNER = D_MODEL
SSD_HEAD_DIM = 64
SSD_HEADS = SSD_INNER // SSD_HEAD_DIM
SSD_GROUPS = 2
SSD_STATE = 128
SSD_CONV = 4
SSD_CONV_DIM = SSD_INNER + 2 * SSD_GROUPS * SSD_STATE
SSD_CHUNK = 128
SB_HEADS = 8
SB_WIDTH = SB_HEADS * HEAD_DIM
NSA_HEADS = 8
NSA_KV_GROUPS = 2
NSA_WIDTH = NSA_HEADS * HEAD_DIM
NSA_KV_WIDTH = NSA_KV_GROUPS * HEAD_DIM
CMP_BLOCK = 32
CMP_STRIDE = 16
CMP_HIDDEN = 128
SEL_BLOCK = 64
SEL_TOP = 16
SEL_LOCAL = 2
WINDOW = 512
Q_BLOCK = 128
N_BRANCH = 3
FFN_HIDDEN = ((8 * D_MODEL + 3 * 256 - 1) // (3 * 256)) * 256
RMS_EPS = 1e-6
IN_SPLITS = (SSD_INNER, SSD_CONV_DIM, SSD_HEADS, 3 * SB_WIDTH, NSA_WIDTH, 6 * NSA_KV_WIDTH, 3 * NSA_HEADS, N_BRANCH * D_MODEL)
IN_COLS = sum(IN_SPLITS)

kernel_name = 'hybrid_ssd_stickbreak_nsa_decode_step'


def rms_norm(x, w):
    xf = x.astype(jnp.float32)
    y = xf * lax.rsqrt(jnp.mean(xf * xf, axis=-1, keepdims=True) + RMS_EPS)
    return (y * w.astype(jnp.float32)).astype(x.dtype)


def masked_softmax(s, mask):
    s = jnp.where(mask, s.astype(jnp.float32), -jnp.inf)
    m = jnp.max(s, axis=-1, keepdims=True)
    m = jnp.where(jnp.isfinite(m), m, 0.0)
    e = jnp.where(mask, jnp.exp(s - m), 0.0)
    return e / jnp.maximum(jnp.sum(e, axis=-1, keepdims=True), 1e-30)


def split_cols(u):
    outs, start = [], 0
    for n in IN_SPLITS:
        outs.append(u[..., start:start + n])
        start += n
    return outs


def causal_dwconv(x, buf, w, b):
    xp = jnp.concatenate([buf.astype(x.dtype), x], axis=1)
    y = lax.conv_general_dilated(xp, w[:, None, :].astype(x.dtype), window_strides=(1,), padding='VALID',
                                 dimension_numbers=('NWC', 'WIO', 'NWC'), feature_group_count=x.shape[-1])
    return y + b.astype(x.dtype), xp[:, xp.shape[1] - (SSD_CONV - 1):]


def ssd_scan(x, dt, a, bm, cm, h0):
    bsz, t = x.shape[:2]
    q = min(SSD_CHUNK, t)
    pad = (-t) % q
    if pad:
        pw = lambda z: jnp.pad(z, [(0, 0), (0, pad)] + [(0, 0)] * (z.ndim - 2))
        x, dt, bm, cm = pw(x), pw(dt), pw(bm), pw(cm)
    nc = (t + pad) // q
    r = SSD_HEADS // SSD_GROUPS
    xdt = (x * dt[..., None]).reshape(bsz, nc, q, SSD_GROUPS, r, SSD_HEAD_DIM)
    acum = jnp.cumsum((dt * a).reshape(bsz, nc, q, SSD_GROUPS, r), axis=2)
    bm = bm.reshape(bsz, nc, q, SSD_GROUPS, SSD_STATE)
    cm = cm.reshape(bsz, nc, q, SSD_GROUPS, SSD_STATE)
    at = jnp.moveaxis(acum, 2, -1)
    tril = jnp.tril(jnp.ones((q, q), dtype=bool))
    decay = jnp.exp(jnp.where(tril, at[..., :, None] - at[..., None, :], -jnp.inf))
    cb = jnp.einsum('bclgn,bcsgn->bcgls', cm, bm)
    y_diag = jnp.einsum('bcgls,bcgrls,bcsgrp->bclgrp', cb, decay, xdt)
    to_end = jnp.exp(acum[:, :, -1:] - acum)
    states = jnp.einsum('bclgn,bclgr,bclgrp->bcgrpn', bm, to_end, xdt)
    chunk_decay = jnp.exp(acum[:, :, -1])

    def step(h, inp):
        dec, st = inp
        return dec[..., None, None] * h + st, h

    h_init = h0.reshape(bsz, SSD_GROUPS, r, SSD_HEAD_DIM, SSD_STATE)
    h_fin, h_in = lax.scan(step, h_init, (jnp.moveaxis(chunk_decay, 1, 0), jnp.moveaxis(states, 1, 0)))
    h_in = jnp.moveaxis(h_in, 0, 1)
    y_off = jnp.einsum('bclgn,bcgrpn,bclgr->bclgrp', cm, h_in, jnp.exp(acum))
    y = (y_diag + y_off).reshape(bsz, nc * q, SSD_HEADS, SSD_HEAD_DIM)[:, :t]
    return y, h_fin.reshape(bsz, SSD_HEADS, SSD_HEAD_DIM, SSD_STATE)


def ssd_branch(z, xbc, dt_raw, conv_buf, h0, lp):
    bsz, t = z.shape[:2]
    xbc, new_buf = causal_dwconv(xbc, conv_buf, lp['conv_w'], lp['conv_b'])
    xbc = jax.nn.silu(xbc)
    gn = SSD_GROUPS * SSD_STATE
    xs = xbc[..., :SSD_INNER].reshape(bsz, t, SSD_HEADS, SSD_HEAD_DIM).astype(jnp.float32)
    bm = xbc[..., SSD_INNER:SSD_INNER + gn].reshape(bsz, t, SSD_GROUPS, SSD_STATE).astype(jnp.float32)
    cm = xbc[..., SSD_INNER + gn:].reshape(bsz, t, SSD_GROUPS, SSD_STATE).astype(jnp.float32)
    dt = jax.nn.softplus((dt_raw + lp['dt_bias']).astype(jnp.float32))
    a = -jnp.exp(lp['a_log'].astype(jnp.float32))
    y, h_fin = ssd_scan(xs, dt, a, bm, cm, h0.astype(jnp.float32))
    y = y + lp['d_skip'].astype(jnp.float32)[:, None] * xs
    y = y.reshape(bsz, t, SSD_INNER).astype(z.dtype)
    y = rms_norm(y * jax.nn.silu(z), lp['ssd_norm'])
    return y @ lp['w_ssd_out'], new_buf, h_fin.astype(h0.dtype)


def stick_breaking(q, k, v, q_pos, k_pos):
    z = jnp.einsum('bqhd,bkhd->bhqk', q, k).astype(jnp.float32) * (HEAD_DIM ** -0.5)
    mask = k_pos[None, :] < q_pos[:, None]
    log_keep = jnp.where(mask, jax.nn.log_sigmoid(-z), 0.0)
    later = lax.cumsum(log_keep, axis=3, reverse=True) - log_keep
    a = jnp.where(mask, jnp.exp(jax.nn.log_sigmoid(z) + later), 0.0)
    return jnp.einsum('bhqk,bkhd->bqhd', a.astype(v.dtype), v)


def sb_prompt(q, k, v):
    bsz, t = q.shape[:2]
    nb = t // Q_BLOCK
    k_pos = jnp.arange(t)
    qb = jnp.moveaxis(q.reshape(bsz, nb, Q_BLOCK, SB_HEADS, HEAD_DIM), 1, 0)

    def blk(args):
        qi, i = args
        return stick_breaking(qi, k, v, i * Q_BLOCK + jnp.arange(Q_BLOCK), k_pos)

    out = lax.map(blk, (qb, jnp.arange(nb)))
    return jnp.moveaxis(out, 0, 1).reshape(bsz, t, SB_WIDTH)


def compress_blocks(rows, pos_emb, w1, w2):
    bsz, t = rows.shape[:2]
    nc = (t - CMP_BLOCK) // CMP_STRIDE + 1
    starts = jnp.arange(nc) * CMP_STRIDE
    idx = starts[:, None] + jnp.arange(CMP_BLOCK)[None, :]
    blocks = rows[:, idx] + pos_emb[None, None, :, None, :].astype(rows.dtype)
    blocks = jnp.moveaxis(blocks, 3, 2).reshape(bsz, nc, NSA_KV_GROUPS, CMP_BLOCK * HEAD_DIM)
    return jax.nn.silu(blocks @ w1) @ w2, starts + CMP_BLOCK - 1


def nsa_attend(q, gates, q_pos, kc, vc, c_end, ks, vs, kw, vw, kw_pos):
    bsz, tq = q.shape[:2]
    r = NSA_HEADS // NSA_KV_GROUPS
    scale = HEAD_DIM ** -0.5
    qg = q.reshape(bsz, tq, NSA_KV_GROUPS, r, HEAD_DIM)
    s_c = jnp.einsum('bqgrd,bcgd->bqgrc', qg, kc).astype(jnp.float32) * scale
    p_c = masked_softmax(s_c, (c_end[None, :] <= q_pos[:, None])[None, :, None, None, :])
    o_c = jnp.einsum('bqgrc,bcgd->bqgrd', p_c.astype(vc.dtype), vc)
    n_blk = ks.shape[1] // SEL_BLOCK
    blk = jnp.arange(n_blk)
    c_start = c_end - (CMP_BLOCK - 1)
    overlap = ((c_start[:, None] < (blk[None, :] + 1) * SEL_BLOCK) & (c_end[:, None] >= blk[None, :] * SEL_BLOCK)).astype(jnp.float32)
    imp = jnp.einsum('bqgrc,cn->bqgn', p_c, overlap)
    cur = q_pos // SEL_BLOCK
    valid = blk[None, :] <= cur[:, None]
    forced = valid & ((blk[None, :] == 0) | (blk[None, :] > cur[:, None] - SEL_LOCAL))
    imp = jnp.where(forced[None, :, None, :], jnp.inf, jnp.where(valid[None, :, None, :], imp, -jnp.inf))
    n_top = min(SEL_TOP, n_blk)
    _, idx = lax.top_k(imp, n_top)
    kb = jnp.moveaxis(ks.reshape(bsz, n_blk, SEL_BLOCK, NSA_KV_GROUPS, HEAD_DIM), 3, 1)
    vb = jnp.moveaxis(vs.reshape(bsz, n_blk, SEL_BLOCK, NSA_KV_GROUPS, HEAD_DIM), 3, 1)
    idx_g = jnp.moveaxis(idx, 2, 1)
    take = jax.vmap(jax.vmap(lambda blocks, ids: blocks[ids]))
    gk = take(kb, idx_g)
    gv = take(vb, idx_g)
    sel_pos = idx_g[..., None] * SEL_BLOCK + jnp.arange(SEL_BLOCK)
    m_s = jnp.moveaxis(sel_pos <= q_pos[None, None, :, None, None], 1, 2)[:, :, :, None]
    s_s = jnp.einsum('bqgrd,bgqnsd->bqgrns', qg, gk).astype(jnp.float32) * scale
    nk = n_top * SEL_BLOCK
    p_s = masked_softmax(s_s.reshape(bsz, tq, NSA_KV_GROUPS, r, nk), m_s.reshape(bsz, tq, NSA_KV_GROUPS, 1, nk))
    o_s = jnp.einsum('bqgrk,bgqkd->bqgrd', p_s.astype(gv.dtype), gv.reshape(bsz, NSA_KV_GROUPS, tq, nk, HEAD_DIM))
    s_w = jnp.einsum('bqgrd,bkgd->bqgrk', qg, kw).astype(jnp.float32) * scale
    m_w = (kw_pos[None, :] <= q_pos[:, None]) & (kw_pos[None, :] > q_pos[:, None] - WINDOW) & (kw_pos[None, :] >= 0)
    p_w = masked_softmax(s_w, m_w[None, :, None, None, :])
    o_w = jnp.einsum('bqgrk,bkgd->bqgrd', p_w.astype(vw.dtype), vw)
    g = jax.nn.sigmoid(gates.reshape(bsz, tq, NSA_KV_GROUPS, r, 3).astype(jnp.float32)).astype(q.dtype)
    o = g[..., 0:1] * o_c + g[..., 1:2] * o_s + g[..., 2:3] * o_w
    return o.reshape(bsz, tq, NSA_WIDTH)


def nsa_prompt(nq, gates, nkv, lp):
    bsz, t = nq.shape[:2]
    kc, c_end = compress_blocks(nkv[:, :, 0], lp['cmp_pos'][0], lp['cmp_w1'][0], lp['cmp_w2'][0])
    vc, _ = compress_blocks(nkv[:, :, 1], lp['cmp_pos'][1], lp['cmp_w1'][1], lp['cmp_w2'][1])
    ks, vs = nkv[:, :, 2], nkv[:, :, 3]
    w_pad = jnp.pad(nkv[:, :, 4:6], ((0, 0), (WINDOW, 0), (0, 0), (0, 0), (0, 0)))
    nb = t // Q_BLOCK
    qb = jnp.moveaxis(nq.reshape(bsz, nb, Q_BLOCK, NSA_HEADS, HEAD_DIM), 1, 0)
    gb = jnp.moveaxis(gates.reshape(bsz, nb, Q_BLOCK, 3 * NSA_HEADS), 1, 0)

    def blk(args):
        qi, gi, i = args
        t0 = i * Q_BLOCK
        wi = lax.dynamic_slice_in_dim(w_pad, t0, WINDOW + Q_BLOCK, axis=1)
        kw_pos = t0 - WINDOW + jnp.arange(WINDOW + Q_BLOCK)
        return nsa_attend(qi, gi, t0 + jnp.arange(Q_BLOCK), kc, vc, c_end, ks, vs, wi[:, :, 0], wi[:, :, 1], kw_pos)

    out = lax.map(blk, (qb, gb, jnp.arange(nb)))
    return jnp.moveaxis(out, 0, 1).reshape(bsz, t, NSA_WIDTH)


def project_in(x, lp):
    return split_cols(rms_norm(x, lp['mix_pre']) @ lp['w_in'])


def merge_and_ffn(x, ssd_o, sb_o, nsa_o, gate_logits, lp):
    bsz, t = x.shape[:2]
    g = jax.nn.sigmoid(gate_logits.astype(jnp.float32)).astype(x.dtype).reshape(bsz, t, N_BRANCH, D_MODEL)
    merged = g[:, :, 0] * ssd_o + g[:, :, 1] * (sb_o @ lp['w_sb_out']) + g[:, :, 2] * (nsa_o @ lp['w_nsa_out'])
    x = x + rms_norm(merged @ lp['w_o'], lp['mix_post'])
    h = rms_norm(x, lp['ffn_pre'])
    f = (jax.nn.silu(h @ lp['w_ffn_gate']) * (h @ lp['w_ffn_up'])) @ lp['w_ffn_down']
    return x + rms_norm(f, lp['ffn_post'])


def layer_prompt(x, lp):
    bsz, t = x.shape[:2]
    z, xbc, dt_raw, sb_qkv, nsa_q, nsa_kv, nsa_g, br_g = project_in(x, lp)
    conv0 = jnp.zeros((bsz, SSD_CONV - 1, SSD_CONV_DIM), x.dtype)
    h0 = jnp.zeros((bsz, SSD_HEADS, SSD_HEAD_DIM, SSD_STATE), jnp.float32)
    ssd_o, conv_new, h_new = ssd_branch(z, xbc, dt_raw, conv0, h0, lp)
    qkv = sb_qkv.reshape(bsz, t, 3, SB_HEADS, HEAD_DIM)
    sb_o = sb_prompt(qkv[:, :, 0], qkv[:, :, 1], qkv[:, :, 2])
    nq = nsa_q.reshape(bsz, t, NSA_HEADS, HEAD_DIM)
    nkv = nsa_kv.reshape(bsz, t, 6, NSA_KV_GROUPS, HEAD_DIM)
    nsa_o = nsa_prompt(nq, nsa_g, nkv, lp)
    y = merge_and_ffn(x, ssd_o, sb_o, nsa_o, br_g, lp)
    keep = min(WINDOW, t)
    return y, qkv[:, :, 1:3], nkv[:, :, 0:4], nkv[:, t - keep:, 4:6], h_new, conv_new


def layer_sample(x, lp, sb_pool, nsa_pool, win_buf, h0, conv_buf, page_table):
    bsz, t = x.shape[:2]
    past = page_table.shape[1] * sb_pool.shape[1]
    z, xbc, dt_raw, sb_qkv, nsa_q, nsa_kv, nsa_g, br_g = project_in(x, lp)
    ssd_o, conv_new, h_new = ssd_branch(z, xbc, dt_raw, conv_buf, h0, lp)
    q_pos = past + jnp.arange(t)
    qkv = sb_qkv.reshape(bsz, t, 3, SB_HEADS, HEAD_DIM)
    sb_past = sb_pool[page_table].reshape(bsz, past, 2, SB_HEADS, HEAD_DIM)
    sb_all = jnp.concatenate([sb_past, qkv[:, :, 1:3].astype(sb_past.dtype)], axis=1)
    sb_o = stick_breaking(qkv[:, :, 0], sb_all[:, :, 0], sb_all[:, :, 1], q_pos, jnp.arange(past + t)).reshape(bsz, t, SB_WIDTH)
    nq = nsa_q.reshape(bsz, t, NSA_HEADS, HEAD_DIM)
    nkv = nsa_kv.reshape(bsz, t, 6, NSA_KV_GROUPS, HEAD_DIM)
    nsa_past = nsa_pool[page_table].reshape(bsz, past, 4, NSA_KV_GROUPS, HEAD_DIM)
    rows = jnp.concatenate([nsa_past, nkv[:, :, 0:4].astype(nsa_past.dtype)], axis=1)
    kc, c_end = compress_blocks(rows[:, :, 0], lp['cmp_pos'][0], lp['cmp_w1'][0], lp['cmp_w2'][0])
    vc, _ = compress_blocks(rows[:, :, 1], lp['cmp_pos'][1], lp['cmp_w1'][1], lp['cmp_w2'][1])
    pad = (-(past + t)) % SEL_BLOCK
    sel = jnp.pad(rows[:, :, 2:4], ((0, 0), (0, pad), (0, 0), (0, 0), (0, 0)))
    win_all = jnp.concatenate([win_buf, nkv[:, :, 4:6].astype(win_buf.dtype)], axis=1)
    wb = win_buf.shape[1]
    kw_pos = past - wb + jnp.arange(wb + t)
    nsa_o = nsa_attend(nq, nsa_g, q_pos, kc, vc, c_end, sel[:, :, 0], sel[:, :, 1], win_all[:, :, 0], win_all[:, :, 1], kw_pos)
    y = merge_and_ffn(x, ssd_o, sb_o, nsa_o, br_g, lp)
    keep = min(WINDOW, past + t)
    return y, qkv[:, :, 1:3], nkv[:, :, 0:4], win_all[:, wb + t - keep:], h_new, conv_new


def setup_inputs(seed: int = 0) -> dict:
    key = jax.random.key(seed)
    ks = iter(jax.random.split(key, 40))
    nrm = lambda shape, scale: jax.random.normal(next(ks), shape, jnp.float32) * scale
    gain = lambda shape: 1.0 + nrm(shape, 0.02)
    n_pages = PAST_LEN // PAGE_SIZE
    n_used = DEC_BATCH * n_pages
    n_pool = n_used + max(1, n_used // 4)
    page_table = jax.random.permutation(next(ks), n_pool)[:n_used].reshape(DEC_BATCH, n_pages).astype(jnp.int32)
    dt0 = jnp.exp(jax.random.uniform(next(ks), (DEPTH, SSD_HEADS)) * (math.log(0.1) - math.log(0.001)) + math.log(0.001))
    dt_bias = dt0 + jnp.log(-jnp.expm1(-dt0))
    a_log = jnp.log(jax.random.uniform(next(ks), (DEPTH, SSD_HEADS), minval=1.0, maxval=16.0))
    return {
        'x_prompt': nrm((BATCH, SEQ, D_MODEL), 1.0),
        'x_sample': nrm((DEC_BATCH, DEC_SEQ, D_MODEL), 1.0),
        'cache_sb_kv': nrm((DEPTH, n_pool, PAGE_SIZE, 2, SB_HEADS, HEAD_DIM), 1.0),
        'cache_nsa_kv': nrm((DEPTH, n_pool, PAGE_SIZE, 4, NSA_KV_GROUPS, HEAD_DIM), 1.0),
        'cache_nsa_win': nrm((DEPTH, DEC_BATCH, min(WINDOW, PAST_LEN), 2, NSA_KV_GROUPS, HEAD_DIM), 1.0),
        'state_ssd': nrm((DEPTH, DEC_BATCH, SSD_HEADS, SSD_HEAD_DIM, SSD_STATE), 0.1),
        'state_conv': nrm((DEPTH, DEC_BATCH, SSD_CONV - 1, SSD_CONV_DIM), 1.0),
        'page_table': page_table,
        'norm_mix_pre': gain((DEPTH, D_MODEL)),
        'norm_mix_post': gain((DEPTH, D_MODEL)),
        'norm_ffn_pre': gain((DEPTH, D_MODEL)),
        'norm_ffn_post': gain((DEPTH, D_MODEL)),
        'w_in': nrm((DEPTH, D_MODEL, IN_COLS), D_MODEL ** -0.5),
        'ssd_conv_w': nrm((DEPTH, SSD_CONV, SSD_CONV_DIM), SSD_CONV ** -0.5),
        'ssd_conv_b': nrm((DEPTH, SSD_CONV_DIM), 0.02),
        'ssd_dt_bias': dt_bias,
        'ssd_a_log': a_log,
        'ssd_d': gain((DEPTH, SSD_HEADS)),
        'ssd_norm': gain((DEPTH, SSD_INNER)),
        'w_ssd_out': nrm((DEPTH, SSD_INNER, D_MODEL), SSD_INNER ** -0.5),
        'w_sb_out': nrm((DEPTH, SB_WIDTH, D_MODEL), SB_WIDTH ** -0.5),
        'nsa_cmp_pos': nrm((DEPTH, 2, CMP_BLOCK, HEAD_DIM), 0.1),
        'nsa_cmp_w1': nrm((DEPTH, 2, CMP_BLOCK * HEAD_DIM, CMP_HIDDEN), (CMP_BLOCK * HEAD_DIM) ** -0.5),
        'nsa_cmp_w2': nrm((DEPTH, 2, CMP_HIDDEN, HEAD_DIM), CMP_HIDDEN ** -0.5),
        'w_nsa_out': nrm((DEPTH, NSA_WIDTH, D_MODEL), NSA_WIDTH ** -0.5),
        'w_o': nrm((DEPTH, D_MODEL, D_MODEL), D_MODEL ** -0.5),
        'w_ffn_gate': nrm((DEPTH, D_MODEL, FFN_HIDDEN), D_MODEL ** -0.5),
        'w_ffn_up': nrm((DEPTH, D_MODEL, FFN_HIDDEN), D_MODEL ** -0.5),
        'w_ffn_down': nrm((DEPTH, FFN_HIDDEN, D_MODEL), FFN_HIDDEN ** -0.5),
    }


def reference(x_prompt, x_sample, cache_sb_kv, cache_nsa_kv, cache_nsa_win, state_ssd, state_conv, page_table,
              norm_mix_pre, norm_mix_post, norm_ffn_pre, norm_ffn_post, w_in, ssd_conv_w, ssd_conv_b, ssd_dt_bias,
              ssd_a_log, ssd_d, ssd_norm, w_ssd_out, w_sb_out, nsa_cmp_pos, nsa_cmp_w1, nsa_cmp_w2, w_nsa_out, w_o,
              w_ffn_gate, w_ffn_up, w_ffn_down):
    yp, ys = x_prompt, x_sample
    sb_p, sb_s, nk_p, nk_s, win_p, win_s, h_p, h_s, cv_p, cv_s = [], [], [], [], [], [], [], [], [], []
    for l in range(DEPTH):
        lp = {
            'mix_pre': norm_mix_pre[l], 'mix_post': norm_mix_post[l],
            'ffn_pre': norm_ffn_pre[l], 'ffn_post': norm_ffn_post[l],
            'w_in': w_in[l], 'conv_w': ssd_conv_w[l], 'conv_b': ssd_conv_b[l],
            'dt_bias': ssd_dt_bias[l], 'a_log': ssd_a_log[l], 'd_skip': ssd_d[l],
            'ssd_norm': ssd_norm[l], 'w_ssd_out': w_ssd_out[l], 'w_sb_out': w_sb_out[l],
            'cmp_pos': nsa_cmp_pos[l], 'cmp_w1': nsa_cmp_w1[l], 'cmp_w2': nsa_cmp_w2[l],
            'w_nsa_out': w_nsa_out[l], 'w_o': w_o[l],
            'w_ffn_gate': w_ffn_gate[l], 'w_ffn_up': w_ffn_up[l], 'w_ffn_down': w_ffn_down[l],
        }
        yp, a1, a2, a3, a4, a5 = layer_prompt(yp, lp)
        sb_p.append(a1); nk_p.append(a2); win_p.append(a3); h_p.append(a4); cv_p.append(a5)
        ys, b1, b2, b3, b4, b5 = layer_sample(ys, lp, cache_sb_kv[l], cache_nsa_kv[l], cache_nsa_win[l],
                                              state_ssd[l], state_conv[l], page_table)
        sb_s.append(b1); nk_s.append(b2); win_s.append(b3); h_s.append(b4); cv_s.append(b5)
    return (yp, ys, jnp.stack(sb_p), jnp.stack(sb_s), jnp.stack(nk_p), jnp.stack(nk_s),
            jnp.stack(win_p), jnp.stack(win_s), jnp.stack(h_p), jnp.stack(h_s), jnp.stack(cv_p), jnp.stack(cv_s))
```

```python
import functools
import math

import jax
import jax.numpy as jnp
from jax import lax
from jax.experimental import pallas as pl
from jax.experimental.pallas import tpu as pltpu

D_MODEL = 1024
HEAD_DIM = 64
SSD_INNER = D_MODEL
SSD_HEAD_DIM = 64
SSD_HEADS = SSD_INNER // SSD_HEAD_DIM
SSD_GROUPS = 2
SSD_STATE = 128
SSD_CONV = 4
SSD_GN = SSD_GROUPS * SSD_STATE
SSD_CONV_DIM = SSD_INNER + 2 * SSD_GN
SSD_CHUNK = 128
SB_HEADS = 8
SB_WIDTH = SB_HEADS * HEAD_DIM
NSA_HEADS = 8
NSA_KV_GROUPS = 2
NSA_REP = NSA_HEADS // NSA_KV_GROUPS
NSA_WIDTH = NSA_HEADS * HEAD_DIM
NSA_KV_WIDTH = NSA_KV_GROUPS * HEAD_DIM
CMP_BLOCK = 32
CMP_STRIDE = 16
CMP_HIDDEN = 128
SEL_BLOCK = 64
SEL_TOP = 16
SEL_LOCAL = 2
WINDOW = 512
Q_BLOCK = 128
N_BRANCH = 3
FFN_HIDDEN = ((8 * D_MODEL + 3 * 256 - 1) // (3 * 256)) * 256
RMS_EPS = 1e-6
IN_SPLITS = (SSD_INNER, SSD_CONV_DIM, SSD_HEADS, 3 * SB_WIDTH, NSA_WIDTH, 6 * NSA_KV_WIDTH, 3 * NSA_HEADS,
             N_BRANCH * D_MODEL)

V7X_LANES = 128
V7X_VMEM_LIMIT = 56 * 1024 * 1024
NEG = -1e30
BF16 = jnp.bfloat16
F32 = jnp.float32
HI = lax.Precision.HIGHEST


def _params(*sem):
    return pltpu.CompilerParams(dimension_semantics=sem, vmem_limit_bytes=V7X_VMEM_LIMIT)


def _pick(n, cands):
    for c in cands:
        if n % c == 0:
            return c
    return n


def _rms(x, w):
    return x * lax.rsqrt(jnp.mean(x * x, axis=-1, keepdims=True) + RMS_EPS) * w


def _softplus(x):
    return jnp.maximum(x, 0.0) + jnp.log1p(jnp.exp(-jnp.abs(x)))


def _sigmoid(x):
    return 1.0 / (1.0 + jnp.exp(-x))


def _dot_nt(a, b):
    return lax.dot_general(a, b, (((1,), (1,)), ((), ())), preferred_element_type=F32)


def _dot_tn(a, b):
    return lax.dot_general(a, b, (((0,), (0,)), ((), ())), preferred_element_type=F32)


def _norm_mm_kernel(x_ref, g_ref, w_ref, o_ref, h_ref, *, transposed):
    @pl.when(pl.program_id(2) == 0)
    def _():
        h_ref[...] = _rms(x_ref[...], g_ref[...]).astype(BF16)

    if transposed:
        o_ref[...] = _dot_nt(w_ref[...], h_ref[...]).astype(o_ref.dtype)
    else:
        o_ref[...] = jnp.dot(h_ref[...], w_ref[...], preferred_element_type=F32).astype(o_ref.dtype)


def norm_matmul(x, gain, w, out_dtype=F32, transposed=False):
    b, t, k = x.shape
    n = w.shape[0] if transposed else w.shape[1]
    tm = _pick(t, (512, 256, 128))
    tn = _pick(n, (512, 256, 128))
    if transposed:
        w_spec = pl.BlockSpec((tn, k), lambda bi, i, j: (j, 0))
        o_spec = pl.BlockSpec((None, tn, tm), lambda bi, i, j: (bi, j, i))
        o_shape = (b, n, t)
    else:
        w_spec = pl.BlockSpec((k, tn), lambda bi, i, j: (0, j))
        o_spec = pl.BlockSpec((None, tm, tn), lambda bi, i, j: (bi, i, j))
        o_shape = (b, t, n)
    return pl.pallas_call(
        functools.partial(_norm_mm_kernel, transposed=transposed),
        grid=(b, t // tm, n // tn),
        in_specs=[pl.BlockSpec((None, tm, k), lambda bi, i, j: (bi, i, 0)),
                  pl.BlockSpec((1, k), lambda bi, i, j: (0, 0)),
                  w_spec],
        out_specs=o_spec,
        out_shape=jax.ShapeDtypeStruct(o_shape, out_dtype),
        scratch_shapes=[pltpu.VMEM((tm, k), BF16)],
        compiler_params=_params("parallel", "parallel", "arbitrary"),
        name="norm_matmul_t" if transposed else "norm_matmul",
    )(x, gain.reshape(1, k), w)


def _ssd_chunk_kernel(xbc_ref, z_ref, dt_ref, dtt_ref, h0_ref, c0_ref, cw_ref, cb_ref, dtb_ref, dtbt_ref,
                      alog_ref, alogt_ref, dfull_ref, nw_ref, exp_ref, y_ref, h_ref, xp_ref):
    q = SSD_CHUNK
    c = pl.program_id(1)

    @pl.when(c == 0)
    def _():
        h_ref[...] = h0_ref[...]
        xp_ref[5:8, :] = c0_ref[...]

    xp_ref[8:8 + q, :] = xbc_ref[...]
    conv = cb_ref[...]
    for j in range(SSD_CONV):
        conv = conv + cw_ref[j:j + 1, :] * xp_ref[5 + j:5 + j + q, :]
    xp_ref[5:8, :] = xp_ref[q + 5:q + 8, :]
    u = conv * _sigmoid(conv)
    xs = u[:, :SSD_INNER]
    bm = u[:, SSD_INNER:SSD_INNER + SSD_GN].astype(BF16)
    cm = u[:, SSD_INNER + SSD_GN:].astype(BF16)

    dt = _softplus(dt_ref[...] + dtb_ref[...])
    dtt = _softplus(dtt_ref[...] + dtbt_ref[...])
    dta = dt * (-jnp.exp(alog_ref[...]))
    dtat = dtt * (-jnp.exp(alogt_ref[...]))
    row = lax.broadcasted_iota(jnp.int32, (q, q), 0)
    col = lax.broadcasted_iota(jnp.int32, (q, q), 1)
    tril = row >= col
    acum = jnp.dot(tril.astype(F32), dta, precision=HI, preferred_element_type=F32)
    acumt = jnp.dot(dtat, (row <= col).astype(F32), precision=HI, preferred_element_type=F32)
    expand = exp_ref[...]
    dt_full = jnp.dot(dt, expand, precision=HI, preferred_element_type=F32)
    ea_full = jnp.dot(jnp.exp(acum), expand, precision=HI, preferred_element_type=F32)
    te_full = jnp.dot(jnp.exp(acum[q - 1:q, :] - acum), expand, precision=HI, preferred_element_type=F32)
    xdt = xs * dt_full
    xdt_b = xdt.astype(BF16)
    xw_b = (xdt * te_full).astype(BF16)

    r = SSD_HEADS // SSD_GROUPS
    gw = r * SSD_HEAD_DIM
    y_diag, y_off = [], []
    for g in range(SSD_GROUPS):
        cm_g = cm[:, g * SSD_STATE:(g + 1) * SSD_STATE]
        bm_g = bm[:, g * SSD_STATE:(g + 1) * SSD_STATE]
        cb = _dot_nt(cm_g, bm_g)
        h_g = h_ref[g * r:(g + 1) * r].reshape(gw, SSD_STATE)
        y_off.append(_dot_nt(cm_g, h_g.astype(BF16)))
        st = _dot_tn(xw_b[:, g * gw:(g + 1) * gw], bm_g)
        for hh in range(r):
            hd = g * r + hh
            seg = acum[:, hd:hd + 1] - acumt[hd:hd + 1, :]
            decay = jnp.exp(jnp.where(tril, seg, -jnp.inf))
            m = (cb * decay).astype(BF16)
            y_diag.append(jnp.dot(m, xdt_b[:, hd * SSD_HEAD_DIM:(hd + 1) * SSD_HEAD_DIM],
                                  preferred_element_type=F32))
            dec = jnp.exp(acumt[hd:hd + 1, q - 1:q])
            h_ref[hd] = dec * h_ref[hd] + st[hh * SSD_HEAD_DIM:(hh + 1) * SSD_HEAD_DIM, :]
    y = (jnp.concatenate(y_diag, axis=1) + jnp.concatenate(y_off, axis=1) * ea_full
         + dfull_ref[...] * xs)
    zz = z_ref[...]
    y = y * (zz * _sigmoid(zz))
    y_ref[...] = _rms(y, nw_ref[...]).astype(y_ref.dtype)


def ssd_prompt(xbc, z, dt_raw, h0, conv0, conv_w, conv_b, dt_bias, a_log, d_skip, norm_w):
    b, t, _ = xbc.shape
    q = SSD_CHUNK
    nc = t // q
    hds = SSD_HEADS
    expand = (jnp.arange(SSD_INNER)[None, :] // SSD_HEAD_DIM == jnp.arange(hds)[:, None]).astype(F32)
    d_full = jnp.repeat(d_skip, SSD_HEAD_DIM).reshape(1, SSD_INNER)
    dtt = jnp.swapaxes(dt_raw, 1, 2)
    full = lambda shape: pl.BlockSpec(shape, lambda bi, ci: (0,) * len(shape))
    y, h = pl.pallas_call(
        _ssd_chunk_kernel,
        grid=(b, nc),
        in_specs=[pl.BlockSpec((None, q, SSD_CONV_DIM), lambda bi, ci: (bi, ci, 0)),
                  pl.BlockSpec((None, q, SSD_INNER), lambda bi, ci: (bi, ci, 0)),
                  pl.BlockSpec((None, q, hds), lambda bi, ci: (bi, ci, 0)),
                  pl.BlockSpec((None, hds, q), lambda bi, ci: (bi, 0, ci)),
                  pl.BlockSpec((None, hds, SSD_HEAD_DIM, SSD_STATE), lambda bi, ci: (bi, 0, 0, 0)),
                  pl.BlockSpec((None, SSD_CONV - 1, SSD_CONV_DIM), lambda bi, ci: (bi, 0, 0)),
                  full((SSD_CONV, SSD_CONV_DIM)), full((1, SSD_CONV_DIM)),
                  full((1, hds)), full((hds, 1)), full((1, hds)), full((hds, 1)),
                  full((1, SSD_INNER)), full((1, SSD_INNER)), full((hds, SSD_INNER))],
        out_specs=[pl.BlockSpec((None, q, SSD_INNER), lambda bi, ci: (bi, ci, 0)),
                   pl.BlockSpec((None, hds, SSD_HEAD_DIM, SSD_STATE), lambda bi, ci: (bi, 0, 0, 0))],
        out_shape=[jax.ShapeDtypeStruct((b, t, SSD_INNER), BF16),
                   jax.ShapeDtypeStruct((b, hds, SSD_HEAD_DIM, SSD_STATE), F32)],
        scratch_shapes=[pltpu.VMEM((q + 8, SSD_CONV_DIM), F32)],
        compiler_params=_params("parallel", "arbitrary"),
        name="ssd_chunk_scan",
    )(xbc, z, dt_raw, dtt, h0, conv0, conv_w, conv_b.reshape(1, -1), dt_bias.reshape(1, hds),
      dt_bias.reshape(hds, 1), a_log.reshape(1, hds), a_log.reshape(hds, 1), d_full, norm_w.reshape(1, -1), expand)
    return y, h


def _sb_tile(q, kt, vt, carry, acc, upper, mask):
    z = jnp.dot(q, kt, preferred_element_type=F32)
    sp = _softplus(z)
    lk = -sp if mask is None else jnp.where(mask, -sp, 0.0)
    hi = lk.astype(BF16)
    lo = (lk - hi.astype(F32)).astype(BF16)
    later = jnp.dot(hi, upper, preferred_element_type=F32) + jnp.dot(lo, upper, preferred_element_type=F32) + carry
    a = jnp.exp(z - sp + later)
    if mask is not None:
        a = jnp.where(mask, a, 0.0)
    acc = acc + _dot_nt(a.astype(BF16), vt)
    carry = later[:, 0:1] + lk[:, 0:1]
    return carry, acc


def _sb_prompt_kernel(q_ref, kt_ref, vt_ref, o_ref, *, tb):
    i = pl.program_id(2)
    q = (q_ref[...] * (HEAD_DIM ** -0.5)).astype(BF16)
    row = lax.broadcasted_iota(jnp.int32, (tb, tb), 0)
    col = lax.broadcasted_iota(jnp.int32, (tb, tb), 1)
    upper = (row > col).astype(BF16)
    carry = jnp.zeros((tb, 1), F32)
    acc = jnp.zeros((tb, HEAD_DIM), F32)
    off = pl.multiple_of(i * tb, tb)
    carry, acc = _sb_tile(q, kt_ref[:, pl.ds(off, tb)].astype(BF16), vt_ref[:, pl.ds(off, tb)].astype(BF16),
                          carry, acc, upper, col < row)

    def body(n, st):
        carry, acc = st
        off = pl.multiple_of((i - 1 - n) * tb, tb)
        return _sb_tile(q, kt_ref[:, pl.ds(off, tb)].astype(BF16), vt_ref[:, pl.ds(off, tb)].astype(BF16),
                        carry, acc, upper, None)

    carry, acc = lax.fori_loop(0, i, body, (carry, acc))
    o_ref[...] = acc.astype(o_ref.dtype)


def sb_prompt(q, kvt, tb=128):
    b, t, _ = q.shape
    h = SB_HEADS
    qh = jnp.swapaxes(q.reshape(b, t, h, HEAD_DIM), 1, 2)
    out = pl.pallas_call(
        functools.partial(_sb_prompt_kernel, tb=tb),
        grid=(b, h, t // tb),
        in_specs=[pl.BlockSpec((None, None, tb, HEAD_DIM), lambda bi, hi, i: (bi, hi, i, 0)),
                  pl.BlockSpec((None, HEAD_DIM, t), lambda bi, hi, i: (bi, hi, 0)),
                  pl.BlockSpec((None, HEAD_DIM, t), lambda bi, hi, i: (bi, h + hi, 0))],
        out_specs=pl.BlockSpec((None, None, tb, HEAD_DIM), lambda bi, hi, i: (bi, hi, i, 0)),
        out_shape=jax.ShapeDtypeStruct((b, h, t, HEAD_DIM), BF16),
        compiler_params=_params("parallel", "parallel", "arbitrary"),
        name="sb_prompt",
    )(qh, kvt, kvt)
    return jnp.swapaxes(out, 1, 2).reshape(b, t, h * HEAD_DIM)


def _merge_kernel(x_ref, ssd_ref, sb_ref, nsa_ref, gl_ref, wssd_ref, wsb_ref, wnsa_ref, wo_ref, nw_ref, o_ref):
    d = D_MODEL
    gl = gl_ref[...]
    merged = (_sigmoid(gl[:, :d]) * jnp.dot(ssd_ref[...], wssd_ref[...], preferred_element_type=F32)
              + _sigmoid(gl[:, d:2 * d]) * jnp.dot(sb_ref[...], wsb_ref[...], preferred_element_type=F32)
              + _sigmoid(gl[:, 2 * d:]) * jnp.dot(nsa_ref[...], wnsa_ref[...], preferred_element_type=F32))
    y = jnp.dot(merged.astype(BF16), wo_ref[...], preferred_element_type=F32)
    o_ref[...] = x_ref[...] + _rms(y, nw_ref[...])


def merge_branches(x, ssd_y, sb_o, nsa_o, gate_logits, w_ssd_out, w_sb_out, w_nsa_out, w_o, norm_w):
    m, d = x.shape
    tm = _pick(m, (256, 128, 32))
    rows = lambda n: pl.BlockSpec((tm, n), lambda i: (i, 0))
    full = lambda a: pl.BlockSpec(a.shape, lambda i: (0, 0))
    nw = norm_w.reshape(1, d)
    return pl.pallas_call(
        _merge_kernel,
        grid=(m // tm,),
        in_specs=[rows(d), rows(ssd_y.shape[1]), rows(sb_o.shape[1]), rows(nsa_o.shape[1]), rows(N_BRANCH * d),
                  full(w_ssd_out), full(w_sb_out), full(w_nsa_out), full(w_o), full(nw)],
        out_specs=rows(d),
        out_shape=jax.ShapeDtypeStruct((m, d), F32),
        compiler_params=_params("parallel"),
        name="merge_branches",
    )(x, ssd_y, sb_o, nsa_o, gate_logits, w_ssd_out, w_sb_out, w_nsa_out, w_o, nw)


def _ffn_up_kernel(x_ref, g_ref, wg_ref, wu_ref, o_ref, h_ref):
    @pl.when(pl.program_id(1) == 0)
    def _():
        h_ref[...] = _rms(x_ref[...], g_ref[...]).astype(BF16)

    h = h_ref[...]
    a = jnp.dot(h, wg_ref[...], preferred_element_type=F32)
    u = jnp.dot(h, wu_ref[...], preferred_element_type=F32)
    o_ref[...] = (a * _sigmoid(a) * u).astype(o_ref.dtype)


def _ffn_down_kernel(a_ref, x_ref, wd_ref, nw_ref, o_ref):
    f = jnp.dot(a_ref[...], wd_ref[...], preferred_element_type=F32)
    o_ref[...] = x_ref[...] + _rms(f, nw_ref[...])


def ffn(x, pre_w, post_w, w_gate, w_up, w_down):
    m, d = x.shape
    f = w_gate.shape[1]
    tm = _pick(m, (512, 256, 128, 32))
    tn = _pick(f, (256, 128))
    act = pl.pallas_call(
        _ffn_up_kernel,
        grid=(m // tm, f // tn),
        in_specs=[pl.BlockSpec((tm, d), lambda i, j: (i, 0)),
                  pl.BlockSpec((1, d), lambda i, j: (0, 0)),
                  pl.BlockSpec((d, tn), lambda i, j: (0, j)),
                  pl.BlockSpec((d, tn), lambda i, j: (0, j))],
        out_specs=pl.BlockSpec((tm, tn), lambda i, j: (i, j)),
        out_shape=jax.ShapeDtypeStruct((m, f), BF16),
        scratch_shapes=[pltpu.VMEM((tm, d), BF16)],
        compiler_params=_params("parallel", "arbitrary"),
        name="ffn_up",
    )(x, pre_w.reshape(1, d), w_gate, w_up)
    tm2 = _pick(m, (256, 128, 32))
    return pl.pallas_call(
        _ffn_down_kernel,
        grid=(m // tm2,),
        in_specs=[pl.BlockSpec((tm2, f), lambda i: (i, 0)),
                  pl.BlockSpec((tm2, d), lambda i: (i, 0)),
                  pl.BlockSpec((f, d), lambda i: (0, 0)),
                  pl.BlockSpec((1, d), lambda i: (0, 0))],
        out_specs=pl.BlockSpec((tm2, d), lambda i: (i, 0)),
        out_shape=jax.ShapeDtypeStruct((m, d), F32),
        compiler_params=_params("parallel"),
        name="ffn_down",
    )(act, x, w_down, post_w.reshape(1, d))


def _x_rms_norm(x, w):
    xf = x.astype(jnp.float32)
    y = xf * lax.rsqrt(jnp.mean(xf * xf, axis=-1, keepdims=True) + RMS_EPS)
    return (y * w.astype(jnp.float32)).astype(x.dtype)


def _x_masked_softmax(s, mask):
    s = jnp.where(mask, s.astype(jnp.float32), -jnp.inf)
    m = jnp.max(s, axis=-1, keepdims=True)
    m = jnp.where(jnp.isfinite(m), m, 0.0)
    e = jnp.where(mask, jnp.exp(s - m), 0.0)
    return e / jnp.maximum(jnp.sum(e, axis=-1, keepdims=True), 1e-30)


def _x_causal_dwconv(x, buf, w, b):
    xp = jnp.concatenate([buf.astype(x.dtype), x], axis=1)
    y = lax.conv_general_dilated(xp, w[:, None, :].astype(x.dtype), window_strides=(1,), padding='VALID',
                                 dimension_numbers=('NWC', 'WIO', 'NWC'), feature_group_count=x.shape[-1])
    return y + b.astype(x.dtype), xp[:, xp.shape[1] - (SSD_CONV - 1):]


def _x_ssd_scan(x, dt, a, bm, cm, h0):
    bsz, t = x.shape[:2]
    q = min(SSD_CHUNK, t)
    nc = t // q
    r = SSD_HEADS // SSD_GROUPS
    xdt = (x * dt[..., None]).reshape(bsz, nc, q, SSD_GROUPS, r, SSD_HEAD_DIM)
    acum = jnp.cumsum((dt * a).reshape(bsz, nc, q, SSD_GROUPS, r), axis=2)
    bm = bm.reshape(bsz, nc, q, SSD_GROUPS, SSD_STATE)
    cm = cm.reshape(bsz, nc, q, SSD_GROUPS, SSD_STATE)
    at = jnp.moveaxis(acum, 2, -1)
    tril = jnp.tril(jnp.ones((q, q), dtype=bool))
    decay = jnp.exp(jnp.where(tril, at[..., :, None] - at[..., None, :], -jnp.inf))
    cb = jnp.einsum('bclgn,bcsgn->bcgls', cm, bm)
    y_diag = jnp.einsum('bcgls,bcgrls,bcsgrp->bclgrp', cb, decay, xdt)
    to_end = jnp.exp(acum[:, :, -1:] - acum)
    states = jnp.einsum('bclgn,bclgr,bclgrp->bcgrpn', bm, to_end, xdt)
    chunk_decay = jnp.exp(acum[:, :, -1])

    def step(h, inp):
        dec, st = inp
        return dec[..., None, None] * h + st, h

    h_init = h0.reshape(bsz, SSD_GROUPS, r, SSD_HEAD_DIM, SSD_STATE)
    h_fin, h_in = lax.scan(step, h_init, (jnp.moveaxis(chunk_decay, 1, 0), jnp.moveaxis(states, 1, 0)))
    h_in = jnp.moveaxis(h_in, 0, 1)
    y_off = jnp.einsum('bclgn,bcgrpn,bclgr->bclgrp', cm, h_in, jnp.exp(acum))
    y = (y_diag + y_off).reshape(bsz, nc * q, SSD_HEADS, SSD_HEAD_DIM)[:, :t]
    return y, h_fin.reshape(bsz, SSD_HEADS, SSD_HEAD_DIM, SSD_STATE)


def _x_ssd_branch(z, xbc, dt_raw, conv_buf, h0, lp):
    bsz, t = z.shape[:2]
    xbc, new_buf = _x_causal_dwconv(xbc, conv_buf, lp['conv_w'], lp['conv_b'])
    xbc = jax.nn.silu(xbc)
    gn = SSD_GROUPS * SSD_STATE
    xs = xbc[..., :SSD_INNER].reshape(bsz, t, SSD_HEADS, SSD_HEAD_DIM).astype(jnp.float32)
    bm = xbc[..., SSD_INNER:SSD_INNER + gn].reshape(bsz, t, SSD_GROUPS, SSD_STATE).astype(jnp.float32)
    cm = xbc[..., SSD_INNER + gn:].reshape(bsz, t, SSD_GROUPS, SSD_STATE).astype(jnp.float32)
    dt = jax.nn.softplus((dt_raw + lp['dt_bias']).astype(jnp.float32))
    a = -jnp.exp(lp['a_log'].astype(jnp.float32))
    y, h_fin = _x_ssd_scan(xs, dt, a, bm, cm, h0.astype(jnp.float32))
    y = y + lp['d_skip'].astype(jnp.float32)[:, None] * xs
    y = y.reshape(bsz, t, SSD_INNER).astype(z.dtype)
    y = _x_rms_norm(y * jax.nn.silu(z), lp['ssd_norm'])
    return y, new_buf, h_fin.astype(h0.dtype)


def _x_stick_breaking(q, k, v, q_pos, k_pos):
    z = jnp.einsum('bqhd,bkhd->bhqk', q, k).astype(jnp.float32) * (HEAD_DIM ** -0.5)
    mask = k_pos[None, :] < q_pos[:, None]
    log_keep = jnp.where(mask, jax.nn.log_sigmoid(-z), 0.0)
    later = lax.cumsum(log_keep, axis=3, reverse=True) - log_keep
    a = jnp.where(mask, jnp.exp(jax.nn.log_sigmoid(z) + later), 0.0)
    return jnp.einsum('bhqk,bkhd->bqhd', a.astype(v.dtype), v)


def _x_compress_blocks(rows, pos_emb, w1, w2):
    bsz, t = rows.shape[:2]
    nc = (t - CMP_BLOCK) // CMP_STRIDE + 1
    starts = jnp.arange(nc) * CMP_STRIDE
    idx = starts[:, None] + jnp.arange(CMP_BLOCK)[None, :]
    blocks = rows[:, idx] + pos_emb[None, None, :, None, :].astype(rows.dtype)
    blocks = jnp.moveaxis(blocks, 3, 2).reshape(bsz, nc, NSA_KV_GROUPS, CMP_BLOCK * HEAD_DIM)
    return jax.nn.silu(blocks @ w1) @ w2, starts + CMP_BLOCK - 1


def _x_nsa_attend(q, gates, q_pos, kc, vc, c_end, ks, vs, kw, vw, kw_pos):
    bsz, tq = q.shape[:2]
    r = NSA_HEADS // NSA_KV_GROUPS
    scale = HEAD_DIM ** -0.5
    qg = q.reshape(bsz, tq, NSA_KV_GROUPS, r, HEAD_DIM)
    s_c = jnp.einsum('bqgrd,bcgd->bqgrc', qg, kc).astype(jnp.float32) * scale
    p_c = _x_masked_softmax(s_c, (c_end[None, :] <= q_pos[:, None])[None, :, None, None, :])
    o_c = jnp.einsum('bqgrc,bcgd->bqgrd', p_c.astype(vc.dtype), vc)
    n_blk = ks.shape[1] // SEL_BLOCK
    blk = jnp.arange(n_blk)
    c_start = c_end - (CMP_BLOCK - 1)
    overlap = ((c_start[:, None] < (blk[None, :] + 1) * SEL_BLOCK) & (c_end[:, None] >= blk[None, :] * SEL_BLOCK)).astype(jnp.float32)
    imp = jnp.einsum('bqgrc,cn->bqgn', p_c, overlap)
    cur = q_pos // SEL_BLOCK
    valid = blk[None, :] <= cur[:, None]
    forced = valid & ((blk[None, :] == 0) | (blk[None, :] > cur[:, None] - SEL_LOCAL))
    imp = jnp.where(forced[None, :, None, :], jnp.inf, jnp.where(valid[None, :, None, :], imp, -jnp.inf))
    n_top = min(SEL_TOP, n_blk)
    _, idx = lax.top_k(imp, n_top)
    kb = jnp.moveaxis(ks.reshape(bsz, n_blk, SEL_BLOCK, NSA_KV_GROUPS, HEAD_DIM), 3, 1)
    vb = jnp.moveaxis(vs.reshape(bsz, n_blk, SEL_BLOCK, NSA_KV_GROUPS, HEAD_DIM), 3, 1)
    idx_g = jnp.moveaxis(idx, 2, 1)
    take = jax.vmap(jax.vmap(lambda blocks, ids: blocks[ids]))
    gk = take(kb, idx_g)
    gv = take(vb, idx_g)
    sel_pos = idx_g[..., None] * SEL_BLOCK + jnp.arange(SEL_BLOCK)
    m_s = jnp.moveaxis(sel_pos <= q_pos[None, None, :, None, None], 1, 2)[:, :, :, None]
    s_s = jnp.einsum('bqgrd,bgqnsd->bqgrns', qg, gk).astype(jnp.float32) * scale
    nk = n_top * SEL_BLOCK
    p_s = _x_masked_softmax(s_s.reshape(bsz, tq, NSA_KV_GROUPS, r, nk), m_s.reshape(bsz, tq, NSA_KV_GROUPS, 1, nk))
    o_s = jnp.einsum('bqgrk,bgqkd->bqgrd', p_s.astype(gv.dtype), gv.reshape(bsz, NSA_KV_GROUPS, tq, nk, HEAD_DIM))
    s_w = jnp.einsum('bqgrd,bkgd->bqgrk', qg, kw).astype(jnp.float32) * scale
    m_w = (kw_pos[None, :] <= q_pos[:, None]) & (kw_pos[None, :] > q_pos[:, None] - WINDOW) & (kw_pos[None, :] >= 0)
    p_w = _x_masked_softmax(s_w, m_w[None, :, None, None, :])
    o_w = jnp.einsum('bqgrk,bkgd->bqgrd', p_w.astype(vw.dtype), vw)
    g = jax.nn.sigmoid(gates.reshape(bsz, tq, NSA_KV_GROUPS, r, 3).astype(jnp.float32)).astype(q.dtype)
    o = g[..., 0:1] * o_c + g[..., 1:2] * o_s + g[..., 2:3] * o_w
    return o.reshape(bsz, tq, NSA_WIDTH)


def _x_nsa_prompt(nq, gates, nkv, lp):
    bsz, t = nq.shape[:2]
    kc, c_end = _x_compress_blocks(nkv[:, :, 0], lp['cmp_pos'][0], lp['cmp_w1'][0], lp['cmp_w2'][0])
    vc, _ = _x_compress_blocks(nkv[:, :, 1], lp['cmp_pos'][1], lp['cmp_w1'][1], lp['cmp_w2'][1])
    ks, vs = nkv[:, :, 2], nkv[:, :, 3]
    w_pad = jnp.pad(nkv[:, :, 4:6], ((0, 0), (WINDOW, 0), (0, 0), (0, 0), (0, 0)))
    nb = t // Q_BLOCK
    qb = jnp.moveaxis(nq.reshape(bsz, nb, Q_BLOCK, NSA_HEADS, HEAD_DIM), 1, 0)
    gb = jnp.moveaxis(gates.reshape(bsz, nb, Q_BLOCK, 3 * NSA_HEADS), 1, 0)

    def blk(args):
        qi, gi, i = args
        t0 = i * Q_BLOCK
        wi = lax.dynamic_slice_in_dim(w_pad, t0, WINDOW + Q_BLOCK, axis=1)
        kw_pos = t0 - WINDOW + jnp.arange(WINDOW + Q_BLOCK)
        return _x_nsa_attend(qi, gi, t0 + jnp.arange(Q_BLOCK), kc, vc, c_end, ks, vs, wi[:, :, 0], wi[:, :, 1], kw_pos)

    out = lax.map(blk, (qb, gb, jnp.arange(nb)))
    return jnp.moveaxis(out, 0, 1).reshape(bsz, t, NSA_WIDTH)


def _col_offsets():
    offs, s = [], 0
    for n in IN_SPLITS:
        offs.append(s)
        s += n
    return offs


def _layer_weights(l, p):
    o = _col_offsets()
    w_in = p['w_in'][l]
    cols = lambda a, n: w_in[:, a:a + n]
    small = jnp.concatenate([cols(o[2], SSD_HEADS), cols(o[6], 3 * NSA_HEADS)], axis=1)
    small = jnp.pad(small, ((0, 0), (0, V7X_LANES - small.shape[1])))
    bf = lambda a: a.astype(BF16)
    return {
        'w_z': bf(cols(o[0], SSD_INNER)), 'w_xbc': bf(cols(o[1], SSD_CONV_DIM)), 'w_small': bf(small),
        'w_sbq': bf(cols(o[3], SB_WIDTH)), 'w_sbkv': bf(cols(o[3] + SB_WIDTH, 2 * SB_WIDTH)),
        'w_nq': bf(cols(o[4], NSA_WIDTH)), 'w_nkv': bf(cols(o[5], 6 * NSA_KV_WIDTH)),
        'w_brg': bf(cols(o[7], N_BRANCH * D_MODEL)),
        'mix_pre': p['norm_mix_pre'][l], 'mix_post': p['norm_mix_post'][l],
        'ffn_pre': p['norm_ffn_pre'][l], 'ffn_post': p['norm_ffn_post'][l],
        'conv_w': p['ssd_conv_w'][l], 'conv_b': p['ssd_conv_b'][l], 'dt_bias': p['ssd_dt_bias'][l],
        'a_log': p['ssd_a_log'][l], 'd_skip': p['ssd_d'][l], 'ssd_norm': p['ssd_norm'][l],
        'w_ssd_out': bf(p['w_ssd_out'][l]), 'w_sb_out': bf(p['w_sb_out'][l]), 'w_nsa_out': bf(p['w_nsa_out'][l]),
        'w_o': bf(p['w_o'][l]), 'w_ffn_gate': bf(p['w_ffn_gate'][l]), 'w_ffn_up': bf(p['w_ffn_up'][l]),
        'w_ffn_down': bf(p['w_ffn_down'][l]),
        'cmp_pos': p['nsa_cmp_pos'][l], 'cmp_w1': p['nsa_cmp_w1'][l], 'cmp_w2': p['nsa_cmp_w2'][l],
    }


def _trunk_tail(x, ssd_y, sb_o, nsa_o, br_g, lw):
    b, t, d = x.shape
    m = b * t
    x1 = merge_branches(x.reshape(m, d), ssd_y.reshape(m, -1), sb_o.reshape(m, -1), nsa_o.reshape(m, -1),
                        br_g.reshape(m, -1), lw['w_ssd_out'], lw['w_sb_out'], lw['w_nsa_out'], lw['w_o'],
                        lw['mix_post'])
    x2 = ffn(x1, lw['ffn_pre'], lw['ffn_post'], lw['w_ffn_gate'], lw['w_ffn_up'], lw['w_ffn_down'])
    return x2.reshape(b, t, d)


def _layer_prompt(x, lw):
    b, t, _ = x.shape
    g = lw['mix_pre']
    z = norm_matmul(x, g, lw['w_z'])
    xbc = norm_matmul(x, g, lw['w_xbc'])
    small = norm_matmul(x, g, lw['w_small'])
    br_g = norm_matmul(x, g, lw['w_brg'])
    sb_q = norm_matmul(x, g, lw['w_sbq'], out_dtype=BF16)
    nsa_q = norm_matmul(x, g, lw['w_nq'])
    sb_kvt = norm_matmul(x, g, lw['w_sbkv'].T, transposed=True)
    nsa_kvt = norm_matmul(x, g, lw['w_nkv'].T, transposed=True)
    dt_raw = small[..., :SSD_HEADS]
    nsa_g = small[..., SSD_HEADS:SSD_HEADS + 3 * NSA_HEADS]

    h0 = jnp.zeros((b, SSD_HEADS, SSD_HEAD_DIM, SSD_STATE), F32)
    conv0 = jnp.zeros((b, SSD_CONV - 1, SSD_CONV_DIM), F32)
    ssd_y, h_new = ssd_prompt(xbc, z, dt_raw, h0, conv0, lw['conv_w'], lw['conv_b'], lw['dt_bias'], lw['a_log'],
                              lw['d_skip'], lw['ssd_norm'])
    conv_new = xbc[:, t - (SSD_CONV - 1):, :]
    sb_o = sb_prompt(sb_q, sb_kvt)

    nkv = jnp.swapaxes(nsa_kvt, 1, 2).reshape(b, t, 6, NSA_KV_GROUPS, HEAD_DIM)
    nsa_o = _x_nsa_prompt(nsa_q.reshape(b, t, NSA_HEADS, HEAD_DIM), nsa_g, nkv, lw).astype(BF16)

    y = _trunk_tail(x, ssd_y, sb_o, nsa_o, br_g, lw)
    sb_kv = jnp.moveaxis(sb_kvt.reshape(b, 2, SB_HEADS, HEAD_DIM, t), 4, 1)
    nsa_all = jnp.moveaxis(nsa_kvt.reshape(b, 6, NSA_KV_GROUPS, HEAD_DIM, t), 4, 1)
    keep = min(WINDOW, t)
    return y, sb_kv, nsa_all[:, :, 0:4], nsa_all[:, t - keep:, 4:6], h_new, conv_new


def _layer_sample(x, lw, sb_pool, nsa_pool, win_buf, h0, conv_buf, page_table):
    bsz, t = x.shape[:2]
    past = page_table.shape[1] * sb_pool.shape[1]
    xr = x.reshape(1, bsz * t, D_MODEL)
    g = lw['mix_pre']
    pr = lambda w: norm_matmul(xr, g, w).reshape(bsz, t, -1)
    z, xbc, small, br_g = pr(lw['w_z']), pr(lw['w_xbc']), pr(lw['w_small']), pr(lw['w_brg'])
    sb_q, sb_kv, nsa_q, nsa_kv = pr(lw['w_sbq']), pr(lw['w_sbkv']), pr(lw['w_nq']), pr(lw['w_nkv'])
    dt_raw = small[..., :SSD_HEADS]
    nsa_g = small[..., SSD_HEADS:SSD_HEADS + 3 * NSA_HEADS]
    ssd_y, conv_new, h_new = _x_ssd_branch(z, xbc, dt_raw, conv_buf, h0, lw)
    q_pos = past + jnp.arange(t)
    kv_new = sb_kv.reshape(bsz, t, 2, SB_HEADS, HEAD_DIM)
    sb_past = sb_pool[page_table].reshape(bsz, past, 2, SB_HEADS, HEAD_DIM)
    sb_all = jnp.concatenate([sb_past, kv_new], axis=1)
    sb_o = _x_stick_breaking(sb_q.reshape(bsz, t, SB_HEADS, HEAD_DIM), sb_all[:, :, 0], sb_all[:, :, 1], q_pos,
                             jnp.arange(past + t)).reshape(bsz, t, SB_WIDTH)
    nq = nsa_q.reshape(bsz, t, NSA_HEADS, HEAD_DIM)
    nkv = nsa_kv.reshape(bsz, t, 6, NSA_KV_GROUPS, HEAD_DIM)
    nsa_past = nsa_pool[page_table].reshape(bsz, past, 4, NSA_KV_GROUPS, HEAD_DIM)
    rows = jnp.concatenate([nsa_past, nkv[:, :, 0:4]], axis=1)
    kc, c_end = _x_compress_blocks(rows[:, :, 0], lw['cmp_pos'][0], lw['cmp_w1'][0], lw['cmp_w2'][0])
    vc, _ = _x_compress_blocks(rows[:, :, 1], lw['cmp_pos'][1], lw['cmp_w1'][1], lw['cmp_w2'][1])
    pad = (-(past + t)) % SEL_BLOCK
    sel = jnp.pad(rows[:, :, 2:4], ((0, 0), (0, pad), (0, 0), (0, 0), (0, 0)))
    win_all = jnp.concatenate([win_buf, nkv[:, :, 4:6]], axis=1)
    wb = win_buf.shape[1]
    kw_pos = past - wb + jnp.arange(wb + t)
    nsa_o = _x_nsa_attend(nq, nsa_g, q_pos, kc, vc, c_end, sel[:, :, 0], sel[:, :, 1], win_all[:, :, 0],
                          win_all[:, :, 1], kw_pos)
    y = _trunk_tail(x, ssd_y.astype(BF16), sb_o.astype(BF16), nsa_o.astype(BF16), br_g, lw)
    keep = min(WINDOW, past + t)
    return y, kv_new, nkv[:, :, 0:4], win_all[:, wb + t - keep:], h_new, conv_new


def kernel(x_prompt, x_sample, cache_sb_kv, cache_nsa_kv, cache_nsa_win, state_ssd, state_conv, page_table,
           norm_mix_pre, norm_mix_post, norm_ffn_pre, norm_ffn_post, w_in, ssd_conv_w, ssd_conv_b, ssd_dt_bias,
           ssd_a_log, ssd_d, ssd_norm, w_ssd_out, w_sb_out, nsa_cmp_pos, nsa_cmp_w1, nsa_cmp_w2, w_nsa_out, w_o,
           w_ffn_gate, w_ffn_up, w_ffn_down):
    p = dict(norm_mix_pre=norm_mix_pre, norm_mix_post=norm_mix_post, norm_ffn_pre=norm_ffn_pre,
             norm_ffn_post=norm_ffn_post, w_in=w_in, ssd_conv_w=ssd_conv_w, ssd_conv_b=ssd_conv_b,
             ssd_dt_bias=ssd_dt_bias, ssd_a_log=ssd_a_log, ssd_d=ssd_d, ssd_norm=ssd_norm, w_ssd_out=w_ssd_out,
             w_sb_out=w_sb_out, nsa_cmp_pos=nsa_cmp_pos, nsa_cmp_w1=nsa_cmp_w1, nsa_cmp_w2=nsa_cmp_w2,
             w_nsa_out=w_nsa_out, w_o=w_o, w_ffn_gate=w_ffn_gate, w_ffn_up=w_ffn_up, w_ffn_down=w_ffn_down)
    yp, ys = x_prompt, x_sample
    outs_p, outs_s = [], []
    for l in range(w_in.shape[0]):
        lw = _layer_weights(l, p)
        res = _layer_prompt(yp, lw)
        yp = res[0]
        outs_p.append(res[1:])
        res = _layer_sample(ys, lw, cache_sb_kv[l], cache_nsa_kv[l], cache_nsa_win[l], state_ssd[l], state_conv[l],
                            page_table)
        ys = res[0]
        outs_s.append(res[1:])
    st = lambda outs, i: jnp.stack([o[i] for o in outs])
    return (yp, ys, st(outs_p, 0), st(outs_s, 0), st(outs_p, 1), st(outs_s, 1), st(outs_p, 2), st(outs_s, 2),
            st(outs_p, 3), st(outs_s, 3), st(outs_p, 4), st(outs_s, 4))
```

```python
import functools
import math

import jax
import jax.numpy as jnp
from jax import lax
from jax.experimental import pallas as pl
from jax.experimental.pallas import tpu as pltpu

D_MODEL = 1024
HEAD_DIM = 64
SSD_INNER = D_MODEL
SSD_HEAD_DIM = 64
SSD_HEADS = SSD_INNER // SSD_HEAD_DIM
SSD_GROUPS = 2
SSD_STATE = 128
SSD_CONV = 4
SSD_GN = SSD_GROUPS * SSD_STATE
SSD_CONV_DIM = SSD_INNER + 2 * SSD_GN
SSD_CHUNK = 128
SB_HEADS = 8
SB_WIDTH = SB_HEADS * HEAD_DIM
NSA_HEADS = 8
NSA_KV_GROUPS = 2
NSA_REP = NSA_HEADS // NSA_KV_GROUPS
NSA_WIDTH = NSA_HEADS * HEAD_DIM
NSA_KV_WIDTH = NSA_KV_GROUPS * HEAD_DIM
CMP_BLOCK = 32
CMP_STRIDE = 16
CMP_HIDDEN = 128
SEL_BLOCK = 64
SEL_TOP = 16
SEL_LOCAL = 2
WINDOW = 512
Q_BLOCK = 128
N_BRANCH = 3
FFN_HIDDEN = ((8 * D_MODEL + 3 * 256 - 1) // (3 * 256)) * 256
RMS_EPS = 1e-6
IN_SPLITS = (SSD_INNER, SSD_CONV_DIM, SSD_HEADS, 3 * SB_WIDTH, NSA_WIDTH, 6 * NSA_KV_WIDTH, 3 * NSA_HEADS,
             N_BRANCH * D_MODEL)

V7X_LANES = 128
V7X_VMEM_LIMIT = 56 * 1024 * 1024
NEG = -1e30
BF16 = jnp.bfloat16
F32 = jnp.float32
HI = lax.Precision.HIGHEST


def _params(*sem):
    return pltpu.CompilerParams(dimension_semantics=sem, vmem_limit_bytes=V7X_VMEM_LIMIT)


def _pick(n, cands):
    for c in cands:
        if n % c == 0:
            return c
    return n


def _rms(x, w):
    return x * lax.rsqrt(jnp.mean(x * x, axis=-1, keepdims=True) + RMS_EPS) * w


def _softplus(x):
    return jnp.maximum(x, 0.0) + jnp.log1p(jnp.exp(-jnp.abs(x)))


def _sigmoid(x):
    return 1.0 / (1.0 + jnp.exp(-x))


def _dot_nt(a, b):
    return lax.dot_general(a, b, (((1,), (1,)), ((), ())), preferred_element_type=F32)


def _dot_tn(a, b):
    return lax.dot_general(a, b, (((0,), (0,)), ((), ())), preferred_element_type=F32)


def _norm_mm_kernel(x_ref, g_ref, w_ref, o_ref, h_ref, *, transposed):
    @pl.when(pl.program_id(2) == 0)
    def _():
        h_ref[...] = _rms(x_ref[...], g_ref[...]).astype(BF16)

    if transposed:
        o_ref[...] = _dot_nt(w_ref[...], h_ref[...]).astype(o_ref.dtype)
    else:
        o_ref[...] = jnp.dot(h_ref[...], w_ref[...], preferred_element_type=F32).astype(o_ref.dtype)


def norm_matmul(x, gain, w, out_dtype=F32, transposed=False):
    b, t, k = x.shape
    n = w.shape[0] if transposed else w.shape[1]
    tm = _pick(t, (512, 256, 128))
    tn = _pick(n, (512, 256, 128))
    if transposed:
        w_spec = pl.BlockSpec((tn, k), lambda bi, i, j: (j, 0))
        o_spec = pl.BlockSpec((None, tn, tm), lambda bi, i, j: (bi, j, i))
        o_shape = (b, n, t)
    else:
        w_spec = pl.BlockSpec((k, tn), lambda bi, i, j: (0, j))
        o_spec = pl.BlockSpec((None, tm, tn), lambda bi, i, j: (bi, i, j))
        o_shape = (b, t, n)
    return pl.pallas_call(
        functools.partial(_norm_mm_kernel, transposed=transposed),
        grid=(b, t // tm, n // tn),
        in_specs=[pl.BlockSpec((None, tm, k), lambda bi, i, j: (bi, i, 0)),
                  pl.BlockSpec((1, k), lambda bi, i, j: (0, 0)),
                  w_spec],
        out_specs=o_spec,
        out_shape=jax.ShapeDtypeStruct(o_shape, out_dtype),
        scratch_shapes=[pltpu.VMEM((tm, k), BF16)],
        compiler_params=_params("parallel", "parallel", "arbitrary"),
        name="norm_matmul_t" if transposed else "norm_matmul",
    )(x, gain.reshape(1, k), w)


def _ssd_chunk_kernel(xbc_ref, z_ref, dt_ref, dtt_ref, h0_ref, c0_ref, cw_ref, cb_ref, dtb_ref, dtbt_ref,
                      alog_ref, alogt_ref, dfull_ref, nw_ref, exp_ref, y_ref, h_ref, xp_ref):
    q = SSD_CHUNK
    c = pl.program_id(1)

    @pl.when(c == 0)
    def _():
        h_ref[...] = h0_ref[...]
        xp_ref[5:8, :] = c0_ref[...]

    xp_ref[8:8 + q, :] = xbc_ref[...]
    conv = cb_ref[...]
    for j in range(SSD_CONV):
        conv = conv + cw_ref[j:j + 1, :] * xp_ref[5 + j:5 + j + q, :]
    xp_ref[5:8, :] = xp_ref[q + 5:q + 8, :]
    u = conv * _sigmoid(conv)
    xs = u[:, :SSD_INNER]
    bm = u[:, SSD_INNER:SSD_INNER + SSD_GN].astype(BF16)
    cm = u[:, SSD_INNER + SSD_GN:].astype(BF16)

    dt = _softplus(dt_ref[...] + dtb_ref[...])
    dtt = _softplus(dtt_ref[...] + dtbt_ref[...])
    dta = dt * (-jnp.exp(alog_ref[...]))
    dtat = dtt * (-jnp.exp(alogt_ref[...]))
    row = lax.broadcasted_iota(jnp.int32, (q, q), 0)
    col = lax.broadcasted_iota(jnp.int32, (q, q), 1)
    tril = row >= col
    acum = jnp.dot(tril.astype(F32), dta, precision=HI, preferred_element_type=F32)
    acumt = jnp.dot(dtat, (row <= col).astype(F32), precision=HI, preferred_element_type=F32)
    expand = exp_ref[...]
    dt_full = jnp.dot(dt, expand, precision=HI, preferred_element_type=F32)
    ea_full = jnp.dot(jnp.exp(acum), expand, precision=HI, preferred_element_type=F32)
    te_full = jnp.dot(jnp.exp(acum[q - 1:q, :] - acum), expand, precision=HI, preferred_element_type=F32)
    xdt = xs * dt_full
    xdt_b = xdt.astype(BF16)
    xw_b = (xdt * te_full).astype(BF16)

    r = SSD_HEADS // SSD_GROUPS
    gw = r * SSD_HEAD_DIM
    y_diag, y_off = [], []
    for g in range(SSD_GROUPS):
        cm_g = cm[:, g * SSD_STATE:(g + 1) * SSD_STATE]
        bm_g = bm[:, g * SSD_STATE:(g + 1) * SSD_STATE]
        cb = _dot_nt(cm_g, bm_g)
        h_g = h_ref[g * r:(g + 1) * r].reshape(gw, SSD_STATE)
        y_off.append(_dot_nt(cm_g, h_g.astype(BF16)))
        st = _dot_tn(xw_b[:, g * gw:(g + 1) * gw], bm_g)
        for hh in range(r):
            hd = g * r + hh
            seg = acum[:, hd:hd + 1] - acumt[hd:hd + 1, :]
            decay = jnp.exp(jnp.where(tril, seg, -jnp.inf))
            m = (cb * decay).astype(BF16)
            y_diag.append(jnp.dot(m, xdt_b[:, hd * SSD_HEAD_DIM:(hd + 1) * SSD_HEAD_DIM],
                                  preferred_element_type=F32))
            dec = jnp.exp(acumt[hd:hd + 1, q - 1:q])
            h_ref[hd] = dec * h_ref[hd] + st[hh * SSD_HEAD_DIM:(hh + 1) * SSD_HEAD_DIM, :]
    y = (jnp.concatenate(y_diag, axis=1) + jnp.concatenate(y_off, axis=1) * ea_full
         + dfull_ref[...] * xs)
    zz = z_ref[...]
    y = y * (zz * _sigmoid(zz))
    y_ref[...] = _rms(y, nw_ref[...]).astype(y_ref.dtype)


def ssd_prompt(xbc, z, dt_raw, h0, conv0, conv_w, conv_b, dt_bias, a_log, d_skip, norm_w):
    b, t, _ = xbc.shape
    q = SSD_CHUNK
    nc = t // q
    hds = SSD_HEADS
    expand = (jnp.arange(SSD_INNER)[None, :] // SSD_HEAD_DIM == jnp.arange(hds)[:, None]).astype(F32)
    d_full = jnp.repeat(d_skip, SSD_HEAD_DIM).reshape(1, SSD_INNER)
    dtt = jnp.swapaxes(dt_raw, 1, 2)
    full = lambda shape: pl.BlockSpec(shape, lambda bi, ci: (0,) * len(shape))
    y, h = pl.pallas_call(
        _ssd_chunk_kernel,
        grid=(b, nc),
        in_specs=[pl.BlockSpec((None, q, SSD_CONV_DIM), lambda bi, ci: (bi, ci, 0)),
                  pl.BlockSpec((None, q, SSD_INNER), lambda bi, ci: (bi, ci, 0)),
                  pl.BlockSpec((None, q, hds), lambda bi, ci: (bi, ci, 0)),
                  pl.BlockSpec((None, hds, q), lambda bi, ci: (bi, 0, ci)),
                  pl.BlockSpec((None, hds, SSD_HEAD_DIM, SSD_STATE), lambda bi, ci: (bi, 0, 0, 0)),
                  pl.BlockSpec((None, SSD_CONV - 1, SSD_CONV_DIM), lambda bi, ci: (bi, 0, 0)),
                  full((SSD_CONV, SSD_CONV_DIM)), full((1, SSD_CONV_DIM)),
                  full((1, hds)), full((hds, 1)), full((1, hds)), full((hds, 1)),
                  full((1, SSD_INNER)), full((1, SSD_INNER)), full((hds, SSD_INNER))],
        out_specs=[pl.BlockSpec((None, q, SSD_INNER), lambda bi, ci: (bi, ci, 0)),
                   pl.BlockSpec((None, hds, SSD_HEAD_DIM, SSD_STATE), lambda bi, ci: (bi, 0, 0, 0))],
        out_shape=[jax.ShapeDtypeStruct((b, t, SSD_INNER), BF16),
                   jax.ShapeDtypeStruct((b, hds, SSD_HEAD_DIM, SSD_STATE), F32)],
        scratch_shapes=[pltpu.VMEM((q + 8, SSD_CONV_DIM), F32)],
        compiler_params=_params("parallel", "arbitrary"),
        name="ssd_chunk_scan",
    )(xbc, z, dt_raw, dtt, h0, conv0, conv_w, conv_b.reshape(1, -1), dt_bias.reshape(1, hds),
      dt_bias.reshape(hds, 1), a_log.reshape(1, hds), a_log.reshape(hds, 1), d_full, norm_w.reshape(1, -1), expand)
    return y, h


def _sb_tile(q, kt, vt, carry, acc, upper, mask):
    z = jnp.dot(q, kt, preferred_element_type=F32)
    sp = jnp.maximum(z, 0.0) + jnp.log(1.0 + jnp.exp(-jnp.abs(z)))
    lk = -sp if mask is None else jnp.where(mask, -sp, 0.0)
    later = _split_dot(lk, upper)
    a = jnp.exp(z - sp + later + carry)
    if mask is not None:
        a = jnp.where(mask, a, 0.0)
    acc = acc + _dot_nt(a.astype(BF16), vt)
    return carry + later[:, 0:1] + lk[:, 0:1], acc


def _sb_prompt_kernel(q_ref, kt_ref, vt_ref, o_ref, *, tq, tk):
    i = pl.program_id(2)
    q = (q_ref[...] * (HEAD_DIM ** -0.5)).astype(BF16)
    row = lax.broadcasted_iota(jnp.int32, (tk, tk), 0)
    col = lax.broadcasted_iota(jnp.int32, (tk, tk), 1)
    upper = (row > col).astype(BF16)
    qpos = i * tq + lax.broadcasted_iota(jnp.int32, (tq, 1), 0)
    kcol = lax.broadcasted_iota(jnp.int32, (1, tk), 1)
    band = tq // tk

    def step(masked):
        def body(n, st):
            j = (i + 1) * band - 1 - n
            off = pl.multiple_of(j * tk, tk)
            mask = (off + kcol < qpos) if masked else None
            return _sb_tile(q, kt_ref[:, pl.ds(off, tk)].astype(BF16), vt_ref[:, pl.ds(off, tk)].astype(BF16),
                            *st, upper, mask)
        return body

    st = (jnp.zeros((tq, 1), F32), jnp.zeros((tq, HEAD_DIM), F32))
    st = lax.fori_loop(0, band, step(True), st)
    _, acc = lax.fori_loop(band, (i + 1) * band, step(False), st)
    o_ref[...] = acc.astype(o_ref.dtype)


def sb_prompt(q, kvt, tq=512, tk=128):
    b, t, _ = q.shape
    h = SB_HEADS
    tq = min(tq, t)
    assert t % tq == 0 and tq % tk == 0
    qh = jnp.swapaxes(q.reshape(b, t, h, HEAD_DIM), 1, 2)
    out = pl.pallas_call(
        functools.partial(_sb_prompt_kernel, tq=tq, tk=tk),
        grid=(b, h, t // tq),
        in_specs=[pl.BlockSpec((None, None, tq, HEAD_DIM), lambda bi, hi, i: (bi, hi, i, 0)),
                  pl.BlockSpec((None, HEAD_DIM, t), lambda bi, hi, i: (bi, hi, 0)),
                  pl.BlockSpec((None, HEAD_DIM, t), lambda bi, hi, i: (bi, h + hi, 0))],
        out_specs=pl.BlockSpec((None, None, tq, HEAD_DIM), lambda bi, hi, i: (bi, hi, i, 0)),
        out_shape=jax.ShapeDtypeStruct((b, h, t, HEAD_DIM), BF16),
        compiler_params=_params("parallel", "parallel", "arbitrary"),
        name="sb_prompt",
    )(qh, kvt, kvt)
    return jnp.swapaxes(out, 1, 2).reshape(b, t, h * HEAD_DIM)


def _merge_kernel(x_ref, ssd_ref, sb_ref, nsa_ref, gl_ref, wssd_ref, wsb_ref, wnsa_ref, wo_ref, nw_ref, o_ref):
    d = D_MODEL
    gl = gl_ref[...]
    merged = (_sigmoid(gl[:, :d]) * jnp.dot(ssd_ref[...], wssd_ref[...], preferred_element_type=F32)
              + _sigmoid(gl[:, d:2 * d]) * jnp.dot(sb_ref[...], wsb_ref[...], preferred_element_type=F32)
              + _sigmoid(gl[:, 2 * d:]) * jnp.dot(nsa_ref[...], wnsa_ref[...], preferred_element_type=F32))
    y = jnp.dot(merged.astype(BF16), wo_ref[...], preferred_element_type=F32)
    o_ref[...] = x_ref[...] + _rms(y, nw_ref[...])


def merge_branches(x, ssd_y, sb_o, nsa_o, gate_logits, w_ssd_out, w_sb_out, w_nsa_out, w_o, norm_w):
    m, d = x.shape
    tm = _pick(m, (256, 128, 32))
    rows = lambda n: pl.BlockSpec((tm, n), lambda i: (i, 0))
    full = lambda a: pl.BlockSpec(a.shape, lambda i: (0, 0))
    nw = norm_w.reshape(1, d)
    return pl.pallas_call(
        _merge_kernel,
        grid=(m // tm,),
        in_specs=[rows(d), rows(ssd_y.shape[1]), rows(sb_o.shape[1]), rows(nsa_o.shape[1]), rows(N_BRANCH * d),
                  full(w_ssd_out), full(w_sb_out), full(w_nsa_out), full(w_o), full(nw)],
        out_specs=rows(d),
        out_shape=jax.ShapeDtypeStruct((m, d), F32),
        compiler_params=_params("parallel"),
        name="merge_branches",
    )(x, ssd_y, sb_o, nsa_o, gate_logits, w_ssd_out, w_sb_out, w_nsa_out, w_o, nw)


def _ffn_up_kernel(x_ref, g_ref, wg_ref, wu_ref, o_ref, h_ref):
    @pl.when(pl.program_id(1) == 0)
    def _():
        h_ref[...] = _rms(x_ref[...], g_ref[...]).astype(BF16)

    h = h_ref[...]
    a = jnp.dot(h, wg_ref[...], preferred_element_type=F32)
    u = jnp.dot(h, wu_ref[...], preferred_element_type=F32)
    o_ref[...] = (a * _sigmoid(a) * u).astype(o_ref.dtype)


def _ffn_down_kernel(a_ref, x_ref, wd_ref, nw_ref, o_ref):
    f = jnp.dot(a_ref[...], wd_ref[...], preferred_element_type=F32)
    o_ref[...] = x_ref[...] + _rms(f, nw_ref[...])


def ffn(x, pre_w, post_w, w_gate, w_up, w_down):
    m, d = x.shape
    f = w_gate.shape[1]
    tm = _pick(m, (512, 256, 128, 32))
    tn = _pick(f, (256, 128))
    act = pl.pallas_call(
        _ffn_up_kernel,
        grid=(m // tm, f // tn),
        in_specs=[pl.BlockSpec((tm, d), lambda i, j: (i, 0)),
                  pl.BlockSpec((1, d), lambda i, j: (0, 0)),
                  pl.BlockSpec((d, tn), lambda i, j: (0, j)),
                  pl.BlockSpec((d, tn), lambda i, j: (0, j))],
        out_specs=pl.BlockSpec((tm, tn), lambda i, j: (i, j)),
        out_shape=jax.ShapeDtypeStruct((m, f), BF16),
        scratch_shapes=[pltpu.VMEM((tm, d), BF16)],
        compiler_params=_params("parallel", "arbitrary"),
        name="ffn_up",
    )(x, pre_w.reshape(1, d), w_gate, w_up)
    tm2 = _pick(m, (256, 128, 32))
    return pl.pallas_call(
        _ffn_down_kernel,
        grid=(m // tm2,),
        in_specs=[pl.BlockSpec((tm2, f), lambda i: (i, 0)),
                  pl.BlockSpec((tm2, d), lambda i: (i, 0)),
                  pl.BlockSpec((f, d), lambda i: (0, 0)),
                  pl.BlockSpec((1, d), lambda i: (0, 0))],
        out_specs=pl.BlockSpec((tm2, d), lambda i: (i, 0)),
        out_shape=jax.ShapeDtypeStruct((m, d), F32),
        compiler_params=_params("parallel"),
        name="ffn_down",
    )(act, x, w_down, post_w.reshape(1, d))


def _nsa_compress_kernel(r_ref, pos_ref, w1_ref, w2_ref, o_ref, sh_ref):
    nr = r_ref.shape[0]
    half = CMP_STRIDE * HEAD_DIM
    r = r_ref[...]
    top = jnp.dot((r + pos_ref[0:1, :]).astype(BF16), w1_ref[:half, :], preferred_element_type=F32)
    bot = jnp.dot((r + pos_ref[1:2, :]).astype(BF16), w1_ref[half:, :], preferred_element_type=F32)
    sh_ref[0:nr, :] = bot
    sh_ref[nr:nr + 8, :] = jnp.zeros((8, CMP_HIDDEN), F32)
    pre = top + sh_ref[1:nr + 1, :]
    hid = pre * _sigmoid(pre)
    o_ref[...] = jnp.dot(hid.astype(BF16), w2_ref[...], preferred_element_type=F32).astype(o_ref.dtype)


def nsa_compress(rows16, cmp_pos, cmp_w1, cmp_w2):
    assert CMP_BLOCK == 2 * CMP_STRIDE
    b, _, g, nr, w = rows16.shape
    pos = cmp_pos.reshape(2, 2, w)
    return pl.pallas_call(
        _nsa_compress_kernel,
        grid=(b, 2, g),
        in_specs=[pl.BlockSpec((None, None, None, nr, w), lambda bi, ki, gi: (bi, ki, gi, 0, 0)),
                  pl.BlockSpec((None, 2, w), lambda bi, ki, gi: (ki, 0, 0)),
                  pl.BlockSpec((None, 2 * w, CMP_HIDDEN), lambda bi, ki, gi: (ki, 0, 0)),
                  pl.BlockSpec((None, CMP_HIDDEN, HEAD_DIM), lambda bi, ki, gi: (ki, 0, 0))],
        out_specs=pl.BlockSpec((None, None, None, nr, HEAD_DIM), lambda bi, ki, gi: (bi, ki, gi, 0, 0)),
        out_shape=jax.ShapeDtypeStruct((b, 2, g, nr, HEAD_DIM), BF16),
        scratch_shapes=[pltpu.VMEM((nr + 8, CMP_HIDDEN), F32)],
        compiler_params=_params("parallel", "parallel", "parallel"),
        name="nsa_compress",
    )(rows16, pos, cmp_w1.astype(BF16), cmp_w2.astype(BF16))


def _overlap_matrix(n_cmp_rows, n_cmp, n_blk):
    c = jnp.arange(n_cmp_rows)[:, None]
    n = jnp.arange(n_blk)[None, :]
    c_start, c_end = c * CMP_STRIDE, c * CMP_STRIDE + CMP_BLOCK - 1
    return ((c_start < (n + 1) * SEL_BLOCK) & (c_end >= n * SEL_BLOCK) & (c < n_cmp)).astype(BF16)


def _split_dot(x, w):
    hi = x.astype(BF16)
    lo = (x - hi.astype(F32)).astype(BF16)
    return jnp.dot(hi, w, preferred_element_type=F32) + jnp.dot(lo, w, preferred_element_type=F32)


def _top_blocks(imp, blk, n_top):
    n_blk = imp.shape[1]
    sel = jnp.zeros(imp.shape, F32)
    for _ in range(n_top):
        m = jnp.max(imp, axis=-1, keepdims=True)
        idx = jnp.min(jnp.where(imp == m, blk, float(n_blk)), axis=-1, keepdims=True)
        hit = blk == idx
        sel = jnp.where(hit, 1.0, sel)
        imp = jnp.where(hit, -jnp.inf, imp)
    return sel


def _flash_step(q, kt, vt, mask, m, acc):
    s = jnp.where(mask, jnp.dot(q, kt, preferred_element_type=F32), NEG)
    m_new = jnp.maximum(m, jnp.max(s, axis=-1, keepdims=True))
    p = jnp.exp(s - m_new).astype(BF16)
    vt_ext = jnp.concatenate([vt, jnp.ones(vt.shape, BF16)], axis=0)
    return m_new, acc * jnp.exp(m - m_new) + _dot_nt(p, vt_ext)


def _nsa_prompt_kernel(q_ref, gt_ref, kc_ref, vc_ref, ov_ref, ks_ref, vs_ref, kw_ref, vw_ref, o_ref, *, tq, n_cmp):
    i = pl.program_id(2)
    rep, d = NSA_REP, HEAD_DIM
    t0 = i * tq
    qv = q_ref[...]
    q = jnp.concatenate([qv[:, r * d:(r + 1) * d] for r in range(rep)], axis=0)
    q = (q * (d ** -0.5)).astype(BF16)
    qpos = t0 + lax.broadcasted_iota(jnp.int32, (tq, 1), 0)
    tile_rows = lambda a: jnp.concatenate([a] * rep, axis=0)

    n_rows = kc_ref.shape[0]
    cidx = lax.broadcasted_iota(jnp.int32, (1, n_rows), 1)
    vis_c = tile_rows(((cidx * CMP_STRIDE + (CMP_BLOCK - 1) <= qpos) & (cidx < n_cmp)).astype(F32)) > 0.5
    s_c = jnp.where(vis_c, _dot_nt(q, kc_ref[...]), NEG)
    e_c = jnp.where(vis_c, jnp.exp(s_c - jnp.max(s_c, axis=-1, keepdims=True)), 0.0)
    p_c = e_c / jnp.maximum(jnp.sum(e_c, axis=-1, keepdims=True), 1e-30)
    o_c = jnp.dot(p_c.astype(BF16), vc_ref[...], preferred_element_type=F32)

    p_sum = p_c[0:tq]
    for r in range(1, rep):
        p_sum = p_sum + p_c[r * tq:(r + 1) * tq]
    n_blk = ov_ref.shape[1]
    blk_i = lax.broadcasted_iota(jnp.int32, (1, n_blk), 1)
    cur = lax.shift_right_logical(qpos, int(math.log2(SEL_BLOCK)))
    valid = blk_i <= cur
    forced = valid & ((blk_i == 0) | (blk_i > cur - SEL_LOCAL))
    imp = jnp.where(forced, jnp.inf, jnp.where(valid, _split_dot(p_sum, ov_ref[...]), -jnp.inf))
    sel = _top_blocks(imp, blk_i.astype(F32), min(SEL_TOP, n_blk)).astype(BF16)

    col = lax.broadcasted_iota(jnp.int32, (1, tq), 1)
    exp_row = lax.broadcasted_iota(jnp.int32, (n_blk, tq), 0)
    exp_col = lax.shift_right_logical(lax.broadcasted_iota(jnp.int32, (n_blk, tq), 1), int(math.log2(SEL_BLOCK)))
    m0 = jnp.full((rep * tq, 1), NEG, F32)
    acc0 = jnp.zeros((rep * tq, 2 * d), F32)

    def sel_step(n, st):
        j = i - n
        off = pl.multiple_of(j * tq, tq)
        expand = (exp_row == j * (tq // SEL_BLOCK) + exp_col).astype(BF16)
        chosen = jnp.dot(sel, expand, preferred_element_type=F32)
        mask = tile_rows(jnp.where(off + col <= qpos, chosen, 0.0)) > 0.5
        return _flash_step(q, ks_ref[:, pl.ds(off, tq)].astype(BF16), vs_ref[:, pl.ds(off, tq)].astype(BF16),
                           mask, *st)

    _, acc_s = lax.fori_loop(0, i + 1, sel_step, (m0, acc0))

    def win_step(n, st):
        j = i - n
        off = pl.multiple_of(j * tq, tq)
        kpos = off + col
        mask = tile_rows(((kpos <= qpos) & (kpos > qpos - WINDOW)).astype(F32)) > 0.5
        return _flash_step(q, kw_ref[:, pl.ds(off, tq)].astype(BF16), vw_ref[:, pl.ds(off, tq)].astype(BF16),
                           mask, *st)

    _, acc_w = lax.fori_loop(0, jnp.minimum(i, WINDOW // tq) + 1, win_step, (m0, acc0))

    o_s = acc_s[:, :d] / acc_s[:, d:]
    o_w = acc_w[:, :d] / acc_w[:, d:]
    gate = _sigmoid(gt_ref[...])
    outs = []
    for r in range(rep):
        sl = slice(r * tq, (r + 1) * tq)
        outs.append(gate[:, 3 * r:3 * r + 1] * o_c[sl] + gate[:, 3 * r + 1:3 * r + 2] * o_s[sl]
                    + gate[:, 3 * r + 2:3 * r + 3] * o_w[sl])
    o_ref[...] = jnp.concatenate(outs, axis=1).astype(o_ref.dtype)


def nsa_prompt(nsa_q, nsa_g, nsa_kvt, cmp_pos, cmp_w1, cmp_w2, tq=128):
    b, t, _ = nsa_q.shape
    g, d, rep = NSA_KV_GROUPS, HEAD_DIM, NSA_REP
    assert t % tq == 0 and tq % SEL_BLOCK == 0 and WINDOW % tq == 0 and t % CMP_STRIDE == 0
    n_cmp = (t - CMP_BLOCK) // CMP_STRIDE + 1
    nr = t // CMP_STRIDE
    n_blk = t // SEL_BLOCK
    rows16 = jnp.swapaxes(nsa_kvt[:, :2 * g * d].reshape(b, 2, g, d, t), 3, 4).reshape(b, 2, g, nr, CMP_STRIDE * d)
    kcvc = nsa_compress(rows16, cmp_pos, cmp_w1, cmp_w2)
    overlap = _overlap_matrix(nr, n_cmp, n_blk)
    gates = jnp.swapaxes(nsa_g.reshape(b, t, g, 3 * rep), 1, 2)
    gates = jnp.pad(gates, ((0, 0), (0, 0), (0, 0), (0, V7X_LANES - 3 * rep)))
    stream = lambda kind: pl.BlockSpec((None, d, t), lambda bi, gi, i: (bi, kind * g + gi, 0))
    return pl.pallas_call(
        functools.partial(_nsa_prompt_kernel, tq=tq, n_cmp=n_cmp),
        grid=(b, g, t // tq),
        in_specs=[pl.BlockSpec((None, tq, rep * d), lambda bi, gi, i: (bi, i, gi)),
                  pl.BlockSpec((None, None, tq, V7X_LANES), lambda bi, gi, i: (bi, gi, i, 0)),
                  pl.BlockSpec((None, None, None, nr, d), lambda bi, gi, i: (bi, 0, gi, 0, 0)),
                  pl.BlockSpec((None, None, None, nr, d), lambda bi, gi, i: (bi, 1, gi, 0, 0)),
                  pl.BlockSpec((nr, n_blk), lambda bi, gi, i: (0, 0)),
                  stream(2), stream(3), stream(4), stream(5)],
        out_specs=pl.BlockSpec((None, tq, rep * d), lambda bi, gi, i: (bi, i, gi)),
        out_shape=jax.ShapeDtypeStruct((b, t, g * rep * d), BF16),
        compiler_params=_params("parallel", "parallel", "arbitrary"),
        name="nsa_prompt",
    )(nsa_q, gates, kcvc, kcvc, overlap, nsa_kvt, nsa_kvt, nsa_kvt, nsa_kvt)


def _ssd_step_pre_kernel(x_ref, buf_ref, cw_ref, cb_ref, dt_ref, dtb_ref, alog_ref, exp_ref,
                         xs_ref, xdt_ref, bm_ref, cm_ref, dec_ref):
    conv = cb_ref[...] + cw_ref[SSD_CONV - 1:SSD_CONV, :] * x_ref[...]
    for j in range(SSD_CONV - 1):
        conv = conv + cw_ref[j:j + 1, :] * buf_ref[j]
    u = conv * _sigmoid(conv)
    xs = u[:, :SSD_INNER]
    dt = _softplus(dt_ref[...] + dtb_ref[...])
    xs_ref[...] = xs
    xdt_ref[...] = xs * jnp.dot(dt, exp_ref[...], precision=HI, preferred_element_type=F32)
    bm_ref[...] = u[:, SSD_INNER:SSD_INNER + SSD_GN]
    cm_ref[...] = u[:, SSD_INNER + SSD_GN:]
    dec_ref[...] = jnp.exp(dt * (-jnp.exp(alog_ref[...])))


def _ssd_step_state_kernel(h0_ref, xdt_ref, dec_ref, bm_ref, cm_ref, h_ref, y_ref):
    r = SSD_HEADS // SSD_GROUPS
    for hd in range(SSD_HEADS):
        g = hd // r
        hn = dec_ref[hd] * h0_ref[hd] + xdt_ref[hd] * bm_ref[g]
        h_ref[hd] = hn
        y_ref[hd] = jnp.sum(hn * cm_ref[g], axis=-1, keepdims=True)


def _ssd_step_post_kernel(y_ref, xs_ref, z_ref, dfull_ref, nw_ref, o_ref):
    zz = z_ref[...]
    y = (y_ref[...] + dfull_ref[...] * xs_ref[...]) * (zz * _sigmoid(zz))
    o_ref[...] = _rms(y, nw_ref[...]).astype(o_ref.dtype)


def ssd_step(xbc, z, dt_raw, h0, conv_buf, conv_w, conv_b, dt_bias, a_log, d_skip, norm_w):
    b = xbc.shape[0]
    hds, p, n = SSD_HEADS, SSD_HEAD_DIM, SSD_STATE
    expand = (jnp.arange(SSD_INNER)[None, :] // p == jnp.arange(hds)[:, None]).astype(F32)
    d_full = jnp.repeat(d_skip, p).reshape(1, SSD_INNER)
    sds = lambda shape: jax.ShapeDtypeStruct(shape, F32)
    xs, xdt, bm, cm, dec = pl.pallas_call(
        _ssd_step_pre_kernel,
        out_shape=[sds((b, SSD_INNER)), sds((b, SSD_INNER)), sds((b, SSD_GN)), sds((b, SSD_GN)), sds((b, hds))],
        name="ssd_step_pre",
    )(xbc, jnp.swapaxes(conv_buf, 0, 1), conv_w, conv_b.reshape(1, -1), dt_raw, dt_bias.reshape(1, hds),
      a_log.reshape(1, hds), expand)
    per_b = lambda *dims: pl.BlockSpec((None,) + dims, lambda bi: (bi,) + (0,) * len(dims))
    h_new, y_col = pl.pallas_call(
        _ssd_step_state_kernel,
        grid=(b,),
        in_specs=[per_b(hds, p, n), per_b(hds, p, 1), per_b(hds, 1, 1), per_b(SSD_GROUPS, 1, n),
                  per_b(SSD_GROUPS, 1, n)],
        out_specs=[per_b(hds, p, n), per_b(hds, p, 1)],
        out_shape=[sds((b, hds, p, n)), sds((b, hds, p, 1))],
        compiler_params=_params("parallel"),
        name="ssd_step_state",
    )(h0, xdt.reshape(b, hds, p, 1), dec.reshape(b, hds, 1, 1), bm.reshape(b, SSD_GROUPS, 1, n),
      cm.reshape(b, SSD_GROUPS, 1, n))
    y = pl.pallas_call(
        _ssd_step_post_kernel,
        out_shape=jax.ShapeDtypeStruct((b, SSD_INNER), BF16),
        name="ssd_step_post",
    )(y_col.reshape(b, SSD_INNER), xs, z, d_full, norm_w.reshape(1, -1))
    return y, h_new


def _sb_decode_kernel(pt_ref, q_ref, page_ref, o_ref, carry_ref, acc_ref):
    p = pl.program_id(1)
    tk = page_ref.shape[-1]

    @pl.when(p == 0)
    def _():
        carry_ref[...] = jnp.zeros(carry_ref.shape, F32)
        acc_ref[...] = jnp.zeros(acc_ref.shape, F32)

    scale = HEAD_DIM ** -0.5
    z = jnp.concatenate([jnp.sum(page_ref[0, h] * (q_ref[h] * scale), axis=0, keepdims=True)
                         for h in range(SB_HEADS)], axis=0)
    row = lax.broadcasted_iota(jnp.int32, (tk, tk), 0)
    col = lax.broadcasted_iota(jnp.int32, (tk, tk), 1)
    sp = _softplus(z)
    later = _split_dot(-sp, (row > col).astype(BF16)) + carry_ref[...]
    a = jnp.exp(z - sp + later)
    carry_ref[...] = later[:, 0:1] - sp[:, 0:1]
    for h in range(SB_HEADS):
        acc_ref[h] += page_ref[1, h] * a[h:h + 1, :]

    @pl.when(p == pl.num_programs(1) - 1)
    def _():
        for h in range(SB_HEADS):
            o_ref[h] = jnp.sum(acc_ref[h], axis=-1, keepdims=True)


def sb_decode(q, pool_t, layer, page_table):
    b = q.shape[0]
    h, d = SB_HEADS, HEAD_DIM
    n_pages = page_table.shape[1]
    page = pool_t.shape[-1]
    out = pl.pallas_call(
        _sb_decode_kernel,
        grid_spec=pltpu.PrefetchScalarGridSpec(
            num_scalar_prefetch=1,
            grid=(b, n_pages),
            in_specs=[pl.BlockSpec((None, h, d, 1), lambda bi, p, pt: (bi, 0, 0, 0)),
                      pl.BlockSpec((None, None, 2, h, d, page),
                                   lambda bi, p, pt: (layer, pt[bi, n_pages - 1 - p], 0, 0, 0, 0))],
            out_specs=pl.BlockSpec((None, h, d, 1), lambda bi, p, pt: (bi, 0, 0, 0)),
            scratch_shapes=[pltpu.VMEM((h, 1), F32), pltpu.VMEM((h, d, page), F32)]),
        out_shape=jax.ShapeDtypeStruct((b, h, d, 1), F32),
        compiler_params=_params("parallel", "arbitrary"),
        name="sb_decode",
    )(page_table, q.reshape(b, h, d, 1), pool_t)
    return out.reshape(b, h * d)


def _nsa_gather_kernel(pt_ref, page_ref, o_ref, x_ref):
    page = page_ref.shape[-1]
    d = HEAD_DIM
    n_out = page // CMP_STRIDE
    for kind in range(2):
        for g in range(NSA_KV_GROUPS):
            xt = page_ref[kind, g]
            sq = jnp.concatenate([xt, jnp.zeros((page - d, page), F32)], axis=0)
            x_ref[...] = sq.T
            pieces = [x_ref[pl.ds(s, n_out, stride=CMP_STRIDE), 0:d] for s in range(CMP_STRIDE)]
            o_ref[kind, g] = jnp.concatenate(pieces, axis=1)


def nsa_gather_rows16(pool_t, layer, page_table):
    b, n_pages = page_table.shape
    g, d = NSA_KV_GROUPS, HEAD_DIM
    page = pool_t.shape[-1]
    assert page % CMP_STRIDE == 0 and page >= d
    n_out = page // CMP_STRIDE
    return pl.pallas_call(
        _nsa_gather_kernel,
        grid_spec=pltpu.PrefetchScalarGridSpec(
            num_scalar_prefetch=1,
            grid=(b, n_pages),
            in_specs=[pl.BlockSpec((None, None, 2, g, d, page), lambda bi, p, pt: (layer, pt[bi, p], 0, 0, 0, 0))],
            out_specs=pl.BlockSpec((None, 2, g, n_out, CMP_STRIDE * d), lambda bi, p, pt: (bi, 0, 0, p, 0)),
            scratch_shapes=[pltpu.VMEM((page, page), F32)]),
        out_shape=jax.ShapeDtypeStruct((b, 2, g, n_pages * n_out, CMP_STRIDE * d), F32),
        compiler_params=_params("parallel", "arbitrary"),
        name="nsa_gather_rows16",
    )(page_table, pool_t)


def _nsa_decode_select_kernel(q_ref, kc_ref, vc_ref, ov_ref, oc_ref, sel_ref, *, n_cmp, q_pos):
    rep, d = NSA_REP, HEAD_DIM
    n_rows = kc_ref.shape[1]
    n_blk = ov_ref.shape[1]
    q = (q_ref[...] * (d ** -0.5)).astype(BF16)
    cidx = lax.broadcasted_iota(jnp.int32, (1, n_rows), 1)
    vis = (cidx * CMP_STRIDE + (CMP_BLOCK - 1) <= q_pos) & (cidx < n_cmp)
    o_c, imp = [], []
    for g in range(NSA_KV_GROUPS):
        s = jnp.where(vis, _dot_nt(q[g * rep:(g + 1) * rep], kc_ref[g]), NEG)
        e = jnp.where(vis, jnp.exp(s - jnp.max(s, axis=-1, keepdims=True)), 0.0)
        p = e / jnp.maximum(jnp.sum(e, axis=-1, keepdims=True), 1e-30)
        o_c.append(jnp.dot(p.astype(BF16), vc_ref[g], preferred_element_type=F32))
        imp.append(_split_dot(jnp.sum(p, axis=0, keepdims=True), ov_ref[...]))
    oc_ref[...] = jnp.concatenate(o_c, axis=0)
    imp = jnp.concatenate(imp, axis=0)
    blk_i = lax.broadcasted_iota(jnp.int32, (1, n_blk), 1)
    forced = (blk_i == 0) | (blk_i > n_blk - SEL_LOCAL)
    sel = _top_blocks(jnp.where(forced, jnp.inf, imp), blk_i.astype(F32), min(SEL_TOP - 1, n_blk))
    sel_ref[...] = jnp.concatenate([sel, jnp.zeros((sel_ref.shape[0] - NSA_KV_GROUPS, n_blk), F32)], axis=0)


def _nsa_decode_attend_kernel(pt_ref, q_ref, sel_ref, page_ref, new_ref, win_ref, oc_ref, gt_ref, o_ref,
                              m_ref, acc_ref, *, win_skip):
    p = pl.program_id(1)
    n_pages = pl.num_programs(1)
    rep, d, grp = NSA_REP, HEAD_DIM, NSA_KV_GROUPS
    tk = page_ref.shape[-1]
    scale = d ** -0.5
    qf = q_ref[...] * scale
    q = qf.astype(BF16)
    new = new_ref[...]
    new_row = lambda kind, g: new[kind * grp + g:kind * grp + g + 1, :]
    per_head = lambda f: jnp.concatenate([f(g) for g in range(grp)], axis=0)

    @pl.when(p == 0)
    def _():
        m_ref[...] = per_head(lambda g: jnp.sum(qf[g * rep:(g + 1) * rep] * new_row(2, g), axis=-1, keepdims=True))
        acc_ref[...] = per_head(lambda g: jnp.concatenate(
            [jnp.broadcast_to(new_row(3, g), (rep, d)), jnp.ones((rep, d), F32)], axis=1))

    pg = n_pages - 1 - p
    n_blk = sel_ref.shape[1]
    exp_row = lax.broadcasted_iota(jnp.int32, (n_blk, tk), 0)
    exp_col = lax.shift_right_logical(lax.broadcasted_iota(jnp.int32, (n_blk, tk), 1), int(math.log2(SEL_BLOCK)))
    expand = (exp_row == pg * (tk // SEL_BLOCK) + exp_col).astype(BF16)
    chosen = jnp.dot(sel_ref[...].astype(BF16), expand, preferred_element_type=F32)
    s = per_head(lambda g: jnp.where(chosen[g:g + 1, :] > 0.5,
                                     jnp.dot(q[g * rep:(g + 1) * rep], page_ref[0, g].astype(BF16),
                                             preferred_element_type=F32), NEG))
    m_old = m_ref[...]
    m_new = jnp.maximum(m_old, jnp.max(s, axis=-1, keepdims=True))
    pr = jnp.exp(s - m_new).astype(BF16)
    pv = per_head(lambda g: _dot_nt(pr[g * rep:(g + 1) * rep],
                                    jnp.concatenate([page_ref[1, g].astype(BF16), jnp.ones((d, tk), BF16)], axis=0)))
    acc_ref[...] = acc_ref[...] * jnp.exp(m_old - m_new) + pv
    m_ref[...] = m_new

    @pl.when(p == n_pages - 1)
    def _():
        acc = acc_ref[...]
        o_s = acc[:, :d] / acc[:, d:]
        wlen = win_ref.shape[-1]
        vis = lax.broadcasted_iota(jnp.int32, (1, wlen), 1) >= win_skip

        def window(g):
            qg = q[g * rep:(g + 1) * rep]
            s_w = jnp.where(vis, jnp.dot(qg, win_ref[0, g].astype(BF16), preferred_element_type=F32), NEG)
            s_n = jnp.sum(qf[g * rep:(g + 1) * rep] * new_row(4, g), axis=-1, keepdims=True)
            mx = jnp.maximum(jnp.max(s_w, axis=-1, keepdims=True), s_n)
            e_w = jnp.where(vis, jnp.exp(s_w - mx), 0.0)
            e_n = jnp.exp(s_n - mx)
            num = _dot_nt(e_w.astype(BF16), win_ref[1, g].astype(BF16)) + e_n * new_row(5, g)
            return num / (jnp.sum(e_w, axis=-1, keepdims=True) + e_n)

        o_w = per_head(window)
        gate = _sigmoid(gt_ref[...])
        o_ref[...] = gate[:, 0:1] * oc_ref[...] + gate[:, 1:2] * o_s + gate[:, 2:3] * o_w


def nsa_decode(nsa_q, nsa_g, nsa_kv_new, pool_t, win_t, layer, page_table, cmp_pos, cmp_w1, cmp_w2):
    b = nsa_q.shape[0]
    g, d, rep, hds = NSA_KV_GROUPS, HEAD_DIM, NSA_REP, NSA_HEADS
    n_pages = page_table.shape[1]
    page = pool_t.shape[-1]
    past = n_pages * page
    wlen = win_t.shape[-1]
    assert past % SEL_BLOCK == 0 and past % CMP_STRIDE == 0 and page % SEL_BLOCK == 0 and wlen <= past
    n_cmp = (past + 1 - CMP_BLOCK) // CMP_STRIDE + 1
    nr = past // CMP_STRIDE
    n_blk = past // SEL_BLOCK
    rows16 = nsa_gather_rows16(pool_t, layer, page_table)
    kcvc = nsa_compress(rows16, cmp_pos, cmp_w1, cmp_w2)
    overlap = _overlap_matrix(nr, n_cmp, n_blk)
    q3 = nsa_q.reshape(b, hds, d)
    per_b = lambda *dims: pl.BlockSpec((None,) + dims, lambda bi: (bi,) + (0,) * len(dims))
    o_c, sel = pl.pallas_call(
        functools.partial(_nsa_decode_select_kernel, n_cmp=n_cmp, q_pos=past),
        grid=(b,),
        in_specs=[per_b(hds, d),
                  pl.BlockSpec((None, None, g, nr, d), lambda bi: (bi, 0, 0, 0, 0)),
                  pl.BlockSpec((None, None, g, nr, d), lambda bi: (bi, 1, 0, 0, 0)),
                  pl.BlockSpec((nr, n_blk), lambda bi: (0, 0))],
        out_specs=[per_b(hds, d), per_b(8, n_blk)],
        out_shape=[jax.ShapeDtypeStruct((b, hds, d), F32), jax.ShapeDtypeStruct((b, 8, n_blk), F32)],
        compiler_params=_params("parallel"),
        name="nsa_decode_select",
    )(q3, kcvc, kcvc, overlap)
    gates = jnp.pad(nsa_g.reshape(b, hds, 3), ((0, 0), (0, 0), (0, V7X_LANES - 3)))
    fixed = lambda *dims: pl.BlockSpec((None,) + dims, lambda bi, p, pt: (bi,) + (0,) * len(dims))
    out = pl.pallas_call(
        functools.partial(_nsa_decode_attend_kernel, win_skip=wlen - WINDOW + 1),
        grid_spec=pltpu.PrefetchScalarGridSpec(
            num_scalar_prefetch=1,
            grid=(b, n_pages),
            in_specs=[fixed(hds, d), fixed(8, n_blk),
                      pl.BlockSpec((None, None, 2, g, d, page),
                                   lambda bi, p, pt: (layer, pt[bi, n_pages - 1 - p], 1, 0, 0, 0)),
                      fixed(6 * g, d),
                      pl.BlockSpec((None, None, 2, g, d, wlen), lambda bi, p, pt: (layer, bi, 0, 0, 0, 0)),
                      fixed(hds, d), fixed(hds, V7X_LANES)],
            out_specs=fixed(hds, d),
            scratch_shapes=[pltpu.VMEM((hds, 1), F32), pltpu.VMEM((hds, 2 * d), F32)]),
        out_shape=jax.ShapeDtypeStruct((b, hds, d), F32),
        compiler_params=_params("parallel", "arbitrary"),
        name="nsa_decode_attend",
    )(page_table, q3, sel, pool_t, nsa_kv_new.reshape(b, 6 * g, d), win_t, o_c, gates)
    return out.reshape(b, hds * d)


def _x_rms_norm(x, w):
    xf = x.astype(jnp.float32)
    y = xf * lax.rsqrt(jnp.mean(xf * xf, axis=-1, keepdims=True) + RMS_EPS)
    return (y * w.astype(jnp.float32)).astype(x.dtype)


def _x_masked_softmax(s, mask):
    s = jnp.where(mask, s.astype(jnp.float32), -jnp.inf)
    m = jnp.max(s, axis=-1, keepdims=True)
    m = jnp.where(jnp.isfinite(m), m, 0.0)
    e = jnp.where(mask, jnp.exp(s - m), 0.0)
    return e / jnp.maximum(jnp.sum(e, axis=-1, keepdims=True), 1e-30)


def _x_causal_dwconv(x, buf, w, b):
    xp = jnp.concatenate([buf.astype(x.dtype), x], axis=1)
    y = lax.conv_general_dilated(xp, w[:, None, :].astype(x.dtype), window_strides=(1,), padding='VALID',
                                 dimension_numbers=('NWC', 'WIO', 'NWC'), feature_group_count=x.shape[-1])
    return y + b.astype(x.dtype), xp[:, xp.shape[1] - (SSD_CONV - 1):]


def _x_ssd_scan(x, dt, a, bm, cm, h0):
    bsz, t = x.shape[:2]
    q = min(SSD_CHUNK, t)
    nc = t // q
    r = SSD_HEADS // SSD_GROUPS
    xdt = (x * dt[..., None]).reshape(bsz, nc, q, SSD_GROUPS, r, SSD_HEAD_DIM)
    acum = jnp.cumsum((dt * a).reshape(bsz, nc, q, SSD_GROUPS, r), axis=2)
    bm = bm.reshape(bsz, nc, q, SSD_GROUPS, SSD_STATE)
    cm = cm.reshape(bsz, nc, q, SSD_GROUPS, SSD_STATE)
    at = jnp.moveaxis(acum, 2, -1)
    tril = jnp.tril(jnp.ones((q, q), dtype=bool))
    decay = jnp.exp(jnp.where(tril, at[..., :, None] - at[..., None, :], -jnp.inf))
    cb = jnp.einsum('bclgn,bcsgn->bcgls', cm, bm)
    y_diag = jnp.einsum('bcgls,bcgrls,bcsgrp->bclgrp', cb, decay, xdt)
    to_end = jnp.exp(acum[:, :, -1:] - acum)
    states = jnp.einsum('bclgn,bclgr,bclgrp->bcgrpn', bm, to_end, xdt)
    chunk_decay = jnp.exp(acum[:, :, -1])

    def step(h, inp):
        dec, st = inp
        return dec[..., None, None] * h + st, h

    h_init = h0.reshape(bsz, SSD_GROUPS, r, SSD_HEAD_DIM, SSD_STATE)
    h_fin, h_in = lax.scan(step, h_init, (jnp.moveaxis(chunk_decay, 1, 0), jnp.moveaxis(states, 1, 0)))
    h_in = jnp.moveaxis(h_in, 0, 1)
    y_off = jnp.einsum('bclgn,bcgrpn,bclgr->bclgrp', cm, h_in, jnp.exp(acum))
    y = (y_diag + y_off).reshape(bsz, nc * q, SSD_HEADS, SSD_HEAD_DIM)[:, :t]
    return y, h_fin.reshape(bsz, SSD_HEADS, SSD_HEAD_DIM, SSD_STATE)


def _x_ssd_branch(z, xbc, dt_raw, conv_buf, h0, lp):
    bsz, t = z.shape[:2]
    xbc, new_buf = _x_causal_dwconv(xbc, conv_buf, lp['conv_w'], lp['conv_b'])
    xbc = jax.nn.silu(xbc)
    gn = SSD_GROUPS * SSD_STATE
    xs = xbc[..., :SSD_INNER].reshape(bsz, t, SSD_HEADS, SSD_HEAD_DIM).astype(jnp.float32)
    bm = xbc[..., SSD_INNER:SSD_INNER + gn].reshape(bsz, t, SSD_GROUPS, SSD_STATE).astype(jnp.float32)
    cm = xbc[..., SSD_INNER + gn:].reshape(bsz, t, SSD_GROUPS, SSD_STATE).astype(jnp.float32)
    dt = jax.nn.softplus((dt_raw + lp['dt_bias']).astype(jnp.float32))
    a = -jnp.exp(lp['a_log'].astype(jnp.float32))
    y, h_fin = _x_ssd_scan(xs, dt, a, bm, cm, h0.astype(jnp.float32))
    y = y + lp['d_skip'].astype(jnp.float32)[:, None] * xs
    y = y.reshape(bsz, t, SSD_INNER).astype(z.dtype)
    y = _x_rms_norm(y * jax.nn.silu(z), lp['ssd_norm'])
    return y, new_buf, h_fin.astype(h0.dtype)


def _x_stick_breaking(q, k, v, q_pos, k_pos):
    z = jnp.einsum('bqhd,bkhd->bhqk', q, k).astype(jnp.float32) * (HEAD_DIM ** -0.5)
    mask = k_pos[None, :] < q_pos[:, None]
    log_keep = jnp.where(mask, jax.nn.log_sigmoid(-z), 0.0)
    later = lax.cumsum(log_keep, axis=3, reverse=True) - log_keep
    a = jnp.where(mask, jnp.exp(jax.nn.log_sigmoid(z) + later), 0.0)
    return jnp.einsum('bhqk,bkhd->bqhd', a.astype(v.dtype), v)


def _x_compress_blocks(rows, pos_emb, w1, w2):
    bsz, t = rows.shape[:2]
    nc = (t - CMP_BLOCK) // CMP_STRIDE + 1
    starts = jnp.arange(nc) * CMP_STRIDE
    idx = starts[:, None] + jnp.arange(CMP_BLOCK)[None, :]
    blocks = rows[:, idx] + pos_emb[None, None, :, None, :].astype(rows.dtype)
    blocks = jnp.moveaxis(blocks, 3, 2).reshape(bsz, nc, NSA_KV_GROUPS, CMP_BLOCK * HEAD_DIM)
    return jax.nn.silu(blocks @ w1) @ w2, starts + CMP_BLOCK - 1


def _x_nsa_attend(q, gates, q_pos, kc, vc, c_end, ks, vs, kw, vw, kw_pos):
    bsz, tq = q.shape[:2]
    r = NSA_HEADS // NSA_KV_GROUPS
    scale = HEAD_DIM ** -0.5
    qg = q.reshape(bsz, tq, NSA_KV_GROUPS, r, HEAD_DIM)
    s_c = jnp.einsum('bqgrd,bcgd->bqgrc', qg, kc).astype(jnp.float32) * scale
    p_c = _x_masked_softmax(s_c, (c_end[None, :] <= q_pos[:, None])[None, :, None, None, :])
    o_c = jnp.einsum('bqgrc,bcgd->bqgrd', p_c.astype(vc.dtype), vc)
    n_blk = ks.shape[1] // SEL_BLOCK
    blk = jnp.arange(n_blk)
    c_start = c_end - (CMP_BLOCK - 1)
    overlap = ((c_start[:, None] < (blk[None, :] + 1) * SEL_BLOCK) & (c_end[:, None] >= blk[None, :] * SEL_BLOCK)).astype(jnp.float32)
    imp = jnp.einsum('bqgrc,cn->bqgn', p_c, overlap)
    cur = q_pos // SEL_BLOCK
    valid = blk[None, :] <= cur[:, None]
    forced = valid & ((blk[None, :] == 0) | (blk[None, :] > cur[:, None] - SEL_LOCAL))
    imp = jnp.where(forced[None, :, None, :], jnp.inf, jnp.where(valid[None, :, None, :], imp, -jnp.inf))
    n_top = min(SEL_TOP, n_blk)
    _, idx = lax.top_k(imp, n_top)
    kb = jnp.moveaxis(ks.reshape(bsz, n_blk, SEL_BLOCK, NSA_KV_GROUPS, HEAD_DIM), 3, 1)
    vb = jnp.moveaxis(vs.reshape(bsz, n_blk, SEL_BLOCK, NSA_KV_GROUPS, HEAD_DIM), 3, 1)
    idx_g = jnp.moveaxis(idx, 2, 1)
    take = jax.vmap(jax.vmap(lambda blocks, ids: blocks[ids]))
    gk = take(kb, idx_g)
    gv = take(vb, idx_g)
    sel_pos = idx_g[..., None] * SEL_BLOCK + jnp.arange(SEL_BLOCK)
    m_s = jnp.moveaxis(sel_pos <= q_pos[None, None, :, None, None], 1, 2)[:, :, :, None]
    s_s = jnp.einsum('bqgrd,bgqnsd->bqgrns', qg, gk).astype(jnp.float32) * scale
    nk = n_top * SEL_BLOCK
    p_s = _x_masked_softmax(s_s.reshape(bsz, tq, NSA_KV_GROUPS, r, nk), m_s.reshape(bsz, tq, NSA_KV_GROUPS, 1, nk))
    o_s = jnp.einsum('bqgrk,bgqkd->bqgrd', p_s.astype(gv.dtype), gv.reshape(bsz, NSA_KV_GROUPS, tq, nk, HEAD_DIM))
    s_w = jnp.einsum('bqgrd,bkgd->bqgrk', qg, kw).astype(jnp.float32) * scale
    m_w = (kw_pos[None, :] <= q_pos[:, None]) & (kw_pos[None, :] > q_pos[:, None] - WINDOW) & (kw_pos[None, :] >= 0)
    p_w = _x_masked_softmax(s_w, m_w[None, :, None, None, :])
    o_w = jnp.einsum('bqgrk,bkgd->bqgrd', p_w.astype(vw.dtype), vw)
    g = jax.nn.sigmoid(gates.reshape(bsz, tq, NSA_KV_GROUPS, r, 3).astype(jnp.float32)).astype(q.dtype)
    o = g[..., 0:1] * o_c + g[..., 1:2] * o_s + g[..., 2:3] * o_w
    return o.reshape(bsz, tq, NSA_WIDTH)


def _x_nsa_prompt(nq, gates, nkv, lp):
    bsz, t = nq.shape[:2]
    kc, c_end = _x_compress_blocks(nkv[:, :, 0], lp['cmp_pos'][0], lp['cmp_w1'][0], lp['cmp_w2'][0])
    vc, _ = _x_compress_blocks(nkv[:, :, 1], lp['cmp_pos'][1], lp['cmp_w1'][1], lp['cmp_w2'][1])
    ks, vs = nkv[:, :, 2], nkv[:, :, 3]
    w_pad = jnp.pad(nkv[:, :, 4:6], ((0, 0), (WINDOW, 0), (0, 0), (0, 0), (0, 0)))
    nb = t // Q_BLOCK
    qb = jnp.moveaxis(nq.reshape(bsz, nb, Q_BLOCK, NSA_HEADS, HEAD_DIM), 1, 0)
    gb = jnp.moveaxis(gates.reshape(bsz, nb, Q_BLOCK, 3 * NSA_HEADS), 1, 0)

    def blk(args):
        qi, gi, i = args
        t0 = i * Q_BLOCK
        wi = lax.dynamic_slice_in_dim(w_pad, t0, WINDOW + Q_BLOCK, axis=1)
        kw_pos = t0 - WINDOW + jnp.arange(WINDOW + Q_BLOCK)
        return _x_nsa_attend(qi, gi, t0 + jnp.arange(Q_BLOCK), kc, vc, c_end, ks, vs, wi[:, :, 0], wi[:, :, 1], kw_pos)

    out = lax.map(blk, (qb, gb, jnp.arange(nb)))
    return jnp.moveaxis(out, 0, 1).reshape(bsz, t, NSA_WIDTH)


def _col_offsets():
    offs, s = [], 0
    for n in IN_SPLITS:
        offs.append(s)
        s += n
    return offs


def _layer_weights(l, p):
    o = _col_offsets()
    w_in = p['w_in'][l]
    cols = lambda a, n: w_in[:, a:a + n]
    small = jnp.concatenate([cols(o[2], SSD_HEADS), cols(o[6], 3 * NSA_HEADS)], axis=1)
    small = jnp.pad(small, ((0, 0), (0, V7X_LANES - small.shape[1])))
    bf = lambda a: a.astype(BF16)
    return {
        'w_z': bf(cols(o[0], SSD_INNER)), 'w_xbc': bf(cols(o[1], SSD_CONV_DIM)), 'w_small': bf(small),
        'w_sbq': bf(cols(o[3], SB_WIDTH)), 'w_sbkv': bf(cols(o[3] + SB_WIDTH, 2 * SB_WIDTH)),
        'w_nq': bf(cols(o[4], NSA_WIDTH)), 'w_nkv': bf(cols(o[5], 6 * NSA_KV_WIDTH)),
        'w_brg': bf(cols(o[7], N_BRANCH * D_MODEL)),
        'mix_pre': p['norm_mix_pre'][l], 'mix_post': p['norm_mix_post'][l],
        'ffn_pre': p['norm_ffn_pre'][l], 'ffn_post': p['norm_ffn_post'][l],
        'conv_w': p['ssd_conv_w'][l], 'conv_b': p['ssd_conv_b'][l], 'dt_bias': p['ssd_dt_bias'][l],
        'a_log': p['ssd_a_log'][l], 'd_skip': p['ssd_d'][l], 'ssd_norm': p['ssd_norm'][l],
        'w_ssd_out': bf(p['w_ssd_out'][l]), 'w_sb_out': bf(p['w_sb_out'][l]), 'w_nsa_out': bf(p['w_nsa_out'][l]),
        'w_o': bf(p['w_o'][l]), 'w_ffn_gate': bf(p['w_ffn_gate'][l]), 'w_ffn_up': bf(p['w_ffn_up'][l]),
        'w_ffn_down': bf(p['w_ffn_down'][l]),
        'cmp_pos': p['nsa_cmp_pos'][l], 'cmp_w1': p['nsa_cmp_w1'][l], 'cmp_w2': p['nsa_cmp_w2'][l],
    }


def _trunk_tail(x, ssd_y, sb_o, nsa_o, br_g, lw):
    b, t, d = x.shape
    m = b * t
    x1 = merge_branches(x.reshape(m, d), ssd_y.reshape(m, -1), sb_o.reshape(m, -1), nsa_o.reshape(m, -1),
                        br_g.reshape(m, -1), lw['w_ssd_out'], lw['w_sb_out'], lw['w_nsa_out'], lw['w_o'],
                        lw['mix_post'])
    x2 = ffn(x1, lw['ffn_pre'], lw['ffn_post'], lw['w_ffn_gate'], lw['w_ffn_up'], lw['w_ffn_down'])
    return x2.reshape(b, t, d)


def _layer_prompt(x, lw):
    b, t, _ = x.shape
    g = lw['mix_pre']
    z = norm_matmul(x, g, lw['w_z'])
    xbc = norm_matmul(x, g, lw['w_xbc'])
    small = norm_matmul(x, g, lw['w_small'])
    br_g = norm_matmul(x, g, lw['w_brg'])
    sb_q = norm_matmul(x, g, lw['w_sbq'], out_dtype=BF16)
    nsa_q = norm_matmul(x, g, lw['w_nq'])
    sb_kvt = norm_matmul(x, g, lw['w_sbkv'].T, transposed=True)
    nsa_kvt = norm_matmul(x, g, lw['w_nkv'].T, transposed=True)
    dt_raw = small[..., :SSD_HEADS]
    nsa_g = small[..., SSD_HEADS:SSD_HEADS + 3 * NSA_HEADS]

    h0 = jnp.zeros((b, SSD_HEADS, SSD_HEAD_DIM, SSD_STATE), F32)
    conv0 = jnp.zeros((b, SSD_CONV - 1, SSD_CONV_DIM), F32)
    ssd_y, h_new = ssd_prompt(xbc, z, dt_raw, h0, conv0, lw['conv_w'], lw['conv_b'], lw['dt_bias'], lw['a_log'],
                              lw['d_skip'], lw['ssd_norm'])
    conv_new = xbc[:, t - (SSD_CONV - 1):, :]
    sb_o = sb_prompt(sb_q, sb_kvt)

    nsa_o = nsa_prompt(nsa_q, nsa_g, nsa_kvt, lw['cmp_pos'], lw['cmp_w1'], lw['cmp_w2'])

    y = _trunk_tail(x, ssd_y, sb_o, nsa_o, br_g, lw)
    sb_kv = jnp.moveaxis(sb_kvt.reshape(b, 2, SB_HEADS, HEAD_DIM, t), 4, 1)
    nsa_all = jnp.moveaxis(nsa_kvt.reshape(b, 6, NSA_KV_GROUPS, HEAD_DIM, t), 4, 1)
    keep = min(WINDOW, t)
    return y, sb_kv, nsa_all[:, :, 0:4], nsa_all[:, t - keep:, 4:6], h_new, conv_new


def _layer_sample(x, lw, layer, sb_pool_t, nsa_pool_t, win_t, h0, conv_buf, page_table):
    bsz, t = x.shape[:2]
    assert t == 1
    past = page_table.shape[1] * sb_pool_t.shape[-1]
    xr = x.reshape(1, bsz, D_MODEL)
    g = lw['mix_pre']
    pr = lambda w: norm_matmul(xr, g, w)[0]
    z, xbc, small, br_g = pr(lw['w_z']), pr(lw['w_xbc']), pr(lw['w_small']), pr(lw['w_brg'])
    sb_q, sb_kv, nsa_q, nsa_kv = pr(lw['w_sbq']), pr(lw['w_sbkv']), pr(lw['w_nq']), pr(lw['w_nkv'])
    dt_raw = small[:, :SSD_HEADS]
    nsa_g = small[:, SSD_HEADS:SSD_HEADS + 3 * NSA_HEADS]
    ssd_y, h_new = ssd_step(xbc, z, dt_raw, h0, conv_buf, lw['conv_w'], lw['conv_b'], lw['dt_bias'], lw['a_log'],
                            lw['d_skip'], lw['ssd_norm'])
    conv_new = jnp.concatenate([conv_buf[:, 1:], xbc[:, None, :]], axis=1)
    sb_o = sb_decode(sb_q, sb_pool_t, layer, page_table)
    nsa_o = nsa_decode(nsa_q, nsa_g, nsa_kv, nsa_pool_t, win_t, layer, page_table, lw['cmp_pos'], lw['cmp_w1'],
                       lw['cmp_w2'])
    y = _trunk_tail(x, ssd_y[:, None], sb_o.astype(BF16)[:, None], nsa_o.astype(BF16)[:, None], br_g[:, None], lw)
    kv_new = sb_kv.reshape(bsz, 1, 2, SB_HEADS, HEAD_DIM)
    nkv = nsa_kv.reshape(bsz, 1, 6, NSA_KV_GROUPS, HEAD_DIM)
    keep = min(WINDOW, past + 1)
    win_all_t = jnp.concatenate([win_t[layer], nkv[:, 0, 4:6][..., None]], axis=-1)
    win_new = jnp.moveaxis(win_all_t[..., win_all_t.shape[-1] - keep:], 4, 1)
    return y, kv_new, nkv[:, :, 0:4], win_new, h_new, conv_new


def kernel(x_prompt, x_sample, cache_sb_kv, cache_nsa_kv, cache_nsa_win, state_ssd, state_conv, page_table,
           norm_mix_pre, norm_mix_post, norm_ffn_pre, norm_ffn_post, w_in, ssd_conv_w, ssd_conv_b, ssd_dt_bias,
           ssd_a_log, ssd_d, ssd_norm, w_ssd_out, w_sb_out, nsa_cmp_pos, nsa_cmp_w1, nsa_cmp_w2, w_nsa_out, w_o,
           w_ffn_gate, w_ffn_up, w_ffn_down):
    p = dict(norm_mix_pre=norm_mix_pre, norm_mix_post=norm_mix_post, norm_ffn_pre=norm_ffn_pre,
             norm_ffn_post=norm_ffn_post, w_in=w_in, ssd_conv_w=ssd_conv_w, ssd_conv_b=ssd_conv_b,
             ssd_dt_bias=ssd_dt_bias, ssd_a_log=ssd_a_log, ssd_d=ssd_d, ssd_norm=ssd_norm, w_ssd_out=w_ssd_out,
             w_sb_out=w_sb_out, nsa_cmp_pos=nsa_cmp_pos, nsa_cmp_w1=nsa_cmp_w1, nsa_cmp_w2=nsa_cmp_w2,
             w_nsa_out=w_nsa_out, w_o=w_o, w_ffn_gate=w_ffn_gate, w_ffn_up=w_ffn_up, w_ffn_down=w_ffn_down)
    yp, ys = x_prompt, x_sample
    outs_p, outs_s = [], []
    time_minor = lambda a: jnp.transpose(a, (0, 1, 3, 4, 5, 2))
    sb_pool_t, nsa_pool_t, win_t = time_minor(cache_sb_kv), time_minor(cache_nsa_kv), time_minor(cache_nsa_win)
    for l in range(w_in.shape[0]):
        lw = _layer_weights(l, p)
        res = _layer_prompt(yp, lw)
        yp = res[0]
        outs_p.append(res[1:])
        res = _layer_sample(ys, lw, l, sb_pool_t, nsa_pool_t, win_t, state_ssd[l], state_conv[l], page_table)
        ys = res[0]
        outs_s.append(res[1:])
    st = lambda outs, i: jnp.stack([o[i] for o in outs])
    return (yp, ys, st(outs_p, 0), st(outs_s, 0), st(outs_p, 1), st(outs_s, 1), st(outs_p, 2), st(outs_s, 2),
            st(outs_p, 3), st(outs_s, 3), st(outs_p, 4), st(outs_s, 4))
```

```python
import functools
import math

import jax
import jax.numpy as jnp
from jax import lax
from jax.experimental import pallas as pl
from jax.experimental.pallas import tpu as pltpu

D_MODEL = 1024
HEAD_DIM = 64
SSD_INNER = D_MODEL
SSD_HEAD_DIM = 64
SSD_HEADS = SSD_INNER // SSD_HEAD_DIM
SSD_GROUPS = 2
SSD_STATE = 128
SSD_CONV = 4
SSD_GN = SSD_GROUPS * SSD_STATE
SSD_CONV_DIM = SSD_INNER + 2 * SSD_GN
SSD_CHUNK = 128
SB_HEADS = 8
SB_WIDTH = SB_HEADS * HEAD_DIM
NSA_HEADS = 8
NSA_KV_GROUPS = 2
NSA_REP = NSA_HEADS // NSA_KV_GROUPS
NSA_WIDTH = NSA_HEADS * HEAD_DIM
NSA_KV_WIDTH = NSA_KV_GROUPS * HEAD_DIM
CMP_BLOCK = 32
CMP_STRIDE = 16
CMP_HIDDEN = 128
SEL_BLOCK = 64
SEL_TOP = 16
SEL_LOCAL = 2
WINDOW = 512
Q_BLOCK = 128
N_BRANCH = 3
FFN_HIDDEN = ((8 * D_MODEL + 3 * 256 - 1) // (3 * 256)) * 256
RMS_EPS = 1e-6
IN_SPLITS = (SSD_INNER, SSD_CONV_DIM, SSD_HEADS, 3 * SB_WIDTH, NSA_WIDTH, 6 * NSA_KV_WIDTH, 3 * NSA_HEADS,
             N_BRANCH * D_MODEL)

V7X_LANES = 128
V7X_VMEM_LIMIT = 56 * 1024 * 1024
PAGES_PER_STEP = 4
NEG = -1e30
BF16 = jnp.bfloat16
F32 = jnp.float32
HI = lax.Precision.HIGHEST


def _params(*sem):
    return pltpu.CompilerParams(dimension_semantics=sem, vmem_limit_bytes=V7X_VMEM_LIMIT)


def _pick(n, cands):
    for c in cands:
        if n % c == 0:
            return c
    return n


def _rms(x, w):
    return x * lax.rsqrt(jnp.mean(x * x, axis=-1, keepdims=True) + RMS_EPS) * w


def _softplus(x):
    return jnp.maximum(x, 0.0) + jnp.log1p(jnp.exp(-jnp.abs(x)))


def _sigmoid(x):
    return 1.0 / (1.0 + jnp.exp(-x))


def _dot_nt(a, b):
    return lax.dot_general(a, b, (((1,), (1,)), ((), ())), preferred_element_type=F32)


def _dot_tn(a, b):
    return lax.dot_general(a, b, (((0,), (0,)), ((), ())), preferred_element_type=F32)


def _norm_mm_kernel(x_ref, g_ref, w_ref, o_ref, h_ref, *, transposed):
    @pl.when(pl.program_id(2) == 0)
    def _():
        h_ref[...] = _rms(x_ref[...], g_ref[...]).astype(BF16)

    if transposed:
        o_ref[...] = _dot_nt(w_ref[...], h_ref[...]).astype(o_ref.dtype)
    else:
        o_ref[...] = jnp.dot(h_ref[...], w_ref[...], preferred_element_type=F32).astype(o_ref.dtype)


def norm_matmul(x, gain, w, out_dtype=F32, transposed=False):
    b, t, k = x.shape
    n = w.shape[0] if transposed else w.shape[1]
    tm = _pick(t, (512, 256, 128))
    tn = _pick(n, (512, 256, 128))
    if transposed:
        w_spec = pl.BlockSpec((tn, k), lambda bi, i, j: (j, 0))
        o_spec = pl.BlockSpec((None, tn, tm), lambda bi, i, j: (bi, j, i))
        o_shape = (b, n, t)
    else:
        w_spec = pl.BlockSpec((k, tn), lambda bi, i, j: (0, j))
        o_spec = pl.BlockSpec((None, tm, tn), lambda bi, i, j: (bi, i, j))
        o_shape = (b, t, n)
    return pl.pallas_call(
        functools.partial(_norm_mm_kernel, transposed=transposed),
        grid=(b, t // tm, n // tn),
        in_specs=[pl.BlockSpec((None, tm, k), lambda bi, i, j: (bi, i, 0)),
                  pl.BlockSpec((1, k), lambda bi, i, j: (0, 0)),
                  w_spec],
        out_specs=o_spec,
        out_shape=jax.ShapeDtypeStruct(o_shape, out_dtype),
        scratch_shapes=[pltpu.VMEM((tm, k), BF16)],
        compiler_params=_params("parallel", "parallel", "arbitrary"),
        name="norm_matmul_t" if transposed else "norm_matmul",
    )(x, gain.reshape(1, k), w)


def _ssd_chunk_kernel(xbc_ref, z_ref, dt_ref, dtt_ref, h0_ref, c0_ref, cw_ref, cb_ref, dtb_ref, dtbt_ref,
                      alog_ref, alogt_ref, dfull_ref, nw_ref, exp_ref, y_ref, h_ref, xp_ref):
    q = SSD_CHUNK
    c = pl.program_id(1)

    @pl.when(c == 0)
    def _():
        h_ref[...] = h0_ref[...]
        xp_ref[5:8, :] = c0_ref[...]

    xp_ref[8:8 + q, :] = xbc_ref[...]
    conv = cb_ref[...]
    for j in range(SSD_CONV):
        conv = conv + cw_ref[j:j + 1, :] * xp_ref[5 + j:5 + j + q, :]
    xp_ref[5:8, :] = xp_ref[q + 5:q + 8, :]
    u = conv * _sigmoid(conv)
    xs = u[:, :SSD_INNER]
    bm = u[:, SSD_INNER:SSD_INNER + SSD_GN].astype(BF16)
    cm = u[:, SSD_INNER + SSD_GN:].astype(BF16)

    dt = _softplus(dt_ref[...] + dtb_ref[...])
    dtt = _softplus(dtt_ref[...] + dtbt_ref[...])
    dta = dt * (-jnp.exp(alog_ref[...]))
    dtat = dtt * (-jnp.exp(alogt_ref[...]))
    row = lax.broadcasted_iota(jnp.int32, (q, q), 0)
    col = lax.broadcasted_iota(jnp.int32, (q, q), 1)
    tril = row >= col
    acum = jnp.dot(tril.astype(F32), dta, precision=HI, preferred_element_type=F32)
    acumt = jnp.dot(dtat, (row <= col).astype(F32), precision=HI, preferred_element_type=F32)
    expand = exp_ref[...]
    dt_full = jnp.dot(dt, expand, precision=HI, preferred_element_type=F32)
    ea_full = jnp.dot(jnp.exp(acum), expand, precision=HI, preferred_element_type=F32)
    te_full = jnp.dot(jnp.exp(acum[q - 1:q, :] - acum), expand, precision=HI, preferred_element_type=F32)
    xdt = xs * dt_full
    xdt_b = xdt.astype(BF16)
    xw_b = (xdt * te_full).astype(BF16)

    r = SSD_HEADS // SSD_GROUPS
    gw = r * SSD_HEAD_DIM
    y_diag, y_off = [], []
    for g in range(SSD_GROUPS):
        cm_g = cm[:, g * SSD_STATE:(g + 1) * SSD_STATE]
        bm_g = bm[:, g * SSD_STATE:(g + 1) * SSD_STATE]
        cb = _dot_nt(cm_g, bm_g)
        h_g = h_ref[g * r:(g + 1) * r].reshape(gw, SSD_STATE)
        y_off.append(_dot_nt(cm_g, h_g.astype(BF16)))
        st = _dot_tn(xw_b[:, g * gw:(g + 1) * gw], bm_g)
        for hh in range(r):
            hd = g * r + hh
            seg = acum[:, hd:hd + 1] - acumt[hd:hd + 1, :]
            decay = jnp.exp(jnp.where(tril, seg, -jnp.inf))
            m = (cb * decay).astype(BF16)
            y_diag.append(jnp.dot(m, xdt_b[:, hd * SSD_HEAD_DIM:(hd + 1) * SSD_HEAD_DIM],
                                  preferred_element_type=F32))
            dec = jnp.exp(acumt[hd:hd + 1, q - 1:q])
            h_ref[hd] = dec * h_ref[hd] + st[hh * SSD_HEAD_DIM:(hh + 1) * SSD_HEAD_DIM, :]
    y = (jnp.concatenate(y_diag, axis=1) + jnp.concatenate(y_off, axis=1) * ea_full
         + dfull_ref[...] * xs)
    zz = z_ref[...]
    y = y * (zz * _sigmoid(zz))
    y_ref[...] = _rms(y, nw_ref[...]).astype(y_ref.dtype)


def ssd_prompt(xbc, z, dt_raw, h0, conv0, conv_w, conv_b, dt_bias, a_log, d_skip, norm_w):
    b, t, _ = xbc.shape
    q = SSD_CHUNK
    nc = t // q
    hds = SSD_HEADS
    expand = (jnp.arange(SSD_INNER)[None, :] // SSD_HEAD_DIM == jnp.arange(hds)[:, None]).astype(F32)
    d_full = jnp.repeat(d_skip, SSD_HEAD_DIM).reshape(1, SSD_INNER)
    dtt = jnp.swapaxes(dt_raw, 1, 2)
    full = lambda shape: pl.BlockSpec(shape, lambda bi, ci: (0,) * len(shape))
    y, h = pl.pallas_call(
        _ssd_chunk_kernel,
        grid=(b, nc),
        in_specs=[pl.BlockSpec((None, q, SSD_CONV_DIM), lambda bi, ci: (bi, ci, 0)),
                  pl.BlockSpec((None, q, SSD_INNER), lambda bi, ci: (bi, ci, 0)),
                  pl.BlockSpec((None, q, hds), lambda bi, ci: (bi, ci, 0)),
                  pl.BlockSpec((None, hds, q), lambda bi, ci: (bi, 0, ci)),
                  pl.BlockSpec((None, hds, SSD_HEAD_DIM, SSD_STATE), lambda bi, ci: (bi, 0, 0, 0)),
                  pl.BlockSpec((None, SSD_CONV - 1, SSD_CONV_DIM), lambda bi, ci: (bi, 0, 0)),
                  full((SSD_CONV, SSD_CONV_DIM)), full((1, SSD_CONV_DIM)),
                  full((1, hds)), full((hds, 1)), full((1, hds)), full((hds, 1)),
                  full((1, SSD_INNER)), full((1, SSD_INNER)), full((hds, SSD_INNER))],
        out_specs=[pl.BlockSpec((None, q, SSD_INNER), lambda bi, ci: (bi, ci, 0)),
                   pl.BlockSpec((None, hds, SSD_HEAD_DIM, SSD_STATE), lambda bi, ci: (bi, 0, 0, 0))],
        out_shape=[jax.ShapeDtypeStruct((b, t, SSD_INNER), BF16),
                   jax.ShapeDtypeStruct((b, hds, SSD_HEAD_DIM, SSD_STATE), F32)],
        scratch_shapes=[pltpu.VMEM((q + 8, SSD_CONV_DIM), F32)],
        compiler_params=_params("parallel", "arbitrary"),
        name="ssd_chunk_scan",
    )(xbc, z, dt_raw, dtt, h0, conv0, conv_w, conv_b.reshape(1, -1), dt_bias.reshape(1, hds),
      dt_bias.reshape(hds, 1), a_log.reshape(1, hds), a_log.reshape(hds, 1), d_full, norm_w.reshape(1, -1), expand)
    return y, h


def _sb_stage(z):
    sp = jnp.maximum(z, 0.0) + jnp.log(1.0 + jnp.exp(-jnp.abs(z)))
    lk = -sp
    hi = lk.astype(BF16)
    return z - sp, hi, (lk - hi.astype(F32)).astype(BF16), lk[:, 0:1]


def _sb_prompt_kernel(q_ref, kt_ref, vt_ref, o_ref, z_ref, d_ref, hi_ref, lo_ref, a_ref, *, tb):
    i = pl.program_id(2)
    q = (q_ref[...] * (HEAD_DIM ** -0.5)).astype(BF16)
    row = lax.broadcasted_iota(jnp.int32, (tb, tb), 0)
    col = lax.broadcasted_iota(jnp.int32, (tb, tb), 1)
    upper = (row > col).astype(BF16)

    def tile(ref, s):
        off = pl.multiple_of(jnp.clip(i - s, 0, i) * tb, tb)
        return ref[:, pl.ds(off, tb)].astype(BF16)

    z0 = jnp.where(col < row, jnp.dot(q, tile(kt_ref, 0), preferred_element_type=F32), NEG)
    d0, hi0, lo0, first0 = _sb_stage(z0)
    d_ref[0], hi_ref[0], lo_ref[0] = d0, hi0, lo0
    z_ref[0] = jnp.dot(q, tile(kt_ref, 1), preferred_element_type=F32)
    a_ref[0] = jnp.zeros((tb, tb), BF16)

    def step(n, rd, wr, st):
        carry, first, acc = st
        later = (jnp.dot(hi_ref[rd], upper, preferred_element_type=F32)
                 + jnp.dot(lo_ref[rd], upper, preferred_element_type=F32))
        acc = acc + _dot_nt(a_ref[rd], tile(vt_ref, n - 1))
        z_ref[wr] = jnp.dot(q, tile(kt_ref, n + 2), preferred_element_type=F32)
        d_ref[wr], hi_ref[wr], lo_ref[wr], first_new = _sb_stage(z_ref[rd])
        a = jnp.exp(d_ref[rd] + later + carry)
        a_ref[wr] = jnp.where(n <= i, a, 0.0).astype(BF16)
        return carry + later[:, 0:1] + first, first_new, acc

    def body(pair, st):
        return step(2 * pair + 1, 1, 0, step(2 * pair, 0, 1, st))

    st = (jnp.zeros((tb, 1), F32), first0, jnp.zeros((tb, HEAD_DIM), F32))
    _, _, acc = lax.fori_loop(0, (i + 3) >> 1, body, st)
    o_ref[...] = acc.astype(o_ref.dtype)


def sb_prompt(q, kvt, tb=256):
    b, t, _ = q.shape
    h = SB_HEADS
    tb = min(tb, t)
    assert t % tb == 0
    qh = jnp.swapaxes(q.reshape(b, t, h, HEAD_DIM), 1, 2)
    out = pl.pallas_call(
        functools.partial(_sb_prompt_kernel, tb=tb),
        grid=(b, h, t // tb),
        in_specs=[pl.BlockSpec((None, None, tb, HEAD_DIM), lambda bi, hi, i: (bi, hi, i, 0)),
                  pl.BlockSpec((None, HEAD_DIM, t), lambda bi, hi, i: (bi, hi, 0)),
                  pl.BlockSpec((None, HEAD_DIM, t), lambda bi, hi, i: (bi, h + hi, 0))],
        out_specs=pl.BlockSpec((None, None, tb, HEAD_DIM), lambda bi, hi, i: (bi, hi, i, 0)),
        out_shape=jax.ShapeDtypeStruct((b, h, t, HEAD_DIM), BF16),
        scratch_shapes=[pltpu.VMEM((2, tb, tb), F32), pltpu.VMEM((2, tb, tb), F32), pltpu.VMEM((2, tb, tb), BF16),
                        pltpu.VMEM((2, tb, tb), BF16), pltpu.VMEM((2, tb, tb), BF16)],
        compiler_params=_params("parallel", "parallel", "arbitrary"),
        name="sb_prompt",
    )(qh, kvt, kvt)
    return jnp.swapaxes(out, 1, 2).reshape(b, t, h * HEAD_DIM)


def _merge_kernel(x_ref, ssd_ref, sb_ref, nsa_ref, gl_ref, wssd_ref, wsb_ref, wnsa_ref, wo_ref, nw_ref, o_ref):
    d = D_MODEL
    gl = gl_ref[...]
    merged = (_sigmoid(gl[:, :d]) * jnp.dot(ssd_ref[...], wssd_ref[...], preferred_element_type=F32)
              + _sigmoid(gl[:, d:2 * d]) * jnp.dot(sb_ref[...], wsb_ref[...], preferred_element_type=F32)
              + _sigmoid(gl[:, 2 * d:]) * jnp.dot(nsa_ref[...], wnsa_ref[...], preferred_element_type=F32))
    y = jnp.dot(merged.astype(BF16), wo_ref[...], preferred_element_type=F32)
    o_ref[...] = x_ref[...] + _rms(y, nw_ref[...])


def merge_branches(x, ssd_y, sb_o, nsa_o, gate_logits, w_ssd_out, w_sb_out, w_nsa_out, w_o, norm_w):
    m, d = x.shape
    tm = _pick(m, (256, 128, 32))
    rows = lambda n: pl.BlockSpec((tm, n), lambda i: (i, 0))
    full = lambda a: pl.BlockSpec(a.shape, lambda i: (0, 0))
    nw = norm_w.reshape(1, d)
    return pl.pallas_call(
        _merge_kernel,
        grid=(m // tm,),
        in_specs=[rows(d), rows(ssd_y.shape[1]), rows(sb_o.shape[1]), rows(nsa_o.shape[1]), rows(N_BRANCH * d),
                  full(w_ssd_out), full(w_sb_out), full(w_nsa_out), full(w_o), full(nw)],
        out_specs=rows(d),
        out_shape=jax.ShapeDtypeStruct((m, d), F32),
        compiler_params=_params("parallel"),
        name="merge_branches",
    )(x, ssd_y, sb_o, nsa_o, gate_logits, w_ssd_out, w_sb_out, w_nsa_out, w_o, nw)


def _ffn_up_kernel(x_ref, g_ref, wg_ref, wu_ref, o_ref, h_ref):
    @pl.when(pl.program_id(1) == 0)
    def _():
        h_ref[...] = _rms(x_ref[...], g_ref[...]).astype(BF16)

    h = h_ref[...]
    a = jnp.dot(h, wg_ref[...], preferred_element_type=F32)
    u = jnp.dot(h, wu_ref[...], preferred_element_type=F32)
    o_ref[...] = (a * _sigmoid(a) * u).astype(o_ref.dtype)


def _ffn_down_kernel(a_ref, x_ref, wd_ref, nw_ref, o_ref):
    f = jnp.dot(a_ref[...], wd_ref[...], preferred_element_type=F32)
    o_ref[...] = x_ref[...] + _rms(f, nw_ref[...])


def ffn(x, pre_w, post_w, w_gate, w_up, w_down):
    m, d = x.shape
    f = w_gate.shape[1]
    tm = _pick(m, (512, 256, 128, 32))
    tn = _pick(f, (256, 128))
    act = pl.pallas_call(
        _ffn_up_kernel,
        grid=(m // tm, f // tn),
        in_specs=[pl.BlockSpec((tm, d), lambda i, j: (i, 0)),
                  pl.BlockSpec((1, d), lambda i, j: (0, 0)),
                  pl.BlockSpec((d, tn), lambda i, j: (0, j)),
                  pl.BlockSpec((d, tn), lambda i, j: (0, j))],
        out_specs=pl.BlockSpec((tm, tn), lambda i, j: (i, j)),
        out_shape=jax.ShapeDtypeStruct((m, f), BF16),
        scratch_shapes=[pltpu.VMEM((tm, d), BF16)],
        compiler_params=_params("parallel", "arbitrary"),
        name="ffn_up",
    )(x, pre_w.reshape(1, d), w_gate, w_up)
    tm2 = _pick(m, (256, 128, 32))
    return pl.pallas_call(
        _ffn_down_kernel,
        grid=(m // tm2,),
        in_specs=[pl.BlockSpec((tm2, f), lambda i: (i, 0)),
                  pl.BlockSpec((tm2, d), lambda i: (i, 0)),
                  pl.BlockSpec((f, d), lambda i: (0, 0)),
                  pl.BlockSpec((1, d), lambda i: (0, 0))],
        out_specs=pl.BlockSpec((tm2, d), lambda i: (i, 0)),
        out_shape=jax.ShapeDtypeStruct((m, d), F32),
        compiler_params=_params("parallel"),
        name="ffn_down",
    )(act, x, w_down, post_w.reshape(1, d))


def _nsa_compress_kernel(r_ref, pos_ref, w1_ref, w2_ref, o_ref, sh_ref):
    nr = r_ref.shape[0]
    half = CMP_STRIDE * HEAD_DIM
    r = r_ref[...]
    top = jnp.dot((r + pos_ref[0:1, :]).astype(BF16), w1_ref[:half, :], preferred_element_type=F32)
    bot = jnp.dot((r + pos_ref[1:2, :]).astype(BF16), w1_ref[half:, :], preferred_element_type=F32)
    sh_ref[0:nr, :] = bot
    sh_ref[nr:nr + 8, :] = jnp.zeros((8, CMP_HIDDEN), F32)
    pre = top + sh_ref[1:nr + 1, :]
    hid = pre * _sigmoid(pre)
    o_ref[...] = jnp.dot(hid.astype(BF16), w2_ref[...], preferred_element_type=F32).astype(o_ref.dtype)


def nsa_compress(rows16, cmp_pos, cmp_w1, cmp_w2):
    assert CMP_BLOCK == 2 * CMP_STRIDE
    b, _, g, nr, w = rows16.shape
    pos = cmp_pos.reshape(2, 2, w)
    return pl.pallas_call(
        _nsa_compress_kernel,
        grid=(b, 2, g),
        in_specs=[pl.BlockSpec((None, None, None, nr, w), lambda bi, ki, gi: (bi, ki, gi, 0, 0)),
                  pl.BlockSpec((None, 2, w), lambda bi, ki, gi: (ki, 0, 0)),
                  pl.BlockSpec((None, 2 * w, CMP_HIDDEN), lambda bi, ki, gi: (ki, 0, 0)),
                  pl.BlockSpec((None, CMP_HIDDEN, HEAD_DIM), lambda bi, ki, gi: (ki, 0, 0))],
        out_specs=pl.BlockSpec((None, None, None, nr, HEAD_DIM), lambda bi, ki, gi: (bi, ki, gi, 0, 0)),
        out_shape=jax.ShapeDtypeStruct((b, 2, g, nr, HEAD_DIM), BF16),
        scratch_shapes=[pltpu.VMEM((nr + 8, CMP_HIDDEN), F32)],
        compiler_params=_params("parallel", "parallel", "parallel"),
        name="nsa_compress",
    )(rows16, pos, cmp_w1.astype(BF16), cmp_w2.astype(BF16))


def _overlap_matrix(n_cmp_rows, n_cmp, n_blk):
    c = jnp.arange(n_cmp_rows)[:, None]
    n = jnp.arange(n_blk)[None, :]
    c_start, c_end = c * CMP_STRIDE, c * CMP_STRIDE + CMP_BLOCK - 1
    return ((c_start < (n + 1) * SEL_BLOCK) & (c_end >= n * SEL_BLOCK) & (c < n_cmp)).astype(BF16)


def _split_dot(x, w):
    hi = x.astype(BF16)
    lo = (x - hi.astype(F32)).astype(BF16)
    return jnp.dot(hi, w, preferred_element_type=F32) + jnp.dot(lo, w, preferred_element_type=F32)


def _top_blocks(imp, blk, n_top):
    n_blk = imp.shape[1]
    sel = jnp.zeros(imp.shape, F32)
    for _ in range(n_top):
        m = jnp.max(imp, axis=-1, keepdims=True)
        idx = jnp.min(jnp.where(imp == m, blk, float(n_blk)), axis=-1, keepdims=True)
        hit = blk == idx
        sel = jnp.where(hit, 1.0, sel)
        imp = jnp.where(hit, -jnp.inf, imp)
    return sel


def _flash_step(q, kt, vt, mask, m, acc):
    s = jnp.where(mask, jnp.dot(q, kt, preferred_element_type=F32), NEG)
    m_new = jnp.maximum(m, jnp.max(s, axis=-1, keepdims=True))
    p = jnp.exp(s - m_new).astype(BF16)
    vt_ext = jnp.concatenate([vt, jnp.ones(vt.shape, BF16)], axis=0)
    return m_new, acc * jnp.exp(m - m_new) + _dot_nt(p, vt_ext)


def _nsa_prompt_kernel(q_ref, gt_ref, kc_ref, vc_ref, ov_ref, ks_ref, vs_ref, kw_ref, vw_ref, o_ref,
                       s_ref, ch_ref, p_ref, *, tq, n_cmp):
    i = pl.program_id(2)
    rep, d = NSA_REP, HEAD_DIM
    t0 = i * tq
    qv = q_ref[...]
    q = jnp.concatenate([qv[:, r * d:(r + 1) * d] for r in range(rep)], axis=0)
    q = (q * (d ** -0.5)).astype(BF16)
    qpos = t0 + lax.broadcasted_iota(jnp.int32, (tq, 1), 0)
    tile_rows = lambda a: jnp.concatenate([a] * rep, axis=0)

    n_rows = kc_ref.shape[0]
    cidx = lax.broadcasted_iota(jnp.int32, (1, n_rows), 1)
    vis_c = tile_rows(((cidx * CMP_STRIDE + (CMP_BLOCK - 1) <= qpos) & (cidx < n_cmp)).astype(F32)) > 0.5
    s_c = jnp.where(vis_c, _dot_nt(q, kc_ref[...]), NEG)
    e_c = jnp.where(vis_c, jnp.exp(s_c - jnp.max(s_c, axis=-1, keepdims=True)), 0.0)
    p_c = e_c / jnp.maximum(jnp.sum(e_c, axis=-1, keepdims=True), 1e-30)
    o_c = jnp.dot(p_c.astype(BF16), vc_ref[...], preferred_element_type=F32)

    p_sum = p_c[0:tq]
    for r in range(1, rep):
        p_sum = p_sum + p_c[r * tq:(r + 1) * tq]
    n_blk = ov_ref.shape[1]
    blk_i = lax.broadcasted_iota(jnp.int32, (1, n_blk), 1)
    cur = lax.shift_right_logical(qpos, int(math.log2(SEL_BLOCK)))
    valid = blk_i <= cur
    forced = valid & ((blk_i == 0) | (blk_i > cur - SEL_LOCAL))
    imp = jnp.where(forced, jnp.inf, jnp.where(valid, _split_dot(p_sum, ov_ref[...]), -jnp.inf))

    col = lax.broadcasted_iota(jnp.int32, (1, tq), 1)
    m0 = jnp.full((rep * tq, 1), NEG, F32)
    acc0 = jnp.zeros((rep * tq, 2 * d), F32)

    st = (m0, acc0)
    for n in range(WINDOW // tq + 1):
        j = i - n
        off = pl.multiple_of(jnp.maximum(j, 0) * tq, tq)
        kpos = off + col + jnp.where(j < 0, 1 << 30, 0)
        mask = tile_rows(((kpos <= qpos) & (kpos > qpos - WINDOW)).astype(F32)) > 0.5
        st = _flash_step(q, kw_ref[:, pl.ds(off, tq)].astype(BF16), vw_ref[:, pl.ds(off, tq)].astype(BF16),
                         mask, *st)
    acc_w = st[1]

    sel = _top_blocks(imp, blk_i.astype(F32), min(SEL_TOP, n_blk)).astype(BF16)

    exp_row = lax.broadcasted_iota(jnp.int32, (n_blk, tq), 0)
    exp_col = lax.shift_right_logical(lax.broadcasted_iota(jnp.int32, (n_blk, tq), 1), int(math.log2(SEL_BLOCK)))

    def key_tile(n):
        return jnp.clip(i - n, 0, i)

    def scores(n):
        j = key_tile(n)
        expand = (exp_row == j * (tq // SEL_BLOCK) + exp_col).astype(BF16)
        return (jnp.dot(q, ks_ref[:, pl.ds(pl.multiple_of(j * tq, tq), tq)].astype(BF16), preferred_element_type=F32),
                jnp.dot(sel, expand, preferred_element_type=F32))

    def weighted_values(n, slot):
        vt = vs_ref[:, pl.ds(pl.multiple_of(key_tile(n) * tq, tq), tq)].astype(BF16)
        return _dot_nt(p_ref[slot], jnp.concatenate([vt, jnp.ones(vt.shape, BF16)], axis=0))

    s_ref[0], ch_ref[0] = scores(0)
    p_ref[0] = jnp.zeros(p_ref.shape[1:], BF16)

    def sel_step(n, rd, wr, st):
        m, alpha, acc = st
        pv = weighted_values(n - 1, rd)
        s_ref[wr], ch_ref[wr] = scores(n + 1)
        kpos = (i - n) * tq + col + jnp.where(n > i, 1 << 30, 0)
        mask = tile_rows(jnp.where(kpos <= qpos, ch_ref[rd], 0.0)) > 0.5
        s = jnp.where(mask, s_ref[rd], NEG)
        m_new = jnp.maximum(m, jnp.max(s, axis=-1, keepdims=True))
        p_ref[wr] = jnp.exp(s - m_new).astype(BF16)
        return m_new, jnp.exp(m - m_new), acc * alpha + pv

    def sel_pair(pair, st):
        return sel_step(2 * pair + 1, 1, 0, sel_step(2 * pair, 0, 1, st))

    _, _, acc_s = lax.fori_loop(0, (i + 3) >> 1, sel_pair, (m0, jnp.ones((rep * tq, 1), F32), acc0))

    o_s = acc_s[:, :d] / acc_s[:, d:]
    o_w = acc_w[:, :d] / acc_w[:, d:]
    gate = _sigmoid(gt_ref[...])
    outs = []
    for r in range(rep):
        sl = slice(r * tq, (r + 1) * tq)
        outs.append(gate[:, 3 * r:3 * r + 1] * o_c[sl] + gate[:, 3 * r + 1:3 * r + 2] * o_s[sl]
                    + gate[:, 3 * r + 2:3 * r + 3] * o_w[sl])
    o_ref[...] = jnp.concatenate(outs, axis=1).astype(o_ref.dtype)


def nsa_prompt(nsa_q, nsa_g, nsa_kvt, cmp_pos, cmp_w1, cmp_w2, tq=128):
    b, t, _ = nsa_q.shape
    g, d, rep = NSA_KV_GROUPS, HEAD_DIM, NSA_REP
    assert t % tq == 0 and tq % SEL_BLOCK == 0 and WINDOW % tq == 0 and t % CMP_STRIDE == 0
    n_cmp = (t - CMP_BLOCK) // CMP_STRIDE + 1
    nr = t // CMP_STRIDE
    n_blk = t // SEL_BLOCK
    rows16 = jnp.swapaxes(nsa_kvt[:, :2 * g * d].reshape(b, 2, g, d, t), 3, 4).reshape(b, 2, g, nr, CMP_STRIDE * d)
    kcvc = nsa_compress(rows16, cmp_pos, cmp_w1, cmp_w2)
    overlap = _overlap_matrix(nr, n_cmp, n_blk)
    gates = jnp.swapaxes(nsa_g.reshape(b, t, g, 3 * rep), 1, 2)
    gates = jnp.pad(gates, ((0, 0), (0, 0), (0, 0), (0, V7X_LANES - 3 * rep)))
    stream = lambda kind: pl.BlockSpec((None, d, t), lambda bi, gi, i: (bi, kind * g + gi, 0))
    return pl.pallas_call(
        functools.partial(_nsa_prompt_kernel, tq=tq, n_cmp=n_cmp),
        grid=(b, g, t // tq),
        in_specs=[pl.BlockSpec((None, tq, rep * d), lambda bi, gi, i: (bi, i, gi)),
                  pl.BlockSpec((None, None, tq, V7X_LANES), lambda bi, gi, i: (bi, gi, i, 0)),
                  pl.BlockSpec((None, None, None, nr, d), lambda bi, gi, i: (bi, 0, gi, 0, 0)),
                  pl.BlockSpec((None, None, None, nr, d), lambda bi, gi, i: (bi, 1, gi, 0, 0)),
                  pl.BlockSpec((nr, n_blk), lambda bi, gi, i: (0, 0)),
                  stream(2), stream(3), stream(4), stream(5)],
        out_specs=pl.BlockSpec((None, tq, rep * d), lambda bi, gi, i: (bi, i, gi)),
        out_shape=jax.ShapeDtypeStruct((b, t, g * rep * d), BF16),
        scratch_shapes=[pltpu.VMEM((2, rep * tq, tq), F32), pltpu.VMEM((2, tq, tq), F32),
                        pltpu.VMEM((2, rep * tq, tq), BF16)],
        compiler_params=_params("parallel", "parallel", "arbitrary"),
        name="nsa_prompt",
    )(nsa_q, gates, kcvc, kcvc, overlap, nsa_kvt, nsa_kvt, nsa_kvt, nsa_kvt)


def _ssd_step_pre_kernel(x_ref, buf_ref, cw_ref, cb_ref, dt_ref, dtb_ref, alog_ref, exp_ref,
                         xs_ref, xdt_ref, bm_ref, cm_ref, dec_ref):
    conv = cb_ref[...] + cw_ref[SSD_CONV - 1:SSD_CONV, :] * x_ref[...]
    for j in range(SSD_CONV - 1):
        conv = conv + cw_ref[j:j + 1, :] * buf_ref[j]
    u = conv * _sigmoid(conv)
    xs = u[:, :SSD_INNER]
    dt = _softplus(dt_ref[...] + dtb_ref[...])
    xs_ref[...] = xs
    xdt_ref[...] = xs * jnp.dot(dt, exp_ref[...], precision=HI, preferred_element_type=F32)
    bm_ref[...] = u[:, SSD_INNER:SSD_INNER + SSD_GN]
    cm_ref[...] = u[:, SSD_INNER + SSD_GN:]
    dec_ref[...] = jnp.exp(dt * (-jnp.exp(alog_ref[...])))


def _ssd_step_state_kernel(h0_ref, xdt_ref, dec_ref, bm_ref, cm_ref, h_ref, y_ref):
    r = SSD_HEADS // SSD_GROUPS
    for hd in range(SSD_HEADS):
        g = hd // r
        hn = dec_ref[hd] * h0_ref[hd] + xdt_ref[hd] * bm_ref[g]
        h_ref[hd] = hn
        y_ref[hd] = jnp.sum(hn * cm_ref[g], axis=-1, keepdims=True)


def _ssd_step_post_kernel(y_ref, xs_ref, z_ref, dfull_ref, nw_ref, o_ref):
    zz = z_ref[...]
    y = (y_ref[...] + dfull_ref[...] * xs_ref[...]) * (zz * _sigmoid(zz))
    o_ref[...] = _rms(y, nw_ref[...]).astype(o_ref.dtype)


def ssd_step(xbc, z, dt_raw, h0, conv_buf, conv_w, conv_b, dt_bias, a_log, d_skip, norm_w):
    b = xbc.shape[0]
    hds, p, n = SSD_HEADS, SSD_HEAD_DIM, SSD_STATE
    expand = (jnp.arange(SSD_INNER)[None, :] // p == jnp.arange(hds)[:, None]).astype(F32)
    d_full = jnp.repeat(d_skip, p).reshape(1, SSD_INNER)
    sds = lambda shape: jax.ShapeDtypeStruct(shape, F32)
    xs, xdt, bm, cm, dec = pl.pallas_call(
        _ssd_step_pre_kernel,
        out_shape=[sds((b, SSD_INNER)), sds((b, SSD_INNER)), sds((b, SSD_GN)), sds((b, SSD_GN)), sds((b, hds))],
        name="ssd_step_pre",
    )(xbc, jnp.swapaxes(conv_buf, 0, 1), conv_w, conv_b.reshape(1, -1), dt_raw, dt_bias.reshape(1, hds),
      a_log.reshape(1, hds), expand)
    per_b = lambda *dims: pl.BlockSpec((None,) + dims, lambda bi: (bi,) + (0,) * len(dims))
    h_new, y_col = pl.pallas_call(
        _ssd_step_state_kernel,
        grid=(b,),
        in_specs=[per_b(hds, p, n), per_b(hds, p, 1), per_b(hds, 1, 1), per_b(SSD_GROUPS, 1, n),
                  per_b(SSD_GROUPS, 1, n)],
        out_specs=[per_b(hds, p, n), per_b(hds, p, 1)],
        out_shape=[sds((b, hds, p, n)), sds((b, hds, p, 1))],
        compiler_params=_params("parallel"),
        name="ssd_step_state",
    )(h0, xdt.reshape(b, hds, p, 1), dec.reshape(b, hds, 1, 1), bm.reshape(b, SSD_GROUPS, 1, n),
      cm.reshape(b, SSD_GROUPS, 1, n))
    y = pl.pallas_call(
        _ssd_step_post_kernel,
        out_shape=jax.ShapeDtypeStruct((b, SSD_INNER), BF16),
        name="ssd_step_post",
    )(y_col.reshape(b, SSD_INNER), xs, z, d_full, norm_w.reshape(1, -1))
    return y, h_new


def _sb_decode_kernel(pt_ref, q_ref, *refs):
    page_refs, (o_ref, carry_ref, acc_ref) = refs[:PAGES_PER_STEP], refs[PAGES_PER_STEP:]
    p = pl.program_id(1)
    tk = page_refs[0].shape[-1]
    hds = SB_HEADS

    @pl.when(p == 0)
    def _():
        carry_ref[...] = jnp.zeros(carry_ref.shape, F32)
        acc_ref[...] = jnp.zeros(acc_ref.shape, F32)

    scale = HEAD_DIM ** -0.5
    qs = [q_ref[h] * scale for h in range(hds)]
    z = jnp.concatenate([jnp.sum(ref[0, h] * qs[h], axis=0, keepdims=True)
                         for ref in page_refs for h in range(hds)], axis=0)
    row = lax.broadcasted_iota(jnp.int32, (tk, tk), 0)
    col = lax.broadcasted_iota(jnp.int32, (tk, tk), 1)
    sp = _softplus(z)
    later = _split_dot(-sp, (row > col).astype(BF16))
    total = later[:, 0:1] - sp[:, 0:1]
    carry = carry_ref[...]
    for k, ref in enumerate(page_refs):
        rows = slice(k * hds, (k + 1) * hds)
        a = jnp.exp(z[rows] - sp[rows] + later[rows] + carry)
        carry = carry + total[rows]
        for h in range(hds):
            acc_ref[h] += ref[1, h] * a[h:h + 1, :]
    carry_ref[...] = carry

    @pl.when(p == pl.num_programs(1) - 1)
    def _():
        for h in range(hds):
            o_ref[h] = jnp.sum(acc_ref[h], axis=-1, keepdims=True)


def _page_specs(block, layer, n_pages, kind_block, descending):
    def spec(k):
        def index(bi, p, pt):
            pos = p * PAGES_PER_STEP + k
            pos = n_pages - 1 - pos if descending else pos
            return (layer, pt[bi, pos], kind_block) + (0,) * (len(block) - 3)
        return pl.BlockSpec(block, index)
    return [spec(k) for k in range(PAGES_PER_STEP)]


def sb_decode(q, pool_t, layer, page_table):
    b = q.shape[0]
    h, d = SB_HEADS, HEAD_DIM
    n_pages = page_table.shape[1]
    page = pool_t.shape[-1]
    assert n_pages % PAGES_PER_STEP == 0
    out = pl.pallas_call(
        _sb_decode_kernel,
        grid_spec=pltpu.PrefetchScalarGridSpec(
            num_scalar_prefetch=1,
            grid=(b, n_pages // PAGES_PER_STEP),
            in_specs=[pl.BlockSpec((None, h, d, 1), lambda bi, p, pt: (bi, 0, 0, 0))]
            + _page_specs((None, None, 2, h, d, page), layer, n_pages, 0, descending=True),
            out_specs=pl.BlockSpec((None, h, d, 1), lambda bi, p, pt: (bi, 0, 0, 0)),
            scratch_shapes=[pltpu.VMEM((h, 1), F32), pltpu.VMEM((h, d, page), F32)]),
        out_shape=jax.ShapeDtypeStruct((b, h, d, 1), F32),
        compiler_params=_params("parallel", "arbitrary"),
        name="sb_decode",
    )(page_table, q.reshape(b, h, d, 1), *([pool_t] * PAGES_PER_STEP))
    return out.reshape(b, h * d)


def _nsa_gather_kernel(pt_ref, *refs):
    page_refs, (o_ref, x_ref) = refs[:PAGES_PER_STEP], refs[PAGES_PER_STEP:]
    page = page_refs[0].shape[-1]
    grp, d = NSA_KV_GROUPS, HEAD_DIM
    n_out = page // CMP_STRIDE
    for k, ref in enumerate(page_refs):
        for kind in range(2):
            x = x_ref.at[2 * k + kind]
            x[...] = ref[kind].reshape(grp * d, page).T
            steps = [x[pl.ds(s, n_out, stride=CMP_STRIDE), :] for s in range(CMP_STRIDE)]
            for g in range(grp):
                o_ref[kind, g, k * n_out:(k + 1) * n_out, :] = jnp.concatenate(
                    [st[:, g * d:(g + 1) * d] for st in steps], axis=1)


def nsa_gather_rows16(pool_t, layer, page_table):
    b, n_pages = page_table.shape
    g, d = NSA_KV_GROUPS, HEAD_DIM
    page = pool_t.shape[-1]
    assert page % CMP_STRIDE == 0 and n_pages % PAGES_PER_STEP == 0
    n_out = PAGES_PER_STEP * (page // CMP_STRIDE)
    return pl.pallas_call(
        _nsa_gather_kernel,
        grid_spec=pltpu.PrefetchScalarGridSpec(
            num_scalar_prefetch=1,
            grid=(b, n_pages // PAGES_PER_STEP),
            in_specs=_page_specs((None, None, 2, g, d, page), layer, n_pages, 0, descending=False),
            out_specs=pl.BlockSpec((None, 2, g, n_out, CMP_STRIDE * d), lambda bi, p, pt: (bi, 0, 0, p, 0)),
            scratch_shapes=[pltpu.VMEM((2 * PAGES_PER_STEP, page, g * d), F32)]),
        out_shape=jax.ShapeDtypeStruct((b, 2, g, n_pages * (page // CMP_STRIDE), CMP_STRIDE * d), F32),
        compiler_params=_params("parallel", "arbitrary"),
        name="nsa_gather_rows16",
    )(page_table, *([pool_t] * PAGES_PER_STEP))


def _nsa_decode_select_kernel(q_ref, kc_ref, vc_ref, ov_ref, oc_ref, sel_ref, *, n_cmp, q_pos):
    rep, d = NSA_REP, HEAD_DIM
    n_rows = kc_ref.shape[1]
    n_blk = ov_ref.shape[1]
    q = (q_ref[...] * (d ** -0.5)).astype(BF16)
    cidx = lax.broadcasted_iota(jnp.int32, (1, n_rows), 1)
    vis = (cidx * CMP_STRIDE + (CMP_BLOCK - 1) <= q_pos) & (cidx < n_cmp)
    o_c, imp = [], []
    for g in range(NSA_KV_GROUPS):
        s = jnp.where(vis, _dot_nt(q[g * rep:(g + 1) * rep], kc_ref[g]), NEG)
        e = jnp.where(vis, jnp.exp(s - jnp.max(s, axis=-1, keepdims=True)), 0.0)
        p = e / jnp.maximum(jnp.sum(e, axis=-1, keepdims=True), 1e-30)
        o_c.append(jnp.dot(p.astype(BF16), vc_ref[g], preferred_element_type=F32))
        imp.append(_split_dot(jnp.sum(p, axis=0, keepdims=True), ov_ref[...]))
    oc_ref[...] = jnp.concatenate(o_c, axis=0)
    imp = jnp.concatenate(imp, axis=0)
    blk_i = lax.broadcasted_iota(jnp.int32, (1, n_blk), 1)
    forced = (blk_i == 0) | (blk_i > n_blk - SEL_LOCAL)
    sel = _top_blocks(jnp.where(forced, jnp.inf, imp), blk_i.astype(F32), min(SEL_TOP - 1, n_blk))
    sel_ref[...] = jnp.concatenate([sel, jnp.zeros((sel_ref.shape[0] - NSA_KV_GROUPS, n_blk), F32)], axis=0)


def _nsa_decode_attend_kernel(pt_ref, q_ref, sel_ref, *refs, win_skip):
    page_refs = refs[:PAGES_PER_STEP]
    new_ref, win_ref, oc_ref, gt_ref, o_ref, m_ref, acc_ref = refs[PAGES_PER_STEP:]
    p = pl.program_id(1)
    n_steps = pl.num_programs(1)
    rep, d, grp = NSA_REP, HEAD_DIM, NSA_KV_GROUPS
    tk = page_refs[0].shape[-1]
    lanes = PAGES_PER_STEP * tk
    scale = d ** -0.5
    qf = q_ref[...] * scale
    q = qf.astype(BF16)
    new = new_ref[...]
    new_row = lambda kind, g: new[kind * grp + g:kind * grp + g + 1, :]
    per_head = lambda f: jnp.concatenate([f(g) for g in range(grp)], axis=0)

    @pl.when(p == 0)
    def _():
        m_ref[...] = per_head(lambda g: jnp.sum(qf[g * rep:(g + 1) * rep] * new_row(2, g), axis=-1, keepdims=True))
        acc_ref[...] = per_head(lambda g: jnp.concatenate(
            [jnp.broadcast_to(new_row(3, g), (rep, d)), jnp.ones((rep, d), F32)], axis=1))

    n_blk = sel_ref.shape[1]
    lane = lax.broadcasted_iota(jnp.int32, (n_blk, lanes), 1)
    page_pos = (n_steps - p) * PAGES_PER_STEP - 1 - lax.shift_right_logical(lane, int(math.log2(tk)))
    blk_of_lane = page_pos * (tk // SEL_BLOCK) + lax.shift_right_logical(lane & (tk - 1), int(math.log2(SEL_BLOCK)))
    expand = (lax.broadcasted_iota(jnp.int32, (n_blk, lanes), 0) == blk_of_lane).astype(BF16)
    chosen = jnp.dot(sel_ref[...].astype(BF16), expand, preferred_element_type=F32)

    def scores(g):
        qg = q[g * rep:(g + 1) * rep]
        sg = jnp.concatenate([jnp.dot(qg, ref[0, g].astype(BF16), preferred_element_type=F32) for ref in page_refs],
                             axis=1)
        return jnp.where(chosen[g:g + 1, :] > 0.5, sg, NEG)

    s = per_head(scores)
    m_old = m_ref[...]
    m_new = jnp.maximum(m_old, jnp.max(s, axis=-1, keepdims=True))
    pr = jnp.exp(s - m_new).astype(BF16)

    def weighted_values(g):
        out = jnp.zeros((rep, 2 * d), F32)
        for k, ref in enumerate(page_refs):
            vt_ext = jnp.concatenate([ref[1, g].astype(BF16), jnp.ones((d, tk), BF16)], axis=0)
            out = out + _dot_nt(pr[g * rep:(g + 1) * rep, k * tk:(k + 1) * tk], vt_ext)
        return out

    acc_ref[...] = acc_ref[...] * jnp.exp(m_old - m_new) + per_head(weighted_values)
    m_ref[...] = m_new

    @pl.when(p == n_steps - 1)
    def _():
        acc = acc_ref[...]
        o_s = acc[:, :d] / acc[:, d:]
        wlen = win_ref.shape[-1]
        vis = lax.broadcasted_iota(jnp.int32, (1, wlen), 1) >= win_skip

        def window(g):
            qg = q[g * rep:(g + 1) * rep]
            s_w = jnp.where(vis, jnp.dot(qg, win_ref[0, g].astype(BF16), preferred_element_type=F32), NEG)
            s_n = jnp.sum(qf[g * rep:(g + 1) * rep] * new_row(4, g), axis=-1, keepdims=True)
            mx = jnp.maximum(jnp.max(s_w, axis=-1, keepdims=True), s_n)
            e_w = jnp.where(vis, jnp.exp(s_w - mx), 0.0)
            e_n = jnp.exp(s_n - mx)
            num = _dot_nt(e_w.astype(BF16), win_ref[1, g].astype(BF16)) + e_n * new_row(5, g)
            return num / (jnp.sum(e_w, axis=-1, keepdims=True) + e_n)

        o_w = per_head(window)
        gate = _sigmoid(gt_ref[...])
        o_ref[...] = gate[:, 0:1] * oc_ref[...] + gate[:, 1:2] * o_s + gate[:, 2:3] * o_w


def nsa_decode(nsa_q, nsa_g, nsa_kv_new, pool_t, win_t, layer, page_table, cmp_pos, cmp_w1, cmp_w2):
    b = nsa_q.shape[0]
    g, d, rep, hds = NSA_KV_GROUPS, HEAD_DIM, NSA_REP, NSA_HEADS
    n_pages = page_table.shape[1]
    page = pool_t.shape[-1]
    past = n_pages * page
    wlen = win_t.shape[-1]
    assert past % SEL_BLOCK == 0 and past % CMP_STRIDE == 0 and page % SEL_BLOCK == 0 and wlen <= past
    n_cmp = (past + 1 - CMP_BLOCK) // CMP_STRIDE + 1
    nr = past // CMP_STRIDE
    n_blk = past // SEL_BLOCK
    rows16 = nsa_gather_rows16(pool_t, layer, page_table)
    kcvc = nsa_compress(rows16, cmp_pos, cmp_w1, cmp_w2)
    overlap = _overlap_matrix(nr, n_cmp, n_blk)
    q3 = nsa_q.reshape(b, hds, d)
    per_b = lambda *dims: pl.BlockSpec((None,) + dims, lambda bi: (bi,) + (0,) * len(dims))
    o_c, sel = pl.pallas_call(
        functools.partial(_nsa_decode_select_kernel, n_cmp=n_cmp, q_pos=past),
        grid=(b,),
        in_specs=[per_b(hds, d),
                  pl.BlockSpec((None, None, g, nr, d), lambda bi: (bi, 0, 0, 0, 0)),
                  pl.BlockSpec((None, None, g, nr, d), lambda bi: (bi, 1, 0, 0, 0)),
                  pl.BlockSpec((nr, n_blk), lambda bi: (0, 0))],
        out_specs=[per_b(hds, d), per_b(8, n_blk)],
        out_shape=[jax.ShapeDtypeStruct((b, hds, d), F32), jax.ShapeDtypeStruct((b, 8, n_blk), F32)],
        compiler_params=_params("parallel"),
        name="nsa_decode_select",
    )(q3, kcvc, kcvc, overlap)
    gates = jnp.pad(nsa_g.reshape(b, hds, 3), ((0, 0), (0, 0), (0, V7X_LANES - 3)))
    fixed = lambda *dims: pl.BlockSpec((None,) + dims, lambda bi, p, pt: (bi,) + (0,) * len(dims))
    out = pl.pallas_call(
        functools.partial(_nsa_decode_attend_kernel, win_skip=wlen - WINDOW + 1),
        grid_spec=pltpu.PrefetchScalarGridSpec(
            num_scalar_prefetch=1,
            grid=(b, n_pages // PAGES_PER_STEP),
            in_specs=[fixed(hds, d), fixed(8, n_blk)]
            + _page_specs((None, None, 2, g, d, page), layer, n_pages, 1, descending=True)
            + [fixed(6 * g, d),
               pl.BlockSpec((None, None, 2, g, d, wlen), lambda bi, p, pt: (layer, bi, 0, 0, 0, 0)),
               fixed(hds, d), fixed(hds, V7X_LANES)],
            out_specs=fixed(hds, d),
            scratch_shapes=[pltpu.VMEM((hds, 1), F32), pltpu.VMEM((hds, 2 * d), F32)]),
        out_shape=jax.ShapeDtypeStruct((b, hds, d), F32),
        compiler_params=_params("parallel", "arbitrary"),
        name="nsa_decode_attend",
    )(page_table, q3, sel, *([pool_t] * PAGES_PER_STEP), nsa_kv_new.reshape(b, 6 * g, d), win_t, o_c, gates)
    return out.reshape(b, hds * d)


def _x_rms_norm(x, w):
    xf = x.astype(jnp.float32)
    y = xf * lax.rsqrt(jnp.mean(xf * xf, axis=-1, keepdims=True) + RMS_EPS)
    return (y * w.astype(jnp.float32)).astype(x.dtype)


def _x_masked_softmax(s, mask):
    s = jnp.where(mask, s.astype(jnp.float32), -jnp.inf)
    m = jnp.max(s, axis=-1, keepdims=True)
    m = jnp.where(jnp.isfinite(m), m, 0.0)
    e = jnp.where(mask, jnp.exp(s - m), 0.0)
    return e / jnp.maximum(jnp.sum(e, axis=-1, keepdims=True), 1e-30)


def _x_causal_dwconv(x, buf, w, b):
    xp = jnp.concatenate([buf.astype(x.dtype), x], axis=1)
    y = lax.conv_general_dilated(xp, w[:, None, :].astype(x.dtype), window_strides=(1,), padding='VALID',
                                 dimension_numbers=('NWC', 'WIO', 'NWC'), feature_group_count=x.shape[-1])
    return y + b.astype(x.dtype), xp[:, xp.shape[1] - (SSD_CONV - 1):]


def _x_ssd_scan(x, dt, a, bm, cm, h0):
    bsz, t = x.shape[:2]
    q = min(SSD_CHUNK, t)
    nc = t // q
    r = SSD_HEADS // SSD_GROUPS
    xdt = (x * dt[..., None]).reshape(bsz, nc, q, SSD_GROUPS, r, SSD_HEAD_DIM)
    acum = jnp.cumsum((dt * a).reshape(bsz, nc, q, SSD_GROUPS, r), axis=2)
    bm = bm.reshape(bsz, nc, q, SSD_GROUPS, SSD_STATE)
    cm = cm.reshape(bsz, nc, q, SSD_GROUPS, SSD_STATE)
    at = jnp.moveaxis(acum, 2, -1)
    tril = jnp.tril(jnp.ones((q, q), dtype=bool))
    decay = jnp.exp(jnp.where(tril, at[..., :, None] - at[..., None, :], -jnp.inf))
    cb = jnp.einsum('bclgn,bcsgn->bcgls', cm, bm)
    y_diag = jnp.einsum('bcgls,bcgrls,bcsgrp->bclgrp', cb, decay, xdt)
    to_end = jnp.exp(acum[:, :, -1:] - acum)
    states = jnp.einsum('bclgn,bclgr,bclgrp->bcgrpn', bm, to_end, xdt)
    chunk_decay = jnp.exp(acum[:, :, -1])

    def step(h, inp):
        dec, st = inp
        return dec[..., None, None] * h + st, h

    h_init = h0.reshape(bsz, SSD_GROUPS, r, SSD_HEAD_DIM, SSD_STATE)
    h_fin, h_in = lax.scan(step, h_init, (jnp.moveaxis(chunk_decay, 1, 0), jnp.moveaxis(states, 1, 0)))
    h_in = jnp.moveaxis(h_in, 0, 1)
    y_off = jnp.einsum('bclgn,bcgrpn,bclgr->bclgrp', cm, h_in, jnp.exp(acum))
    y = (y_diag + y_off).reshape(bsz, nc * q, SSD_HEADS, SSD_HEAD_DIM)[:, :t]
    return y, h_fin.reshape(bsz, SSD_HEADS, SSD_HEAD_DIM, SSD_STATE)


def _x_ssd_branch(z, xbc, dt_raw, conv_buf, h0, lp):
    bsz, t = z.shape[:2]
    xbc, new_buf = _x_causal_dwconv(xbc, conv_buf, lp['conv_w'], lp['conv_b'])
    xbc = jax.nn.silu(xbc)
    gn = SSD_GROUPS * SSD_STATE
    xs = xbc[..., :SSD_INNER].reshape(bsz, t, SSD_HEADS, SSD_HEAD_DIM).astype(jnp.float32)
    bm = xbc[..., SSD_INNER:SSD_INNER + gn].reshape(bsz, t, SSD_GROUPS, SSD_STATE).astype(jnp.float32)
    cm = xbc[..., SSD_INNER + gn:].reshape(bsz, t, SSD_GROUPS, SSD_STATE).astype(jnp.float32)
    dt = jax.nn.softplus((dt_raw + lp['dt_bias']).astype(jnp.float32))
    a = -jnp.exp(lp['a_log'].astype(jnp.float32))
    y, h_fin = _x_ssd_scan(xs, dt, a, bm, cm, h0.astype(jnp.float32))
    y = y + lp['d_skip'].astype(jnp.float32)[:, None] * xs
    y = y.reshape(bsz, t, SSD_INNER).astype(z.dtype)
    y = _x_rms_norm(y * jax.nn.silu(z), lp['ssd_norm'])
    return y, new_buf, h_fin.astype(h0.dtype)


def _x_stick_breaking(q, k, v, q_pos, k_pos):
    z = jnp.einsum('bqhd,bkhd->bhqk', q, k).astype(jnp.float32) * (HEAD_DIM ** -0.5)
    mask = k_pos[None, :] < q_pos[:, None]
    log_keep = jnp.where(mask, jax.nn.log_sigmoid(-z), 0.0)
    later = lax.cumsum(log_keep, axis=3, reverse=True) - log_keep
    a = jnp.where(mask, jnp.exp(jax.nn.log_sigmoid(z) + later), 0.0)
    return jnp.einsum('bhqk,bkhd->bqhd', a.astype(v.dtype), v)


def _x_compress_blocks(rows, pos_emb, w1, w2):
    bsz, t = rows.shape[:2]
    nc = (t - CMP_BLOCK) // CMP_STRIDE + 1
    starts = jnp.arange(nc) * CMP_STRIDE
    idx = starts[:, None] + jnp.arange(CMP_BLOCK)[None, :]
    blocks = rows[:, idx] + pos_emb[None, None, :, None, :].astype(rows.dtype)
    blocks = jnp.moveaxis(blocks, 3, 2).reshape(bsz, nc, NSA_KV_GROUPS, CMP_BLOCK * HEAD_DIM)
    return jax.nn.silu(blocks @ w1) @ w2, starts + CMP_BLOCK - 1


def _x_nsa_attend(q, gates, q_pos, kc, vc, c_end, ks, vs, kw, vw, kw_pos):
    bsz, tq = q.shape[:2]
    r = NSA_HEADS // NSA_KV_GROUPS
    scale = HEAD_DIM ** -0.5
    qg = q.reshape(bsz, tq, NSA_KV_GROUPS, r, HEAD_DIM)
    s_c = jnp.einsum('bqgrd,bcgd->bqgrc', qg, kc).astype(jnp.float32) * scale
    p_c = _x_masked_softmax(s_c, (c_end[None, :] <= q_pos[:, None])[None, :, None, None, :])
    o_c = jnp.einsum('bqgrc,bcgd->bqgrd', p_c.astype(vc.dtype), vc)
    n_blk = ks.shape[1] // SEL_BLOCK
    blk = jnp.arange(n_blk)
    c_start = c_end - (CMP_BLOCK - 1)
    overlap = ((c_start[:, None] < (blk[None, :] + 1) * SEL_BLOCK) & (c_end[:, None] >= blk[None, :] * SEL_BLOCK)).astype(jnp.float32)
    imp = jnp.einsum('bqgrc,cn->bqgn', p_c, overlap)
    cur = q_pos // SEL_BLOCK
    valid = blk[None, :] <= cur[:, None]
    forced = valid & ((blk[None, :] == 0) | (blk[None, :] > cur[:, None] - SEL_LOCAL))
    imp = jnp.where(forced[None, :, None, :], jnp.inf, jnp.where(valid[None, :, None, :], imp, -jnp.inf))
    n_top = min(SEL_TOP, n_blk)
    _, idx = lax.top_k(imp, n_top)
    kb = jnp.moveaxis(ks.reshape(bsz, n_blk, SEL_BLOCK, NSA_KV_GROUPS, HEAD_DIM), 3, 1)
    vb = jnp.moveaxis(vs.reshape(bsz, n_blk, SEL_BLOCK, NSA_KV_GROUPS, HEAD_DIM), 3, 1)
    idx_g = jnp.moveaxis(idx, 2, 1)
    take = jax.vmap(jax.vmap(lambda blocks, ids: blocks[ids]))
    gk = take(kb, idx_g)
    gv = take(vb, idx_g)
    sel_pos = idx_g[..., None] * SEL_BLOCK + jnp.arange(SEL_BLOCK)
    m_s = jnp.moveaxis(sel_pos <= q_pos[None, None, :, None, None], 1, 2)[:, :, :, None]
    s_s = jnp.einsum('bqgrd,bgqnsd->bqgrns', qg, gk).astype(jnp.float32) * scale
    nk = n_top * SEL_BLOCK
    p_s = _x_masked_softmax(s_s.reshape(bsz, tq, NSA_KV_GROUPS, r, nk), m_s.reshape(bsz, tq, NSA_KV_GROUPS, 1, nk))
    o_s = jnp.einsum('bqgrk,bgqkd->bqgrd', p_s.astype(gv.dtype), gv.reshape(bsz, NSA_KV_GROUPS, tq, nk, HEAD_DIM))
    s_w = jnp.einsum('bqgrd,bkgd->bqgrk', qg, kw).astype(jnp.float32) * scale
    m_w = (kw_pos[None, :] <= q_pos[:, None]) & (kw_pos[None, :] > q_pos[:, None] - WINDOW) & (kw_pos[None, :] >= 0)
    p_w = _x_masked_softmax(s_w, m_w[None, :, None, None, :])
    o_w = jnp.einsum('bqgrk,bkgd->bqgrd', p_w.astype(vw.dtype), vw)
    g = jax.nn.sigmoid(gates.reshape(bsz, tq, NSA_KV_GROUPS, r, 3).astype(jnp.float32)).astype(q.dtype)
    o = g[..., 0:1] * o_c + g[..., 1:2] * o_s + g[..., 2:3] * o_w
    return o.reshape(bsz, tq, NSA_WIDTH)


def _x_nsa_prompt(nq, gates, nkv, lp):
    bsz, t = nq.shape[:2]
    kc, c_end = _x_compress_blocks(nkv[:, :, 0], lp['cmp_pos'][0], lp['cmp_w1'][0], lp['cmp_w2'][0])
    vc, _ = _x_compress_blocks(nkv[:, :, 1], lp['cmp_pos'][1], lp['cmp_w1'][1], lp['cmp_w2'][1])
    ks, vs = nkv[:, :, 2], nkv[:, :, 3]
    w_pad = jnp.pad(nkv[:, :, 4:6], ((0, 0), (WINDOW, 0), (0, 0), (0, 0), (0, 0)))
    nb = t // Q_BLOCK
    qb = jnp.moveaxis(nq.reshape(bsz, nb, Q_BLOCK, NSA_HEADS, HEAD_DIM), 1, 0)
    gb = jnp.moveaxis(gates.reshape(bsz, nb, Q_BLOCK, 3 * NSA_HEADS), 1, 0)

    def blk(args):
        qi, gi, i = args
        t0 = i * Q_BLOCK
        wi = lax.dynamic_slice_in_dim(w_pad, t0, WINDOW + Q_BLOCK, axis=1)
        kw_pos = t0 - WINDOW + jnp.arange(WINDOW + Q_BLOCK)
        return _x_nsa_attend(qi, gi, t0 + jnp.arange(Q_BLOCK), kc, vc, c_end, ks, vs, wi[:, :, 0], wi[:, :, 1], kw_pos)

    out = lax.map(blk, (qb, gb, jnp.arange(nb)))
    return jnp.moveaxis(out, 0, 1).reshape(bsz, t, NSA_WIDTH)


def _col_offsets():
    offs, s = [], 0
    for n in IN_SPLITS:
        offs.append(s)
        s += n
    return offs


def _layer_weights(l, p):
    o = _col_offsets()
    w_in = p['w_in'][l]
    cols = lambda a, n: w_in[:, a:a + n]
    small = jnp.concatenate([cols(o[2], SSD_HEADS), cols(o[6], 3 * NSA_HEADS)], axis=1)
    small = jnp.pad(small, ((0, 0), (0, V7X_LANES - small.shape[1])))
    bf = lambda a: a.astype(BF16)
    return {
        'w_z': bf(cols(o[0], SSD_INNER)), 'w_xbc': bf(cols(o[1], SSD_CONV_DIM)), 'w_small': bf(small),
        'w_sbq': bf(cols(o[3], SB_WIDTH)), 'w_sbkv': bf(cols(o[3] + SB_WIDTH, 2 * SB_WIDTH)),
        'w_nq': bf(cols(o[4], NSA_WIDTH)), 'w_nkv': bf(cols(o[5], 6 * NSA_KV_WIDTH)),
        'w_brg': bf(cols(o[7], N_BRANCH * D_MODEL)),
        'mix_pre': p['norm_mix_pre'][l], 'mix_post': p['norm_mix_post'][l],
        'ffn_pre': p['norm_ffn_pre'][l], 'ffn_post': p['norm_ffn_post'][l],
        'conv_w': p['ssd_conv_w'][l], 'conv_b': p['ssd_conv_b'][l], 'dt_bias': p['ssd_dt_bias'][l],
        'a_log': p['ssd_a_log'][l], 'd_skip': p['ssd_d'][l], 'ssd_norm': p['ssd_norm'][l],
        'w_ssd_out': bf(p['w_ssd_out'][l]), 'w_sb_out': bf(p['w_sb_out'][l]), 'w_nsa_out': bf(p['w_nsa_out'][l]),
        'w_o': bf(p['w_o'][l]), 'w_ffn_gate': bf(p['w_ffn_gate'][l]), 'w_ffn_up': bf(p['w_ffn_up'][l]),
        'w_ffn_down': bf(p['w_ffn_down'][l]),
        'cmp_pos': p['nsa_cmp_pos'][l], 'cmp_w1': p['nsa_cmp_w1'][l], 'cmp_w2': p['nsa_cmp_w2'][l],
    }


def _trunk_tail(x, ssd_y, sb_o, nsa_o, br_g, lw):
    b, t, d = x.shape
    m = b * t
    x1 = merge_branches(x.reshape(m, d), ssd_y.reshape(m, -1), sb_o.reshape(m, -1), nsa_o.reshape(m, -1),
                        br_g.reshape(m, -1), lw['w_ssd_out'], lw['w_sb_out'], lw['w_nsa_out'], lw['w_o'],
                        lw['mix_post'])
    x2 = ffn(x1, lw['ffn_pre'], lw['ffn_post'], lw['w_ffn_gate'], lw['w_ffn_up'], lw['w_ffn_down'])
    return x2.reshape(b, t, d)


def _layer_prompt(x, lw):
    b, t, _ = x.shape
    g = lw['mix_pre']
    z = norm_matmul(x, g, lw['w_z'])
    xbc = norm_matmul(x, g, lw['w_xbc'])
    small = norm_matmul(x, g, lw['w_small'])
    br_g = norm_matmul(x, g, lw['w_brg'])
    sb_q = norm_matmul(x, g, lw['w_sbq'], out_dtype=BF16)
    nsa_q = norm_matmul(x, g, lw['w_nq'])
    sb_kvt = norm_matmul(x, g, lw['w_sbkv'].T, transposed=True)
    nsa_kvt = norm_matmul(x, g, lw['w_nkv'].T, transposed=True)
    dt_raw = small[..., :SSD_HEADS]
    nsa_g = small[..., SSD_HEADS:SSD_HEADS + 3 * NSA_HEADS]

    h0 = jnp.zeros((b, SSD_HEADS, SSD_HEAD_DIM, SSD_STATE), F32)
    conv0 = jnp.zeros((b, SSD_CONV - 1, SSD_CONV_DIM), F32)
    ssd_y, h_new = ssd_prompt(xbc, z, dt_raw, h0, conv0, lw['conv_w'], lw['conv_b'], lw['dt_bias'], lw['a_log'],
                              lw['d_skip'], lw['ssd_norm'])
    conv_new = xbc[:, t - (SSD_CONV - 1):, :]
    sb_o = sb_prompt(sb_q, sb_kvt)

    nsa_o = nsa_prompt(nsa_q, nsa_g, nsa_kvt, lw['cmp_pos'], lw['cmp_w1'], lw['cmp_w2'])

    y = _trunk_tail(x, ssd_y, sb_o, nsa_o, br_g, lw)
    sb_kv = jnp.moveaxis(sb_kvt.reshape(b, 2, SB_HEADS, HEAD_DIM, t), 4, 1)
    nsa_all = jnp.moveaxis(nsa_kvt.reshape(b, 6, NSA_KV_GROUPS, HEAD_DIM, t), 4, 1)
    keep = min(WINDOW, t)
    return y, sb_kv, nsa_all[:, :, 0:4], nsa_all[:, t - keep:, 4:6], h_new, conv_new


def _layer_sample(x, lw, layer, sb_pool_t, nsa_pool_t, win_t, h0, conv_buf, page_table):
    bsz, t = x.shape[:2]
    assert t == 1
    past = page_table.shape[1] * sb_pool_t.shape[-1]
    xr = x.reshape(1, bsz, D_MODEL)
    g = lw['mix_pre']
    pr = lambda w: norm_matmul(xr, g, w)[0]
    z, xbc, small, br_g = pr(lw['w_z']), pr(lw['w_xbc']), pr(lw['w_small']), pr(lw['w_brg'])
    sb_q, sb_kv, nsa_q, nsa_kv = pr(lw['w_sbq']), pr(lw['w_sbkv']), pr(lw['w_nq']), pr(lw['w_nkv'])
    dt_raw = small[:, :SSD_HEADS]
    nsa_g = small[:, SSD_HEADS:SSD_HEADS + 3 * NSA_HEADS]
    ssd_y, h_new = ssd_step(xbc, z, dt_raw, h0, conv_buf, lw['conv_w'], lw['conv_b'], lw['dt_bias'], lw['a_log'],
                            lw['d_skip'], lw['ssd_norm'])
    conv_new = jnp.concatenate([conv_buf[:, 1:], xbc[:, None, :]], axis=1)
    sb_o = sb_decode(sb_q, sb_pool_t, layer, page_table)
    nsa_o = nsa_decode(nsa_q, nsa_g, nsa_kv, nsa_pool_t, win_t, layer, page_table, lw['cmp_pos'], lw['cmp_w1'],
                       lw['cmp_w2'])
    y = _trunk_tail(x, ssd_y[:, None], sb_o.astype(BF16)[:, None], nsa_o.astype(BF16)[:, None], br_g[:, None], lw)
    kv_new = sb_kv.reshape(bsz, 1, 2, SB_HEADS, HEAD_DIM)
    nkv = nsa_kv.reshape(bsz, 1, 6, NSA_KV_GROUPS, HEAD_DIM)
    keep = min(WINDOW, past + 1)
    win_all_t = jnp.concatenate([win_t[layer], nkv[:, 0, 4:6][..., None]], axis=-1)
    win_new = jnp.moveaxis(win_all_t[..., win_all_t.shape[-1] - keep:], 4, 1)
    return y, kv_new, nkv[:, :, 0:4], win_new, h_new, conv_new


def kernel(x_prompt, x_sample, cache_sb_kv, cache_nsa_kv, cache_nsa_win, state_ssd, state_conv, page_table,
           norm_mix_pre, norm_mix_post, norm_ffn_pre, norm_ffn_post, w_in, ssd_conv_w, ssd_conv_b, ssd_dt_bias,
           ssd_a_log, ssd_d, ssd_norm, w_ssd_out, w_sb_out, nsa_cmp_pos, nsa_cmp_w1, nsa_cmp_w2, w_nsa_out, w_o,
           w_ffn_gate, w_ffn_up, w_ffn_down):
    p = dict(norm_mix_pre=norm_mix_pre, norm_mix_post=norm_mix_post, norm_ffn_pre=norm_ffn_pre,
             norm_ffn_post=norm_ffn_post, w_in=w_in, ssd_conv_w=ssd_conv_w, ssd_conv_b=ssd_conv_b,
             ssd_dt_bias=ssd_dt_bias, ssd_a_log=ssd_a_log, ssd_d=ssd_d, ssd_norm=ssd_norm, w_ssd_out=w_ssd_out,
             w_sb_out=w_sb_out, nsa_cmp_pos=nsa_cmp_pos, nsa_cmp_w1=nsa_cmp_w1, nsa_cmp_w2=nsa_cmp_w2,
             w_nsa_out=w_nsa_out, w_o=w_o, w_ffn_gate=w_ffn_gate, w_ffn_up=w_ffn_up, w_ffn_down=w_ffn_down)
    yp, ys = x_prompt, x_sample
    outs_p, outs_s = [], []
    time_minor = lambda a: jnp.transpose(a, (0, 1, 3, 4, 5, 2))
    sb_pool_t, nsa_pool_t, win_t = time_minor(cache_sb_kv), time_minor(cache_nsa_kv), time_minor(cache_nsa_win)
    for l in range(w_in.shape[0]):
        lw = _layer_weights(l, p)
        res = _layer_prompt(yp, lw)
        yp = res[0]
        outs_p.append(res[1:])
        res = _layer_sample(ys, lw, l, sb_pool_t, nsa_pool_t, win_t, state_ssd[l], state_conv[l], page_table)
        ys = res[0]
        outs_s.append(res[1:])
    st = lambda outs, i: jnp.stack([o[i] for o in outs])
    return (yp, ys, st(outs_p, 0), st(outs_s, 0), st(outs_p, 1), st(outs_s, 1), st(outs_p, 2), st(outs_s, 2),
            st(outs_p, 3), st(outs_s, 3), st(outs_p, 4), st(outs_s, 4))
```

```python
import functools
import math

import jax
import jax.numpy as jnp
from jax import lax
from jax.experimental import pallas as pl
from jax.experimental.pallas import tpu as pltpu

D_MODEL = 1024
HEAD_DIM = 64
SSD_INNER = D_MODEL
SSD_HEAD_DIM = 64
SSD_HEADS = SSD_INNER // SSD_HEAD_DIM
SSD_GROUPS = 2
SSD_STATE = 128
SSD_CONV = 4
SSD_GN = SSD_GROUPS * SSD_STATE
SSD_CONV_DIM = SSD_INNER + 2 * SSD_GN
SSD_CHUNK = 128
SB_HEADS = 8
SB_WIDTH = SB_HEADS * HEAD_DIM
NSA_HEADS = 8
NSA_KV_GROUPS = 2
NSA_REP = NSA_HEADS // NSA_KV_GROUPS
NSA_WIDTH = NSA_HEADS * HEAD_DIM
NSA_KV_WIDTH = NSA_KV_GROUPS * HEAD_DIM
CMP_BLOCK = 32
CMP_STRIDE = 16
CMP_HIDDEN = 128
SEL_BLOCK = 64
SEL_TOP = 16
SEL_LOCAL = 2
WINDOW = 512
Q_BLOCK = 128
N_BRANCH = 3
FFN_HIDDEN = ((8 * D_MODEL + 3 * 256 - 1) // (3 * 256)) * 256
RMS_EPS = 1e-6
IN_SPLITS = (SSD_INNER, SSD_CONV_DIM, SSD_HEADS, 3 * SB_WIDTH, NSA_WIDTH, 6 * NSA_KV_WIDTH, 3 * NSA_HEADS,
             N_BRANCH * D_MODEL)

V7X_LANES = 128
V7X_VMEM_LIMIT = 56 * 1024 * 1024
PAGES_PER_STEP = 8
NEG = -1e30
BF16 = jnp.bfloat16
F32 = jnp.float32
HI = lax.Precision.HIGHEST


def _params(*sem):
    return pltpu.CompilerParams(dimension_semantics=sem, vmem_limit_bytes=V7X_VMEM_LIMIT)


def _pick(n, cands):
    for c in cands:
        if n % c == 0:
            return c
    return n


def _rms(x, w):
    return x * lax.rsqrt(jnp.mean(x * x, axis=-1, keepdims=True) + RMS_EPS) * w


def _softplus(x):
    return jnp.maximum(x, 0.0) + jnp.log1p(jnp.exp(-jnp.abs(x)))


def _sigmoid(x):
    return 1.0 / (1.0 + jnp.exp(-x))


def _dot_nt(a, b):
    return lax.dot_general(a, b, (((1,), (1,)), ((), ())), preferred_element_type=F32)


def _dot_tn(a, b):
    return lax.dot_general(a, b, (((0,), (0,)), ((), ())), preferred_element_type=F32)


def _norm_mm_kernel(x_ref, g_ref, w_ref, o_ref, h_ref, *, transposed):
    @pl.when(pl.program_id(2) == 0)
    def _():
        h_ref[...] = _rms(x_ref[...], g_ref[...]).astype(BF16)

    if transposed:
        o_ref[...] = _dot_nt(w_ref[...], h_ref[...]).astype(o_ref.dtype)
    else:
        o_ref[...] = jnp.dot(h_ref[...], w_ref[...], preferred_element_type=F32).astype(o_ref.dtype)


def norm_matmul(x, gain, w, out_dtype=F32, transposed=False):
    b, t, k = x.shape
    n = w.shape[0] if transposed else w.shape[1]
    tm = _pick(t, (1024, 512, 256, 128))
    tn = _pick(n, (1024, 768, 512, 256, 128))
    if transposed:
        w_spec = pl.BlockSpec((tn, k), lambda bi, i, j: (j, 0))
        o_spec = pl.BlockSpec((None, tn, tm), lambda bi, i, j: (bi, j, i))
        o_shape = (b, n, t)
    else:
        w_spec = pl.BlockSpec((k, tn), lambda bi, i, j: (0, j))
        o_spec = pl.BlockSpec((None, tm, tn), lambda bi, i, j: (bi, i, j))
        o_shape = (b, t, n)
    return pl.pallas_call(
        functools.partial(_norm_mm_kernel, transposed=transposed),
        grid=(b, t // tm, n // tn),
        in_specs=[pl.BlockSpec((None, tm, k), lambda bi, i, j: (bi, i, 0)),
                  pl.BlockSpec((1, k), lambda bi, i, j: (0, 0)),
                  w_spec],
        out_specs=o_spec,
        out_shape=jax.ShapeDtypeStruct(o_shape, out_dtype),
        scratch_shapes=[pltpu.VMEM((tm, k), BF16)],
        compiler_params=_params("parallel", "parallel", "arbitrary"),
        name="norm_matmul_t" if transposed else "norm_matmul",
    )(x, gain.reshape(1, k), w)


def _ssd_chunk_kernel(xbc_ref, z_ref, dt_ref, dtt_ref, h0_ref, c0_ref, cw_ref, cb_ref, dtb_ref, dtbt_ref,
                      alog_ref, alogt_ref, dfull_ref, nw_ref, exp_ref, y_ref, h_ref, xp_ref):
    q = SSD_CHUNK
    c = pl.program_id(1)

    @pl.when(c == 0)
    def _():
        h_ref[...] = h0_ref[...]
        xp_ref[5:8, :] = c0_ref[...]

    xp_ref[8:8 + q, :] = xbc_ref[...]
    conv = cb_ref[...]
    for j in range(SSD_CONV):
        conv = conv + cw_ref[j:j + 1, :] * xp_ref[5 + j:5 + j + q, :]
    xp_ref[5:8, :] = xp_ref[q + 5:q + 8, :]
    u = conv * _sigmoid(conv)
    xs = u[:, :SSD_INNER]
    bm = u[:, SSD_INNER:SSD_INNER + SSD_GN].astype(BF16)
    cm = u[:, SSD_INNER + SSD_GN:].astype(BF16)

    dt = _softplus(dt_ref[...] + dtb_ref[...])
    dtt = _softplus(dtt_ref[...] + dtbt_ref[...])
    dta = dt * (-jnp.exp(alog_ref[...]))
    dtat = dtt * (-jnp.exp(alogt_ref[...]))
    row = lax.broadcasted_iota(jnp.int32, (q, q), 0)
    col = lax.broadcasted_iota(jnp.int32, (q, q), 1)
    tril = row >= col
    acum = jnp.dot(tril.astype(F32), dta, precision=HI, preferred_element_type=F32)
    acumt = jnp.dot(dtat, (row <= col).astype(F32), precision=HI, preferred_element_type=F32)
    expand = exp_ref[...]
    dt_full = jnp.dot(dt, expand, precision=HI, preferred_element_type=F32)
    ea_full = jnp.dot(jnp.exp(acum), expand, precision=HI, preferred_element_type=F32)
    te_full = jnp.dot(jnp.exp(acum[q - 1:q, :] - acum), expand, precision=HI, preferred_element_type=F32)
    xdt = xs * dt_full
    xdt_b = xdt.astype(BF16)
    xw_b = (xdt * te_full).astype(BF16)

    r = SSD_HEADS // SSD_GROUPS
    gw = r * SSD_HEAD_DIM
    y_diag, y_off = [], []
    for g in range(SSD_GROUPS):
        cm_g = cm[:, g * SSD_STATE:(g + 1) * SSD_STATE]
        bm_g = bm[:, g * SSD_STATE:(g + 1) * SSD_STATE]
        cb = _dot_nt(cm_g, bm_g)
        h_g = h_ref[g * r:(g + 1) * r].reshape(gw, SSD_STATE)
        y_off.append(_dot_nt(cm_g, h_g.astype(BF16)))
        st = _dot_tn(xw_b[:, g * gw:(g + 1) * gw], bm_g)
        for hh in range(r):
            hd = g * r + hh
            seg = acum[:, hd:hd + 1] - acumt[hd:hd + 1, :]
            decay = jnp.exp(jnp.where(tril, seg, -jnp.inf))
            m = (cb * decay).astype(BF16)
            y_diag.append(jnp.dot(m, xdt_b[:, hd * SSD_HEAD_DIM:(hd + 1) * SSD_HEAD_DIM],
                                  preferred_element_type=F32))
            dec = jnp.exp(acumt[hd:hd + 1, q - 1:q])
            h_ref[hd] = dec * h_ref[hd] + st[hh * SSD_HEAD_DIM:(hh + 1) * SSD_HEAD_DIM, :]
    y = (jnp.concatenate(y_diag, axis=1) + jnp.concatenate(y_off, axis=1) * ea_full
         + dfull_ref[...] * xs)
    zz = z_ref[...]
    y = y * (zz * _sigmoid(zz))
    y_ref[...] = _rms(y, nw_ref[...]).astype(y_ref.dtype)


def ssd_prompt(xbc, z, dt_raw, h0, conv0, conv_w, conv_b, dt_bias, a_log, d_skip, norm_w):
    b, t, _ = xbc.shape
    q = SSD_CHUNK
    nc = t // q
    hds = SSD_HEADS
    expand = (jnp.arange(SSD_INNER)[None, :] // SSD_HEAD_DIM == jnp.arange(hds)[:, None]).astype(F32)
    d_full = jnp.repeat(d_skip, SSD_HEAD_DIM).reshape(1, SSD_INNER)
    dtt = jnp.swapaxes(dt_raw, 1, 2)
    full = lambda shape: pl.BlockSpec(shape, lambda bi, ci: (0,) * len(shape))
    y, h = pl.pallas_call(
        _ssd_chunk_kernel,
        grid=(b, nc),
        in_specs=[pl.BlockSpec((None, q, SSD_CONV_DIM), lambda bi, ci: (bi, ci, 0)),
                  pl.BlockSpec((None, q, SSD_INNER), lambda bi, ci: (bi, ci, 0)),
                  pl.BlockSpec((None, q, hds), lambda bi, ci: (bi, ci, 0)),
                  pl.BlockSpec((None, hds, q), lambda bi, ci: (bi, 0, ci)),
                  pl.BlockSpec((None, hds, SSD_HEAD_DIM, SSD_STATE), lambda bi, ci: (bi, 0, 0, 0)),
                  pl.BlockSpec((None, SSD_CONV - 1, SSD_CONV_DIM), lambda bi, ci: (bi, 0, 0)),
                  full((SSD_CONV, SSD_CONV_DIM)), full((1, SSD_CONV_DIM)),
                  full((1, hds)), full((hds, 1)), full((1, hds)), full((hds, 1)),
                  full((1, SSD_INNER)), full((1, SSD_INNER)), full((hds, SSD_INNER))],
        out_specs=[pl.BlockSpec((None, q, SSD_INNER), lambda bi, ci: (bi, ci, 0)),
                   pl.BlockSpec((None, hds, SSD_HEAD_DIM, SSD_STATE), lambda bi, ci: (bi, 0, 0, 0))],
        out_shape=[jax.ShapeDtypeStruct((b, t, SSD_INNER), BF16),
                   jax.ShapeDtypeStruct((b, hds, SSD_HEAD_DIM, SSD_STATE), F32)],
        scratch_shapes=[pltpu.VMEM((q + 8, SSD_CONV_DIM), F32)],
        compiler_params=_params("parallel", "arbitrary"),
        name="ssd_chunk_scan",
    )(xbc, z, dt_raw, dtt, h0, conv0, conv_w, conv_b.reshape(1, -1), dt_bias.reshape(1, hds),
      dt_bias.reshape(hds, 1), a_log.reshape(1, hds), a_log.reshape(hds, 1), d_full, norm_w.reshape(1, -1), expand)
    return y, h


def _sb_stage(z):
    sp = jnp.maximum(z, 0.0) + jnp.log(1.0 + jnp.exp(-jnp.abs(z)))
    hi = sp.astype(BF16)
    return z - sp, hi, (sp - hi.astype(F32)).astype(BF16), sp[:, 0:1]


def _sb_prompt_kernel(q_ref, kt_ref, vt_ref, o_ref, z_ref, d_ref, hl_ref, a_ref, *, tb):
    i = pl.program_id(2)
    q = (q_ref[...] * (HEAD_DIM ** -0.5)).astype(BF16)
    row = lax.broadcasted_iota(jnp.int32, (tb, tb), 0)
    col = lax.broadcasted_iota(jnp.int32, (tb, tb), 1)
    upper = jnp.where(row > col, -1.0, 0.0).astype(BF16)
    upper2 = jnp.concatenate([upper, upper], axis=0)

    def tile(ref, s):
        off = pl.multiple_of(jnp.clip(i - s, 0, i) * tb, tb)
        return ref[:, pl.ds(off, tb)].astype(BF16)

    z0 = jnp.where(col < row, jnp.dot(q, tile(kt_ref, 0), preferred_element_type=F32), NEG)
    d_ref[0], hl_ref[0, :, :tb], hl_ref[0, :, tb:], first0 = _sb_stage(z0)
    z_ref[0] = jnp.dot(q, tile(kt_ref, 1), preferred_element_type=F32)
    a_ref[0] = jnp.zeros((tb, tb), BF16)

    def step(n, rd, wr, st):
        carry, first, acc = st
        later = jnp.dot(hl_ref[rd], upper2, preferred_element_type=F32)
        acc = acc + _dot_nt(a_ref[rd], tile(vt_ref, n - 1))
        z_ref[wr] = jnp.dot(q, tile(kt_ref, n + 2), preferred_element_type=F32)
        d_ref[wr], hl_ref[wr, :, :tb], hl_ref[wr, :, tb:], first_new = _sb_stage(z_ref[rd])
        dead = jnp.where(n <= i, 0.0, NEG)
        a_ref[wr] = jnp.exp(d_ref[rd] + later + (carry + dead)).astype(BF16)
        return carry + later[:, 0:1] - first, first_new, acc

    def body(pair, st):
        return step(2 * pair + 1, 1, 0, step(2 * pair, 0, 1, st))

    st = (jnp.zeros((tb, 1), F32), first0, jnp.zeros((tb, HEAD_DIM), F32))
    _, _, acc = lax.fori_loop(0, (i + 3) >> 1, body, st)
    o_ref[...] = acc.astype(o_ref.dtype)


def sb_prompt(q, kvt, tb=256):
    b, t, _ = q.shape
    h = SB_HEADS
    tb = min(tb, t)
    assert t % tb == 0
    qh = jnp.swapaxes(q.reshape(b, t, h, HEAD_DIM), 1, 2)
    out = pl.pallas_call(
        functools.partial(_sb_prompt_kernel, tb=tb),
        grid=(b, h, t // tb),
        in_specs=[pl.BlockSpec((None, None, tb, HEAD_DIM), lambda bi, hi, i: (bi, hi, i, 0)),
                  pl.BlockSpec((None, HEAD_DIM, t), lambda bi, hi, i: (bi, hi, 0)),
                  pl.BlockSpec((None, HEAD_DIM, t), lambda bi, hi, i: (bi, h + hi, 0))],
        out_specs=pl.BlockSpec((None, None, tb, HEAD_DIM), lambda bi, hi, i: (bi, hi, i, 0)),
        out_shape=jax.ShapeDtypeStruct((b, h, t, HEAD_DIM), BF16),
        scratch_shapes=[pltpu.VMEM((2, tb, tb), F32), pltpu.VMEM((2, tb, tb), F32), pltpu.VMEM((2, tb, 2 * tb), BF16),
                        pltpu.VMEM((2, tb, tb), BF16)],
        compiler_params=_params("parallel", "parallel", "arbitrary"),
        name="sb_prompt",
    )(qh, kvt, kvt)
    return jnp.swapaxes(out, 1, 2).reshape(b, t, h * HEAD_DIM)


def _merge_kernel(x_ref, ssd_ref, sb_ref, nsa_ref, gl_ref, wssd_ref, wsb_ref, wnsa_ref, wo_ref, nw_ref, o_ref):
    d = D_MODEL
    gl = gl_ref[...]
    merged = (_sigmoid(gl[:, :d]) * jnp.dot(ssd_ref[...], wssd_ref[...], preferred_element_type=F32)
              + _sigmoid(gl[:, d:2 * d]) * jnp.dot(sb_ref[...], wsb_ref[...], preferred_element_type=F32)
              + _sigmoid(gl[:, 2 * d:]) * jnp.dot(nsa_ref[...], wnsa_ref[...], preferred_element_type=F32))
    y = jnp.dot(merged.astype(BF16), wo_ref[...], preferred_element_type=F32)
    o_ref[...] = x_ref[...] + _rms(y, nw_ref[...])


def merge_branches(x, ssd_y, sb_o, nsa_o, gate_logits, w_ssd_out, w_sb_out, w_nsa_out, w_o, norm_w):
    m, d = x.shape
    tm = _pick(m, (512, 256, 128, 32))
    rows = lambda n: pl.BlockSpec((tm, n), lambda i: (i, 0))
    full = lambda a: pl.BlockSpec(a.shape, lambda i: (0, 0))
    nw = norm_w.reshape(1, d)
    return pl.pallas_call(
        _merge_kernel,
        grid=(m // tm,),
        in_specs=[rows(d), rows(ssd_y.shape[1]), rows(sb_o.shape[1]), rows(nsa_o.shape[1]), rows(N_BRANCH * d),
                  full(w_ssd_out), full(w_sb_out), full(w_nsa_out), full(w_o), full(nw)],
        out_specs=rows(d),
        out_shape=jax.ShapeDtypeStruct((m, d), F32),
        compiler_params=_params("parallel"),
        name="merge_branches",
    )(x, ssd_y, sb_o, nsa_o, gate_logits, w_ssd_out, w_sb_out, w_nsa_out, w_o, nw)


def _ffn_up_kernel(x_ref, g_ref, wg_ref, wu_ref, o_ref, h_ref):
    @pl.when(pl.program_id(1) == 0)
    def _():
        h_ref[...] = _rms(x_ref[...], g_ref[...]).astype(BF16)

    h = h_ref[...]
    a = jnp.dot(h, wg_ref[...], preferred_element_type=F32)
    u = jnp.dot(h, wu_ref[...], preferred_element_type=F32)
    o_ref[...] = (a * _sigmoid(a) * u).astype(o_ref.dtype)


def _ffn_down_kernel(a_ref, x_ref, wd_ref, nw_ref, o_ref):
    f = jnp.dot(a_ref[...], wd_ref[...], preferred_element_type=F32)
    o_ref[...] = x_ref[...] + _rms(f, nw_ref[...])


def ffn(x, pre_w, post_w, w_gate, w_up, w_down):
    m, d = x.shape
    f = w_gate.shape[1]
    tm = _pick(m, (1024, 512, 256, 128, 32))
    tn = _pick(f, (256, 128))
    act = pl.pallas_call(
        _ffn_up_kernel,
        grid=(m // tm, f // tn),
        in_specs=[pl.BlockSpec((tm, d), lambda i, j: (i, 0)),
                  pl.BlockSpec((1, d), lambda i, j: (0, 0)),
                  pl.BlockSpec((d, tn), lambda i, j: (0, j)),
                  pl.BlockSpec((d, tn), lambda i, j: (0, j))],
        out_specs=pl.BlockSpec((tm, tn), lambda i, j: (i, j)),
        out_shape=jax.ShapeDtypeStruct((m, f), BF16),
        scratch_shapes=[pltpu.VMEM((tm, d), BF16)],
        compiler_params=_params("parallel", "arbitrary"),
        name="ffn_up",
    )(x, pre_w.reshape(1, d), w_gate, w_up)
    tm2 = _pick(m, (512, 256, 128, 32))
    return pl.pallas_call(
        _ffn_down_kernel,
        grid=(m // tm2,),
        in_specs=[pl.BlockSpec((tm2, f), lambda i: (i, 0)),
                  pl.BlockSpec((tm2, d), lambda i: (i, 0)),
                  pl.BlockSpec((f, d), lambda i: (0, 0)),
                  pl.BlockSpec((1, d), lambda i: (0, 0))],
        out_specs=pl.BlockSpec((tm2, d), lambda i: (i, 0)),
        out_shape=jax.ShapeDtypeStruct((m, d), F32),
        compiler_params=_params("parallel"),
        name="ffn_down",
    )(act, x, w_down, post_w.reshape(1, d))


def _nsa_compress_kernel(r_ref, pos_ref, w1_ref, w2_ref, o_ref, sh_ref):
    nr = r_ref.shape[0]
    half = CMP_STRIDE * HEAD_DIM
    r = r_ref[...]
    top = jnp.dot((r + pos_ref[0:1, :]).astype(BF16), w1_ref[:half, :], preferred_element_type=F32)
    bot = jnp.dot((r + pos_ref[1:2, :]).astype(BF16), w1_ref[half:, :], preferred_element_type=F32)
    sh_ref[0:nr, :] = bot
    sh_ref[nr:nr + 8, :] = jnp.zeros((8, CMP_HIDDEN), F32)
    pre = top + sh_ref[1:nr + 1, :]
    hid = pre * _sigmoid(pre)
    o_ref[...] = jnp.dot(hid.astype(BF16), w2_ref[...], preferred_element_type=F32).astype(o_ref.dtype)


def nsa_compress(rows16, cmp_pos, cmp_w1, cmp_w2):
    assert CMP_BLOCK == 2 * CMP_STRIDE
    b, _, g, nr, w = rows16.shape
    pos = cmp_pos.reshape(2, 2, w)
    return pl.pallas_call(
        _nsa_compress_kernel,
        grid=(b, 2, g),
        in_specs=[pl.BlockSpec((None, None, None, nr, w), lambda bi, ki, gi: (bi, ki, gi, 0, 0)),
                  pl.BlockSpec((None, 2, w), lambda bi, ki, gi: (ki, 0, 0)),
                  pl.BlockSpec((None, 2 * w, CMP_HIDDEN), lambda bi, ki, gi: (ki, 0, 0)),
                  pl.BlockSpec((None, CMP_HIDDEN, HEAD_DIM), lambda bi, ki, gi: (ki, 0, 0))],
        out_specs=pl.BlockSpec((None, None, None, nr, HEAD_DIM), lambda bi, ki, gi: (bi, ki, gi, 0, 0)),
        out_shape=jax.ShapeDtypeStruct((b, 2, g, nr, HEAD_DIM), BF16),
        scratch_shapes=[pltpu.VMEM((nr + 8, CMP_HIDDEN), F32)],
        compiler_params=_params("parallel", "parallel", "parallel"),
        name="nsa_compress",
    )(rows16, pos, cmp_w1.astype(BF16), cmp_w2.astype(BF16))


def _overlap_matrix(n_cmp_rows, n_cmp, n_blk):
    c = jnp.arange(n_cmp_rows)[:, None]
    n = jnp.arange(n_blk)[None, :]
    c_start, c_end = c * CMP_STRIDE, c * CMP_STRIDE + CMP_BLOCK - 1
    return ((c_start < (n + 1) * SEL_BLOCK) & (c_end >= n * SEL_BLOCK) & (c < n_cmp)).astype(BF16)


def _split_dot(x, w):
    hi = x.astype(BF16)
    lo = (x - hi.astype(F32)).astype(BF16)
    return jnp.dot(hi, w, preferred_element_type=F32) + jnp.dot(lo, w, preferred_element_type=F32)


def _top_blocks(imp, blk, n_top):
    n_blk = imp.shape[1]
    sel = jnp.zeros(imp.shape, F32)
    for _ in range(n_top):
        m = jnp.max(imp, axis=-1, keepdims=True)
        idx = jnp.min(jnp.where(imp == m, blk, float(n_blk)), axis=-1, keepdims=True)
        hit = blk == idx
        sel = jnp.where(hit, 1.0, sel)
        imp = jnp.where(hit, -jnp.inf, imp)
    return sel


def _flash_step(q, kt, vt, mask, m, acc):
    s = jnp.where(mask, jnp.dot(q, kt, preferred_element_type=F32), NEG)
    m_new = jnp.maximum(m, jnp.max(s, axis=-1, keepdims=True))
    p = jnp.exp(s - m_new).astype(BF16)
    vt_ext = jnp.concatenate([vt, jnp.ones(vt.shape, BF16)], axis=0)
    return m_new, acc * jnp.exp(m - m_new) + _dot_nt(p, vt_ext)


def _nsa_prompt_kernel(q_ref, gt_ref, kc_ref, vc_ref, ov_ref, ks_ref, vs_ref, kw_ref, vw_ref, o_ref,
                       s_ref, ch_ref, p_ref, *, tq, n_cmp):
    i = pl.program_id(2)
    rep, d = NSA_REP, HEAD_DIM
    t0 = i * tq
    qv = q_ref[...]
    q = jnp.concatenate([qv[:, r * d:(r + 1) * d] for r in range(rep)], axis=0)
    q = (q * (d ** -0.5)).astype(BF16)
    qpos = t0 + lax.broadcasted_iota(jnp.int32, (tq, 1), 0)
    tile_rows = lambda a: jnp.concatenate([a] * rep, axis=0)

    n_rows = kc_ref.shape[0]
    cidx = lax.broadcasted_iota(jnp.int32, (1, n_rows), 1)
    vis_c = tile_rows(((cidx * CMP_STRIDE + (CMP_BLOCK - 1) <= qpos) & (cidx < n_cmp)).astype(F32)) > 0.5
    s_c = jnp.where(vis_c, _dot_nt(q, kc_ref[...]), NEG)
    e_c = jnp.where(vis_c, jnp.exp(s_c - jnp.max(s_c, axis=-1, keepdims=True)), 0.0)
    p_c = e_c / jnp.maximum(jnp.sum(e_c, axis=-1, keepdims=True), 1e-30)
    o_c = jnp.dot(p_c.astype(BF16), vc_ref[...], preferred_element_type=F32)

    p_sum = p_c[0:tq]
    for r in range(1, rep):
        p_sum = p_sum + p_c[r * tq:(r + 1) * tq]
    n_blk = ov_ref.shape[1]
    blk_i = lax.broadcasted_iota(jnp.int32, (1, n_blk), 1)
    cur = lax.shift_right_logical(qpos, int(math.log2(SEL_BLOCK)))
    valid = blk_i <= cur
    forced = valid & ((blk_i == 0) | (blk_i > cur - SEL_LOCAL))
    imp = jnp.where(forced, jnp.inf, jnp.where(valid, _split_dot(p_sum, ov_ref[...]), -jnp.inf))

    col = lax.broadcasted_iota(jnp.int32, (1, tq), 1)
    m0 = jnp.full((rep * tq, 1), NEG, F32)
    acc0 = jnp.zeros((rep * tq, 2 * d), F32)

    st = (m0, acc0)
    for n in range(WINDOW // tq + 1):
        j = i - n
        off = pl.multiple_of(jnp.maximum(j, 0) * tq, tq)
        kpos = off + col + jnp.where(j < 0, 1 << 30, 0)
        mask = tile_rows(((kpos <= qpos) & (kpos > qpos - WINDOW)).astype(F32)) > 0.5
        st = _flash_step(q, kw_ref[:, pl.ds(off, tq)].astype(BF16), vw_ref[:, pl.ds(off, tq)].astype(BF16),
                         mask, *st)
    acc_w = st[1]

    sel = _top_blocks(imp, blk_i.astype(F32), min(SEL_TOP, n_blk)).astype(BF16)

    exp_row = lax.broadcasted_iota(jnp.int32, (n_blk, tq), 0)
    exp_col = lax.shift_right_logical(lax.broadcasted_iota(jnp.int32, (n_blk, tq), 1), int(math.log2(SEL_BLOCK)))

    def key_tile(n):
        return jnp.clip(i - n, 0, i)

    def scores(n):
        j = key_tile(n)
        expand = (exp_row == j * (tq // SEL_BLOCK) + exp_col).astype(BF16)
        return (jnp.dot(q, ks_ref[:, pl.ds(pl.multiple_of(j * tq, tq), tq)].astype(BF16), preferred_element_type=F32),
                jnp.dot(sel, expand, preferred_element_type=F32))

    def weighted_values(n, slot):
        vt = vs_ref[:, pl.ds(pl.multiple_of(key_tile(n) * tq, tq), tq)].astype(BF16)
        return _dot_nt(p_ref[slot], jnp.concatenate([vt, jnp.ones(vt.shape, BF16)], axis=0))

    s_ref[0], ch_ref[0] = scores(0)
    p_ref[0] = jnp.zeros(p_ref.shape[1:], BF16)

    def sel_step(n, rd, wr, st):
        m, alpha, acc = st
        pv = weighted_values(n - 1, rd)
        s_ref[wr], ch_ref[wr] = scores(n + 1)
        kpos = (i - n) * tq + col + jnp.where(n > i, 1 << 30, 0)
        mask = tile_rows(jnp.where(kpos <= qpos, ch_ref[rd], 0.0)) > 0.5
        s = jnp.where(mask, s_ref[rd], NEG)
        m_new = jnp.maximum(m, jnp.max(s, axis=-1, keepdims=True))
        p_ref[wr] = jnp.exp(s - m_new).astype(BF16)
        return m_new, jnp.exp(m - m_new), acc * alpha + pv

    def sel_pair(pair, st):
        return sel_step(2 * pair + 1, 1, 0, sel_step(2 * pair, 0, 1, st))

    _, _, acc_s = lax.fori_loop(0, (i + 3) >> 1, sel_pair, (m0, jnp.ones((rep * tq, 1), F32), acc0))

    o_s = acc_s[:, :d] / acc_s[:, d:]
    o_w = acc_w[:, :d] / acc_w[:, d:]
    gate = _sigmoid(gt_ref[...])
    outs = []
    for r in range(rep):
        sl = slice(r * tq, (r + 1) * tq)
        outs.append(gate[:, 3 * r:3 * r + 1] * o_c[sl] + gate[:, 3 * r + 1:3 * r + 2] * o_s[sl]
                    + gate[:, 3 * r + 2:3 * r + 3] * o_w[sl])
    o_ref[...] = jnp.concatenate(outs, axis=1).astype(o_ref.dtype)


def nsa_prompt(nsa_q, nsa_g, nsa_kvt, cmp_pos, cmp_w1, cmp_w2, tq=128):
    b, t, _ = nsa_q.shape
    g, d, rep = NSA_KV_GROUPS, HEAD_DIM, NSA_REP
    assert t % tq == 0 and tq % SEL_BLOCK == 0 and WINDOW % tq == 0 and t % CMP_STRIDE == 0
    n_cmp = (t - CMP_BLOCK) // CMP_STRIDE + 1
    nr = t // CMP_STRIDE
    n_blk = t // SEL_BLOCK
    rows16 = jnp.swapaxes(nsa_kvt[:, :2 * g * d].reshape(b, 2, g, d, t), 3, 4).reshape(b, 2, g, nr, CMP_STRIDE * d)
    kcvc = nsa_compress(rows16, cmp_pos, cmp_w1, cmp_w2)
    overlap = _overlap_matrix(nr, n_cmp, n_blk)
    gates = jnp.swapaxes(nsa_g.reshape(b, t, g, 3 * rep), 1, 2)
    gates = jnp.pad(gates, ((0, 0), (0, 0), (0, 0), (0, V7X_LANES - 3 * rep)))
    stream = lambda kind: pl.BlockSpec((None, d, t), lambda bi, gi, i: (bi, kind * g + gi, 0))
    return pl.pallas_call(
        functools.partial(_nsa_prompt_kernel, tq=tq, n_cmp=n_cmp),
        grid=(b, g, t // tq),
        in_specs=[pl.BlockSpec((None, tq, rep * d), lambda bi, gi, i: (bi, i, gi)),
                  pl.BlockSpec((None, None, tq, V7X_LANES), lambda bi, gi, i: (bi, gi, i, 0)),
                  pl.BlockSpec((None, None, None, nr, d), lambda bi, gi, i: (bi, 0, gi, 0, 0)),
                  pl.BlockSpec((None, None, None, nr, d), lambda bi, gi, i: (bi, 1, gi, 0, 0)),
                  pl.BlockSpec((nr, n_blk), lambda bi, gi, i: (0, 0)),
                  stream(2), stream(3), stream(4), stream(5)],
        out_specs=pl.BlockSpec((None, tq, rep * d), lambda bi, gi, i: (bi, i, gi)),
        out_shape=jax.ShapeDtypeStruct((b, t, g * rep * d), BF16),
        scratch_shapes=[pltpu.VMEM((2, rep * tq, tq), F32), pltpu.VMEM((2, tq, tq), F32),
                        pltpu.VMEM((2, rep * tq, tq), BF16)],
        compiler_params=_params("parallel", "parallel", "arbitrary"),
        name="nsa_prompt",
    )(nsa_q, gates, kcvc, kcvc, overlap, nsa_kvt, nsa_kvt, nsa_kvt, nsa_kvt)


def _flash_step_t(qt, k, vt, mask, m, acc):
    s = jnp.where(mask, jnp.dot(k, qt, preferred_element_type=F32), NEG)
    m_new = jnp.maximum(m, jnp.max(s, axis=0, keepdims=True))
    p = jnp.exp(s - m_new).astype(BF16)
    vt_ext = jnp.concatenate([vt, jnp.ones((8, vt.shape[1]), BF16)], axis=0)
    return m_new, acc * jnp.exp(m - m_new) + jnp.dot(vt_ext, p, preferred_element_type=F32)


def _nsa_prompt_t_kernel(qt_ref, gt_ref, kc_ref, vct_ref, ovt_ref, ks_ref, vs_ref, kw_ref, vw_ref, o_ref,
                         s_ref, ch_ref, p_ref, *, tq, n_cmp):
    i = pl.program_id(2)
    rep, d = NSA_REP, HEAD_DIM
    lanes = rep * tq
    sel_shift = int(math.log2(SEL_BLOCK))
    qt_blk = qt_ref[...]
    qt = jnp.concatenate([qt_blk[r * d:(r + 1) * d, :] for r in range(rep)], axis=1)
    qt = (qt * (d ** -0.5)).astype(BF16)
    qpos = i * tq + lax.broadcasted_iota(jnp.int32, (1, tq), 1)
    per_head = lambda a: jnp.concatenate([a] * rep, axis=1)

    n_rows = kc_ref.shape[0]
    cidx = lax.broadcasted_iota(jnp.int32, (n_rows, 1), 0)
    vis_c = per_head(((cidx * CMP_STRIDE + (CMP_BLOCK - 1) <= qpos) & (cidx < n_cmp)).astype(F32)) > 0.5
    s_c = jnp.where(vis_c, jnp.dot(kc_ref[...], qt, preferred_element_type=F32), NEG)
    e_c = jnp.where(vis_c, jnp.exp(s_c - jnp.max(s_c, axis=0, keepdims=True)), 0.0)
    p_c = e_c / jnp.maximum(jnp.sum(e_c, axis=0, keepdims=True), 1e-30)
    o_c = jnp.dot(vct_ref[...], p_c.astype(BF16), preferred_element_type=F32)

    p_sum = p_c[:, 0:tq]
    for r in range(1, rep):
        p_sum = p_sum + p_c[:, r * tq:(r + 1) * tq]
    p_hi = p_sum.astype(BF16)
    p_lo = (p_sum - p_hi.astype(F32)).astype(BF16)
    imp = (jnp.dot(ovt_ref[...], p_hi, preferred_element_type=F32)
           + jnp.dot(ovt_ref[...], p_lo, preferred_element_type=F32))
    n_blk = ovt_ref.shape[0]
    blk_i = lax.broadcasted_iota(jnp.int32, (n_blk, 1), 0)
    cur = lax.shift_right_logical(qpos, sel_shift)
    valid = blk_i <= cur
    forced = valid & ((blk_i == 0) | (blk_i > cur - SEL_LOCAL))
    imp = jnp.where(forced, jnp.inf, jnp.where(valid, imp, -jnp.inf))

    m0 = jnp.full((1, lanes), NEG, F32)
    acc0 = jnp.zeros((d + 8, lanes), F32)
    krow = lax.broadcasted_iota(jnp.int32, (tq, 1), 0)

    st = (m0, acc0)
    for n in range(WINDOW // tq + 1):
        j = i - n
        off = pl.multiple_of(jnp.maximum(j, 0) * tq, tq)
        kpos = off + krow + jnp.where(j < 0, 1 << 30, 0)
        mask = per_head(((kpos <= qpos) & (kpos > qpos - WINDOW)).astype(F32)) > 0.5
        st = _flash_step_t(qt, kw_ref[pl.ds(off, tq), :].astype(BF16), vw_ref[:, pl.ds(off, tq)].astype(BF16),
                           mask, *st)
    acc_w = st[1]

    blk_f = blk_i.astype(F32)
    sel = jnp.zeros((n_blk, tq), F32)
    for _ in range(min(SEL_TOP, n_blk)):
        top = jnp.max(imp, axis=0, keepdims=True)
        idx = jnp.min(jnp.where(imp == top, blk_f, float(n_blk)), axis=0, keepdims=True)
        hit = blk_f == idx
        sel = jnp.where(hit, 1.0, sel)
        imp = jnp.where(hit, -jnp.inf, imp)
    sel = sel.astype(BF16)

    exp_blk = lax.broadcasted_iota(jnp.int32, (tq, n_blk), 1)
    exp_key = lax.shift_right_logical(lax.broadcasted_iota(jnp.int32, (tq, n_blk), 0), sel_shift)

    def key_tile(n):
        return jnp.clip(i - n, 0, i)

    def scores(n):
        j = key_tile(n)
        expand = (exp_blk == j * (tq // SEL_BLOCK) + exp_key).astype(BF16)
        return (jnp.dot(ks_ref[pl.ds(pl.multiple_of(j * tq, tq), tq), :].astype(BF16), qt, preferred_element_type=F32),
                jnp.dot(expand, sel, preferred_element_type=F32))

    def weighted_values(n, slot):
        vt = vs_ref[:, pl.ds(pl.multiple_of(key_tile(n) * tq, tq), tq)].astype(BF16)
        return jnp.dot(jnp.concatenate([vt, jnp.ones((8, tq), BF16)], axis=0), p_ref[slot],
                       preferred_element_type=F32)

    s_ref[0], ch_ref[0] = scores(0)
    p_ref[0] = jnp.zeros(p_ref.shape[1:], BF16)

    def sel_step(n, rd, wr, st):
        m, alpha, acc = st
        pv = weighted_values(n - 1, rd)
        s_ref[wr], ch_ref[wr] = scores(n + 1)
        kpos = (i - n) * tq + krow + jnp.where(n > i, 1 << 30, 0)
        mask = per_head(jnp.where(kpos <= qpos, ch_ref[rd], 0.0)) > 0.5
        s = jnp.where(mask, s_ref[rd], NEG)
        m_new = jnp.maximum(m, jnp.max(s, axis=0, keepdims=True))
        p_ref[wr] = jnp.exp(s - m_new).astype(BF16)
        return m_new, jnp.exp(m - m_new), acc * alpha + pv

    def sel_pair(pair, st):
        return sel_step(2 * pair + 1, 1, 0, sel_step(2 * pair, 0, 1, st))

    _, _, acc_s = lax.fori_loop(0, (i + 3) >> 1, sel_pair, (m0, jnp.ones((1, lanes), F32), acc0))

    o_s = acc_s[:d] / acc_s[d:d + 1]
    o_w = acc_w[:d] / acc_w[d:d + 1]
    gate = _sigmoid(gt_ref[...])
    outs = []
    for r in range(rep):
        sl = slice(r * tq, (r + 1) * tq)
        outs.append(gate[3 * r:3 * r + 1] * o_c[:, sl] + gate[3 * r + 1:3 * r + 2] * o_s[:, sl]
                    + gate[3 * r + 2:3 * r + 3] * o_w[:, sl])
    o_ref[...] = jnp.concatenate(outs, axis=0).astype(o_ref.dtype)


def nsa_prompt_t(nsa_qt, nsa_g, nsa_kvt, cmp_pos, cmp_w1, cmp_w2, tq=128):
    b, _, t = nsa_qt.shape
    g, d, rep = NSA_KV_GROUPS, HEAD_DIM, NSA_REP
    assert t % tq == 0 and tq % SEL_BLOCK == 0 and WINDOW % tq == 0 and t % CMP_STRIDE == 0
    n_cmp = (t - CMP_BLOCK) // CMP_STRIDE + 1
    nr = t // CMP_STRIDE
    n_blk = t // SEL_BLOCK
    kinds = nsa_kvt.reshape(b, 6, g, d, t)
    rows16 = jnp.swapaxes(kinds[:, 0:2], 3, 4).reshape(b, 2, g, nr, CMP_STRIDE * d)
    kcvc = nsa_compress(rows16, cmp_pos, cmp_w1, cmp_w2)
    vct = jnp.swapaxes(kcvc[:, 1], 2, 3)
    overlap_t = _overlap_matrix(nr, n_cmp, n_blk).T
    keys = jnp.swapaxes(kinds[:, 2::2], 3, 4).astype(BF16)
    gates = jnp.transpose(nsa_g.reshape(b, t, g, 3 * rep), (0, 2, 3, 1))
    gates = jnp.pad(gates, ((0, 0), (0, 0), (0, 16 - 3 * rep), (0, 0)))
    values = lambda kind: pl.BlockSpec((None, d, t), lambda bi, gi, i: (bi, kind * g + gi, 0))
    rows = lambda kind: pl.BlockSpec((None, None, None, t, d), lambda bi, gi, i: (bi, kind, gi, 0, 0))
    out = pl.pallas_call(
        functools.partial(_nsa_prompt_t_kernel, tq=tq, n_cmp=n_cmp),
        grid=(b, g, t // tq),
        in_specs=[pl.BlockSpec((None, rep * d, tq), lambda bi, gi, i: (bi, gi, i)),
                  pl.BlockSpec((None, None, 16, tq), lambda bi, gi, i: (bi, gi, 0, i)),
                  pl.BlockSpec((None, None, None, nr, d), lambda bi, gi, i: (bi, 0, gi, 0, 0)),
                  pl.BlockSpec((None, None, d, nr), lambda bi, gi, i: (bi, gi, 0, 0)),
                  pl.BlockSpec((n_blk, nr), lambda bi, gi, i: (0, 0)),
                  rows(0), values(3), rows(1), values(5)],
        out_specs=pl.BlockSpec((None, rep * d, tq), lambda bi, gi, i: (bi, gi, i)),
        out_shape=jax.ShapeDtypeStruct((b, g * rep * d, t), BF16),
        scratch_shapes=[pltpu.VMEM((2, tq, rep * tq), F32), pltpu.VMEM((2, tq, tq), F32),
                        pltpu.VMEM((2, tq, rep * tq), BF16)],
        compiler_params=_params("parallel", "parallel", "arbitrary"),
        name="nsa_prompt",
    )(nsa_qt, gates, kcvc, vct, overlap_t, keys, nsa_kvt, keys, nsa_kvt)
    return jnp.swapaxes(out, 1, 2)


def _ssd_step_pre_kernel(x_ref, buf_ref, cw_ref, cb_ref, dt_ref, dtb_ref, alog_ref, exp_ref,
                         xs_ref, xdt_ref, bm_ref, cm_ref, dec_ref):
    conv = cb_ref[...] + cw_ref[SSD_CONV - 1:SSD_CONV, :] * x_ref[...]
    for j in range(SSD_CONV - 1):
        conv = conv + cw_ref[j:j + 1, :] * buf_ref[j]
    u = conv * _sigmoid(conv)
    xs = u[:, :SSD_INNER]
    dt = _softplus(dt_ref[...] + dtb_ref[...])
    xs_ref[...] = xs
    xdt_ref[...] = xs * jnp.dot(dt, exp_ref[...], precision=HI, preferred_element_type=F32)
    bm_ref[...] = u[:, SSD_INNER:SSD_INNER + SSD_GN]
    cm_ref[...] = u[:, SSD_INNER + SSD_GN:]
    dec_ref[...] = jnp.exp(dt * (-jnp.exp(alog_ref[...])))


def _ssd_step_state_kernel(h0_ref, xdt_ref, dec_ref, bm_ref, cm_ref, h_ref, y_ref):
    r = SSD_HEADS // SSD_GROUPS
    for hd in range(SSD_HEADS):
        g = hd // r
        hn = dec_ref[hd] * h0_ref[hd] + xdt_ref[hd] * bm_ref[g]
        h_ref[hd] = hn
        y_ref[hd] = jnp.sum(hn * cm_ref[g], axis=-1, keepdims=True)


def _ssd_step_post_kernel(y_ref, xs_ref, z_ref, dfull_ref, nw_ref, o_ref):
    zz = z_ref[...]
    y = (y_ref[...] + dfull_ref[...] * xs_ref[...]) * (zz * _sigmoid(zz))
    o_ref[...] = _rms(y, nw_ref[...]).astype(o_ref.dtype)


def ssd_step(xbc, z, dt_raw, h0, conv_buf, conv_w, conv_b, dt_bias, a_log, d_skip, norm_w):
    b = xbc.shape[0]
    hds, p, n = SSD_HEADS, SSD_HEAD_DIM, SSD_STATE
    expand = (jnp.arange(SSD_INNER)[None, :] // p == jnp.arange(hds)[:, None]).astype(F32)
    d_full = jnp.repeat(d_skip, p).reshape(1, SSD_INNER)
    sds = lambda shape: jax.ShapeDtypeStruct(shape, F32)
    xs, xdt, bm, cm, dec = pl.pallas_call(
        _ssd_step_pre_kernel,
        out_shape=[sds((b, SSD_INNER)), sds((b, SSD_INNER)), sds((b, SSD_GN)), sds((b, SSD_GN)), sds((b, hds))],
        name="ssd_step_pre",
    )(xbc, jnp.swapaxes(conv_buf, 0, 1), conv_w, conv_b.reshape(1, -1), dt_raw, dt_bias.reshape(1, hds),
      a_log.reshape(1, hds), expand)
    per_b = lambda *dims: pl.BlockSpec((None,) + dims, lambda bi: (bi,) + (0,) * len(dims))
    h_new, y_col = pl.pallas_call(
        _ssd_step_state_kernel,
        grid=(b,),
        in_specs=[per_b(hds, p, n), per_b(hds, p, 1), per_b(hds, 1, 1), per_b(SSD_GROUPS, 1, n),
                  per_b(SSD_GROUPS, 1, n)],
        out_specs=[per_b(hds, p, n), per_b(hds, p, 1)],
        out_shape=[sds((b, hds, p, n)), sds((b, hds, p, 1))],
        compiler_params=_params("parallel"),
        name="ssd_step_state",
    )(h0, xdt.reshape(b, hds, p, 1), dec.reshape(b, hds, 1, 1), bm.reshape(b, SSD_GROUPS, 1, n),
      cm.reshape(b, SSD_GROUPS, 1, n))
    y = pl.pallas_call(
        _ssd_step_post_kernel,
        out_shape=jax.ShapeDtypeStruct((b, SSD_INNER), BF16),
        name="ssd_step_post",
    )(y_col.reshape(b, SSD_INNER), xs, z, d_full, norm_w.reshape(1, -1))
    return y, h_new


def _sb_decode_kernel(pt_ref, q_ref, *refs):
    page_refs, (o_ref, carry_ref, acc_ref) = refs[:PAGES_PER_STEP], refs[PAGES_PER_STEP:]
    p = pl.program_id(1)
    tk = page_refs[0].shape[-1]
    hds = SB_HEADS

    @pl.when(p == 0)
    def _():
        carry_ref[...] = jnp.zeros(carry_ref.shape, F32)
        acc_ref[...] = jnp.zeros(acc_ref.shape, F32)

    scale = HEAD_DIM ** -0.5
    qs = [q_ref[h] * scale for h in range(hds)]
    z = jnp.concatenate([jnp.sum(ref[0, h] * qs[h], axis=0, keepdims=True)
                         for ref in page_refs for h in range(hds)], axis=0)
    row = lax.broadcasted_iota(jnp.int32, (tk, tk), 0)
    col = lax.broadcasted_iota(jnp.int32, (tk, tk), 1)
    sp = _softplus(z)
    later = _split_dot(-sp, (row > col).astype(BF16))
    total = later[:, 0:1] - sp[:, 0:1]
    carry = carry_ref[...]
    for k, ref in enumerate(page_refs):
        rows = slice(k * hds, (k + 1) * hds)
        a = jnp.exp(z[rows] - sp[rows] + later[rows] + carry)
        carry = carry + total[rows]
        for h in range(hds):
            acc_ref[h] += ref[1, h] * a[h:h + 1, :]
    carry_ref[...] = carry

    @pl.when(p == pl.num_programs(1) - 1)
    def _():
        for h in range(hds):
            o_ref[h] = jnp.sum(acc_ref[h], axis=-1, keepdims=True)


def _page_specs(block, layer, n_pages, kind_block, descending):
    def spec(k):
        def index(bi, p, pt):
            pos = p * PAGES_PER_STEP + k
            pos = n_pages - 1 - pos if descending else pos
            return (layer, pt[bi, pos], kind_block) + (0,) * (len(block) - 3)
        return pl.BlockSpec(block, index)
    return [spec(k) for k in range(PAGES_PER_STEP)]


def sb_decode(q, pool_t, layer, page_table):
    b = q.shape[0]
    h, d = SB_HEADS, HEAD_DIM
    n_pages = page_table.shape[1]
    page = pool_t.shape[-1]
    assert n_pages % PAGES_PER_STEP == 0
    out = pl.pallas_call(
        _sb_decode_kernel,
        grid_spec=pltpu.PrefetchScalarGridSpec(
            num_scalar_prefetch=1,
            grid=(b, n_pages // PAGES_PER_STEP),
            in_specs=[pl.BlockSpec((None, h, d, 1), lambda bi, p, pt: (bi, 0, 0, 0))]
            + _page_specs((None, None, 2, h, d, page), layer, n_pages, 0, descending=True),
            out_specs=pl.BlockSpec((None, h, d, 1), lambda bi, p, pt: (bi, 0, 0, 0)),
            scratch_shapes=[pltpu.VMEM((h, 1), F32), pltpu.VMEM((h, d, page), F32)]),
        out_shape=jax.ShapeDtypeStruct((b, h, d, 1), F32),
        compiler_params=_params("parallel", "arbitrary"),
        name="sb_decode",
    )(page_table, q.reshape(b, h, d, 1), *([pool_t] * PAGES_PER_STEP))
    return out.reshape(b, h * d)


def _nsa_gather_kernel(pt_ref, *refs):
    page_refs, (o_ref, x_ref) = refs[:PAGES_PER_STEP], refs[PAGES_PER_STEP:]
    page = page_refs[0].shape[-1]
    grp, d = NSA_KV_GROUPS, HEAD_DIM
    n_out = page // CMP_STRIDE
    for k, ref in enumerate(page_refs):
        for kind in range(2):
            x = x_ref.at[2 * k + kind]
            x[...] = ref[kind].reshape(grp * d, page).T
            steps = [x[pl.ds(s, n_out, stride=CMP_STRIDE), :] for s in range(CMP_STRIDE)]
            for g in range(grp):
                o_ref[kind, g, k * n_out:(k + 1) * n_out, :] = jnp.concatenate(
                    [st[:, g * d:(g + 1) * d] for st in steps], axis=1)


def nsa_gather_rows16(pool_t, layer, page_table):
    b, n_pages = page_table.shape
    g, d = NSA_KV_GROUPS, HEAD_DIM
    page = pool_t.shape[-1]
    assert page % CMP_STRIDE == 0 and n_pages % PAGES_PER_STEP == 0
    n_out = PAGES_PER_STEP * (page // CMP_STRIDE)
    return pl.pallas_call(
        _nsa_gather_kernel,
        grid_spec=pltpu.PrefetchScalarGridSpec(
            num_scalar_prefetch=1,
            grid=(b, n_pages // PAGES_PER_STEP),
            in_specs=_page_specs((None, None, 2, g, d, page), layer, n_pages, 0, descending=False),
            out_specs=pl.BlockSpec((None, 2, g, n_out, CMP_STRIDE * d), lambda bi, p, pt: (bi, 0, 0, p, 0)),
            scratch_shapes=[pltpu.VMEM((2 * PAGES_PER_STEP, page, g * d), F32)]),
        out_shape=jax.ShapeDtypeStruct((b, 2, g, n_pages * (page // CMP_STRIDE), CMP_STRIDE * d), F32),
        compiler_params=_params("parallel", "arbitrary"),
        name="nsa_gather_rows16",
    )(page_table, *([pool_t] * PAGES_PER_STEP))


def _nsa_decode_select_kernel(q_ref, kc_ref, vc_ref, ov_ref, oc_ref, sel_ref, *, n_cmp, q_pos):
    rep, d = NSA_REP, HEAD_DIM
    n_rows = kc_ref.shape[1]
    n_blk = ov_ref.shape[1]
    q = (q_ref[...] * (d ** -0.5)).astype(BF16)
    cidx = lax.broadcasted_iota(jnp.int32, (1, n_rows), 1)
    vis = (cidx * CMP_STRIDE + (CMP_BLOCK - 1) <= q_pos) & (cidx < n_cmp)
    o_c, imp = [], []
    for g in range(NSA_KV_GROUPS):
        s = jnp.where(vis, _dot_nt(q[g * rep:(g + 1) * rep], kc_ref[g]), NEG)
        e = jnp.where(vis, jnp.exp(s - jnp.max(s, axis=-1, keepdims=True)), 0.0)
        p = e / jnp.maximum(jnp.sum(e, axis=-1, keepdims=True), 1e-30)
        o_c.append(jnp.dot(p.astype(BF16), vc_ref[g], preferred_element_type=F32))
        imp.append(_split_dot(jnp.sum(p, axis=0, keepdims=True), ov_ref[...]))
    oc_ref[...] = jnp.concatenate(o_c, axis=0)
    imp = jnp.concatenate(imp, axis=0)
    blk_i = lax.broadcasted_iota(jnp.int32, (1, n_blk), 1)
    forced = (blk_i == 0) | (blk_i > n_blk - SEL_LOCAL)
    sel = _top_blocks(jnp.where(forced, jnp.inf, imp), blk_i.astype(F32), min(SEL_TOP - 1, n_blk))
    sel_ref[...] = jnp.concatenate([sel, jnp.zeros((sel_ref.shape[0] - NSA_KV_GROUPS, n_blk), F32)], axis=0)


def _nsa_decode_attend_kernel(pt_ref, q_ref, sel_ref, *refs, win_skip):
    page_refs = refs[:PAGES_PER_STEP]
    new_ref, win_ref, oc_ref, gt_ref, o_ref, m_ref, acc_ref = refs[PAGES_PER_STEP:]
    p = pl.program_id(1)
    n_steps = pl.num_programs(1)
    rep, d, grp = NSA_REP, HEAD_DIM, NSA_KV_GROUPS
    tk = page_refs[0].shape[-1]
    lanes = PAGES_PER_STEP * tk
    scale = d ** -0.5
    qf = q_ref[...] * scale
    q = qf.astype(BF16)
    new = new_ref[...]
    new_row = lambda kind, g: new[kind * grp + g:kind * grp + g + 1, :]
    per_head = lambda f: jnp.concatenate([f(g) for g in range(grp)], axis=0)

    @pl.when(p == 0)
    def _():
        m_ref[...] = per_head(lambda g: jnp.sum(qf[g * rep:(g + 1) * rep] * new_row(2, g), axis=-1, keepdims=True))
        acc_ref[...] = per_head(lambda g: jnp.concatenate(
            [jnp.broadcast_to(new_row(3, g), (rep, d)), jnp.ones((rep, d), F32)], axis=1))

    n_blk = sel_ref.shape[1]
    lane = lax.broadcasted_iota(jnp.int32, (n_blk, lanes), 1)
    page_pos = (n_steps - p) * PAGES_PER_STEP - 1 - lax.shift_right_logical(lane, int(math.log2(tk)))
    blk_of_lane = page_pos * (tk // SEL_BLOCK) + lax.shift_right_logical(lane & (tk - 1), int(math.log2(SEL_BLOCK)))
    expand = (lax.broadcasted_iota(jnp.int32, (n_blk, lanes), 0) == blk_of_lane).astype(BF16)
    chosen = jnp.dot(sel_ref[...].astype(BF16), expand, preferred_element_type=F32)

    def scores(g):
        qg = q[g * rep:(g + 1) * rep]
        sg = jnp.concatenate([jnp.dot(qg, ref[0, g].astype(BF16), preferred_element_type=F32) for ref in page_refs],
                             axis=1)
        return jnp.where(chosen[g:g + 1, :] > 0.5, sg, NEG)

    s = per_head(scores)
    m_old = m_ref[...]
    m_new = jnp.maximum(m_old, jnp.max(s, axis=-1, keepdims=True))
    pr = jnp.exp(s - m_new).astype(BF16)

    def weighted_values(g):
        out = jnp.zeros((rep, 2 * d), F32)
        for k, ref in enumerate(page_refs):
            vt_ext = jnp.concatenate([ref[1, g].astype(BF16), jnp.ones((d, tk), BF16)], axis=0)
            out = out + _dot_nt(pr[g * rep:(g + 1) * rep, k * tk:(k + 1) * tk], vt_ext)
        return out

    acc_ref[...] = acc_ref[...] * jnp.exp(m_old - m_new) + per_head(weighted_values)
    m_ref[...] = m_new

    @pl.when(p == n_steps - 1)
    def _():
        acc = acc_ref[...]
        o_s = acc[:, :d] / acc[:, d:]
        wlen = win_ref.shape[-1]
        vis = lax.broadcasted_iota(jnp.int32, (1, wlen), 1) >= win_skip

        def window(g):
            qg = q[g * rep:(g + 1) * rep]
            s_w = jnp.where(vis, jnp.dot(qg, win_ref[0, g].astype(BF16), preferred_element_type=F32), NEG)
            s_n = jnp.sum(qf[g * rep:(g + 1) * rep] * new_row(4, g), axis=-1, keepdims=True)
            mx = jnp.maximum(jnp.max(s_w, axis=-1, keepdims=True), s_n)
            e_w = jnp.where(vis, jnp.exp(s_w - mx), 0.0)
            e_n = jnp.exp(s_n - mx)
            num = _dot_nt(e_w.astype(BF16), win_ref[1, g].astype(BF16)) + e_n * new_row(5, g)
            return num / (jnp.sum(e_w, axis=-1, keepdims=True) + e_n)

        o_w = per_head(window)
        gate = _sigmoid(gt_ref[...])
        o_ref[...] = gate[:, 0:1] * oc_ref[...] + gate[:, 1:2] * o_s + gate[:, 2:3] * o_w


def nsa_decode(nsa_q, nsa_g, nsa_kv_new, pool_t, win_t, layer, page_table, cmp_pos, cmp_w1, cmp_w2):
    b = nsa_q.shape[0]
    g, d, rep, hds = NSA_KV_GROUPS, HEAD_DIM, NSA_REP, NSA_HEADS
    n_pages = page_table.shape[1]
    page = pool_t.shape[-1]
    past = n_pages * page
    wlen = win_t.shape[-1]
    assert past % SEL_BLOCK == 0 and past % CMP_STRIDE == 0 and page % SEL_BLOCK == 0 and wlen <= past
    n_cmp = (past + 1 - CMP_BLOCK) // CMP_STRIDE + 1
    nr = past // CMP_STRIDE
    n_blk = past // SEL_BLOCK
    rows16 = nsa_gather_rows16(pool_t, layer, page_table)
    kcvc = nsa_compress(rows16, cmp_pos, cmp_w1, cmp_w2)
    overlap = _overlap_matrix(nr, n_cmp, n_blk)
    q3 = nsa_q.reshape(b, hds, d)
    per_b = lambda *dims: pl.BlockSpec((None,) + dims, lambda bi: (bi,) + (0,) * len(dims))
    o_c, sel = pl.pallas_call(
        functools.partial(_nsa_decode_select_kernel, n_cmp=n_cmp, q_pos=past),
        grid=(b,),
        in_specs=[per_b(hds, d),
                  pl.BlockSpec((None, None, g, nr, d), lambda bi: (bi, 0, 0, 0, 0)),
                  pl.BlockSpec((None, None, g, nr, d), lambda bi: (bi, 1, 0, 0, 0)),
                  pl.BlockSpec((nr, n_blk), lambda bi: (0, 0))],
        out_specs=[per_b(hds, d), per_b(8, n_blk)],
        out_shape=[jax.ShapeDtypeStruct((b, hds, d), F32), jax.ShapeDtypeStruct((b, 8, n_blk), F32)],
        compiler_params=_params("parallel"),
        name="nsa_decode_select",
    )(q3, kcvc, kcvc, overlap)
    gates = jnp.pad(nsa_g.reshape(b, hds, 3), ((0, 0), (0, 0), (0, V7X_LANES - 3)))
    fixed = lambda *dims: pl.BlockSpec((None,) + dims, lambda bi, p, pt: (bi,) + (0,) * len(dims))
    out = pl.pallas_call(
        functools.partial(_nsa_decode_attend_kernel, win_skip=wlen - WINDOW + 1),
        grid_spec=pltpu.PrefetchScalarGridSpec(
            num_scalar_prefetch=1,
            grid=(b, n_pages // PAGES_PER_STEP),
            in_specs=[fixed(hds, d), fixed(8, n_blk)]
            + _page_specs((None, None, 2, g, d, page), layer, n_pages, 1, descending=True)
            + [fixed(6 * g, d),
               pl.BlockSpec((None, None, 2, g, d, wlen), lambda bi, p, pt: (layer, bi, 0, 0, 0, 0)),
               fixed(hds, d), fixed(hds, V7X_LANES)],
            out_specs=fixed(hds, d),
            scratch_shapes=[pltpu.VMEM((hds, 1), F32), pltpu.VMEM((hds, 2 * d), F32)]),
        out_shape=jax.ShapeDtypeStruct((b, hds, d), F32),
        compiler_params=_params("parallel", "arbitrary"),
        name="nsa_decode_attend",
    )(page_table, q3, sel, *([pool_t] * PAGES_PER_STEP), nsa_kv_new.reshape(b, 6 * g, d), win_t, o_c, gates)
    return out.reshape(b, hds * d)


def _x_rms_norm(x, w):
    xf = x.astype(jnp.float32)
    y = xf * lax.rsqrt(jnp.mean(xf * xf, axis=-1, keepdims=True) + RMS_EPS)
    return (y * w.astype(jnp.float32)).astype(x.dtype)


def _x_masked_softmax(s, mask):
    s = jnp.where(mask, s.astype(jnp.float32), -jnp.inf)
    m = jnp.max(s, axis=-1, keepdims=True)
    m = jnp.where(jnp.isfinite(m), m, 0.0)
    e = jnp.where(mask, jnp.exp(s - m), 0.0)
    return e / jnp.maximum(jnp.sum(e, axis=-1, keepdims=True), 1e-30)


def _x_causal_dwconv(x, buf, w, b):
    xp = jnp.concatenate([buf.astype(x.dtype), x], axis=1)
    y = lax.conv_general_dilated(xp, w[:, None, :].astype(x.dtype), window_strides=(1,), padding='VALID',
                                 dimension_numbers=('NWC', 'WIO', 'NWC'), feature_group_count=x.shape[-1])
    return y + b.astype(x.dtype), xp[:, xp.shape[1] - (SSD_CONV - 1):]


def _x_ssd_scan(x, dt, a, bm, cm, h0):
    bsz, t = x.shape[:2]
    q = min(SSD_CHUNK, t)
    nc = t // q
    r = SSD_HEADS // SSD_GROUPS
    xdt = (x * dt[..., None]).reshape(bsz, nc, q, SSD_GROUPS, r, SSD_HEAD_DIM)
    acum = jnp.cumsum((dt * a).reshape(bsz, nc, q, SSD_GROUPS, r), axis=2)
    bm = bm.reshape(bsz, nc, q, SSD_GROUPS, SSD_STATE)
    cm = cm.reshape(bsz, nc, q, SSD_GROUPS, SSD_STATE)
    at = jnp.moveaxis(acum, 2, -1)
    tril = jnp.tril(jnp.ones((q, q), dtype=bool))
    decay = jnp.exp(jnp.where(tril, at[..., :, None] - at[..., None, :], -jnp.inf))
    cb = jnp.einsum('bclgn,bcsgn->bcgls', cm, bm)
    y_diag = jnp.einsum('bcgls,bcgrls,bcsgrp->bclgrp', cb, decay, xdt)
    to_end = jnp.exp(acum[:, :, -1:] - acum)
    states = jnp.einsum('bclgn,bclgr,bclgrp->bcgrpn', bm, to_end, xdt)
    chunk_decay = jnp.exp(acum[:, :, -1])

    def step(h, inp):
        dec, st = inp
        return dec[..., None, None] * h + st, h

    h_init = h0.reshape(bsz, SSD_GROUPS, r, SSD_HEAD_DIM, SSD_STATE)
    h_fin, h_in = lax.scan(step, h_init, (jnp.moveaxis(chunk_decay, 1, 0), jnp.moveaxis(states, 1, 0)))
    h_in = jnp.moveaxis(h_in, 0, 1)
    y_off = jnp.einsum('bclgn,bcgrpn,bclgr->bclgrp', cm, h_in, jnp.exp(acum))
    y = (y_diag + y_off).reshape(bsz, nc * q, SSD_HEADS, SSD_HEAD_DIM)[:, :t]
    return y, h_fin.reshape(bsz, SSD_HEADS, SSD_HEAD_DIM, SSD_STATE)


def _x_ssd_branch(z, xbc, dt_raw, conv_buf, h0, lp):
    bsz, t = z.shape[:2]
    xbc, new_buf = _x_causal_dwconv(xbc, conv_buf, lp['conv_w'], lp['conv_b'])
    xbc = jax.nn.silu(xbc)
    gn = SSD_GROUPS * SSD_STATE
    xs = xbc[..., :SSD_INNER].reshape(bsz, t, SSD_HEADS, SSD_HEAD_DIM).astype(jnp.float32)
    bm = xbc[..., SSD_INNER:SSD_INNER + gn].reshape(bsz, t, SSD_GROUPS, SSD_STATE).astype(jnp.float32)
    cm = xbc[..., SSD_INNER + gn:].reshape(bsz, t, SSD_GROUPS, SSD_STATE).astype(jnp.float32)
    dt = jax.nn.softplus((dt_raw + lp['dt_bias']).astype(jnp.float32))
    a = -jnp.exp(lp['a_log'].astype(jnp.float32))
    y, h_fin = _x_ssd_scan(xs, dt, a, bm, cm, h0.astype(jnp.float32))
    y = y + lp['d_skip'].astype(jnp.float32)[:, None] * xs
    y = y.reshape(bsz, t, SSD_INNER).astype(z.dtype)
    y = _x_rms_norm(y * jax.nn.silu(z), lp['ssd_norm'])
    return y, new_buf, h_fin.astype(h0.dtype)


def _x_stick_breaking(q, k, v, q_pos, k_pos):
    z = jnp.einsum('bqhd,bkhd->bhqk', q, k).astype(jnp.float32) * (HEAD_DIM ** -0.5)
    mask = k_pos[None, :] < q_pos[:, None]
    log_keep = jnp.where(mask, jax.nn.log_sigmoid(-z), 0.0)
    later = lax.cumsum(log_keep, axis=3, reverse=True) - log_keep
    a = jnp.where(mask, jnp.exp(jax.nn.log_sigmoid(z) + later), 0.0)
    return jnp.einsum('bhqk,bkhd->bqhd', a.astype(v.dtype), v)


def _x_compress_blocks(rows, pos_emb, w1, w2):
    bsz, t = rows.shape[:2]
    nc = (t - CMP_BLOCK) // CMP_STRIDE + 1
    starts = jnp.arange(nc) * CMP_STRIDE
    idx = starts[:, None] + jnp.arange(CMP_BLOCK)[None, :]
    blocks = rows[:, idx] + pos_emb[None, None, :, None, :].astype(rows.dtype)
    blocks = jnp.moveaxis(blocks, 3, 2).reshape(bsz, nc, NSA_KV_GROUPS, CMP_BLOCK * HEAD_DIM)
    return jax.nn.silu(blocks @ w1) @ w2, starts + CMP_BLOCK - 1


def _x_nsa_attend(q, gates, q_pos, kc, vc, c_end, ks, vs, kw, vw, kw_pos):
    bsz, tq = q.shape[:2]
    r = NSA_HEADS // NSA_KV_GROUPS
    scale = HEAD_DIM ** -0.5
    qg = q.reshape(bsz, tq, NSA_KV_GROUPS, r, HEAD_DIM)
    s_c = jnp.einsum('bqgrd,bcgd->bqgrc', qg, kc).astype(jnp.float32) * scale
    p_c = _x_masked_softmax(s_c, (c_end[None, :] <= q_pos[:, None])[None, :, None, None, :])
    o_c = jnp.einsum('bqgrc,bcgd->bqgrd', p_c.astype(vc.dtype), vc)
    n_blk = ks.shape[1] // SEL_BLOCK
    blk = jnp.arange(n_blk)
    c_start = c_end - (CMP_BLOCK - 1)
    overlap = ((c_start[:, None] < (blk[None, :] + 1) * SEL_BLOCK) & (c_end[:, None] >= blk[None, :] * SEL_BLOCK)).astype(jnp.float32)
    imp = jnp.einsum('bqgrc,cn->bqgn', p_c, overlap)
    cur = q_pos // SEL_BLOCK
    valid = blk[None, :] <= cur[:, None]
    forced = valid & ((blk[None, :] == 0) | (blk[None, :] > cur[:, None] - SEL_LOCAL))
    imp = jnp.where(forced[None, :, None, :], jnp.inf, jnp.where(valid[None, :, None, :], imp, -jnp.inf))
    n_top = min(SEL_TOP, n_blk)
    _, idx = lax.top_k(imp, n_top)
    kb = jnp.moveaxis(ks.reshape(bsz, n_blk, SEL_BLOCK, NSA_KV_GROUPS, HEAD_DIM), 3, 1)
    vb = jnp.moveaxis(vs.reshape(bsz, n_blk, SEL_BLOCK, NSA_KV_GROUPS, HEAD_DIM), 3, 1)
    idx_g = jnp.moveaxis(idx, 2, 1)
    take = jax.vmap(jax.vmap(lambda blocks, ids: blocks[ids]))
    gk = take(kb, idx_g)
    gv = take(vb, idx_g)
    sel_pos = idx_g[..., None] * SEL_BLOCK + jnp.arange(SEL_BLOCK)
    m_s = jnp.moveaxis(sel_pos <= q_pos[None, None, :, None, None], 1, 2)[:, :, :, None]
    s_s = jnp.einsum('bqgrd,bgqnsd->bqgrns', qg, gk).astype(jnp.float32) * scale
    nk = n_top * SEL_BLOCK
    p_s = _x_masked_softmax(s_s.reshape(bsz, tq, NSA_KV_GROUPS, r, nk), m_s.reshape(bsz, tq, NSA_KV_GROUPS, 1, nk))
    o_s = jnp.einsum('bqgrk,bgqkd->bqgrd', p_s.astype(gv.dtype), gv.reshape(bsz, NSA_KV_GROUPS, tq, nk, HEAD_DIM))
    s_w = jnp.einsum('bqgrd,bkgd->bqgrk', qg, kw).astype(jnp.float32) * scale
    m_w = (kw_pos[None, :] <= q_pos[:, None]) & (kw_pos[None, :] > q_pos[:, None] - WINDOW) & (kw_pos[None, :] >= 0)
    p_w = _x_masked_softmax(s_w, m_w[None, :, None, None, :])
    o_w = jnp.einsum('bqgrk,bkgd->bqgrd', p_w.astype(vw.dtype), vw)
    g = jax.nn.sigmoid(gates.reshape(bsz, tq, NSA_KV_GROUPS, r, 3).astype(jnp.float32)).astype(q.dtype)
    o = g[..., 0:1] * o_c + g[..., 1:2] * o_s + g[..., 2:3] * o_w
    return o.reshape(bsz, tq, NSA_WIDTH)


def _x_nsa_prompt(nq, gates, nkv, lp):
    bsz, t = nq.shape[:2]
    kc, c_end = _x_compress_blocks(nkv[:, :, 0], lp['cmp_pos'][0], lp['cmp_w1'][0], lp['cmp_w2'][0])
    vc, _ = _x_compress_blocks(nkv[:, :, 1], lp['cmp_pos'][1], lp['cmp_w1'][1], lp['cmp_w2'][1])
    ks, vs = nkv[:, :, 2], nkv[:, :, 3]
    w_pad = jnp.pad(nkv[:, :, 4:6], ((0, 0), (WINDOW, 0), (0, 0), (0, 0), (0, 0)))
    nb = t // Q_BLOCK
    qb = jnp.moveaxis(nq.reshape(bsz, nb, Q_BLOCK, NSA_HEADS, HEAD_DIM), 1, 0)
    gb = jnp.moveaxis(gates.reshape(bsz, nb, Q_BLOCK, 3 * NSA_HEADS), 1, 0)

    def blk(args):
        qi, gi, i = args
        t0 = i * Q_BLOCK
        wi = lax.dynamic_slice_in_dim(w_pad, t0, WINDOW + Q_BLOCK, axis=1)
        kw_pos = t0 - WINDOW + jnp.arange(WINDOW + Q_BLOCK)
        return _x_nsa_attend(qi, gi, t0 + jnp.arange(Q_BLOCK), kc, vc, c_end, ks, vs, wi[:, :, 0], wi[:, :, 1], kw_pos)

    out = lax.map(blk, (qb, gb, jnp.arange(nb)))
    return jnp.moveaxis(out, 0, 1).reshape(bsz, t, NSA_WIDTH)


def _col_offsets():
    offs, s = [], 0
    for n in IN_SPLITS:
        offs.append(s)
        s += n
    return offs


def _layer_weights(l, p):
    o = _col_offsets()
    w_in = p['w_in'][l]
    cols = lambda a, n: w_in[:, a:a + n]
    small = jnp.concatenate([cols(o[2], SSD_HEADS), cols(o[6], 3 * NSA_HEADS)], axis=1)
    small = jnp.pad(small, ((0, 0), (0, V7X_LANES - small.shape[1])))
    bf = lambda a: a.astype(BF16)
    return {
        'w_z': bf(cols(o[0], SSD_INNER)), 'w_xbc': bf(cols(o[1], SSD_CONV_DIM)), 'w_small': bf(small),
        'w_sbq': bf(cols(o[3], SB_WIDTH)), 'w_sbkv': bf(cols(o[3] + SB_WIDTH, 2 * SB_WIDTH)),
        'w_nq': bf(cols(o[4], NSA_WIDTH)), 'w_nkv': bf(cols(o[5], 6 * NSA_KV_WIDTH)),
        'w_brg': bf(cols(o[7], N_BRANCH * D_MODEL)),
        'mix_pre': p['norm_mix_pre'][l], 'mix_post': p['norm_mix_post'][l],
        'ffn_pre': p['norm_ffn_pre'][l], 'ffn_post': p['norm_ffn_post'][l],
        'conv_w': p['ssd_conv_w'][l], 'conv_b': p['ssd_conv_b'][l], 'dt_bias': p['ssd_dt_bias'][l],
        'a_log': p['ssd_a_log'][l], 'd_skip': p['ssd_d'][l], 'ssd_norm': p['ssd_norm'][l],
        'w_ssd_out': bf(p['w_ssd_out'][l]), 'w_sb_out': bf(p['w_sb_out'][l]), 'w_nsa_out': bf(p['w_nsa_out'][l]),
        'w_o': bf(p['w_o'][l]), 'w_ffn_gate': bf(p['w_ffn_gate'][l]), 'w_ffn_up': bf(p['w_ffn_up'][l]),
        'w_ffn_down': bf(p['w_ffn_down'][l]),
        'cmp_pos': p['nsa_cmp_pos'][l], 'cmp_w1': p['nsa_cmp_w1'][l], 'cmp_w2': p['nsa_cmp_w2'][l],
    }


def _trunk_tail(x, ssd_y, sb_o, nsa_o, br_g, lw):
    b, t, d = x.shape
    m = b * t
    x1 = merge_branches(x.reshape(m, d), ssd_y.reshape(m, -1), sb_o.reshape(m, -1), nsa_o.reshape(m, -1),
                        br_g.reshape(m, -1), lw['w_ssd_out'], lw['w_sb_out'], lw['w_nsa_out'], lw['w_o'],
                        lw['mix_post'])
    x2 = ffn(x1, lw['ffn_pre'], lw['ffn_post'], lw['w_ffn_gate'], lw['w_ffn_up'], lw['w_ffn_down'])
    return x2.reshape(b, t, d)


def _layer_prompt(x, lw):
    b, t, _ = x.shape
    g = lw['mix_pre']
    z = norm_matmul(x, g, lw['w_z'])
    xbc = norm_matmul(x, g, lw['w_xbc'])
    small = norm_matmul(x, g, lw['w_small'])
    br_g = norm_matmul(x, g, lw['w_brg'])
    sb_q = norm_matmul(x, g, lw['w_sbq'], out_dtype=BF16)
    nsa_qt = norm_matmul(x, g, lw['w_nq'].T, transposed=True)
    sb_kvt = norm_matmul(x, g, lw['w_sbkv'].T, transposed=True)
    nsa_kvt = norm_matmul(x, g, lw['w_nkv'].T, transposed=True)
    dt_raw = small[..., :SSD_HEADS]
    nsa_g = small[..., SSD_HEADS:SSD_HEADS + 3 * NSA_HEADS]

    h0 = jnp.zeros((b, SSD_HEADS, SSD_HEAD_DIM, SSD_STATE), F32)
    conv0 = jnp.zeros((b, SSD_CONV - 1, SSD_CONV_DIM), F32)
    ssd_y, h_new = ssd_prompt(xbc, z, dt_raw, h0, conv0, lw['conv_w'], lw['conv_b'], lw['dt_bias'], lw['a_log'],
                              lw['d_skip'], lw['ssd_norm'])
    conv_new = xbc[:, t - (SSD_CONV - 1):, :]
    sb_o = sb_prompt(sb_q, sb_kvt)

    nsa_o = nsa_prompt_t(nsa_qt, nsa_g, nsa_kvt, lw['cmp_pos'], lw['cmp_w1'], lw['cmp_w2'])

    y = _trunk_tail(x, ssd_y, sb_o, nsa_o, br_g, lw)
    sb_kv = jnp.moveaxis(sb_kvt.reshape(b, 2, SB_HEADS, HEAD_DIM, t), 4, 1)
    nsa_all = jnp.moveaxis(nsa_kvt.reshape(b, 6, NSA_KV_GROUPS, HEAD_DIM, t), 4, 1)
    keep = min(WINDOW, t)
    return y, sb_kv, nsa_all[:, :, 0:4], nsa_all[:, t - keep:, 4:6], h_new, conv_new


def _layer_sample(x, lw, layer, sb_pool_t, nsa_pool_t, win_t, h0, conv_buf, page_table):
    bsz, t = x.shape[:2]
    assert t == 1
    past = page_table.shape[1] * sb_pool_t.shape[-1]
    xr = x.reshape(1, bsz, D_MODEL)
    g = lw['mix_pre']
    pr = lambda w: norm_matmul(xr, g, w)[0]
    z, xbc, small, br_g = pr(lw['w_z']), pr(lw['w_xbc']), pr(lw['w_small']), pr(lw['w_brg'])
    sb_q, sb_kv, nsa_q, nsa_kv = pr(lw['w_sbq']), pr(lw['w_sbkv']), pr(lw['w_nq']), pr(lw['w_nkv'])
    dt_raw = small[:, :SSD_HEADS]
    nsa_g = small[:, SSD_HEADS:SSD_HEADS + 3 * NSA_HEADS]
    ssd_y, h_new = ssd_step(xbc, z, dt_raw, h0, conv_buf, lw['conv_w'], lw['conv_b'], lw['dt_bias'], lw['a_log'],
                            lw['d_skip'], lw['ssd_norm'])
    conv_new = jnp.concatenate([conv_buf[:, 1:], xbc[:, None, :]], axis=1)
    sb_o = sb_decode(sb_q, sb_pool_t, layer, page_table)
    nsa_o = nsa_decode(nsa_q, nsa_g, nsa_kv, nsa_pool_t, win_t, layer, page_table, lw['cmp_pos'], lw['cmp_w1'],
                       lw['cmp_w2'])
    y = _trunk_tail(x, ssd_y[:, None], sb_o.astype(BF16)[:, None], nsa_o.astype(BF16)[:, None], br_g[:, None], lw)
    kv_new = sb_kv.reshape(bsz, 1, 2, SB_HEADS, HEAD_DIM)
    nkv = nsa_kv.reshape(bsz, 1, 6, NSA_KV_GROUPS, HEAD_DIM)
    keep = min(WINDOW, past + 1)
    win_all_t = jnp.concatenate([win_t[layer], nkv[:, 0, 4:6][..., None]], axis=-1)
    win_new = jnp.moveaxis(win_all_t[..., win_all_t.shape[-1] - keep:], 4, 1)
    return y, kv_new, nkv[:, :, 0:4], win_new, h_new, conv_new


def kernel(x_prompt, x_sample, cache_sb_kv, cache_nsa_kv, cache_nsa_win, state_ssd, state_conv, page_table,
           norm_mix_pre, norm_mix_post, norm_ffn_pre, norm_ffn_post, w_in, ssd_conv_w, ssd_conv_b, ssd_dt_bias,
           ssd_a_log, ssd_d, ssd_norm, w_ssd_out, w_sb_out, nsa_cmp_pos, nsa_cmp_w1, nsa_cmp_w2, w_nsa_out, w_o,
           w_ffn_gate, w_ffn_up, w_ffn_down):
    p = dict(norm_mix_pre=norm_mix_pre, norm_mix_post=norm_mix_post, norm_ffn_pre=norm_ffn_pre,
             norm_ffn_post=norm_ffn_post, w_in=w_in, ssd_conv_w=ssd_conv_w, ssd_conv_b=ssd_conv_b,
             ssd_dt_bias=ssd_dt_bias, ssd_a_log=ssd_a_log, ssd_d=ssd_d, ssd_norm=ssd_norm, w_ssd_out=w_ssd_out,
             w_sb_out=w_sb_out, nsa_cmp_pos=nsa_cmp_pos, nsa_cmp_w1=nsa_cmp_w1, nsa_cmp_w2=nsa_cmp_w2,
             w_nsa_out=w_nsa_out, w_o=w_o, w_ffn_gate=w_ffn_gate, w_ffn_up=w_ffn_up, w_ffn_down=w_ffn_down)
    yp, ys = x_prompt, x_sample
    outs_p, outs_s = [], []
    time_minor = lambda a: jnp.transpose(a, (0, 1, 3, 4, 5, 2))
    sb_pool_t, nsa_pool_t, win_t = time_minor(cache_sb_kv), time_minor(cache_nsa_kv), time_minor(cache_nsa_win)
    for l in range(w_in.shape[0]):
        lw = _layer_weights(l, p)
        res = _layer_prompt(yp, lw)
        yp = res[0]
        outs_p.append(res[1:])
        res = _layer_sample(ys, lw, l, sb_pool_t, nsa_pool_t, win_t, state_ssd[l], state_conv[l], page_table)
        ys = res[0]
        outs_s.append(res[1:])
    st = lambda outs, i: jnp.stack([o[i] for o in outs])
    return (yp, ys, st(outs_p, 0), st(outs_s, 0), st(outs_p, 1), st(outs_s, 1), st(outs_p, 2), st(outs_s, 2),
            st(outs_p, 3), st(outs_s, 3), st(outs_p, 4), st(outs_s, 4))
```

```python
import functools
import math

import jax
import jax.numpy as jnp
from jax import lax
from jax.experimental import pallas as pl
from jax.experimental.pallas import tpu as pltpu

D_MODEL = 1024
HEAD_DIM = 64
SSD_INNER = D_MODEL
SSD_HEAD_DIM = 64
SSD_HEADS = SSD_INNER // SSD_HEAD_DIM
SSD_GROUPS = 2
SSD_STATE = 128
SSD_CONV = 4
SSD_GN = SSD_GROUPS * SSD_STATE
SSD_CONV_DIM = SSD_INNER + 2 * SSD_GN
SSD_CHUNK = 128
SB_HEADS = 8
SB_WIDTH = SB_HEADS * HEAD_DIM
NSA_HEADS = 8
NSA_KV_GROUPS = 2
NSA_REP = NSA_HEADS // NSA_KV_GROUPS
NSA_WIDTH = NSA_HEADS * HEAD_DIM
NSA_KV_WIDTH = NSA_KV_GROUPS * HEAD_DIM
CMP_BLOCK = 32
CMP_STRIDE = 16
CMP_HIDDEN = 128
SEL_BLOCK = 64
SEL_TOP = 16
SEL_LOCAL = 2
WINDOW = 512
Q_BLOCK = 128
N_BRANCH = 3
FFN_HIDDEN = ((8 * D_MODEL + 3 * 256 - 1) // (3 * 256)) * 256
RMS_EPS = 1e-6
IN_SPLITS = (SSD_INNER, SSD_CONV_DIM, SSD_HEADS, 3 * SB_WIDTH, NSA_WIDTH, 6 * NSA_KV_WIDTH, 3 * NSA_HEADS,
             N_BRANCH * D_MODEL)

V7X_LANES = 128
V7X_VMEM_LIMIT = 56 * 1024 * 1024
PAGES_PER_STEP = 8
NEG = -1e30
BF16 = jnp.bfloat16
F32 = jnp.float32
HI = lax.Precision.HIGHEST


def _params(*sem):
    return pltpu.CompilerParams(dimension_semantics=sem, vmem_limit_bytes=V7X_VMEM_LIMIT)


def _pick(n, cands):
    for c in cands:
        if n % c == 0:
            return c
    return n


def _rms(x, w):
    return x * lax.rsqrt(jnp.mean(x * x, axis=-1, keepdims=True) + RMS_EPS) * w


def _softplus(x):
    return jnp.maximum(x, 0.0) + jnp.log1p(jnp.exp(-jnp.abs(x)))


def _sigmoid(x):
    return 1.0 / (1.0 + jnp.exp(-x))


def _dot_nt(a, b):
    return lax.dot_general(a, b, (((1,), (1,)), ((), ())), preferred_element_type=F32)


def _dot_tn(a, b):
    return lax.dot_general(a, b, (((0,), (0,)), ((), ())), preferred_element_type=F32)


def _norm_mm_kernel(x_ref, g_ref, w_ref, o_ref, h_ref, *, transposed):
    @pl.when(pl.program_id(2) == 0)
    def _():
        h_ref[...] = _rms(x_ref[...], g_ref[...]).astype(BF16)

    if transposed:
        o_ref[...] = _dot_nt(w_ref[...], h_ref[...]).astype(o_ref.dtype)
    else:
        o_ref[...] = jnp.dot(h_ref[...], w_ref[...], preferred_element_type=F32).astype(o_ref.dtype)


def norm_matmul(x, gain, w, out_dtype=F32, transposed=False):
    b, t, k = x.shape
    n = w.shape[0] if transposed else w.shape[1]
    tm = _pick(t, (1024, 512, 256, 128))
    tn = _pick(n, (1024, 768, 512, 256, 128))
    if transposed:
        w_spec = pl.BlockSpec((tn, k), lambda bi, i, j: (j, 0))
        o_spec = pl.BlockSpec((None, tn, tm), lambda bi, i, j: (bi, j, i))
        o_shape = (b, n, t)
    else:
        w_spec = pl.BlockSpec((k, tn), lambda bi, i, j: (0, j))
        o_spec = pl.BlockSpec((None, tm, tn), lambda bi, i, j: (bi, i, j))
        o_shape = (b, t, n)
    return pl.pallas_call(
        functools.partial(_norm_mm_kernel, transposed=transposed),
        grid=(b, t // tm, n // tn),
        in_specs=[pl.BlockSpec((None, tm, k), lambda bi, i, j: (bi, i, 0)),
                  pl.BlockSpec((1, k), lambda bi, i, j: (0, 0)),
                  w_spec],
        out_specs=o_spec,
        out_shape=jax.ShapeDtypeStruct(o_shape, out_dtype),
        scratch_shapes=[pltpu.VMEM((tm, k), BF16)],
        compiler_params=_params("parallel", "parallel", "arbitrary"),
        name="norm_matmul_t" if transposed else "norm_matmul",
    )(x, gain.reshape(1, k), w)


def _ssd_chunk_kernel(xbc_ref, z_ref, dt_ref, dtt_ref, h0_ref, c0_ref, cw_ref, cb_ref, dtb_ref, dtbt_ref,
                      alog_ref, alogt_ref, dfull_ref, nw_ref, exp_ref, y_ref, h_ref, xp_ref):
    q = SSD_CHUNK
    c = pl.program_id(1)

    @pl.when(c == 0)
    def _():
        h_ref[...] = h0_ref[...]
        xp_ref[5:8, :] = c0_ref[...]

    xp_ref[8:8 + q, :] = xbc_ref[...]
    conv = cb_ref[...]
    for j in range(SSD_CONV):
        conv = conv + cw_ref[j:j + 1, :] * xp_ref[5 + j:5 + j + q, :]
    xp_ref[5:8, :] = xp_ref[q + 5:q + 8, :]
    u = conv * _sigmoid(conv)
    xs = u[:, :SSD_INNER]
    bm = u[:, SSD_INNER:SSD_INNER + SSD_GN].astype(BF16)
    cm = u[:, SSD_INNER + SSD_GN:].astype(BF16)

    dt = _softplus(dt_ref[...] + dtb_ref[...])
    dtt = _softplus(dtt_ref[...] + dtbt_ref[...])
    dta = dt * (-jnp.exp(alog_ref[...]))
    dtat = dtt * (-jnp.exp(alogt_ref[...]))
    row = lax.broadcasted_iota(jnp.int32, (q, q), 0)
    col = lax.broadcasted_iota(jnp.int32, (q, q), 1)
    tril = row >= col
    acum = jnp.dot(tril.astype(F32), dta, precision=HI, preferred_element_type=F32)
    acumt = jnp.dot(dtat, (row <= col).astype(F32), precision=HI, preferred_element_type=F32)
    expand = exp_ref[...]
    dt_full = jnp.dot(dt, expand, precision=HI, preferred_element_type=F32)
    ea_full = jnp.dot(jnp.exp(acum), expand, precision=HI, preferred_element_type=F32)
    te_full = jnp.dot(jnp.exp(acum[q - 1:q, :] - acum), expand, precision=HI, preferred_element_type=F32)
    xdt = xs * dt_full
    xdt_b = xdt.astype(BF16)
    xw_b = (xdt * te_full).astype(BF16)

    r = SSD_HEADS // SSD_GROUPS
    gw = r * SSD_HEAD_DIM
    y_diag, y_off = [], []
    for g in range(SSD_GROUPS):
        cm_g = cm[:, g * SSD_STATE:(g + 1) * SSD_STATE]
        bm_g = bm[:, g * SSD_STATE:(g + 1) * SSD_STATE]
        cb = _dot_nt(cm_g, bm_g)
        h_g = h_ref[g * r:(g + 1) * r].reshape(gw, SSD_STATE)
        y_off.append(_dot_nt(cm_g, h_g.astype(BF16)))
        st = _dot_tn(xw_b[:, g * gw:(g + 1) * gw], bm_g)
        for hh in range(r):
            hd = g * r + hh
            seg = acum[:, hd:hd + 1] - acumt[hd:hd + 1, :]
            decay = jnp.exp(jnp.where(tril, seg, -jnp.inf))
            m = (cb * decay).astype(BF16)
            y_diag.append(jnp.dot(m, xdt_b[:, hd * SSD_HEAD_DIM:(hd + 1) * SSD_HEAD_DIM],
                                  preferred_element_type=F32))
            dec = jnp.exp(acumt[hd:hd + 1, q - 1:q])
            h_ref[hd] = dec * h_ref[hd] + st[hh * SSD_HEAD_DIM:(hh + 1) * SSD_HEAD_DIM, :]
    y = (jnp.concatenate(y_diag, axis=1) + jnp.concatenate(y_off, axis=1) * ea_full
         + dfull_ref[...] * xs)
    zz = z_ref[...]
    y = y * (zz * _sigmoid(zz))
    y_ref[...] = _rms(y, nw_ref[...]).astype(y_ref.dtype)


def ssd_prompt(xbc, z, dt_raw, h0, conv0, conv_w, conv_b, dt_bias, a_log, d_skip, norm_w):
    b, t, _ = xbc.shape
    q = SSD_CHUNK
    nc = t // q
    hds = SSD_HEADS
    expand = (jnp.arange(SSD_INNER)[None, :] // SSD_HEAD_DIM == jnp.arange(hds)[:, None]).astype(F32)
    d_full = jnp.repeat(d_skip, SSD_HEAD_DIM).reshape(1, SSD_INNER)
    dtt = jnp.swapaxes(dt_raw, 1, 2)
    full = lambda shape: pl.BlockSpec(shape, lambda bi, ci: (0,) * len(shape))
    y, h = pl.pallas_call(
        _ssd_chunk_kernel,
        grid=(b, nc),
        in_specs=[pl.BlockSpec((None, q, SSD_CONV_DIM), lambda bi, ci: (bi, ci, 0)),
                  pl.BlockSpec((None, q, SSD_INNER), lambda bi, ci: (bi, ci, 0)),
                  pl.BlockSpec((None, q, hds), lambda bi, ci: (bi, ci, 0)),
                  pl.BlockSpec((None, hds, q), lambda bi, ci: (bi, 0, ci)),
                  pl.BlockSpec((None, hds, SSD_HEAD_DIM, SSD_STATE), lambda bi, ci: (bi, 0, 0, 0)),
                  pl.BlockSpec((None, SSD_CONV - 1, SSD_CONV_DIM), lambda bi, ci: (bi, 0, 0)),
                  full((SSD_CONV, SSD_CONV_DIM)), full((1, SSD_CONV_DIM)),
                  full((1, hds)), full((hds, 1)), full((1, hds)), full((hds, 1)),
                  full((1, SSD_INNER)), full((1, SSD_INNER)), full((hds, SSD_INNER))],
        out_specs=[pl.BlockSpec((None, q, SSD_INNER), lambda bi, ci: (bi, ci, 0)),
                   pl.BlockSpec((None, hds, SSD_HEAD_DIM, SSD_STATE), lambda bi, ci: (bi, 0, 0, 0))],
        out_shape=[jax.ShapeDtypeStruct((b, t, SSD_INNER), BF16),
                   jax.ShapeDtypeStruct((b, hds, SSD_HEAD_DIM, SSD_STATE), F32)],
        scratch_shapes=[pltpu.VMEM((q + 8, SSD_CONV_DIM), F32)],
        compiler_params=_params("parallel", "arbitrary"),
        name="ssd_chunk_scan",
    )(xbc, z, dt_raw, dtt, h0, conv0, conv_w, conv_b.reshape(1, -1), dt_bias.reshape(1, hds),
      dt_bias.reshape(hds, 1), a_log.reshape(1, hds), a_log.reshape(hds, 1), d_full, norm_w.reshape(1, -1), expand)
    return y, h


def _sb_stage(z):
    sp = jnp.maximum(z, 0.0) + jnp.log(1.0 + jnp.exp(-jnp.abs(z)))
    hi = sp.astype(BF16)
    return hi, (sp - hi.astype(F32)).astype(BF16)


def _sb_prompt_kernel(q_ref, kt_ref, vt_ref, o_ref, z_ref, hl_ref, a_ref, *, tb):
    i = pl.program_id(2)
    q = (q_ref[...] * (HEAD_DIM ** -0.5)).astype(BF16)
    row = lax.broadcasted_iota(jnp.int32, (tb, tb), 0)
    col = lax.broadcasted_iota(jnp.int32, (tb, tb), 1)
    upper = jnp.where(row >= col, -1.0, 0.0).astype(BF16)
    upper2 = jnp.concatenate([upper, upper], axis=0)

    def tile(ref, s):
        off = pl.multiple_of(jnp.clip(i - s, 0, i) * tb, tb)
        return ref[:, pl.ds(off, tb)].astype(BF16)

    z0 = jnp.where(col < row, jnp.dot(q, tile(kt_ref, 0), preferred_element_type=F32), NEG)
    z_ref[0] = z0
    hl_ref[0, :, :tb], hl_ref[0, :, tb:] = _sb_stage(z0)
    z_ref[1] = jnp.dot(q, tile(kt_ref, 1), preferred_element_type=F32)
    a_ref[2] = jnp.zeros((tb, tb), BF16)

    def step(n, c, st):
        carry, acc = st
        nxt, prv = (c + 1) % 3, (c + 2) % 3
        later = jnp.dot(hl_ref[c], upper2, preferred_element_type=F32)
        acc = acc + _dot_nt(a_ref[prv], tile(vt_ref, n - 1))
        z_ref[prv] = jnp.dot(q, tile(kt_ref, n + 2), preferred_element_type=F32)
        hl_ref[nxt, :, :tb], hl_ref[nxt, :, tb:] = _sb_stage(z_ref[nxt])
        dead = jnp.where(n <= i, 0.0, NEG)
        a_ref[c] = jnp.exp(z_ref[c] + later + (carry + dead)).astype(BF16)
        return carry + later[:, 0:1], acc

    def body(trip, st):
        for c in range(3):
            st = step(3 * trip + c, c, st)
        return st

    st = (jnp.zeros((tb, 1), F32), jnp.zeros((tb, HEAD_DIM), F32))
    _, acc = lax.fori_loop(0, lax.div(i + 4, 3), body, st)
    o_ref[...] = acc.astype(o_ref.dtype)


def sb_prompt(q, kvt, tb=256):
    b, t, _ = q.shape
    h = SB_HEADS
    tb = min(tb, t)
    assert t % tb == 0
    qh = jnp.swapaxes(q.reshape(b, t, h, HEAD_DIM), 1, 2)
    out = pl.pallas_call(
        functools.partial(_sb_prompt_kernel, tb=tb),
        grid=(b, h, t // tb),
        in_specs=[pl.BlockSpec((None, None, tb, HEAD_DIM), lambda bi, hi, i: (bi, hi, i, 0)),
                  pl.BlockSpec((None, HEAD_DIM, t), lambda bi, hi, i: (bi, hi, 0)),
                  pl.BlockSpec((None, HEAD_DIM, t), lambda bi, hi, i: (bi, h + hi, 0))],
        out_specs=pl.BlockSpec((None, None, tb, HEAD_DIM), lambda bi, hi, i: (bi, hi, i, 0)),
        out_shape=jax.ShapeDtypeStruct((b, h, t, HEAD_DIM), BF16),
        scratch_shapes=[pltpu.VMEM((3, tb, tb), F32), pltpu.VMEM((3, tb, 2 * tb), BF16),
                        pltpu.VMEM((3, tb, tb), BF16)],
        compiler_params=_params("parallel", "parallel", "arbitrary"),
        name="sb_prompt",
    )(qh, kvt, kvt)
    return jnp.swapaxes(out, 1, 2).reshape(b, t, h * HEAD_DIM)


def _merge_kernel(x_ref, ssd_ref, sb_ref, nsa_ref, gl_ref, wssd_ref, wsb_ref, wnsa_ref, wo_ref, nw_ref, o_ref):
    d = D_MODEL
    gl = gl_ref[...]
    merged = (_sigmoid(gl[:, :d]) * jnp.dot(ssd_ref[...], wssd_ref[...], preferred_element_type=F32)
              + _sigmoid(gl[:, d:2 * d]) * jnp.dot(sb_ref[...], wsb_ref[...], preferred_element_type=F32)
              + _sigmoid(gl[:, 2 * d:]) * jnp.dot(nsa_ref[...], wnsa_ref[...], preferred_element_type=F32))
    y = jnp.dot(merged.astype(BF16), wo_ref[...], preferred_element_type=F32)
    o_ref[...] = x_ref[...] + _rms(y, nw_ref[...])


def merge_branches(x, ssd_y, sb_o, nsa_o, gate_logits, w_ssd_out, w_sb_out, w_nsa_out, w_o, norm_w):
    m, d = x.shape
    tm = _pick(m, (512, 256, 128, 32))
    rows = lambda n: pl.BlockSpec((tm, n), lambda i: (i, 0))
    full = lambda a: pl.BlockSpec(a.shape, lambda i: (0, 0))
    nw = norm_w.reshape(1, d)
    return pl.pallas_call(
        _merge_kernel,
        grid=(m // tm,),
        in_specs=[rows(d), rows(ssd_y.shape[1]), rows(sb_o.shape[1]), rows(nsa_o.shape[1]), rows(N_BRANCH * d),
                  full(w_ssd_out), full(w_sb_out), full(w_nsa_out), full(w_o), full(nw)],
        out_specs=rows(d),
        out_shape=jax.ShapeDtypeStruct((m, d), F32),
        compiler_params=_params("parallel"),
        name="merge_branches",
    )(x, ssd_y, sb_o, nsa_o, gate_logits, w_ssd_out, w_sb_out, w_nsa_out, w_o, nw)


def _ffn_up_kernel(x_ref, g_ref, wg_ref, wu_ref, o_ref, h_ref):
    @pl.when(pl.program_id(1) == 0)
    def _():
        h_ref[...] = _rms(x_ref[...], g_ref[...]).astype(BF16)

    h = h_ref[...]
    a = jnp.dot(h, wg_ref[...], preferred_element_type=F32)
    u = jnp.dot(h, wu_ref[...], preferred_element_type=F32)
    o_ref[...] = (a * _sigmoid(a) * u).astype(o_ref.dtype)


def _ffn_down_kernel(a_ref, x_ref, wd_ref, nw_ref, o_ref):
    f = jnp.dot(a_ref[...], wd_ref[...], preferred_element_type=F32)
    o_ref[...] = x_ref[...] + _rms(f, nw_ref[...])


def ffn(x, pre_w, post_w, w_gate, w_up, w_down):
    m, d = x.shape
    f = w_gate.shape[1]
    tm = _pick(m, (1024, 512, 256, 128, 32))
    tn = _pick(f, (256, 128))
    act = pl.pallas_call(
        _ffn_up_kernel,
        grid=(m // tm, f // tn),
        in_specs=[pl.BlockSpec((tm, d), lambda i, j: (i, 0)),
                  pl.BlockSpec((1, d), lambda i, j: (0, 0)),
                  pl.BlockSpec((d, tn), lambda i, j: (0, j)),
                  pl.BlockSpec((d, tn), lambda i, j: (0, j))],
        out_specs=pl.BlockSpec((tm, tn), lambda i, j: (i, j)),
        out_shape=jax.ShapeDtypeStruct((m, f), BF16),
        scratch_shapes=[pltpu.VMEM((tm, d), BF16)],
        compiler_params=_params("parallel", "arbitrary"),
        name="ffn_up",
    )(x, pre_w.reshape(1, d), w_gate, w_up)
    tm2 = _pick(m, (512, 256, 128, 32))
    return pl.pallas_call(
        _ffn_down_kernel,
        grid=(m // tm2,),
        in_specs=[pl.BlockSpec((tm2, f), lambda i: (i, 0)),
                  pl.BlockSpec((tm2, d), lambda i: (i, 0)),
                  pl.BlockSpec((f, d), lambda i: (0, 0)),
                  pl.BlockSpec((1, d), lambda i: (0, 0))],
        out_specs=pl.BlockSpec((tm2, d), lambda i: (i, 0)),
        out_shape=jax.ShapeDtypeStruct((m, d), F32),
        compiler_params=_params("parallel"),
        name="ffn_down",
    )(act, x, w_down, post_w.reshape(1, d))


def _nsa_compress_kernel(r_ref, pos_ref, w1_ref, w2_ref, o_ref, sh_ref):
    nr = r_ref.shape[0]
    half = CMP_STRIDE * HEAD_DIM
    r = r_ref[...]
    top = jnp.dot((r + pos_ref[0:1, :]).astype(BF16), w1_ref[:half, :], preferred_element_type=F32)
    bot = jnp.dot((r + pos_ref[1:2, :]).astype(BF16), w1_ref[half:, :], preferred_element_type=F32)
    sh_ref[0:nr, :] = bot
    sh_ref[nr:nr + 8, :] = jnp.zeros((8, CMP_HIDDEN), F32)
    pre = top + sh_ref[1:nr + 1, :]
    hid = pre * _sigmoid(pre)
    o_ref[...] = jnp.dot(hid.astype(BF16), w2_ref[...], preferred_element_type=F32).astype(o_ref.dtype)


def nsa_compress(rows16, cmp_pos, cmp_w1, cmp_w2):
    assert CMP_BLOCK == 2 * CMP_STRIDE
    b, _, g, nr, w = rows16.shape
    pos = cmp_pos.reshape(2, 2, w)
    return pl.pallas_call(
        _nsa_compress_kernel,
        grid=(b, 2, g),
        in_specs=[pl.BlockSpec((None, None, None, nr, w), lambda bi, ki, gi: (bi, ki, gi, 0, 0)),
                  pl.BlockSpec((None, 2, w), lambda bi, ki, gi: (ki, 0, 0)),
                  pl.BlockSpec((None, 2 * w, CMP_HIDDEN), lambda bi, ki, gi: (ki, 0, 0)),
                  pl.BlockSpec((None, CMP_HIDDEN, HEAD_DIM), lambda bi, ki, gi: (ki, 0, 0))],
        out_specs=pl.BlockSpec((None, None, None, nr, HEAD_DIM), lambda bi, ki, gi: (bi, ki, gi, 0, 0)),
        out_shape=jax.ShapeDtypeStruct((b, 2, g, nr, HEAD_DIM), BF16),
        scratch_shapes=[pltpu.VMEM((nr + 8, CMP_HIDDEN), F32)],
        compiler_params=_params("parallel", "parallel", "parallel"),
        name="nsa_compress",
    )(rows16, pos, cmp_w1.astype(BF16), cmp_w2.astype(BF16))


def _overlap_matrix(n_cmp_rows, n_cmp, n_blk):
    c = jnp.arange(n_cmp_rows)[:, None]
    n = jnp.arange(n_blk)[None, :]
    c_start, c_end = c * CMP_STRIDE, c * CMP_STRIDE + CMP_BLOCK - 1
    return ((c_start < (n + 1) * SEL_BLOCK) & (c_end >= n * SEL_BLOCK) & (c < n_cmp)).astype(BF16)


def _split_dot(x, w):
    hi = x.astype(BF16)
    lo = (x - hi.astype(F32)).astype(BF16)
    return jnp.dot(hi, w, preferred_element_type=F32) + jnp.dot(lo, w, preferred_element_type=F32)


def _top_blocks(imp, blk, n_top):
    n_blk = imp.shape[1]
    sel = jnp.zeros(imp.shape, F32)
    for _ in range(n_top):
        m = jnp.max(imp, axis=-1, keepdims=True)
        idx = jnp.min(jnp.where(imp == m, blk, float(n_blk)), axis=-1, keepdims=True)
        hit = blk == idx
        sel = jnp.where(hit, 1.0, sel)
        imp = jnp.where(hit, -jnp.inf, imp)
    return sel


def _flash_step_t(qt, k, vt, mask, m, acc):
    s = jnp.where(mask, jnp.dot(k, qt, preferred_element_type=F32), NEG)
    m_new = jnp.maximum(m, jnp.max(s, axis=0, keepdims=True))
    p = jnp.exp(s - m_new).astype(BF16)
    vt_ext = jnp.concatenate([vt, jnp.ones((8, vt.shape[1]), BF16)], axis=0)
    return m_new, acc * jnp.exp(m - m_new) + jnp.dot(vt_ext, p, preferred_element_type=F32)


def _nsa_prompt_t_kernel(qt_ref, gt_ref, kc_ref, vct_ref, ovt_ref, ks_ref, vs_ref, kw_ref, vw_ref, o_ref,
                         s_ref, ch_ref, p_ref, *, tq, n_cmp):
    i = pl.program_id(2)
    rep, d = NSA_REP, HEAD_DIM
    lanes = rep * tq
    sel_shift = int(math.log2(SEL_BLOCK))
    qt_blk = qt_ref[...]
    qt = jnp.concatenate([qt_blk[r * d:(r + 1) * d, :] for r in range(rep)], axis=1)
    qt = (qt * (d ** -0.5)).astype(BF16)
    qpos = i * tq + lax.broadcasted_iota(jnp.int32, (1, tq), 1)
    per_head = lambda a: jnp.concatenate([a] * rep, axis=1)

    n_rows = kc_ref.shape[0]
    cidx = lax.broadcasted_iota(jnp.int32, (n_rows, 1), 0)
    vis_c = per_head(((cidx * CMP_STRIDE + (CMP_BLOCK - 1) <= qpos) & (cidx < n_cmp)).astype(F32)) > 0.5
    s_c = jnp.where(vis_c, jnp.dot(kc_ref[...], qt, preferred_element_type=F32), NEG)
    e_c = jnp.where(vis_c, jnp.exp(s_c - jnp.max(s_c, axis=0, keepdims=True)), 0.0)
    p_c = e_c / jnp.maximum(jnp.sum(e_c, axis=0, keepdims=True), 1e-30)
    o_c = jnp.dot(vct_ref[...], p_c.astype(BF16), preferred_element_type=F32)

    p_sum = p_c[:, 0:tq]
    for r in range(1, rep):
        p_sum = p_sum + p_c[:, r * tq:(r + 1) * tq]
    p_hi = p_sum.astype(BF16)
    p_lo = (p_sum - p_hi.astype(F32)).astype(BF16)
    imp = (jnp.dot(ovt_ref[...], p_hi, preferred_element_type=F32)
           + jnp.dot(ovt_ref[...], p_lo, preferred_element_type=F32))
    n_blk = ovt_ref.shape[0]
    blk_i = lax.broadcasted_iota(jnp.int32, (n_blk, 1), 0)
    cur = lax.shift_right_logical(qpos, sel_shift)
    valid = blk_i <= cur
    forced = valid & ((blk_i == 0) | (blk_i > cur - SEL_LOCAL))
    imp = jnp.where(forced, jnp.inf, jnp.where(valid, imp, -jnp.inf))

    m0 = jnp.full((1, lanes), NEG, F32)
    acc0 = jnp.zeros((d + 8, lanes), F32)
    krow = lax.broadcasted_iota(jnp.int32, (tq, 1), 0)

    st = (m0, acc0)
    for n in range(WINDOW // tq + 1):
        j = i - n
        off = pl.multiple_of(jnp.maximum(j, 0) * tq, tq)
        kpos = off + krow + jnp.where(j < 0, 1 << 30, 0)
        mask = per_head(((kpos <= qpos) & (kpos > qpos - WINDOW)).astype(F32)) > 0.5
        st = _flash_step_t(qt, kw_ref[pl.ds(off, tq), :].astype(BF16), vw_ref[:, pl.ds(off, tq)].astype(BF16),
                           mask, *st)
    acc_w = st[1]

    blk_f = blk_i.astype(F32)
    sel = jnp.zeros((n_blk, tq), F32)
    for _ in range(min(SEL_TOP, n_blk)):
        top = jnp.max(imp, axis=0, keepdims=True)
        idx = jnp.min(jnp.where(imp == top, blk_f, float(n_blk)), axis=0, keepdims=True)
        hit = blk_f == idx
        sel = jnp.where(hit, 1.0, sel)
        imp = jnp.where(hit, -jnp.inf, imp)
    sel = sel.astype(BF16)

    exp_blk = lax.broadcasted_iota(jnp.int32, (tq, n_blk), 1)
    exp_key = lax.shift_right_logical(lax.broadcasted_iota(jnp.int32, (tq, n_blk), 0), sel_shift)

    def key_tile(n):
        return jnp.clip(i - n, 0, i)

    def scores(n):
        j = key_tile(n)
        expand = (exp_blk == j * (tq // SEL_BLOCK) + exp_key).astype(BF16)
        return (jnp.dot(ks_ref[pl.ds(pl.multiple_of(j * tq, tq), tq), :].astype(BF16), qt, preferred_element_type=F32),
                jnp.dot(expand, sel, preferred_element_type=F32))

    def weighted_values(n, slot):
        vt = vs_ref[:, pl.ds(pl.multiple_of(key_tile(n) * tq, tq), tq)].astype(BF16)
        return jnp.dot(jnp.concatenate([vt, jnp.ones((8, tq), BF16)], axis=0), p_ref[slot],
                       preferred_element_type=F32)

    s_ref[0], ch_ref[0] = scores(0)
    p_ref[0] = jnp.zeros(p_ref.shape[1:], BF16)

    def sel_step(n, rd, wr, st):
        m, alpha, acc = st
        pv = weighted_values(n - 1, rd)
        s_ref[wr], ch_ref[wr] = scores(n + 1)
        kpos = (i - n) * tq + krow + jnp.where(n > i, 1 << 30, 0)
        mask = per_head(jnp.where(kpos <= qpos, ch_ref[rd], 0.0)) > 0.5
        s = jnp.where(mask, s_ref[rd], NEG)
        m_new = jnp.maximum(m, jnp.max(s, axis=0, keepdims=True))
        p_ref[wr] = jnp.exp(s - m_new).astype(BF16)
        return m_new, jnp.exp(m - m_new), acc * alpha + pv

    def sel_pair(pair, st):
        return sel_step(2 * pair + 1, 1, 0, sel_step(2 * pair, 0, 1, st))

    _, _, acc_s = lax.fori_loop(0, (i + 3) >> 1, sel_pair, (m0, jnp.ones((1, lanes), F32), acc0))

    o_s = acc_s[:d] / acc_s[d:d + 1]
    o_w = acc_w[:d] / acc_w[d:d + 1]
    gate = _sigmoid(gt_ref[...])
    outs = []
    for r in range(rep):
        sl = slice(r * tq, (r + 1) * tq)
        outs.append(gate[3 * r:3 * r + 1] * o_c[:, sl] + gate[3 * r + 1:3 * r + 2] * o_s[:, sl]
                    + gate[3 * r + 2:3 * r + 3] * o_w[:, sl])
    o_ref[...] = jnp.concatenate(outs, axis=0).astype(o_ref.dtype)


def nsa_prompt_t(nsa_qt, nsa_g, nsa_kvt, cmp_pos, cmp_w1, cmp_w2, tq=128):
    b, _, t = nsa_qt.shape
    g, d, rep = NSA_KV_GROUPS, HEAD_DIM, NSA_REP
    assert t % tq == 0 and tq % SEL_BLOCK == 0 and WINDOW % tq == 0 and t % CMP_STRIDE == 0
    n_cmp = (t - CMP_BLOCK) // CMP_STRIDE + 1
    nr = t // CMP_STRIDE
    n_blk = t // SEL_BLOCK
    kinds = nsa_kvt.reshape(b, 6, g, d, t)
    rows16 = jnp.swapaxes(kinds[:, 0:2], 3, 4).reshape(b, 2, g, nr, CMP_STRIDE * d)
    kcvc = nsa_compress(rows16, cmp_pos, cmp_w1, cmp_w2)
    vct = jnp.swapaxes(kcvc[:, 1], 2, 3)
    overlap_t = _overlap_matrix(nr, n_cmp, n_blk).T
    keys = jnp.swapaxes(kinds[:, 2::2], 3, 4).astype(BF16)
    gates = jnp.transpose(nsa_g.reshape(b, t, g, 3 * rep), (0, 2, 3, 1))
    gates = jnp.pad(gates, ((0, 0), (0, 0), (0, 16 - 3 * rep), (0, 0)))
    values = lambda kind: pl.BlockSpec((None, d, t), lambda bi, gi, i: (bi, kind * g + gi, 0))
    rows = lambda kind: pl.BlockSpec((None, None, None, t, d), lambda bi, gi, i: (bi, kind, gi, 0, 0))
    out = pl.pallas_call(
        functools.partial(_nsa_prompt_t_kernel, tq=tq, n_cmp=n_cmp),
        grid=(b, g, t // tq),
        in_specs=[pl.BlockSpec((None, rep * d, tq), lambda bi, gi, i: (bi, gi, i)),
                  pl.BlockSpec((None, None, 16, tq), lambda bi, gi, i: (bi, gi, 0, i)),
                  pl.BlockSpec((None, None, None, nr, d), lambda bi, gi, i: (bi, 0, gi, 0, 0)),
                  pl.BlockSpec((None, None, d, nr), lambda bi, gi, i: (bi, gi, 0, 0)),
                  pl.BlockSpec((n_blk, nr), lambda bi, gi, i: (0, 0)),
                  rows(0), values(3), rows(1), values(5)],
        out_specs=pl.BlockSpec((None, rep * d, tq), lambda bi, gi, i: (bi, gi, i)),
        out_shape=jax.ShapeDtypeStruct((b, g * rep * d, t), BF16),
        scratch_shapes=[pltpu.VMEM((2, tq, rep * tq), F32), pltpu.VMEM((2, tq, tq), F32),
                        pltpu.VMEM((2, tq, rep * tq), BF16)],
        compiler_params=_params("parallel", "parallel", "arbitrary"),
        name="nsa_prompt",
    )(nsa_qt, gates, kcvc, vct, overlap_t, keys, nsa_kvt, keys, nsa_kvt)
    return jnp.swapaxes(out, 1, 2)


def _ssd_step_pre_kernel(x_ref, buf_ref, cw_ref, cb_ref, dt_ref, dtb_ref, alog_ref, exp_ref,
                         xs_ref, xdt_ref, bm_ref, cm_ref, dec_ref):
    conv = cb_ref[...] + cw_ref[SSD_CONV - 1:SSD_CONV, :] * x_ref[...]
    for j in range(SSD_CONV - 1):
        conv = conv + cw_ref[j:j + 1, :] * buf_ref[j]
    u = conv * _sigmoid(conv)
    xs = u[:, :SSD_INNER]
    dt = _softplus(dt_ref[...] + dtb_ref[...])
    xs_ref[...] = xs
    xdt_ref[...] = xs * jnp.dot(dt, exp_ref[...], precision=HI, preferred_element_type=F32)
    bm_ref[...] = u[:, SSD_INNER:SSD_INNER + SSD_GN]
    cm_ref[...] = u[:, SSD_INNER + SSD_GN:]
    dec_ref[...] = jnp.exp(dt * (-jnp.exp(alog_ref[...])))


def _ssd_step_state_kernel(h0_ref, xdt_ref, dec_ref, bm_ref, cm_ref, h_ref, y_ref):
    r = SSD_HEADS // SSD_GROUPS
    for hd in range(SSD_HEADS):
        g = hd // r
        hn = dec_ref[hd] * h0_ref[hd] + xdt_ref[hd] * bm_ref[g]
        h_ref[hd] = hn
        y_ref[hd] = jnp.sum(hn * cm_ref[g], axis=-1, keepdims=True)


def _ssd_step_post_kernel(y_ref, xs_ref, z_ref, dfull_ref, nw_ref, o_ref):
    zz = z_ref[...]
    y = (y_ref[...] + dfull_ref[...] * xs_ref[...]) * (zz * _sigmoid(zz))
    o_ref[...] = _rms(y, nw_ref[...]).astype(o_ref.dtype)


def ssd_step(xbc, z, dt_raw, h0, conv_buf, conv_w, conv_b, dt_bias, a_log, d_skip, norm_w):
    b = xbc.shape[0]
    hds, p, n = SSD_HEADS, SSD_HEAD_DIM, SSD_STATE
    expand = (jnp.arange(SSD_INNER)[None, :] // p == jnp.arange(hds)[:, None]).astype(F32)
    d_full = jnp.repeat(d_skip, p).reshape(1, SSD_INNER)
    sds = lambda shape: jax.ShapeDtypeStruct(shape, F32)
    xs, xdt, bm, cm, dec = pl.pallas_call(
        _ssd_step_pre_kernel,
        out_shape=[sds((b, SSD_INNER)), sds((b, SSD_INNER)), sds((b, SSD_GN)), sds((b, SSD_GN)), sds((b, hds))],
        name="ssd_step_pre",
    )(xbc, jnp.swapaxes(conv_buf, 0, 1), conv_w, conv_b.reshape(1, -1), dt_raw, dt_bias.reshape(1, hds),
      a_log.reshape(1, hds), expand)
    per_b = lambda *dims: pl.BlockSpec((None,) + dims, lambda bi: (bi,) + (0,) * len(dims))
    h_new, y_col = pl.pallas_call(
        _ssd_step_state_kernel,
        grid=(b,),
        in_specs=[per_b(hds, p, n), per_b(hds, p, 1), per_b(hds, 1, 1), per_b(SSD_GROUPS, 1, n),
                  per_b(SSD_GROUPS, 1, n)],
        out_specs=[per_b(hds, p, n), per_b(hds, p, 1)],
        out_shape=[sds((b, hds, p, n)), sds((b, hds, p, 1))],
        compiler_params=_params("parallel"),
        name="ssd_step_state",
    )(h0, xdt.reshape(b, hds, p, 1), dec.reshape(b, hds, 1, 1), bm.reshape(b, SSD_GROUPS, 1, n),
      cm.reshape(b, SSD_GROUPS, 1, n))
    y = pl.pallas_call(
        _ssd_step_post_kernel,
        out_shape=jax.ShapeDtypeStruct((b, SSD_INNER), BF16),
        name="ssd_step_post",
    )(y_col.reshape(b, SSD_INNER), xs, z, d_full, norm_w.reshape(1, -1))
    return y, h_new


def _sb_decode_kernel(pt_ref, q_ref, *refs):
    page_refs, (o_ref, carry_ref, acc_ref) = refs[:PAGES_PER_STEP], refs[PAGES_PER_STEP:]
    p = pl.program_id(1)
    tk = page_refs[0].shape[-1]
    hds = SB_HEADS

    @pl.when(p == 0)
    def _():
        carry_ref[...] = jnp.zeros(carry_ref.shape, F32)
        acc_ref[...] = jnp.zeros(acc_ref.shape, F32)

    scale = HEAD_DIM ** -0.5
    qs = [q_ref[h] * scale for h in range(hds)]
    z = jnp.concatenate([jnp.sum(ref[0, h] * qs[h], axis=0, keepdims=True)
                         for ref in page_refs for h in range(hds)], axis=0)
    row = lax.broadcasted_iota(jnp.int32, (tk, tk), 0)
    col = lax.broadcasted_iota(jnp.int32, (tk, tk), 1)
    sp = _softplus(z)
    later = _split_dot(-sp, (row > col).astype(BF16))
    total = later[:, 0:1] - sp[:, 0:1]
    carry = carry_ref[...]
    for k, ref in enumerate(page_refs):
        rows = slice(k * hds, (k + 1) * hds)
        a = jnp.exp(z[rows] - sp[rows] + later[rows] + carry)
        carry = carry + total[rows]
        for h in range(hds):
            acc_ref[h] += ref[1, h] * a[h:h + 1, :]
    carry_ref[...] = carry

    @pl.when(p == pl.num_programs(1) - 1)
    def _():
        for h in range(hds):
            o_ref[h] = jnp.sum(acc_ref[h], axis=-1, keepdims=True)


def _page_specs(block, layer, n_pages, kind_block, descending):
    def spec(k):
        def index(bi, p, pt):
            pos = p * PAGES_PER_STEP + k
            pos = n_pages - 1 - pos if descending else pos
            return (layer, pt[bi, pos], kind_block) + (0,) * (len(block) - 3)
        return pl.BlockSpec(block, index)
    return [spec(k) for k in range(PAGES_PER_STEP)]


def sb_decode(q, pool_t, layer, page_table):
    b = q.shape[0]
    h, d = SB_HEADS, HEAD_DIM
    n_pages = page_table.shape[1]
    page = pool_t.shape[-1]
    assert n_pages % PAGES_PER_STEP == 0
    out = pl.pallas_call(
        _sb_decode_kernel,
        grid_spec=pltpu.PrefetchScalarGridSpec(
            num_scalar_prefetch=1,
            grid=(b, n_pages // PAGES_PER_STEP),
            in_specs=[pl.BlockSpec((None, h, d, 1), lambda bi, p, pt: (bi, 0, 0, 0))]
            + _page_specs((None, None, 2, h, d, page), layer, n_pages, 0, descending=True),
            out_specs=pl.BlockSpec((None, h, d, 1), lambda bi, p, pt: (bi, 0, 0, 0)),
            scratch_shapes=[pltpu.VMEM((h, 1), F32), pltpu.VMEM((h, d, page), F32)]),
        out_shape=jax.ShapeDtypeStruct((b, h, d, 1), F32),
        compiler_params=_params("parallel", "arbitrary"),
        name="sb_decode",
    )(page_table, q.reshape(b, h, d, 1), *([pool_t] * PAGES_PER_STEP))
    return out.reshape(b, h * d)


def _nsa_gather_kernel(pt_ref, *refs):
    page_refs, (o_ref, x_ref) = refs[:PAGES_PER_STEP], refs[PAGES_PER_STEP:]
    page = page_refs[0].shape[-1]
    grp, d = NSA_KV_GROUPS, HEAD_DIM
    n_out = page // CMP_STRIDE
    for k, ref in enumerate(page_refs):
        for kind in range(2):
            x = x_ref.at[2 * k + kind]
            x[...] = ref[kind].reshape(grp * d, page).T
            steps = [x[pl.ds(s, n_out, stride=CMP_STRIDE), :] for s in range(CMP_STRIDE)]
            for g in range(grp):
                o_ref[kind, g, k * n_out:(k + 1) * n_out, :] = jnp.concatenate(
                    [st[:, g * d:(g + 1) * d] for st in steps], axis=1)


def nsa_gather_rows16(pool_t, layer, page_table):
    b, n_pages = page_table.shape
    g, d = NSA_KV_GROUPS, HEAD_DIM
    page = pool_t.shape[-1]
    assert page % CMP_STRIDE == 0 and n_pages % PAGES_PER_STEP == 0
    n_out = PAGES_PER_STEP * (page // CMP_STRIDE)
    return pl.pallas_call(
        _nsa_gather_kernel,
        grid_spec=pltpu.PrefetchScalarGridSpec(
            num_scalar_prefetch=1,
            grid=(b, n_pages // PAGES_PER_STEP),
            in_specs=_page_specs((None, None, 2, g, d, page), layer, n_pages, 0, descending=False),
            out_specs=pl.BlockSpec((None, 2, g, n_out, CMP_STRIDE * d), lambda bi, p, pt: (bi, 0, 0, p, 0)),
            scratch_shapes=[pltpu.VMEM((2 * PAGES_PER_STEP, page, g * d), F32)]),
        out_shape=jax.ShapeDtypeStruct((b, 2, g, n_pages * (page // CMP_STRIDE), CMP_STRIDE * d), F32),
        compiler_params=_params("parallel", "arbitrary"),
        name="nsa_gather_rows16",
    )(page_table, *([pool_t] * PAGES_PER_STEP))


def _nsa_decode_select_kernel(q_ref, kc_ref, vc_ref, ov_ref, oc_ref, sel_ref, *, n_cmp, q_pos):
    rep, d = NSA_REP, HEAD_DIM
    n_rows = kc_ref.shape[1]
    n_blk = ov_ref.shape[1]
    q = (q_ref[...] * (d ** -0.5)).astype(BF16)
    cidx = lax.broadcasted_iota(jnp.int32, (1, n_rows), 1)
    vis = (cidx * CMP_STRIDE + (CMP_BLOCK - 1) <= q_pos) & (cidx < n_cmp)
    o_c, imp = [], []
    for g in range(NSA_KV_GROUPS):
        s = jnp.where(vis, _dot_nt(q[g * rep:(g + 1) * rep], kc_ref[g]), NEG)
        e = jnp.where(vis, jnp.exp(s - jnp.max(s, axis=-1, keepdims=True)), 0.0)
        p = e / jnp.maximum(jnp.sum(e, axis=-1, keepdims=True), 1e-30)
        o_c.append(jnp.dot(p.astype(BF16), vc_ref[g], preferred_element_type=F32))
        imp.append(_split_dot(jnp.sum(p, axis=0, keepdims=True), ov_ref[...]))
    oc_ref[...] = jnp.concatenate(o_c, axis=0)
    imp = jnp.concatenate(imp, axis=0)
    blk_i = lax.broadcasted_iota(jnp.int32, (1, n_blk), 1)
    forced = (blk_i == 0) | (blk_i > n_blk - SEL_LOCAL)
    sel = _top_blocks(jnp.where(forced, jnp.inf, imp), blk_i.astype(F32), min(SEL_TOP - 1, n_blk))
    sel_ref[...] = jnp.concatenate([sel, jnp.zeros((sel_ref.shape[0] - NSA_KV_GROUPS, n_blk), F32)], axis=0)


def _nsa_decode_attend_kernel(pt_ref, q_ref, sel_ref, *refs, win_skip):
    page_refs = refs[:PAGES_PER_STEP]
    new_ref, win_ref, oc_ref, gt_ref, o_ref, m_ref, acc_ref = refs[PAGES_PER_STEP:]
    p = pl.program_id(1)
    n_steps = pl.num_programs(1)
    rep, d, grp = NSA_REP, HEAD_DIM, NSA_KV_GROUPS
    tk = page_refs[0].shape[-1]
    lanes = PAGES_PER_STEP * tk
    scale = d ** -0.5
    qf = q_ref[...] * scale
    q = qf.astype(BF16)
    new = new_ref[...]
    new_row = lambda kind, g: new[kind * grp + g:kind * grp + g + 1, :]
    per_head = lambda f: jnp.concatenate([f(g) for g in range(grp)], axis=0)

    @pl.when(p == 0)
    def _():
        m_ref[...] = per_head(lambda g: jnp.sum(qf[g * rep:(g + 1) * rep] * new_row(2, g), axis=-1, keepdims=True))
        acc_ref[...] = per_head(lambda g: jnp.concatenate(
            [jnp.broadcast_to(new_row(3, g), (rep, d)), jnp.ones((rep, d), F32)], axis=1))

    n_blk = sel_ref.shape[1]
    lane = lax.broadcasted_iota(jnp.int32, (n_blk, lanes), 1)
    page_pos = (n_steps - p) * PAGES_PER_STEP - 1 - lax.shift_right_logical(lane, int(math.log2(tk)))
    blk_of_lane = page_pos * (tk // SEL_BLOCK) + lax.shift_right_logical(lane & (tk - 1), int(math.log2(SEL_BLOCK)))
    expand = (lax.broadcasted_iota(jnp.int32, (n_blk, lanes), 0) == blk_of_lane).astype(BF16)
    chosen = jnp.dot(sel_ref[...].astype(BF16), expand, preferred_element_type=F32)

    def scores(g):
        qg = q[g * rep:(g + 1) * rep]
        sg = jnp.concatenate([jnp.dot(qg, ref[0, g].astype(BF16), preferred_element_type=F32) for ref in page_refs],
                             axis=1)
        return jnp.where(chosen[g:g + 1, :] > 0.5, sg, NEG)

    s = per_head(scores)
    m_old = m_ref[...]
    m_new = jnp.maximum(m_old, jnp.max(s, axis=-1, keepdims=True))
    pr = jnp.exp(s - m_new).astype(BF16)

    def weighted_values(g):
        out = jnp.zeros((rep, 2 * d), F32)
        for k, ref in enumerate(page_refs):
            vt_ext = jnp.concatenate([ref[1, g].astype(BF16), jnp.ones((d, tk), BF16)], axis=0)
            out = out + _dot_nt(pr[g * rep:(g + 1) * rep, k * tk:(k + 1) * tk], vt_ext)
        return out

    acc_ref[...] = acc_ref[...] * jnp.exp(m_old - m_new) + per_head(weighted_values)
    m_ref[...] = m_new

    @pl.when(p == n_steps - 1)
    def _():
        acc = acc_ref[...]
        o_s = acc[:, :d] / acc[:, d:]
        wlen = win_ref.shape[-1]
        vis = lax.broadcasted_iota(jnp.int32, (1, wlen), 1) >= win_skip

        def window(g):
            qg = q[g * rep:(g + 1) * rep]
            s_w = jnp.where(vis, jnp.dot(qg, win_ref[0, g].astype(BF16), preferred_element_type=F32), NEG)
            s_n = jnp.sum(qf[g * rep:(g + 1) * rep] * new_row(4, g), axis=-1, keepdims=True)
            mx = jnp.maximum(jnp.max(s_w, axis=-1, keepdims=True), s_n)
            e_w = jnp.where(vis, jnp.exp(s_w - mx), 0.0)
            e_n = jnp.exp(s_n - mx)
            num = _dot_nt(e_w.astype(BF16), win_ref[1, g].astype(BF16)) + e_n * new_row(5, g)
            return num / (jnp.sum(e_w, axis=-1, keepdims=True) + e_n)

        o_w = per_head(window)
        gate = _sigmoid(gt_ref[...])
        o_ref[...] = gate[:, 0:1] * oc_ref[...] + gate[:, 1:2] * o_s + gate[:, 2:3] * o_w


def nsa_decode(nsa_q, nsa_g, nsa_kv_new, pool_t, win_t, layer, page_table, cmp_pos, cmp_w1, cmp_w2):
    b = nsa_q.shape[0]
    g, d, rep, hds = NSA_KV_GROUPS, HEAD_DIM, NSA_REP, NSA_HEADS
    n_pages = page_table.shape[1]
    page = pool_t.shape[-1]
    past = n_pages * page
    wlen = win_t.shape[-1]
    assert past % SEL_BLOCK == 0 and past % CMP_STRIDE == 0 and page % SEL_BLOCK == 0 and wlen <= past
    n_cmp = (past + 1 - CMP_BLOCK) // CMP_STRIDE + 1
    nr = past // CMP_STRIDE
    n_blk = past // SEL_BLOCK
    rows16 = nsa_gather_rows16(pool_t, layer, page_table)
    kcvc = nsa_compress(rows16, cmp_pos, cmp_w1, cmp_w2)
    overlap = _overlap_matrix(nr, n_cmp, n_blk)
    q3 = nsa_q.reshape(b, hds, d)
    per_b = lambda *dims: pl.BlockSpec((None,) + dims, lambda bi: (bi,) + (0,) * len(dims))
    o_c, sel = pl.pallas_call(
        functools.partial(_nsa_decode_select_kernel, n_cmp=n_cmp, q_pos=past),
        grid=(b,),
        in_specs=[per_b(hds, d),
                  pl.BlockSpec((None, None, g, nr, d), lambda bi: (bi, 0, 0, 0, 0)),
                  pl.BlockSpec((None, None, g, nr, d), lambda bi: (bi, 1, 0, 0, 0)),
                  pl.BlockSpec((nr, n_blk), lambda bi: (0, 0))],
        out_specs=[per_b(hds, d), per_b(8, n_blk)],
        out_shape=[jax.ShapeDtypeStruct((b, hds, d), F32), jax.ShapeDtypeStruct((b, 8, n_blk), F32)],
        compiler_params=_params("parallel"),
        name="nsa_decode_select",
    )(q3, kcvc, kcvc, overlap)
    gates = jnp.pad(nsa_g.reshape(b, hds, 3), ((0, 0), (0, 0), (0, V7X_LANES - 3)))
    fixed = lambda *dims: pl.BlockSpec((None,) + dims, lambda bi, p, pt: (bi,) + (0,) * len(dims))
    out = pl.pallas_call(
        functools.partial(_nsa_decode_attend_kernel, win_skip=wlen - WINDOW + 1),
        grid_spec=pltpu.PrefetchScalarGridSpec(
            num_scalar_prefetch=1,
            grid=(b, n_pages // PAGES_PER_STEP),
            in_specs=[fixed(hds, d), fixed(8, n_blk)]
            + _page_specs((None, None, 2, g, d, page), layer, n_pages, 1, descending=True)
            + [fixed(6 * g, d),
               pl.BlockSpec((None, None, 2, g, d, wlen), lambda bi, p, pt: (layer, bi, 0, 0, 0, 0)),
               fixed(hds, d), fixed(hds, V7X_LANES)],
            out_specs=fixed(hds, d),
            scratch_shapes=[pltpu.VMEM((hds, 1), F32), pltpu.VMEM((hds, 2 * d), F32)]),
        out_shape=jax.ShapeDtypeStruct((b, hds, d), F32),
        compiler_params=_params("parallel", "arbitrary"),
        name="nsa_decode_attend",
    )(page_table, q3, sel, *([pool_t] * PAGES_PER_STEP), nsa_kv_new.reshape(b, 6 * g, d), win_t, o_c, gates)
    return out.reshape(b, hds * d)


def _col_offsets():
    offs, s = [], 0
    for n in IN_SPLITS:
        offs.append(s)
        s += n
    return offs


def _layer_weights(l, p):
    o = _col_offsets()
    w_in = p['w_in'][l]
    cols = lambda a, n: w_in[:, a:a + n]
    small = jnp.concatenate([cols(o[2], SSD_HEADS), cols(o[6], 3 * NSA_HEADS)], axis=1)
    small = jnp.pad(small, ((0, 0), (0, V7X_LANES - small.shape[1])))
    bf = lambda a: a.astype(BF16)
    return {
        'w_z': bf(cols(o[0], SSD_INNER)), 'w_xbc': bf(cols(o[1], SSD_CONV_DIM)), 'w_small': bf(small),
        'w_sbq': bf(cols(o[3], SB_WIDTH)), 'w_sbkv': bf(cols(o[3] + SB_WIDTH, 2 * SB_WIDTH)),
        'w_nq': bf(cols(o[4], NSA_WIDTH)), 'w_nkv': bf(cols(o[5], 6 * NSA_KV_WIDTH)),
        'w_brg': bf(cols(o[7], N_BRANCH * D_MODEL)),
        'mix_pre': p['norm_mix_pre'][l], 'mix_post': p['norm_mix_post'][l],
        'ffn_pre': p['norm_ffn_pre'][l], 'ffn_post': p['norm_ffn_post'][l],
        'conv_w': p['ssd_conv_w'][l], 'conv_b': p['ssd_conv_b'][l], 'dt_bias': p['ssd_dt_bias'][l],
        'a_log': p['ssd_a_log'][l], 'd_skip': p['ssd_d'][l], 'ssd_norm': p['ssd_norm'][l],
        'w_ssd_out': bf(p['w_ssd_out'][l]), 'w_sb_out': bf(p['w_sb_out'][l]), 'w_nsa_out': bf(p['w_nsa_out'][l]),
        'w_o': bf(p['w_o'][l]), 'w_ffn_gate': bf(p['w_ffn_gate'][l]), 'w_ffn_up': bf(p['w_ffn_up'][l]),
        'w_ffn_down': bf(p['w_ffn_down'][l]),
        'cmp_pos': p['nsa_cmp_pos'][l], 'cmp_w1': p['nsa_cmp_w1'][l], 'cmp_w2': p['nsa_cmp_w2'][l],
    }


def _trunk_tail(x, ssd_y, sb_o, nsa_o, br_g, lw):
    b, t, d = x.shape
    m = b * t
    x1 = merge_branches(x.reshape(m, d), ssd_y.reshape(m, -1), sb_o.reshape(m, -1), nsa_o.reshape(m, -1),
                        br_g.reshape(m, -1), lw['w_ssd_out'], lw['w_sb_out'], lw['w_nsa_out'], lw['w_o'],
                        lw['mix_post'])
    x2 = ffn(x1, lw['ffn_pre'], lw['ffn_post'], lw['w_ffn_gate'], lw['w_ffn_up'], lw['w_ffn_down'])
    return x2.reshape(b, t, d)


def _layer_prompt(x, lw):
    b, t, _ = x.shape
    g = lw['mix_pre']
    z = norm_matmul(x, g, lw['w_z'])
    xbc = norm_matmul(x, g, lw['w_xbc'])
    small = norm_matmul(x, g, lw['w_small'])
    br_g = norm_matmul(x, g, lw['w_brg'])
    sb_q = norm_matmul(x, g, lw['w_sbq'], out_dtype=BF16)
    nsa_qt = norm_matmul(x, g, lw['w_nq'].T, transposed=True)
    sb_kvt = norm_matmul(x, g, lw['w_sbkv'].T, transposed=True)
    nsa_kvt = norm_matmul(x, g, lw['w_nkv'].T, transposed=True)
    dt_raw = small[..., :SSD_HEADS]
    nsa_g = small[..., SSD_HEADS:SSD_HEADS + 3 * NSA_HEADS]

    h0 = jnp.zeros((b, SSD_HEADS, SSD_HEAD_DIM, SSD_STATE), F32)
    conv0 = jnp.zeros((b, SSD_CONV - 1, SSD_CONV_DIM), F32)
    ssd_y, h_new = ssd_prompt(xbc, z, dt_raw, h0, conv0, lw['conv_w'], lw['conv_b'], lw['dt_bias'], lw['a_log'],
                              lw['d_skip'], lw['ssd_norm'])
    conv_new = xbc[:, t - (SSD_CONV - 1):, :]
    sb_o = sb_prompt(sb_q, sb_kvt)

    nsa_o = nsa_prompt_t(nsa_qt, nsa_g, nsa_kvt, lw['cmp_pos'], lw['cmp_w1'], lw['cmp_w2'])

    y = _trunk_tail(x, ssd_y, sb_o, nsa_o, br_g, lw)
    sb_kv = jnp.moveaxis(sb_kvt.reshape(b, 2, SB_HEADS, HEAD_DIM, t), 4, 1)
    nsa_all = jnp.moveaxis(nsa_kvt.reshape(b, 6, NSA_KV_GROUPS, HEAD_DIM, t), 4, 1)
    keep = min(WINDOW, t)
    return y, sb_kv, nsa_all[:, :, 0:4], nsa_all[:, t - keep:, 4:6], h_new, conv_new


def _layer_sample(x, lw, layer, sb_pool_t, nsa_pool_t, win_t, h0, conv_buf, page_table):
    bsz, t = x.shape[:2]
    assert t == 1
    past = page_table.shape[1] * sb_pool_t.shape[-1]
    xr = x.reshape(1, bsz, D_MODEL)
    g = lw['mix_pre']
    pr = lambda w: norm_matmul(xr, g, w)[0]
    z, xbc, small, br_g = pr(lw['w_z']), pr(lw['w_xbc']), pr(lw['w_small']), pr(lw['w_brg'])
    sb_q, sb_kv, nsa_q, nsa_kv = pr(lw['w_sbq']), pr(lw['w_sbkv']), pr(lw['w_nq']), pr(lw['w_nkv'])
    dt_raw = small[:, :SSD_HEADS]
    nsa_g = small[:, SSD_HEADS:SSD_HEADS + 3 * NSA_HEADS]
    ssd_y, h_new = ssd_step(xbc, z, dt_raw, h0, conv_buf, lw['conv_w'], lw['conv_b'], lw['dt_bias'], lw['a_log'],
                            lw['d_skip'], lw['ssd_norm'])
    conv_new = jnp.concatenate([conv_buf[:, 1:], xbc[:, None, :]], axis=1)
    sb_o = sb_decode(sb_q, sb_pool_t, layer, page_table)
    nsa_o = nsa_decode(nsa_q, nsa_g, nsa_kv, nsa_pool_t, win_t, layer, page_table, lw['cmp_pos'], lw['cmp_w1'],
                       lw['cmp_w2'])
    y = _trunk_tail(x, ssd_y[:, None], sb_o.astype(BF16)[:, None], nsa_o.astype(BF16)[:, None], br_g[:, None], lw)
    kv_new = sb_kv.reshape(bsz, 1, 2, SB_HEADS, HEAD_DIM)
    nkv = nsa_kv.reshape(bsz, 1, 6, NSA_KV_GROUPS, HEAD_DIM)
    keep = min(WINDOW, past + 1)
    win_all_t = jnp.concatenate([win_t[layer], nkv[:, 0, 4:6][..., None]], axis=-1)
    win_new = jnp.moveaxis(win_all_t[..., win_all_t.shape[-1] - keep:], 4, 1)
    return y, kv_new, nkv[:, :, 0:4], win_new, h_new, conv_new


def kernel(x_prompt, x_sample, cache_sb_kv, cache_nsa_kv, cache_nsa_win, state_ssd, state_conv, page_table,
           norm_mix_pre, norm_mix_post, norm_ffn_pre, norm_ffn_post, w_in, ssd_conv_w, ssd_conv_b, ssd_dt_bias,
           ssd_a_log, ssd_d, ssd_norm, w_ssd_out, w_sb_out, nsa_cmp_pos, nsa_cmp_w1, nsa_cmp_w2, w_nsa_out, w_o,
           w_ffn_gate, w_ffn_up, w_ffn_down):
    p = dict(norm_mix_pre=norm_mix_pre, norm_mix_post=norm_mix_post, norm_ffn_pre=norm_ffn_pre,
             norm_ffn_post=norm_ffn_post, w_in=w_in, ssd_conv_w=ssd_conv_w, ssd_conv_b=ssd_conv_b,
             ssd_dt_bias=ssd_dt_bias, ssd_a_log=ssd_a_log, ssd_d=ssd_d, ssd_norm=ssd_norm, w_ssd_out=w_ssd_out,
             w_sb_out=w_sb_out, nsa_cmp_pos=nsa_cmp_pos, nsa_cmp_w1=nsa_cmp_w1, nsa_cmp_w2=nsa_cmp_w2,
             w_nsa_out=w_nsa_out, w_o=w_o, w_ffn_gate=w_ffn_gate, w_ffn_up=w_ffn_up, w_ffn_down=w_ffn_down)
    yp, ys = x_prompt, x_sample
    outs_p, outs_s = [], []
    time_minor = lambda a: jnp.transpose(a, (0, 1, 3, 4, 5, 2))
    sb_pool_t, nsa_pool_t, win_t = time_minor(cache_sb_kv), time_minor(cache_nsa_kv), time_minor(cache_nsa_win)
    for l in range(w_in.shape[0]):
        lw = _layer_weights(l, p)
        res = _layer_prompt(yp, lw)
        yp = res[0]
        outs_p.append(res[1:])
        res = _layer_sample(ys, lw, l, sb_pool_t, nsa_pool_t, win_t, state_ssd[l], state_conv[l], page_table)
        ys = res[0]
        outs_s.append(res[1:])
    st = lambda outs, i: jnp.stack([o[i] for o in outs])
    return (yp, ys, st(outs_p, 0), st(outs_s, 0), st(outs_p, 1), st(outs_s, 1), st(outs_p, 2), st(outs_s, 2),
            st(outs_p, 3), st(outs_s, 3), st(outs_p, 4), st(outs_s, 4))
```

```python
import functools
import math

import jax
import jax.numpy as jnp
from jax import lax
from jax.experimental import pallas as pl
from jax.experimental.pallas import tpu as pltpu

D_MODEL = 1024
HEAD_DIM = 64
SSD_INNER = D_MODEL
SSD_HEAD_DIM = 64
SSD_HEADS = SSD_INNER // SSD_HEAD_DIM
SSD_GROUPS = 2
SSD_STATE = 128
SSD_CONV = 4
SSD_GN = SSD_GROUPS * SSD_STATE
SSD_CONV_DIM = SSD_INNER + 2 * SSD_GN
SSD_CHUNK = 128
SB_HEADS = 8
SB_WIDTH = SB_HEADS * HEAD_DIM
NSA_HEADS = 8
NSA_KV_GROUPS = 2
NSA_REP = NSA_HEADS // NSA_KV_GROUPS
NSA_WIDTH = NSA_HEADS * HEAD_DIM
NSA_KV_WIDTH = NSA_KV_GROUPS * HEAD_DIM
CMP_BLOCK = 32
CMP_STRIDE = 16
CMP_HIDDEN = 128
SEL_BLOCK = 64
SEL_TOP = 16
SEL_LOCAL = 2
WINDOW = 512
Q_BLOCK = 128
N_BRANCH = 3
FFN_HIDDEN = ((8 * D_MODEL + 3 * 256 - 1) // (3 * 256)) * 256
RMS_EPS = 1e-6
IN_SPLITS = (SSD_INNER, SSD_CONV_DIM, SSD_HEADS, 3 * SB_WIDTH, NSA_WIDTH, 6 * NSA_KV_WIDTH, 3 * NSA_HEADS,
             N_BRANCH * D_MODEL)

V7X_LANES = 128
V7X_VMEM_LIMIT = 56 * 1024 * 1024
PAGES_PER_STEP = 16
NEG = -1e30
BF16 = jnp.bfloat16
F32 = jnp.float32
HI = lax.Precision.HIGHEST


def _params(*sem):
    return pltpu.CompilerParams(dimension_semantics=sem, vmem_limit_bytes=V7X_VMEM_LIMIT)


def _pick(n, cands):
    for c in cands:
        if n % c == 0:
            return c
    return n


def _rms(x, w):
    return x * lax.rsqrt(jnp.mean(x * x, axis=-1, keepdims=True) + RMS_EPS) * w


def _softplus(x):
    return jnp.maximum(x, 0.0) + jnp.log1p(jnp.exp(-jnp.abs(x)))


def _sigmoid(x):
    return 1.0 / (1.0 + jnp.exp(-x))


def _dot_nt(a, b):
    return lax.dot_general(a, b, (((1,), (1,)), ((), ())), preferred_element_type=F32)


def _dot_tn(a, b):
    return lax.dot_general(a, b, (((0,), (0,)), ((), ())), preferred_element_type=F32)


def _norm_mm_kernel(x_ref, g_ref, w_ref, o_ref, h_ref, *, transposed):
    @pl.when(pl.program_id(2) == 0)
    def _():
        h_ref[...] = _rms(x_ref[...], g_ref[...]).astype(BF16)

    if transposed:
        o_ref[...] = _dot_nt(w_ref[...], h_ref[...]).astype(o_ref.dtype)
    else:
        o_ref[...] = jnp.dot(h_ref[...], w_ref[...], preferred_element_type=F32).astype(o_ref.dtype)


def norm_matmul(x, gain, w, out_dtype=F32, transposed=False):
    b, t, k = x.shape
    n = w.shape[0] if transposed else w.shape[1]
    tm = _pick(t, (1024, 512, 256, 128))
    tn = _pick(n, (1024, 768, 512, 256, 128))
    if transposed:
        w_spec = pl.BlockSpec((tn, k), lambda bi, i, j: (j, 0))
        o_spec = pl.BlockSpec((None, tn, tm), lambda bi, i, j: (bi, j, i))
        o_shape = (b, n, t)
    else:
        w_spec = pl.BlockSpec((k, tn), lambda bi, i, j: (0, j))
        o_spec = pl.BlockSpec((None, tm, tn), lambda bi, i, j: (bi, i, j))
        o_shape = (b, t, n)
    return pl.pallas_call(
        functools.partial(_norm_mm_kernel, transposed=transposed),
        grid=(b, t // tm, n // tn),
        in_specs=[pl.BlockSpec((None, tm, k), lambda bi, i, j: (bi, i, 0)),
                  pl.BlockSpec((1, k), lambda bi, i, j: (0, 0)),
                  w_spec],
        out_specs=o_spec,
        out_shape=jax.ShapeDtypeStruct(o_shape, out_dtype),
        scratch_shapes=[pltpu.VMEM((tm, k), BF16)],
        compiler_params=_params("parallel", "parallel", "arbitrary"),
        name="norm_matmul_t" if transposed else "norm_matmul",
    )(x, gain.reshape(1, k), w)


def _ssd_chunk_kernel(xbc_ref, z_ref, dt_ref, dtt_ref, h0_ref, c0_ref, cw_ref, cb_ref, dtb_ref, dtbt_ref,
                      alog_ref, alogt_ref, dfull_ref, nw_ref, exp_ref, y_ref, h_ref, xp_ref):
    q = SSD_CHUNK
    c = pl.program_id(1)

    @pl.when(c == 0)
    def _():
        h_ref[...] = h0_ref[...]
        xp_ref[5:8, :] = c0_ref[...]

    xp_ref[8:8 + q, :] = xbc_ref[...]
    conv = cb_ref[...]
    for j in range(SSD_CONV):
        conv = conv + cw_ref[j:j + 1, :] * xp_ref[5 + j:5 + j + q, :]
    xp_ref[5:8, :] = xp_ref[q + 5:q + 8, :]
    u = conv * _sigmoid(conv)
    xs = u[:, :SSD_INNER]
    bm = u[:, SSD_INNER:SSD_INNER + SSD_GN].astype(BF16)
    cm = u[:, SSD_INNER + SSD_GN:].astype(BF16)

    dt = _softplus(dt_ref[...] + dtb_ref[...])
    dtt = _softplus(dtt_ref[...] + dtbt_ref[...])
    dta = dt * (-jnp.exp(alog_ref[...]))
    dtat = dtt * (-jnp.exp(alogt_ref[...]))
    row = lax.broadcasted_iota(jnp.int32, (q, q), 0)
    col = lax.broadcasted_iota(jnp.int32, (q, q), 1)
    tril = row >= col
    acum = jnp.dot(tril.astype(F32), dta, precision=HI, preferred_element_type=F32)
    acumt = jnp.dot(dtat, (row <= col).astype(F32), precision=HI, preferred_element_type=F32)
    expand = exp_ref[...]
    dt_full = jnp.dot(dt, expand, precision=HI, preferred_element_type=F32)
    ea_full = jnp.dot(jnp.exp(acum), expand, precision=HI, preferred_element_type=F32)
    te_full = jnp.dot(jnp.exp(acum[q - 1:q, :] - acum), expand, precision=HI, preferred_element_type=F32)
    xdt = xs * dt_full
    xdt_b = xdt.astype(BF16)
    xw_b = (xdt * te_full).astype(BF16)

    r = SSD_HEADS // SSD_GROUPS
    gw = r * SSD_HEAD_DIM
    y_diag, y_off = [], []
    for g in range(SSD_GROUPS):
        cm_g = cm[:, g * SSD_STATE:(g + 1) * SSD_STATE]
        bm_g = bm[:, g * SSD_STATE:(g + 1) * SSD_STATE]
        cb = _dot_nt(cm_g, bm_g)
        h_g = h_ref[g * r:(g + 1) * r].reshape(gw, SSD_STATE)
        y_off.append(_dot_nt(cm_g, h_g.astype(BF16)))
        st = _dot_tn(xw_b[:, g * gw:(g + 1) * gw], bm_g)
        for hh in range(r):
            hd = g * r + hh
            seg = acum[:, hd:hd + 1] - acumt[hd:hd + 1, :]
            decay = jnp.exp(jnp.where(tril, seg, -jnp.inf))
            m = (cb * decay).astype(BF16)
            y_diag.append(jnp.dot(m, xdt_b[:, hd * SSD_HEAD_DIM:(hd + 1) * SSD_HEAD_DIM],
                                  preferred_element_type=F32))
            dec = jnp.exp(acumt[hd:hd + 1, q - 1:q])
            h_ref[hd] = dec * h_ref[hd] + st[hh * SSD_HEAD_DIM:(hh + 1) * SSD_HEAD_DIM, :]
    y = (jnp.concatenate(y_diag, axis=1) + jnp.concatenate(y_off, axis=1) * ea_full
         + dfull_ref[...] * xs)
    zz = z_ref[...]
    y = y * (zz * _sigmoid(zz))
    y_ref[...] = _rms(y, nw_ref[...]).astype(y_ref.dtype)


def ssd_prompt(xbc, z, dt_raw, h0, conv0, conv_w, conv_b, dt_bias, a_log, d_skip, norm_w):
    b, t, _ = xbc.shape
    q = SSD_CHUNK
    nc = t // q
    hds = SSD_HEADS
    expand = (jnp.arange(SSD_INNER)[None, :] // SSD_HEAD_DIM == jnp.arange(hds)[:, None]).astype(F32)
    d_full = jnp.repeat(d_skip, SSD_HEAD_DIM).reshape(1, SSD_INNER)
    dtt = jnp.swapaxes(dt_raw, 1, 2)
    full = lambda shape: pl.BlockSpec(shape, lambda bi, ci: (0,) * len(shape))
    y, h = pl.pallas_call(
        _ssd_chunk_kernel,
        grid=(b, nc),
        in_specs=[pl.BlockSpec((None, q, SSD_CONV_DIM), lambda bi, ci: (bi, ci, 0)),
                  pl.BlockSpec((None, q, SSD_INNER), lambda bi, ci: (bi, ci, 0)),
                  pl.BlockSpec((None, q, hds), lambda bi, ci: (bi, ci, 0)),
                  pl.BlockSpec((None, hds, q), lambda bi, ci: (bi, 0, ci)),
                  pl.BlockSpec((None, hds, SSD_HEAD_DIM, SSD_STATE), lambda bi, ci: (bi, 0, 0, 0)),
                  pl.BlockSpec((None, SSD_CONV - 1, SSD_CONV_DIM), lambda bi, ci: (bi, 0, 0)),
                  full((SSD_CONV, SSD_CONV_DIM)), full((1, SSD_CONV_DIM)),
                  full((1, hds)), full((hds, 1)), full((1, hds)), full((hds, 1)),
                  full((1, SSD_INNER)), full((1, SSD_INNER)), full((hds, SSD_INNER))],
        out_specs=[pl.BlockSpec((None, q, SSD_INNER), lambda bi, ci: (bi, ci, 0)),
                   pl.BlockSpec((None, hds, SSD_HEAD_DIM, SSD_STATE), lambda bi, ci: (bi, 0, 0, 0))],
        out_shape=[jax.ShapeDtypeStruct((b, t, SSD_INNER), BF16),
                   jax.ShapeDtypeStruct((b, hds, SSD_HEAD_DIM, SSD_STATE), F32)],
        scratch_shapes=[pltpu.VMEM((q + 8, SSD_CONV_DIM), F32)],
        compiler_params=_params("parallel", "arbitrary"),
        name="ssd_chunk_scan",
    )(xbc, z, dt_raw, dtt, h0, conv0, conv_w, conv_b.reshape(1, -1), dt_bias.reshape(1, hds),
      dt_bias.reshape(hds, 1), a_log.reshape(1, hds), a_log.reshape(hds, 1), d_full, norm_w.reshape(1, -1), expand)
    return y, h


def _sb_stage(z):
    sp = jnp.maximum(z, 0.0) + jnp.log(1.0 + jnp.exp(-jnp.abs(z)))
    hi = sp.astype(BF16)
    return hi, (sp - hi.astype(F32)).astype(BF16)


def _sb_prompt_kernel(q_ref, kt_ref, vt_ref, o_ref, z_ref, hl_ref, a_ref, *, tq, tk):
    i = pl.program_id(2)
    band = tq // tk
    assert band <= 2
    last = (i + 1) * band - 1
    q = (q_ref[...] * (HEAD_DIM ** -0.5)).astype(BF16)
    row = lax.broadcasted_iota(jnp.int32, (tk, tk), 0)
    col = lax.broadcasted_iota(jnp.int32, (tk, tk), 1)
    upper = jnp.where(row >= col, -1.0, 0.0).astype(BF16)
    upper2 = jnp.concatenate([upper, upper], axis=0)
    qpos = i * tq + lax.broadcasted_iota(jnp.int32, (tq, 1), 0)
    kcol = lax.broadcasted_iota(jnp.int32, (1, tk), 1)

    def tile(ref, s):
        off = pl.multiple_of(jnp.clip(last - s, 0, last) * tk, tk)
        return ref[:, pl.ds(off, tk)].astype(BF16)

    def masked_scores(s):
        kpos = jnp.clip(last - s, 0, last) * tk + kcol
        return jnp.where(kpos < qpos, jnp.dot(q, tile(kt_ref, s), preferred_element_type=F32), NEG)

    z0 = masked_scores(0)
    z_ref[0] = z0
    hl_ref[0, :, :tk], hl_ref[0, :, tk:] = _sb_stage(z0)
    z_ref[1] = masked_scores(1)
    a_ref[2] = jnp.zeros((tq, tk), BF16)

    def step(n, c, st):
        carry, acc = st
        nxt, prv = (c + 1) % 3, (c + 2) % 3
        later = jnp.dot(hl_ref[c], upper2, preferred_element_type=F32)
        acc = acc + _dot_nt(a_ref[prv], tile(vt_ref, n - 1))
        z_ref[prv] = jnp.dot(q, tile(kt_ref, n + 2), preferred_element_type=F32)
        hl_ref[nxt, :, :tk], hl_ref[nxt, :, tk:] = _sb_stage(z_ref[nxt])
        dead = jnp.where(n <= last, 0.0, NEG)
        a_ref[c] = jnp.exp(z_ref[c] + later + (carry + dead)).astype(BF16)
        return carry + later[:, 0:1], acc

    def body(trip, st):
        for c in range(3):
            st = step(3 * trip + c, c, st)
        return st

    st = (jnp.zeros((tq, 1), F32), jnp.zeros((tq, HEAD_DIM), F32))
    _, acc = lax.fori_loop(0, lax.div(last + 4, 3), body, st)
    o_ref[...] = acc.astype(o_ref.dtype)


def sb_prompt(q, kvt, tq=512, tk=256):
    b, t, _ = q.shape
    h = SB_HEADS
    tq, tk = min(tq, t), min(tk, t)
    assert t % tq == 0 and tq % tk == 0
    qh = jnp.swapaxes(q.reshape(b, t, h, HEAD_DIM), 1, 2)
    out = pl.pallas_call(
        functools.partial(_sb_prompt_kernel, tq=tq, tk=tk),
        grid=(b, h, t // tq),
        in_specs=[pl.BlockSpec((None, None, tq, HEAD_DIM), lambda bi, hi, i: (bi, hi, i, 0)),
                  pl.BlockSpec((None, HEAD_DIM, t), lambda bi, hi, i: (bi, hi, 0)),
                  pl.BlockSpec((None, HEAD_DIM, t), lambda bi, hi, i: (bi, h + hi, 0))],
        out_specs=pl.BlockSpec((None, None, tq, HEAD_DIM), lambda bi, hi, i: (bi, hi, i, 0)),
        out_shape=jax.ShapeDtypeStruct((b, h, t, HEAD_DIM), BF16),
        scratch_shapes=[pltpu.VMEM((3, tq, tk), F32), pltpu.VMEM((3, tq, 2 * tk), BF16),
                        pltpu.VMEM((3, tq, tk), BF16)],
        compiler_params=_params("parallel", "parallel", "arbitrary"),
        name="sb_prompt",
    )(qh, kvt, kvt)
    return jnp.swapaxes(out, 1, 2).reshape(b, t, h * HEAD_DIM)


def _merge_kernel(x_ref, ssd_ref, sb_ref, nsa_ref, gl_ref, wssd_ref, wsb_ref, wnsa_ref, wo_ref, nw_ref, o_ref):
    d = D_MODEL
    gl = gl_ref[...]
    merged = (_sigmoid(gl[:, :d]) * jnp.dot(ssd_ref[...], wssd_ref[...], preferred_element_type=F32)
              + _sigmoid(gl[:, d:2 * d]) * jnp.dot(sb_ref[...], wsb_ref[...], preferred_element_type=F32)
              + _sigmoid(gl[:, 2 * d:]) * jnp.dot(nsa_ref[...], wnsa_ref[...], preferred_element_type=F32))
    y = jnp.dot(merged.astype(BF16), wo_ref[...], preferred_element_type=F32)
    o_ref[...] = x_ref[...] + _rms(y, nw_ref[...])


def merge_branches(x, ssd_y, sb_o, nsa_o, gate_logits, w_ssd_out, w_sb_out, w_nsa_out, w_o, norm_w):
    m, d = x.shape
    tm = _pick(m, (512, 256, 128, 32))
    rows = lambda n: pl.BlockSpec((tm, n), lambda i: (i, 0))
    full = lambda a: pl.BlockSpec(a.shape, lambda i: (0, 0))
    nw = norm_w.reshape(1, d)
    return pl.pallas_call(
        _merge_kernel,
        grid=(m // tm,),
        in_specs=[rows(d), rows(ssd_y.shape[1]), rows(sb_o.shape[1]), rows(nsa_o.shape[1]), rows(N_BRANCH * d),
                  full(w_ssd_out), full(w_sb_out), full(w_nsa_out), full(w_o), full(nw)],
        out_specs=rows(d),
        out_shape=jax.ShapeDtypeStruct((m, d), F32),
        compiler_params=_params("parallel"),
        name="merge_branches",
    )(x, ssd_y, sb_o, nsa_o, gate_logits, w_ssd_out, w_sb_out, w_nsa_out, w_o, nw)


def _ffn_up_kernel(x_ref, g_ref, wg_ref, wu_ref, o_ref, h_ref):
    @pl.when(pl.program_id(1) == 0)
    def _():
        h_ref[...] = _rms(x_ref[...], g_ref[...]).astype(BF16)

    h = h_ref[...]
    a = jnp.dot(h, wg_ref[...], preferred_element_type=F32)
    u = jnp.dot(h, wu_ref[...], preferred_element_type=F32)
    o_ref[...] = (a * _sigmoid(a) * u).astype(o_ref.dtype)


def _ffn_down_kernel(a_ref, x_ref, wd_ref, nw_ref, o_ref):
    f = jnp.dot(a_ref[...], wd_ref[...], preferred_element_type=F32)
    o_ref[...] = x_ref[...] + _rms(f, nw_ref[...])


def ffn(x, pre_w, post_w, w_gate, w_up, w_down):
    m, d = x.shape
    f = w_gate.shape[1]
    tm = _pick(m, (1024, 512, 256, 128, 32))
    tn = _pick(f, (256, 128))
    act = pl.pallas_call(
        _ffn_up_kernel,
        grid=(m // tm, f // tn),
        in_specs=[pl.BlockSpec((tm, d), lambda i, j: (i, 0)),
                  pl.BlockSpec((1, d), lambda i, j: (0, 0)),
                  pl.BlockSpec((d, tn), lambda i, j: (0, j)),
                  pl.BlockSpec((d, tn), lambda i, j: (0, j))],
        out_specs=pl.BlockSpec((tm, tn), lambda i, j: (i, j)),
        out_shape=jax.ShapeDtypeStruct((m, f), BF16),
        scratch_shapes=[pltpu.VMEM((tm, d), BF16)],
        compiler_params=_params("parallel", "arbitrary"),
        name="ffn_up",
    )(x, pre_w.reshape(1, d), w_gate, w_up)
    tm2 = _pick(m, (512, 256, 128, 32))
    return pl.pallas_call(
        _ffn_down_kernel,
        grid=(m // tm2,),
        in_specs=[pl.BlockSpec((tm2, f), lambda i: (i, 0)),
                  pl.BlockSpec((tm2, d), lambda i: (i, 0)),
                  pl.BlockSpec((f, d), lambda i: (0, 0)),
                  pl.BlockSpec((1, d), lambda i: (0, 0))],
        out_specs=pl.BlockSpec((tm2, d), lambda i: (i, 0)),
        out_shape=jax.ShapeDtypeStruct((m, d), F32),
        compiler_params=_params("parallel"),
        name="ffn_down",
    )(act, x, w_down, post_w.reshape(1, d))


def _nsa_compress_kernel(r_ref, pos_ref, w1_ref, w2_ref, o_ref, sh_ref):
    nr = r_ref.shape[0]
    half = CMP_STRIDE * HEAD_DIM
    r = r_ref[...]
    top = jnp.dot((r + pos_ref[0:1, :]).astype(BF16), w1_ref[:half, :], preferred_element_type=F32)
    bot = jnp.dot((r + pos_ref[1:2, :]).astype(BF16), w1_ref[half:, :], preferred_element_type=F32)
    sh_ref[0:nr, :] = bot
    sh_ref[nr:nr + 8, :] = jnp.zeros((8, CMP_HIDDEN), F32)
    pre = top + sh_ref[1:nr + 1, :]
    hid = pre * _sigmoid(pre)
    o_ref[...] = jnp.dot(hid.astype(BF16), w2_ref[...], preferred_element_type=F32).astype(o_ref.dtype)


def nsa_compress(rows16, cmp_pos, cmp_w1, cmp_w2):
    assert CMP_BLOCK == 2 * CMP_STRIDE
    b, _, g, nr, w = rows16.shape
    pos = cmp_pos.reshape(2, 2, w)
    return pl.pallas_call(
        _nsa_compress_kernel,
        grid=(b, 2, g),
        in_specs=[pl.BlockSpec((None, None, None, nr, w), lambda bi, ki, gi: (bi, ki, gi, 0, 0)),
                  pl.BlockSpec((None, 2, w), lambda bi, ki, gi: (ki, 0, 0)),
                  pl.BlockSpec((None, 2 * w, CMP_HIDDEN), lambda bi, ki, gi: (ki, 0, 0)),
                  pl.BlockSpec((None, CMP_HIDDEN, HEAD_DIM), lambda bi, ki, gi: (ki, 0, 0))],
        out_specs=pl.BlockSpec((None, None, None, nr, HEAD_DIM), lambda bi, ki, gi: (bi, ki, gi, 0, 0)),
        out_shape=jax.ShapeDtypeStruct((b, 2, g, nr, HEAD_DIM), BF16),
        scratch_shapes=[pltpu.VMEM((nr + 8, CMP_HIDDEN), F32)],
        compiler_params=_params("parallel", "parallel", "parallel"),
        name="nsa_compress",
    )(rows16, pos, cmp_w1.astype(BF16), cmp_w2.astype(BF16))


def _overlap_matrix(n_cmp_rows, n_cmp, n_blk):
    c = jnp.arange(n_cmp_rows)[:, None]
    n = jnp.arange(n_blk)[None, :]
    c_start, c_end = c * CMP_STRIDE, c * CMP_STRIDE + CMP_BLOCK - 1
    return ((c_start < (n + 1) * SEL_BLOCK) & (c_end >= n * SEL_BLOCK) & (c < n_cmp)).astype(BF16)


def _split_dot(x, w):
    hi = x.astype(BF16)
    lo = (x - hi.astype(F32)).astype(BF16)
    return jnp.dot(hi, w, preferred_element_type=F32) + jnp.dot(lo, w, preferred_element_type=F32)


def _top_blocks(imp, blk, n_top):
    n_blk = imp.shape[1]
    sel = jnp.zeros(imp.shape, F32)
    for _ in range(n_top):
        m = jnp.max(imp, axis=-1, keepdims=True)
        idx = jnp.min(jnp.where(imp == m, blk, float(n_blk)), axis=-1, keepdims=True)
        hit = blk == idx
        sel = jnp.where(hit, 1.0, sel)
        imp = jnp.where(hit, -jnp.inf, imp)
    return sel


def _flash_step_t(qt, k, vt, mask, m, acc):
    s = jnp.where(mask, jnp.dot(k, qt, preferred_element_type=F32), NEG)
    m_new = jnp.maximum(m, jnp.max(s, axis=0, keepdims=True))
    p = jnp.exp(s - m_new).astype(BF16)
    vt_ext = jnp.concatenate([vt, jnp.ones((8, vt.shape[1]), BF16)], axis=0)
    return m_new, acc * jnp.exp(m - m_new) + jnp.dot(vt_ext, p, preferred_element_type=F32)


def _nsa_prompt_t_kernel(qt_ref, gt_ref, kc_ref, vct_ref, ovt_ref, ks_ref, vs_ref, kw_ref, vw_ref, o_ref,
                         s_ref, ch_ref, p_ref, *, tq, n_cmp):
    i = pl.program_id(2)
    rep, d = NSA_REP, HEAD_DIM
    lanes = rep * tq
    sel_shift = int(math.log2(SEL_BLOCK))
    qt_blk = qt_ref[...]
    qt = jnp.concatenate([qt_blk[r * d:(r + 1) * d, :] for r in range(rep)], axis=1)
    qt = (qt * (d ** -0.5)).astype(BF16)
    qpos = i * tq + lax.broadcasted_iota(jnp.int32, (1, tq), 1)
    per_head = lambda a: jnp.concatenate([a] * rep, axis=1)

    n_rows = kc_ref.shape[0]
    cidx = lax.broadcasted_iota(jnp.int32, (n_rows, 1), 0)
    vis_c = per_head(((cidx * CMP_STRIDE + (CMP_BLOCK - 1) <= qpos) & (cidx < n_cmp)).astype(F32)) > 0.5
    s_c = jnp.where(vis_c, jnp.dot(kc_ref[...], qt, preferred_element_type=F32), NEG)
    e_c = jnp.where(vis_c, jnp.exp(s_c - jnp.max(s_c, axis=0, keepdims=True)), 0.0)
    p_c = e_c / jnp.maximum(jnp.sum(e_c, axis=0, keepdims=True), 1e-30)
    o_c = jnp.dot(vct_ref[...], p_c.astype(BF16), preferred_element_type=F32)

    p_sum = p_c[:, 0:tq]
    for r in range(1, rep):
        p_sum = p_sum + p_c[:, r * tq:(r + 1) * tq]
    p_hi = p_sum.astype(BF16)
    p_lo = (p_sum - p_hi.astype(F32)).astype(BF16)
    imp = (jnp.dot(ovt_ref[...], p_hi, preferred_element_type=F32)
           + jnp.dot(ovt_ref[...], p_lo, preferred_element_type=F32))
    n_blk = ovt_ref.shape[0]
    blk_i = lax.broadcasted_iota(jnp.int32, (n_blk, 1), 0)
    cur = lax.shift_right_logical(qpos, sel_shift)
    valid = blk_i <= cur
    forced = valid & ((blk_i == 0) | (blk_i > cur - SEL_LOCAL))
    imp = jnp.where(forced, jnp.inf, jnp.where(valid, imp, -jnp.inf))

    m0 = jnp.full((1, lanes), NEG, F32)
    acc0 = jnp.zeros((d + 8, lanes), F32)
    krow = lax.broadcasted_iota(jnp.int32, (tq, 1), 0)

    st = (m0, acc0)
    for n in range(WINDOW // tq + 1):
        j = i - n
        off = pl.multiple_of(jnp.maximum(j, 0) * tq, tq)
        kpos = off + krow + jnp.where(j < 0, 1 << 30, 0)
        mask = per_head(((kpos <= qpos) & (kpos > qpos - WINDOW)).astype(F32)) > 0.5
        st = _flash_step_t(qt, kw_ref[pl.ds(off, tq), :].astype(BF16), vw_ref[:, pl.ds(off, tq)].astype(BF16),
                           mask, *st)
    acc_w = st[1]

    blk_f = blk_i.astype(F32)
    sel = jnp.zeros((n_blk, tq), F32)
    for _ in range(min(SEL_TOP, n_blk)):
        top = jnp.max(imp, axis=0, keepdims=True)
        idx = jnp.min(jnp.where(imp == top, blk_f, float(n_blk)), axis=0, keepdims=True)
        hit = blk_f == idx
        sel = jnp.where(hit, 1.0, sel)
        imp = jnp.where(hit, -jnp.inf, imp)
    sel = sel.astype(BF16)

    exp_blk = lax.broadcasted_iota(jnp.int32, (tq, n_blk), 1)
    exp_key = lax.shift_right_logical(lax.broadcasted_iota(jnp.int32, (tq, n_blk), 0), sel_shift)

    def key_tile(n):
        return jnp.clip(i - n, 0, i)

    def scores(n):
        j = key_tile(n)
        expand = (exp_blk == j * (tq // SEL_BLOCK) + exp_key).astype(BF16)
        return (jnp.dot(ks_ref[pl.ds(pl.multiple_of(j * tq, tq), tq), :].astype(BF16), qt, preferred_element_type=F32),
                jnp.dot(expand, sel, preferred_element_type=F32))

    def weighted_values(n, slot):
        vt = vs_ref[:, pl.ds(pl.multiple_of(key_tile(n) * tq, tq), tq)].astype(BF16)
        return jnp.dot(jnp.concatenate([vt, jnp.ones((8, tq), BF16)], axis=0), p_ref[slot],
                       preferred_element_type=F32)

    s_ref[0], ch_ref[0] = scores(0)
    p_ref[0] = jnp.zeros(p_ref.shape[1:], BF16)

    def sel_step(n, rd, wr, st):
        m, alpha, acc = st
        pv = weighted_values(n - 1, rd)
        s_ref[wr], ch_ref[wr] = scores(n + 1)
        kpos = (i - n) * tq + krow + jnp.where(n > i, 1 << 30, 0)
        mask = per_head(jnp.where(kpos <= qpos, ch_ref[rd], 0.0)) > 0.5
        s = jnp.where(mask, s_ref[rd], NEG)
        m_new = jnp.maximum(m, jnp.max(s, axis=0, keepdims=True))
        p_ref[wr] = jnp.exp(s - m_new).astype(BF16)
        return m_new, jnp.exp(m - m_new), acc * alpha + pv

    def sel_pair(pair, st):
        return sel_step(2 * pair + 1, 1, 0, sel_step(2 * pair, 0, 1, st))

    _, _, acc_s = lax.fori_loop(0, (i + 3) >> 1, sel_pair, (m0, jnp.ones((1, lanes), F32), acc0))

    o_s = acc_s[:d] / acc_s[d:d + 1]
    o_w = acc_w[:d] / acc_w[d:d + 1]
    gate = _sigmoid(gt_ref[...])
    outs = []
    for r in range(rep):
        sl = slice(r * tq, (r + 1) * tq)
        outs.append(gate[3 * r:3 * r + 1] * o_c[:, sl] + gate[3 * r + 1:3 * r + 2] * o_s[:, sl]
                    + gate[3 * r + 2:3 * r + 3] * o_w[:, sl])
    o_ref[...] = jnp.concatenate(outs, axis=0).astype(o_ref.dtype)


def nsa_prompt_t(nsa_qt, nsa_g, nsa_kvt, cmp_pos, cmp_w1, cmp_w2, tq=128):
    b, _, t = nsa_qt.shape
    g, d, rep = NSA_KV_GROUPS, HEAD_DIM, NSA_REP
    assert t % tq == 0 and tq % SEL_BLOCK == 0 and WINDOW % tq == 0 and t % CMP_STRIDE == 0
    n_cmp = (t - CMP_BLOCK) // CMP_STRIDE + 1
    nr = t // CMP_STRIDE
    n_blk = t // SEL_BLOCK
    kinds = nsa_kvt.reshape(b, 6, g, d, t)
    rows16 = jnp.swapaxes(kinds[:, 0:2], 3, 4).reshape(b, 2, g, nr, CMP_STRIDE * d)
    kcvc = nsa_compress(rows16, cmp_pos, cmp_w1, cmp_w2)
    vct = jnp.swapaxes(kcvc[:, 1], 2, 3)
    overlap_t = _overlap_matrix(nr, n_cmp, n_blk).T
    keys = jnp.swapaxes(kinds[:, 2::2], 3, 4).astype(BF16)
    gates = jnp.transpose(nsa_g.reshape(b, t, g, 3 * rep), (0, 2, 3, 1))
    gates = jnp.pad(gates, ((0, 0), (0, 0), (0, 16 - 3 * rep), (0, 0)))
    values = lambda kind: pl.BlockSpec((None, d, t), lambda bi, gi, i: (bi, kind * g + gi, 0))
    rows = lambda kind: pl.BlockSpec((None, None, None, t, d), lambda bi, gi, i: (bi, kind, gi, 0, 0))
    out = pl.pallas_call(
        functools.partial(_nsa_prompt_t_kernel, tq=tq, n_cmp=n_cmp),
        grid=(b, g, t // tq),
        in_specs=[pl.BlockSpec((None, rep * d, tq), lambda bi, gi, i: (bi, gi, i)),
                  pl.BlockSpec((None, None, 16, tq), lambda bi, gi, i: (bi, gi, 0, i)),
                  pl.BlockSpec((None, None, None, nr, d), lambda bi, gi, i: (bi, 0, gi, 0, 0)),
                  pl.BlockSpec((None, None, d, nr), lambda bi, gi, i: (bi, gi, 0, 0)),
                  pl.BlockSpec((n_blk, nr), lambda bi, gi, i: (0, 0)),
                  rows(0), values(3), rows(1), values(5)],
        out_specs=pl.BlockSpec((None, rep * d, tq), lambda bi, gi, i: (bi, gi, i)),
        out_shape=jax.ShapeDtypeStruct((b, g * rep * d, t), BF16),
        scratch_shapes=[pltpu.VMEM((2, tq, rep * tq), F32), pltpu.VMEM((2, tq, tq), F32),
                        pltpu.VMEM((2, tq, rep * tq), BF16)],
        compiler_params=_params("parallel", "parallel", "arbitrary"),
        name="nsa_prompt",
    )(nsa_qt, gates, kcvc, vct, overlap_t, keys, nsa_kvt, keys, nsa_kvt)
    return jnp.swapaxes(out, 1, 2)


def _ssd_step_pre_kernel(x_ref, buf_ref, cw_ref, cb_ref, dt_ref, dtb_ref, alog_ref, exp_ref,
                         xs_ref, xdt_ref, bm_ref, cm_ref, dec_ref):
    conv = cb_ref[...] + cw_ref[SSD_CONV - 1:SSD_CONV, :] * x_ref[...]
    for j in range(SSD_CONV - 1):
        conv = conv + cw_ref[j:j + 1, :] * buf_ref[j]
    u = conv * _sigmoid(conv)
    xs = u[:, :SSD_INNER]
    dt = _softplus(dt_ref[...] + dtb_ref[...])
    xs_ref[...] = xs
    xdt_ref[...] = xs * jnp.dot(dt, exp_ref[...], precision=HI, preferred_element_type=F32)
    bm_ref[...] = u[:, SSD_INNER:SSD_INNER + SSD_GN]
    cm_ref[...] = u[:, SSD_INNER + SSD_GN:]
    dec_ref[...] = jnp.exp(dt * (-jnp.exp(alog_ref[...])))


def _ssd_step_state_kernel(h0_ref, xdt_ref, dec_ref, bm_ref, cm_ref, h_ref, y_ref):
    r = SSD_HEADS // SSD_GROUPS
    for hd in range(SSD_HEADS):
        g = hd // r
        hn = dec_ref[hd] * h0_ref[hd] + xdt_ref[hd] * bm_ref[g]
        h_ref[hd] = hn
        y_ref[hd] = jnp.sum(hn * cm_ref[g], axis=-1, keepdims=True)


def _ssd_step_post_kernel(y_ref, xs_ref, z_ref, dfull_ref, nw_ref, o_ref):
    zz = z_ref[...]
    y = (y_ref[...] + dfull_ref[...] * xs_ref[...]) * (zz * _sigmoid(zz))
    o_ref[...] = _rms(y, nw_ref[...]).astype(o_ref.dtype)


def ssd_step(xbc, z, dt_raw, h0, conv_buf, conv_w, conv_b, dt_bias, a_log, d_skip, norm_w):
    b = xbc.shape[0]
    hds, p, n = SSD_HEADS, SSD_HEAD_DIM, SSD_STATE
    expand = (jnp.arange(SSD_INNER)[None, :] // p == jnp.arange(hds)[:, None]).astype(F32)
    d_full = jnp.repeat(d_skip, p).reshape(1, SSD_INNER)
    sds = lambda shape: jax.ShapeDtypeStruct(shape, F32)
    xs, xdt, bm, cm, dec = pl.pallas_call(
        _ssd_step_pre_kernel,
        out_shape=[sds((b, SSD_INNER)), sds((b, SSD_INNER)), sds((b, SSD_GN)), sds((b, SSD_GN)), sds((b, hds))],
        name="ssd_step_pre",
    )(xbc, jnp.swapaxes(conv_buf, 0, 1), conv_w, conv_b.reshape(1, -1), dt_raw, dt_bias.reshape(1, hds),
      a_log.reshape(1, hds), expand)
    per_b = lambda *dims: pl.BlockSpec((None,) + dims, lambda bi: (bi,) + (0,) * len(dims))
    h_new, y_col = pl.pallas_call(
        _ssd_step_state_kernel,
        grid=(b,),
        in_specs=[per_b(hds, p, n), per_b(hds, p, 1), per_b(hds, 1, 1), per_b(SSD_GROUPS, 1, n),
                  per_b(SSD_GROUPS, 1, n)],
        out_specs=[per_b(hds, p, n), per_b(hds, p, 1)],
        out_shape=[sds((b, hds, p, n)), sds((b, hds, p, 1))],
        compiler_params=_params("parallel"),
        name="ssd_step_state",
    )(h0, xdt.reshape(b, hds, p, 1), dec.reshape(b, hds, 1, 1), bm.reshape(b, SSD_GROUPS, 1, n),
      cm.reshape(b, SSD_GROUPS, 1, n))
    y = pl.pallas_call(
        _ssd_step_post_kernel,
        out_shape=jax.ShapeDtypeStruct((b, SSD_INNER), BF16),
        name="ssd_step_post",
    )(y_col.reshape(b, SSD_INNER), xs, z, d_full, norm_w.reshape(1, -1))
    return y, h_new


def _sb_decode_kernel(pt_ref, q_ref, *refs):
    page_refs, (o_ref, carry_ref, acc_ref) = refs[:PAGES_PER_STEP], refs[PAGES_PER_STEP:]
    p = pl.program_id(1)
    tk = page_refs[0].shape[-1]
    hds = SB_HEADS

    @pl.when(p == 0)
    def _():
        carry_ref[...] = jnp.zeros(carry_ref.shape, F32)
        acc_ref[...] = jnp.zeros(acc_ref.shape, F32)

    scale = HEAD_DIM ** -0.5
    qs = [q_ref[h] * scale for h in range(hds)]
    z = jnp.concatenate([jnp.sum(ref[0, h] * qs[h], axis=0, keepdims=True)
                         for ref in page_refs for h in range(hds)], axis=0)
    row = lax.broadcasted_iota(jnp.int32, (tk, tk), 0)
    col = lax.broadcasted_iota(jnp.int32, (tk, tk), 1)
    sp = _softplus(z)
    later = _split_dot(-sp, (row > col).astype(BF16))
    total = later[:, 0:1] - sp[:, 0:1]
    carry = carry_ref[...]
    for k, ref in enumerate(page_refs):
        rows = slice(k * hds, (k + 1) * hds)
        a = jnp.exp(z[rows] - sp[rows] + later[rows] + carry)
        carry = carry + total[rows]
        for h in range(hds):
            acc_ref[h] += ref[1, h] * a[h:h + 1, :]
    carry_ref[...] = carry

    @pl.when(p == pl.num_programs(1) - 1)
    def _():
        for h in range(hds):
            o_ref[h] = jnp.sum(acc_ref[h], axis=-1, keepdims=True)


def _page_specs(block, layer, n_pages, kind_block, descending):
    def spec(k):
        def index(bi, p, pt):
            pos = p * PAGES_PER_STEP + k
            pos = n_pages - 1 - pos if descending else pos
            return (layer, pt[bi, pos], kind_block) + (0,) * (len(block) - 3)
        return pl.BlockSpec(block, index)
    return [spec(k) for k in range(PAGES_PER_STEP)]


def sb_decode(q, pool_t, layer, page_table):
    b = q.shape[0]
    h, d = SB_HEADS, HEAD_DIM
    n_pages = page_table.shape[1]
    page = pool_t.shape[-1]
    assert n_pages % PAGES_PER_STEP == 0
    out = pl.pallas_call(
        _sb_decode_kernel,
        grid_spec=pltpu.PrefetchScalarGridSpec(
            num_scalar_prefetch=1,
            grid=(b, n_pages // PAGES_PER_STEP),
            in_specs=[pl.BlockSpec((None, h, d, 1), lambda bi, p, pt: (bi, 0, 0, 0))]
            + _page_specs((None, None, 2, h, d, page), layer, n_pages, 0, descending=True),
            out_specs=pl.BlockSpec((None, h, d, 1), lambda bi, p, pt: (bi, 0, 0, 0)),
            scratch_shapes=[pltpu.VMEM((h, 1), F32), pltpu.VMEM((h, d, page), F32)]),
        out_shape=jax.ShapeDtypeStruct((b, h, d, 1), F32),
        compiler_params=_params("parallel", "arbitrary"),
        name="sb_decode",
    )(page_table, q.reshape(b, h, d, 1), *([pool_t] * PAGES_PER_STEP))
    return out.reshape(b, h * d)


def _nsa_gather_kernel(pt_ref, *refs):
    page_refs, (o_ref, x_ref) = refs[:PAGES_PER_STEP], refs[PAGES_PER_STEP:]
    page = page_refs[0].shape[-1]
    grp, d = NSA_KV_GROUPS, HEAD_DIM
    n_out = page // CMP_STRIDE
    for k, ref in enumerate(page_refs):
        for kind in range(2):
            x = x_ref.at[2 * k + kind]
            x[...] = ref[kind].reshape(grp * d, page).T
            steps = [x[pl.ds(s, n_out, stride=CMP_STRIDE), :] for s in range(CMP_STRIDE)]
            for g in range(grp):
                o_ref[kind, g, k * n_out:(k + 1) * n_out, :] = jnp.concatenate(
                    [st[:, g * d:(g + 1) * d] for st in steps], axis=1)


def nsa_gather_rows16(pool_t, layer, page_table):
    b, n_pages = page_table.shape
    g, d = NSA_KV_GROUPS, HEAD_DIM
    page = pool_t.shape[-1]
    assert page % CMP_STRIDE == 0 and n_pages % PAGES_PER_STEP == 0
    n_out = PAGES_PER_STEP * (page // CMP_STRIDE)
    return pl.pallas_call(
        _nsa_gather_kernel,
        grid_spec=pltpu.PrefetchScalarGridSpec(
            num_scalar_prefetch=1,
            grid=(b, n_pages // PAGES_PER_STEP),
            in_specs=_page_specs((None, None, 2, g, d, page), layer, n_pages, 0, descending=False),
            out_specs=pl.BlockSpec((None, 2, g, n_out, CMP_STRIDE * d), lambda bi, p, pt: (bi, 0, 0, p, 0)),
            scratch_shapes=[pltpu.VMEM((2 * PAGES_PER_STEP, page, g * d), F32)]),
        out_shape=jax.ShapeDtypeStruct((b, 2, g, n_pages * (page // CMP_STRIDE), CMP_STRIDE * d), F32),
        compiler_params=_params("parallel", "arbitrary"),
        name="nsa_gather_rows16",
    )(page_table, *([pool_t] * PAGES_PER_STEP))


def _nsa_decode_select_kernel(q_ref, kc_ref, vc_ref, ov_ref, oc_ref, sel_ref, *, n_cmp, q_pos):
    rep, d = NSA_REP, HEAD_DIM
    n_rows = kc_ref.shape[1]
    n_blk = ov_ref.shape[1]
    q = (q_ref[...] * (d ** -0.5)).astype(BF16)
    cidx = lax.broadcasted_iota(jnp.int32, (1, n_rows), 1)
    vis = (cidx * CMP_STRIDE + (CMP_BLOCK - 1) <= q_pos) & (cidx < n_cmp)
    o_c, imp = [], []
    for g in range(NSA_KV_GROUPS):
        s = jnp.where(vis, _dot_nt(q[g * rep:(g + 1) * rep], kc_ref[g]), NEG)
        e = jnp.where(vis, jnp.exp(s - jnp.max(s, axis=-1, keepdims=True)), 0.0)
        p = e / jnp.maximum(jnp.sum(e, axis=-1, keepdims=True), 1e-30)
        o_c.append(jnp.dot(p.astype(BF16), vc_ref[g], preferred_element_type=F32))
        imp.append(_split_dot(jnp.sum(p, axis=0, keepdims=True), ov_ref[...]))
    oc_ref[...] = jnp.concatenate(o_c, axis=0)
    imp = jnp.concatenate(imp, axis=0)
    blk_i = lax.broadcasted_iota(jnp.int32, (1, n_blk), 1)
    forced = (blk_i == 0) | (blk_i > n_blk - SEL_LOCAL)
    sel = _top_blocks(jnp.where(forced, jnp.inf, imp), blk_i.astype(F32), min(SEL_TOP - 1, n_blk))
    sel_ref[...] = jnp.concatenate([sel, jnp.zeros((sel_ref.shape[0] - NSA_KV_GROUPS, n_blk), F32)], axis=0)


def _nsa_decode_attend_kernel(pt_ref, q_ref, sel_ref, *refs, win_skip):
    page_refs = refs[:PAGES_PER_STEP]
    new_ref, win_ref, oc_ref, gt_ref, o_ref, m_ref, acc_ref = refs[PAGES_PER_STEP:]
    p = pl.program_id(1)
    n_steps = pl.num_programs(1)
    rep, d, grp = NSA_REP, HEAD_DIM, NSA_KV_GROUPS
    tk = page_refs[0].shape[-1]
    lanes = PAGES_PER_STEP * tk
    scale = d ** -0.5
    qf = q_ref[...] * scale
    q = qf.astype(BF16)
    new = new_ref[...]
    new_row = lambda kind, g: new[kind * grp + g:kind * grp + g + 1, :]
    per_head = lambda f: jnp.concatenate([f(g) for g in range(grp)], axis=0)

    @pl.when(p == 0)
    def _():
        m_ref[...] = per_head(lambda g: jnp.sum(qf[g * rep:(g + 1) * rep] * new_row(2, g), axis=-1, keepdims=True))
        acc_ref[...] = per_head(lambda g: jnp.concatenate(
            [jnp.broadcast_to(new_row(3, g), (rep, d)), jnp.ones((rep, d), F32)], axis=1))

    n_blk = sel_ref.shape[1]
    lane = lax.broadcasted_iota(jnp.int32, (n_blk, lanes), 1)
    page_pos = (n_steps - p) * PAGES_PER_STEP - 1 - lax.shift_right_logical(lane, int(math.log2(tk)))
    blk_of_lane = page_pos * (tk // SEL_BLOCK) + lax.shift_right_logical(lane & (tk - 1), int(math.log2(SEL_BLOCK)))
    expand = (lax.broadcasted_iota(jnp.int32, (n_blk, lanes), 0) == blk_of_lane).astype(BF16)
    chosen = jnp.dot(sel_ref[...].astype(BF16), expand, preferred_element_type=F32)

    def scores(g):
        qg = q[g * rep:(g + 1) * rep]
        sg = jnp.concatenate([jnp.dot(qg, ref[0, g].astype(BF16), preferred_element_type=F32) for ref in page_refs],
                             axis=1)
        return jnp.where(chosen[g:g + 1, :] > 0.5, sg, NEG)

    s = per_head(scores)
    m_old = m_ref[...]
    m_new = jnp.maximum(m_old, jnp.max(s, axis=-1, keepdims=True))
    pr = jnp.exp(s - m_new).astype(BF16)

    def weighted_values(g):
        out = jnp.zeros((rep, 2 * d), F32)
        for k, ref in enumerate(page_refs):
            vt_ext = jnp.concatenate([ref[1, g].astype(BF16), jnp.ones((d, tk), BF16)], axis=0)
            out = out + _dot_nt(pr[g * rep:(g + 1) * rep, k * tk:(k + 1) * tk], vt_ext)
        return out

    acc_ref[...] = acc_ref[...] * jnp.exp(m_old - m_new) + per_head(weighted_values)
    m_ref[...] = m_new

    @pl.when(p == n_steps - 1)
    def _():
        acc = acc_ref[...]
        o_s = acc[:, :d] / acc[:, d:]
        wlen = win_ref.shape[-1]
        vis = lax.broadcasted_iota(jnp.int32, (1, wlen), 1) >= win_skip

        def window(g):
            qg = q[g * rep:(g + 1) * rep]
            s_w = jnp.where(vis, jnp.dot(qg, win_ref[0, g].astype(BF16), preferred_element_type=F32), NEG)
            s_n = jnp.sum(qf[g * rep:(g + 1) * rep] * new_row(4, g), axis=-1, keepdims=True)
            mx = jnp.maximum(jnp.max(s_w, axis=-1, keepdims=True), s_n)
            e_w = jnp.where(vis, jnp.exp(s_w - mx), 0.0)
            e_n = jnp.exp(s_n - mx)
            num = _dot_nt(e_w.astype(BF16), win_ref[1, g].astype(BF16)) + e_n * new_row(5, g)
            return num / (jnp.sum(e_w, axis=-1, keepdims=True) + e_n)

        o_w = per_head(window)
        gate = _sigmoid(gt_ref[...])
        o_ref[...] = gate[:, 0:1] * oc_ref[...] + gate[:, 1:2] * o_s + gate[:, 2:3] * o_w


def nsa_decode(nsa_q, nsa_g, nsa_kv_new, pool_t, win_t, layer, page_table, cmp_pos, cmp_w1, cmp_w2):
    b = nsa_q.shape[0]
    g, d, rep, hds = NSA_KV_GROUPS, HEAD_DIM, NSA_REP, NSA_HEADS
    n_pages = page_table.shape[1]
    page = pool_t.shape[-1]
    past = n_pages * page
    wlen = win_t.shape[-1]
    assert past % SEL_BLOCK == 0 and past % CMP_STRIDE == 0 and page % SEL_BLOCK == 0 and wlen <= past
    n_cmp = (past + 1 - CMP_BLOCK) // CMP_STRIDE + 1
    nr = past // CMP_STRIDE
    n_blk = past // SEL_BLOCK
    rows16 = nsa_gather_rows16(pool_t, layer, page_table)
    kcvc = nsa_compress(rows16, cmp_pos, cmp_w1, cmp_w2)
    overlap = _overlap_matrix(nr, n_cmp, n_blk)
    q3 = nsa_q.reshape(b, hds, d)
    per_b = lambda *dims: pl.BlockSpec((None,) + dims, lambda bi: (bi,) + (0,) * len(dims))
    o_c, sel = pl.pallas_call(
        functools.partial(_nsa_decode_select_kernel, n_cmp=n_cmp, q_pos=past),
        grid=(b,),
        in_specs=[per_b(hds, d),
                  pl.BlockSpec((None, None, g, nr, d), lambda bi: (bi, 0, 0, 0, 0)),
                  pl.BlockSpec((None, None, g, nr, d), lambda bi: (bi, 1, 0, 0, 0)),
                  pl.BlockSpec((nr, n_blk), lambda bi: (0, 0))],
        out_specs=[per_b(hds, d), per_b(8, n_blk)],
        out_shape=[jax.ShapeDtypeStruct((b, hds, d), F32), jax.ShapeDtypeStruct((b, 8, n_blk), F32)],
        compiler_params=_params("parallel"),
        name="nsa_decode_select",
    )(q3, kcvc, kcvc, overlap)
    gates = jnp.pad(nsa_g.reshape(b, hds, 3), ((0, 0), (0, 0), (0, V7X_LANES - 3)))
    fixed = lambda *dims: pl.BlockSpec((None,) + dims, lambda bi, p, pt: (bi,) + (0,) * len(dims))
    out = pl.pallas_call(
        functools.partial(_nsa_decode_attend_kernel, win_skip=wlen - WINDOW + 1),
        grid_spec=pltpu.PrefetchScalarGridSpec(
            num_scalar_prefetch=1,
            grid=(b, n_pages // PAGES_PER_STEP),
            in_specs=[fixed(hds, d), fixed(8, n_blk)]
            + _page_specs((None, None, 2, g, d, page), layer, n_pages, 1, descending=True)
            + [fixed(6 * g, d),
               pl.BlockSpec((None, None, 2, g, d, wlen), lambda bi, p, pt: (layer, bi, 0, 0, 0, 0)),
               fixed(hds, d), fixed(hds, V7X_LANES)],
            out_specs=fixed(hds, d),
            scratch_shapes=[pltpu.VMEM((hds, 1), F32), pltpu.VMEM((hds, 2 * d), F32)]),
        out_shape=jax.ShapeDtypeStruct((b, hds, d), F32),
        compiler_params=_params("parallel", "arbitrary"),
        name="nsa_decode_attend",
    )(page_table, q3, sel, *([pool_t] * PAGES_PER_STEP), nsa_kv_new.reshape(b, 6 * g, d), win_t, o_c, gates)
    return out.reshape(b, hds * d)


def _col_offsets():
    offs, s = [], 0
    for n in IN_SPLITS:
        offs.append(s)
        s += n
    return offs


def _layer_weights(l, p):
    o = _col_offsets()
    w_in = p['w_in'][l]
    cols = lambda a, n: w_in[:, a:a + n]
    small = jnp.concatenate([cols(o[2], SSD_HEADS), cols(o[6], 3 * NSA_HEADS)], axis=1)
    small = jnp.pad(small, ((0, 0), (0, V7X_LANES - small.shape[1])))
    bf = lambda a: a.astype(BF16)
    return {
        'w_z': bf(cols(o[0], SSD_INNER)), 'w_xbc': bf(cols(o[1], SSD_CONV_DIM)), 'w_small': bf(small),
        'w_sbq': bf(cols(o[3], SB_WIDTH)), 'w_sbkv': bf(cols(o[3] + SB_WIDTH, 2 * SB_WIDTH)),
        'w_nq': bf(cols(o[4], NSA_WIDTH)), 'w_nkv': bf(cols(o[5], 6 * NSA_KV_WIDTH)),
        'w_brg': bf(cols(o[7], N_BRANCH * D_MODEL)),
        'mix_pre': p['norm_mix_pre'][l], 'mix_post': p['norm_mix_post'][l],
        'ffn_pre': p['norm_ffn_pre'][l], 'ffn_post': p['norm_ffn_post'][l],
        'conv_w': p['ssd_conv_w'][l], 'conv_b': p['ssd_conv_b'][l], 'dt_bias': p['ssd_dt_bias'][l],
        'a_log': p['ssd_a_log'][l], 'd_skip': p['ssd_d'][l], 'ssd_norm': p['ssd_norm'][l],
        'w_ssd_out': bf(p['w_ssd_out'][l]), 'w_sb_out': bf(p['w_sb_out'][l]), 'w_nsa_out': bf(p['w_nsa_out'][l]),
        'w_o': bf(p['w_o'][l]), 'w_ffn_gate': bf(p['w_ffn_gate'][l]), 'w_ffn_up': bf(p['w_ffn_up'][l]),
        'w_ffn_down': bf(p['w_ffn_down'][l]),
        'cmp_pos': p['nsa_cmp_pos'][l], 'cmp_w1': p['nsa_cmp_w1'][l], 'cmp_w2': p['nsa_cmp_w2'][l],
    }


def _trunk_tail(x, ssd_y, sb_o, nsa_o, br_g, lw):
    b, t, d = x.shape
    m = b * t
    x1 = merge_branches(x.reshape(m, d), ssd_y.reshape(m, -1), sb_o.reshape(m, -1), nsa_o.reshape(m, -1),
                        br_g.reshape(m, -1), lw['w_ssd_out'], lw['w_sb_out'], lw['w_nsa_out'], lw['w_o'],
                        lw['mix_post'])
    x2 = ffn(x1, lw['ffn_pre'], lw['ffn_post'], lw['w_ffn_gate'], lw['w_ffn_up'], lw['w_ffn_down'])
    return x2.reshape(b, t, d)


def _layer_prompt(x, lw):
    b, t, _ = x.shape
    g = lw['mix_pre']
    z = norm_matmul(x, g, lw['w_z'])
    xbc = norm_matmul(x, g, lw['w_xbc'])
    small = norm_matmul(x, g, lw['w_small'])
    br_g = norm_matmul(x, g, lw['w_brg'])
    sb_q = norm_matmul(x, g, lw['w_sbq'], out_dtype=BF16)
    nsa_qt = norm_matmul(x, g, lw['w_nq'].T, transposed=True)
    sb_kvt = norm_matmul(x, g, lw['w_sbkv'].T, transposed=True)
    nsa_kvt = norm_matmul(x, g, lw['w_nkv'].T, transposed=True)
    dt_raw = small[..., :SSD_HEADS]
    nsa_g = small[..., SSD_HEADS:SSD_HEADS + 3 * NSA_HEADS]

    h0 = jnp.zeros((b, SSD_HEADS, SSD_HEAD_DIM, SSD_STATE), F32)
    conv0 = jnp.zeros((b, SSD_CONV - 1, SSD_CONV_DIM), F32)
    ssd_y, h_new = ssd_prompt(xbc, z, dt_raw, h0, conv0, lw['conv_w'], lw['conv_b'], lw['dt_bias'], lw['a_log'],
                              lw['d_skip'], lw['ssd_norm'])
    conv_new = xbc[:, t - (SSD_CONV - 1):, :]
    sb_o = sb_prompt(sb_q, sb_kvt)

    nsa_o = nsa_prompt_t(nsa_qt, nsa_g, nsa_kvt, lw['cmp_pos'], lw['cmp_w1'], lw['cmp_w2'])

    y = _trunk_tail(x, ssd_y, sb_o, nsa_o, br_g, lw)
    sb_kv = jnp.moveaxis(sb_kvt.reshape(b, 2, SB_HEADS, HEAD_DIM, t), 4, 1)
    nsa_all = jnp.moveaxis(nsa_kvt.reshape(b, 6, NSA_KV_GROUPS, HEAD_DIM, t), 4, 1)
    keep = min(WINDOW, t)
    return y, sb_kv, nsa_all[:, :, 0:4], nsa_all[:, t - keep:, 4:6], h_new, conv_new


def _layer_sample(x, lw, layer, sb_pool_t, nsa_pool_t, win_t, h0, conv_buf, page_table):
    bsz, t = x.shape[:2]
    assert t == 1
    past = page_table.shape[1] * sb_pool_t.shape[-1]
    xr = x.reshape(1, bsz, D_MODEL)
    g = lw['mix_pre']
    pr = lambda w: norm_matmul(xr, g, w)[0]
    z, xbc, small, br_g = pr(lw['w_z']), pr(lw['w_xbc']), pr(lw['w_small']), pr(lw['w_brg'])
    sb_q, sb_kv, nsa_q, nsa_kv = pr(lw['w_sbq']), pr(lw['w_sbkv']), pr(lw['w_nq']), pr(lw['w_nkv'])
    dt_raw = small[:, :SSD_HEADS]
    nsa_g = small[:, SSD_HEADS:SSD_HEADS + 3 * NSA_HEADS]
    ssd_y, h_new = ssd_step(xbc, z, dt_raw, h0, conv_buf, lw['conv_w'], lw['conv_b'], lw['dt_bias'], lw['a_log'],
                            lw['d_skip'], lw['ssd_norm'])
    conv_new = jnp.concatenate([conv_buf[:, 1:], xbc[:, None, :]], axis=1)
    sb_o = sb_decode(sb_q, sb_pool_t, layer, page_table)
    nsa_o = nsa_decode(nsa_q, nsa_g, nsa_kv, nsa_pool_t, win_t, layer, page_table, lw['cmp_pos'], lw['cmp_w1'],
                       lw['cmp_w2'])
    y = _trunk_tail(x, ssd_y[:, None], sb_o.astype(BF16)[:, None], nsa_o.astype(BF16)[:, None], br_g[:, None], lw)
    kv_new = sb_kv.reshape(bsz, 1, 2, SB_HEADS, HEAD_DIM)
    nkv = nsa_kv.reshape(bsz, 1, 6, NSA_KV_GROUPS, HEAD_DIM)
    keep = min(WINDOW, past + 1)
    win_all_t = jnp.concatenate([win_t[layer], nkv[:, 0, 4:6][..., None]], axis=-1)
    win_new = jnp.moveaxis(win_all_t[..., win_all_t.shape[-1] - keep:], 4, 1)
    return y, kv_new, nkv[:, :, 0:4], win_new, h_new, conv_new


def kernel(x_prompt, x_sample, cache_sb_kv, cache_nsa_kv, cache_nsa_win, state_ssd, state_conv, page_table,
           norm_mix_pre, norm_mix_post, norm_ffn_pre, norm_ffn_post, w_in, ssd_conv_w, ssd_conv_b, ssd_dt_bias,
           ssd_a_log, ssd_d, ssd_norm, w_ssd_out, w_sb_out, nsa_cmp_pos, nsa_cmp_w1, nsa_cmp_w2, w_nsa_out, w_o,
           w_ffn_gate, w_ffn_up, w_ffn_down):
    p = dict(norm_mix_pre=norm_mix_pre, norm_mix_post=norm_mix_post, norm_ffn_pre=norm_ffn_pre,
             norm_ffn_post=norm_ffn_post, w_in=w_in, ssd_conv_w=ssd_conv_w, ssd_conv_b=ssd_conv_b,
             ssd_dt_bias=ssd_dt_bias, ssd_a_log=ssd_a_log, ssd_d=ssd_d, ssd_norm=ssd_norm, w_ssd_out=w_ssd_out,
             w_sb_out=w_sb_out, nsa_cmp_pos=nsa_cmp_pos, nsa_cmp_w1=nsa_cmp_w1, nsa_cmp_w2=nsa_cmp_w2,
             w_nsa_out=w_nsa_out, w_o=w_o, w_ffn_gate=w_ffn_gate, w_ffn_up=w_ffn_up, w_ffn_down=w_ffn_down)
    yp, ys = x_prompt, x_sample
    outs_p, outs_s = [], []
    time_minor = lambda a: jnp.transpose(a, (0, 1, 3, 4, 5, 2))
    sb_pool_t, nsa_pool_t, win_t = time_minor(cache_sb_kv), time_minor(cache_nsa_kv), time_minor(cache_nsa_win)
    for l in range(w_in.shape[0]):
        lw = _layer_weights(l, p)
        res = _layer_prompt(yp, lw)
        yp = res[0]
        outs_p.append(res[1:])
        res = _layer_sample(ys, lw, l, sb_pool_t, nsa_pool_t, win_t, state_ssd[l], state_conv[l], page_table)
        ys = res[0]
        outs_s.append(res[1:])
    st = lambda outs, i: jnp.stack([o[i] for o in outs])
    return (yp, ys, st(outs_p, 0), st(outs_s, 0), st(outs_p, 1), st(outs_s, 1), st(outs_p, 2), st(outs_s, 2),
            st(outs_p, 3), st(outs_s, 3), st(outs_p, 4), st(outs_s, 4))
```

```python
import functools
import math

import jax
import jax.numpy as jnp
from jax import lax
from jax.experimental import pallas as pl
from jax.experimental.pallas import tpu as pltpu

D_MODEL = 1024
HEAD_DIM = 64
SSD_INNER = D_MODEL
SSD_HEAD_DIM = 64
SSD_HEADS = SSD_INNER // SSD_HEAD_DIM
SSD_GROUPS = 2
SSD_STATE = 128
SSD_CONV = 4
SSD_GN = SSD_GROUPS * SSD_STATE
SSD_CONV_DIM = SSD_INNER + 2 * SSD_GN
SSD_CHUNK = 128
SB_HEADS = 8
SB_WIDTH = SB_HEADS * HEAD_DIM
NSA_HEADS = 8
NSA_KV_GROUPS = 2
NSA_REP = NSA_HEADS // NSA_KV_GROUPS
NSA_WIDTH = NSA_HEADS * HEAD_DIM
NSA_KV_WIDTH = NSA_KV_GROUPS * HEAD_DIM
CMP_BLOCK = 32
CMP_STRIDE = 16
CMP_HIDDEN = 128
SEL_BLOCK = 64
SEL_TOP = 16
SEL_LOCAL = 2
WINDOW = 512
Q_BLOCK = 128
N_BRANCH = 3
FFN_HIDDEN = ((8 * D_MODEL + 3 * 256 - 1) // (3 * 256)) * 256
RMS_EPS = 1e-6
IN_SPLITS = (SSD_INNER, SSD_CONV_DIM, SSD_HEADS, 3 * SB_WIDTH, NSA_WIDTH, 6 * NSA_KV_WIDTH, 3 * NSA_HEADS,
             N_BRANCH * D_MODEL)

V7X_LANES = 128
V7X_VMEM_LIMIT = 56 * 1024 * 1024
PAGES_PER_STEP = 16
SEQS_PER_STEP = 8
NEG = -1e30
BF16 = jnp.bfloat16
F32 = jnp.float32
HI = lax.Precision.HIGHEST


def _params(*sem):
    return pltpu.CompilerParams(dimension_semantics=sem, vmem_limit_bytes=V7X_VMEM_LIMIT)


def _pick(n, cands):
    for c in cands:
        if n % c == 0:
            return c
    return n


def _rms(x, w):
    return x * lax.rsqrt(jnp.mean(x * x, axis=-1, keepdims=True) + RMS_EPS) * w


def _softplus(x):
    return jnp.maximum(x, 0.0) + jnp.log1p(jnp.exp(-jnp.abs(x)))


def _sigmoid(x):
    return 1.0 / (1.0 + jnp.exp(-x))


def _dot_nt(a, b):
    return lax.dot_general(a, b, (((1,), (1,)), ((), ())), preferred_element_type=F32)


def _dot_tn(a, b):
    return lax.dot_general(a, b, (((0,), (0,)), ((), ())), preferred_element_type=F32)


def _norm_mm_kernel(x_ref, g_ref, w_ref, o_ref, h_ref, *, transposed):
    @pl.when(pl.program_id(2) == 0)
    def _():
        h_ref[...] = _rms(x_ref[...], g_ref[...]).astype(BF16)

    if transposed:
        o_ref[...] = _dot_nt(w_ref[...], h_ref[...]).astype(o_ref.dtype)
    else:
        o_ref[...] = jnp.dot(h_ref[...], w_ref[...], preferred_element_type=F32).astype(o_ref.dtype)


def norm_matmul(x, gain, w, out_dtype=F32, transposed=False):
    b, t, k = x.shape
    n = w.shape[0] if transposed else w.shape[1]
    tm = _pick(t, (1024, 512, 256, 128))
    tn = _pick(n, (1024, 768, 512, 256, 128))
    if transposed:
        w_spec = pl.BlockSpec((tn, k), lambda bi, i, j: (j, 0))
        o_spec = pl.BlockSpec((None, tn, tm), lambda bi, i, j: (bi, j, i))
        o_shape = (b, n, t)
    else:
        w_spec = pl.BlockSpec((k, tn), lambda bi, i, j: (0, j))
        o_spec = pl.BlockSpec((None, tm, tn), lambda bi, i, j: (bi, i, j))
        o_shape = (b, t, n)
    return pl.pallas_call(
        functools.partial(_norm_mm_kernel, transposed=transposed),
        grid=(b, t // tm, n // tn),
        in_specs=[pl.BlockSpec((None, tm, k), lambda bi, i, j: (bi, i, 0)),
                  pl.BlockSpec((1, k), lambda bi, i, j: (0, 0)),
                  w_spec],
        out_specs=o_spec,
        out_shape=jax.ShapeDtypeStruct(o_shape, out_dtype),
        scratch_shapes=[pltpu.VMEM((tm, k), BF16)],
        compiler_params=_params("parallel", "parallel", "arbitrary"),
        name="norm_matmul_t" if transposed else "norm_matmul",
    )(x, gain.reshape(1, k), w)


def _ssd_chunk_kernel(xbc_ref, z_ref, dt_ref, dtt_ref, h0_ref, c0_ref, cw_ref, cb_ref, dtb_ref, dtbt_ref,
                      alog_ref, alogt_ref, dfull_ref, nw_ref, exp_ref, y_ref, h_ref, xp_ref):
    q = SSD_CHUNK
    c = pl.program_id(1)

    @pl.when(c == 0)
    def _():
        h_ref[...] = h0_ref[...]
        xp_ref[5:8, :] = c0_ref[...]

    xp_ref[8:8 + q, :] = xbc_ref[...]
    conv = cb_ref[...]
    for j in range(SSD_CONV):
        conv = conv + cw_ref[j:j + 1, :] * xp_ref[5 + j:5 + j + q, :]
    xp_ref[5:8, :] = xp_ref[q + 5:q + 8, :]
    u = conv * _sigmoid(conv)
    xs = u[:, :SSD_INNER]
    bm = u[:, SSD_INNER:SSD_INNER + SSD_GN].astype(BF16)
    cm = u[:, SSD_INNER + SSD_GN:].astype(BF16)

    dt = _softplus(dt_ref[...] + dtb_ref[...])
    dtt = _softplus(dtt_ref[...] + dtbt_ref[...])
    dta = dt * (-jnp.exp(alog_ref[...]))
    dtat = dtt * (-jnp.exp(alogt_ref[...]))
    row = lax.broadcasted_iota(jnp.int32, (q, q), 0)
    col = lax.broadcasted_iota(jnp.int32, (q, q), 1)
    tril = row >= col
    acum = jnp.dot(tril.astype(F32), dta, precision=HI, preferred_element_type=F32)
    acumt = jnp.dot(dtat, (row <= col).astype(F32), precision=HI, preferred_element_type=F32)
    expand = exp_ref[...]
    dt_full = jnp.dot(dt, expand, precision=HI, preferred_element_type=F32)
    ea_full = jnp.dot(jnp.exp(acum), expand, precision=HI, preferred_element_type=F32)
    te_full = jnp.dot(jnp.exp(acum[q - 1:q, :] - acum), expand, precision=HI, preferred_element_type=F32)
    xdt = xs * dt_full
    xdt_b = xdt.astype(BF16)
    xw_b = (xdt * te_full).astype(BF16)

    r = SSD_HEADS // SSD_GROUPS
    gw = r * SSD_HEAD_DIM
    y_diag, y_off = [], []
    for g in range(SSD_GROUPS):
        cm_g = cm[:, g * SSD_STATE:(g + 1) * SSD_STATE]
        bm_g = bm[:, g * SSD_STATE:(g + 1) * SSD_STATE]
        cb = _dot_nt(cm_g, bm_g)
        h_g = h_ref[g * r:(g + 1) * r].reshape(gw, SSD_STATE)
        y_off.append(_dot_nt(cm_g, h_g.astype(BF16)))
        st = _dot_tn(xw_b[:, g * gw:(g + 1) * gw], bm_g)
        for hh in range(r):
            hd = g * r + hh
            seg = acum[:, hd:hd + 1] - acumt[hd:hd + 1, :]
            decay = jnp.exp(jnp.where(tril, seg, -jnp.inf))
            m = (cb * decay).astype(BF16)
            y_diag.append(jnp.dot(m, xdt_b[:, hd * SSD_HEAD_DIM:(hd + 1) * SSD_HEAD_DIM],
                                  preferred_element_type=F32))
            dec = jnp.exp(acumt[hd:hd + 1, q - 1:q])
            h_ref[hd] = dec * h_ref[hd] + st[hh * SSD_HEAD_DIM:(hh + 1) * SSD_HEAD_DIM, :]
    y = (jnp.concatenate(y_diag, axis=1) + jnp.concatenate(y_off, axis=1) * ea_full
         + dfull_ref[...] * xs)
    zz = z_ref[...]
    y = y * (zz * _sigmoid(zz))
    y_ref[...] = _rms(y, nw_ref[...]).astype(y_ref.dtype)


def ssd_prompt(xbc, z, dt_raw, h0, conv0, conv_w, conv_b, dt_bias, a_log, d_skip, norm_w):
    b, t, _ = xbc.shape
    q = SSD_CHUNK
    nc = t // q
    hds = SSD_HEADS
    expand = (jnp.arange(SSD_INNER)[None, :] // SSD_HEAD_DIM == jnp.arange(hds)[:, None]).astype(F32)
    d_full = jnp.repeat(d_skip, SSD_HEAD_DIM).reshape(1, SSD_INNER)
    dtt = jnp.swapaxes(dt_raw, 1, 2)
    full = lambda shape: pl.BlockSpec(shape, lambda bi, ci: (0,) * len(shape))
    y, h = pl.pallas_call(
        _ssd_chunk_kernel,
        grid=(b, nc),
        in_specs=[pl.BlockSpec((None, q, SSD_CONV_DIM), lambda bi, ci: (bi, ci, 0)),
                  pl.BlockSpec((None, q, SSD_INNER), lambda bi, ci: (bi, ci, 0)),
                  pl.BlockSpec((None, q, hds), lambda bi, ci: (bi, ci, 0)),
                  pl.BlockSpec((None, hds, q), lambda bi, ci: (bi, 0, ci)),
                  pl.BlockSpec((None, hds, SSD_HEAD_DIM, SSD_STATE), lambda bi, ci: (bi, 0, 0, 0)),
                  pl.BlockSpec((None, SSD_CONV - 1, SSD_CONV_DIM), lambda bi, ci: (bi, 0, 0)),
                  full((SSD_CONV, SSD_CONV_DIM)), full((1, SSD_CONV_DIM)),
                  full((1, hds)), full((hds, 1)), full((1, hds)), full((hds, 1)),
                  full((1, SSD_INNER)), full((1, SSD_INNER)), full((hds, SSD_INNER))],
        out_specs=[pl.BlockSpec((None, q, SSD_INNER), lambda bi, ci: (bi, ci, 0)),
                   pl.BlockSpec((None, hds, SSD_HEAD_DIM, SSD_STATE), lambda bi, ci: (bi, 0, 0, 0))],
        out_shape=[jax.ShapeDtypeStruct((b, t, SSD_INNER), BF16),
                   jax.ShapeDtypeStruct((b, hds, SSD_HEAD_DIM, SSD_STATE), F32)],
        scratch_shapes=[pltpu.VMEM((q + 8, SSD_CONV_DIM), F32)],
        compiler_params=_params("parallel", "arbitrary"),
        name="ssd_chunk_scan",
    )(xbc, z, dt_raw, dtt, h0, conv0, conv_w, conv_b.reshape(1, -1), dt_bias.reshape(1, hds),
      dt_bias.reshape(hds, 1), a_log.reshape(1, hds), a_log.reshape(hds, 1), d_full, norm_w.reshape(1, -1), expand)
    return y, h


def _sb_stage(z):
    sp = jnp.maximum(z, 0.0) + jnp.log(1.0 + jnp.exp(-jnp.abs(z)))
    hi = sp.astype(BF16)
    return hi, (sp - hi.astype(F32)).astype(BF16)


def _sb_prompt_kernel(q_ref, kt_ref, vt_ref, o_ref, z_ref, hl_ref, a_ref, *, tq, tk):
    i = pl.program_id(2)
    band = tq // tk
    assert band <= 2
    last = (i + 1) * band - 1
    q = (q_ref[...] * (HEAD_DIM ** -0.5)).astype(BF16)
    row = lax.broadcasted_iota(jnp.int32, (tk, tk), 0)
    col = lax.broadcasted_iota(jnp.int32, (tk, tk), 1)
    upper = jnp.where(row >= col, -1.0, 0.0).astype(BF16)
    upper2 = jnp.concatenate([upper, upper], axis=0)
    qpos = i * tq + lax.broadcasted_iota(jnp.int32, (tq, 1), 0)
    kcol = lax.broadcasted_iota(jnp.int32, (1, tk), 1)

    def tile(ref, s):
        off = pl.multiple_of(jnp.clip(last - s, 0, last) * tk, tk)
        return ref[:, pl.ds(off, tk)].astype(BF16)

    def masked_scores(s):
        kpos = jnp.clip(last - s, 0, last) * tk + kcol
        return jnp.where(kpos < qpos, jnp.dot(q, tile(kt_ref, s), preferred_element_type=F32), NEG)

    z0 = masked_scores(0)
    z_ref[0] = z0
    hl_ref[0, :, :tk], hl_ref[0, :, tk:] = _sb_stage(z0)
    z_ref[1] = masked_scores(1)
    a_ref[2] = jnp.zeros((tq, tk), BF16)

    def step(n, c, st):
        carry, acc = st
        nxt, prv = (c + 1) % 3, (c + 2) % 3
        later = jnp.dot(hl_ref[c], upper2, preferred_element_type=F32)
        acc = acc + _dot_nt(a_ref[prv], tile(vt_ref, n - 1))
        z_ref[prv] = jnp.dot(q, tile(kt_ref, n + 2), preferred_element_type=F32)
        hl_ref[nxt, :, :tk], hl_ref[nxt, :, tk:] = _sb_stage(z_ref[nxt])
        dead = jnp.where(n <= last, 0.0, NEG)
        a_ref[c] = jnp.exp(z_ref[c] + later + (carry + dead)).astype(BF16)
        return carry + later[:, 0:1], acc

    def body(trip, st):
        for c in range(3):
            st = step(3 * trip + c, c, st)
        return st

    st = (jnp.zeros((tq, 1), F32), jnp.zeros((tq, HEAD_DIM), F32))
    _, acc = lax.fori_loop(0, lax.div(last + 4, 3), body, st)
    o_ref[...] = acc.astype(o_ref.dtype)


def sb_prompt(q, kvt, tq=512, tk=256):
    b, t, _ = q.shape
    h = SB_HEADS
    tq, tk = min(tq, t), min(tk, t)
    assert t % tq == 0 and tq % tk == 0
    qh = jnp.swapaxes(q.reshape(b, t, h, HEAD_DIM), 1, 2)
    out = pl.pallas_call(
        functools.partial(_sb_prompt_kernel, tq=tq, tk=tk),
        grid=(b, h, t // tq),
        in_specs=[pl.BlockSpec((None, None, tq, HEAD_DIM), lambda bi, hi, i: (bi, hi, i, 0)),
                  pl.BlockSpec((None, HEAD_DIM, t), lambda bi, hi, i: (bi, hi, 0)),
                  pl.BlockSpec((None, HEAD_DIM, t), lambda bi, hi, i: (bi, h + hi, 0))],
        out_specs=pl.BlockSpec((None, None, tq, HEAD_DIM), lambda bi, hi, i: (bi, hi, i, 0)),
        out_shape=jax.ShapeDtypeStruct((b, h, t, HEAD_DIM), BF16),
        scratch_shapes=[pltpu.VMEM((3, tq, tk), F32), pltpu.VMEM((3, tq, 2 * tk), BF16),
                        pltpu.VMEM((3, tq, tk), BF16)],
        compiler_params=_params("parallel", "parallel", "arbitrary"),
        name="sb_prompt",
    )(qh, kvt, kvt)
    return jnp.swapaxes(out, 1, 2).reshape(b, t, h * HEAD_DIM)


def _merge_kernel(x_ref, ssd_ref, sb_ref, nsa_ref, gl_ref, wssd_ref, wsb_ref, wnsa_ref, wo_ref, nw_ref, o_ref):
    d = D_MODEL
    gl = gl_ref[...]
    merged = (_sigmoid(gl[:, :d]) * jnp.dot(ssd_ref[...], wssd_ref[...], preferred_element_type=F32)
              + _sigmoid(gl[:, d:2 * d]) * jnp.dot(sb_ref[...], wsb_ref[...], preferred_element_type=F32)
              + _sigmoid(gl[:, 2 * d:]) * jnp.dot(nsa_ref[...], wnsa_ref[...], preferred_element_type=F32))
    y = jnp.dot(merged.astype(BF16), wo_ref[...], preferred_element_type=F32)
    o_ref[...] = x_ref[...] + _rms(y, nw_ref[...])


def merge_branches(x, ssd_y, sb_o, nsa_o, gate_logits, w_ssd_out, w_sb_out, w_nsa_out, w_o, norm_w):
    m, d = x.shape
    tm = _pick(m, (512, 256, 128, 32))
    rows = lambda n: pl.BlockSpec((tm, n), lambda i: (i, 0))
    full = lambda a: pl.BlockSpec(a.shape, lambda i: (0, 0))
    nw = norm_w.reshape(1, d)
    return pl.pallas_call(
        _merge_kernel,
        grid=(m // tm,),
        in_specs=[rows(d), rows(ssd_y.shape[1]), rows(sb_o.shape[1]), rows(nsa_o.shape[1]), rows(N_BRANCH * d),
                  full(w_ssd_out), full(w_sb_out), full(w_nsa_out), full(w_o), full(nw)],
        out_specs=rows(d),
        out_shape=jax.ShapeDtypeStruct((m, d), F32),
        compiler_params=_params("parallel"),
        name="merge_branches",
    )(x, ssd_y, sb_o, nsa_o, gate_logits, w_ssd_out, w_sb_out, w_nsa_out, w_o, nw)


def _ffn_up_kernel(x_ref, g_ref, wg_ref, wu_ref, o_ref, h_ref):
    @pl.when(pl.program_id(1) == 0)
    def _():
        h_ref[...] = _rms(x_ref[...], g_ref[...]).astype(BF16)

    h = h_ref[...]
    a = jnp.dot(h, wg_ref[...], preferred_element_type=F32)
    u = jnp.dot(h, wu_ref[...], preferred_element_type=F32)
    o_ref[...] = (a * _sigmoid(a) * u).astype(o_ref.dtype)


def _ffn_down_kernel(a_ref, x_ref, wd_ref, nw_ref, o_ref):
    f = jnp.dot(a_ref[...], wd_ref[...], preferred_element_type=F32)
    o_ref[...] = x_ref[...] + _rms(f, nw_ref[...])


def ffn(x, pre_w, post_w, w_gate, w_up, w_down):
    m, d = x.shape
    f = w_gate.shape[1]
    tm = _pick(m, (1024, 512, 256, 128, 32))
    tn = _pick(f, (256, 128))
    act = pl.pallas_call(
        _ffn_up_kernel,
        grid=(m // tm, f // tn),
        in_specs=[pl.BlockSpec((tm, d), lambda i, j: (i, 0)),
                  pl.BlockSpec((1, d), lambda i, j: (0, 0)),
                  pl.BlockSpec((d, tn), lambda i, j: (0, j)),
                  pl.BlockSpec((d, tn), lambda i, j: (0, j))],
        out_specs=pl.BlockSpec((tm, tn), lambda i, j: (i, j)),
        out_shape=jax.ShapeDtypeStruct((m, f), BF16),
        scratch_shapes=[pltpu.VMEM((tm, d), BF16)],
        compiler_params=_params("parallel", "arbitrary"),
        name="ffn_up",
    )(x, pre_w.reshape(1, d), w_gate, w_up)
    tm2 = _pick(m, (512, 256, 128, 32))
    return pl.pallas_call(
        _ffn_down_kernel,
        grid=(m // tm2,),
        in_specs=[pl.BlockSpec((tm2, f), lambda i: (i, 0)),
                  pl.BlockSpec((tm2, d), lambda i: (i, 0)),
                  pl.BlockSpec((f, d), lambda i: (0, 0)),
                  pl.BlockSpec((1, d), lambda i: (0, 0))],
        out_specs=pl.BlockSpec((tm2, d), lambda i: (i, 0)),
        out_shape=jax.ShapeDtypeStruct((m, d), F32),
        compiler_params=_params("parallel"),
        name="ffn_down",
    )(act, x, w_down, post_w.reshape(1, d))


def _nsa_compress_kernel(r_ref, pos_ref, w1_ref, w2_ref, o_ref, sh_ref):
    nr = r_ref.shape[0]
    half = CMP_STRIDE * HEAD_DIM
    r = r_ref[...]
    top = jnp.dot((r + pos_ref[0:1, :]).astype(BF16), w1_ref[:half, :], preferred_element_type=F32)
    bot = jnp.dot((r + pos_ref[1:2, :]).astype(BF16), w1_ref[half:, :], preferred_element_type=F32)
    sh_ref[0:nr, :] = bot
    sh_ref[nr:nr + 8, :] = jnp.zeros((8, CMP_HIDDEN), F32)
    pre = top + sh_ref[1:nr + 1, :]
    hid = pre * _sigmoid(pre)
    o_ref[...] = jnp.dot(hid.astype(BF16), w2_ref[...], preferred_element_type=F32).astype(o_ref.dtype)


def nsa_compress(rows16, cmp_pos, cmp_w1, cmp_w2):
    assert CMP_BLOCK == 2 * CMP_STRIDE
    b, _, g, nr, w = rows16.shape
    pos = cmp_pos.reshape(2, 2, w)
    return pl.pallas_call(
        _nsa_compress_kernel,
        grid=(b, 2, g),
        in_specs=[pl.BlockSpec((None, None, None, nr, w), lambda bi, ki, gi: (bi, ki, gi, 0, 0)),
                  pl.BlockSpec((None, 2, w), lambda bi, ki, gi: (ki, 0, 0)),
                  pl.BlockSpec((None, 2 * w, CMP_HIDDEN), lambda bi, ki, gi: (ki, 0, 0)),
                  pl.BlockSpec((None, CMP_HIDDEN, HEAD_DIM), lambda bi, ki, gi: (ki, 0, 0))],
        out_specs=pl.BlockSpec((None, None, None, nr, HEAD_DIM), lambda bi, ki, gi: (bi, ki, gi, 0, 0)),
        out_shape=jax.ShapeDtypeStruct((b, 2, g, nr, HEAD_DIM), BF16),
        scratch_shapes=[pltpu.VMEM((nr + 8, CMP_HIDDEN), F32)],
        compiler_params=_params("parallel", "parallel", "parallel"),
        name="nsa_compress",
    )(rows16, pos, cmp_w1.astype(BF16), cmp_w2.astype(BF16))


def _overlap_matrix(n_cmp_rows, n_cmp, n_blk):
    c = jnp.arange(n_cmp_rows)[:, None]
    n = jnp.arange(n_blk)[None, :]
    c_start, c_end = c * CMP_STRIDE, c * CMP_STRIDE + CMP_BLOCK - 1
    return ((c_start < (n + 1) * SEL_BLOCK) & (c_end >= n * SEL_BLOCK) & (c < n_cmp)).astype(BF16)


def _split_dot(x, w):
    hi = x.astype(BF16)
    lo = (x - hi.astype(F32)).astype(BF16)
    return jnp.dot(hi, w, preferred_element_type=F32) + jnp.dot(lo, w, preferred_element_type=F32)


def _top_blocks(imp, blk, n_top):
    n_blk = imp.shape[1]
    sel = jnp.zeros(imp.shape, F32)
    for _ in range(n_top):
        m = jnp.max(imp, axis=-1, keepdims=True)
        idx = jnp.min(jnp.where(imp == m, blk, float(n_blk)), axis=-1, keepdims=True)
        hit = blk == idx
        sel = jnp.where(hit, 1.0, sel)
        imp = jnp.where(hit, -jnp.inf, imp)
    return sel


def _flash_step_t(qt, k, vt, mask, m, acc):
    s = jnp.where(mask, jnp.dot(k, qt, preferred_element_type=F32), NEG)
    m_new = jnp.maximum(m, jnp.max(s, axis=0, keepdims=True))
    p = jnp.exp(s - m_new).astype(BF16)
    vt_ext = jnp.concatenate([vt, jnp.ones((8, vt.shape[1]), BF16)], axis=0)
    return m_new, acc * jnp.exp(m - m_new) + jnp.dot(vt_ext, p, preferred_element_type=F32)


def _nsa_prompt_t_kernel(qt_ref, gt_ref, kc_ref, vct_ref, ovt_ref, ks_ref, vs_ref, kw_ref, vw_ref, o_ref,
                         s_ref, ch_ref, p_ref, *, tq, n_cmp):
    i = pl.program_id(2)
    rep, d = NSA_REP, HEAD_DIM
    lanes = rep * tq
    sel_shift = int(math.log2(SEL_BLOCK))
    qt_blk = qt_ref[...]
    qt = jnp.concatenate([qt_blk[r * d:(r + 1) * d, :] for r in range(rep)], axis=1)
    qt = (qt * (d ** -0.5)).astype(BF16)
    qpos = i * tq + lax.broadcasted_iota(jnp.int32, (1, tq), 1)
    per_head = lambda a: jnp.concatenate([a] * rep, axis=1)

    n_rows = kc_ref.shape[0]
    cidx = lax.broadcasted_iota(jnp.int32, (n_rows, 1), 0)
    vis_c = per_head(((cidx * CMP_STRIDE + (CMP_BLOCK - 1) <= qpos) & (cidx < n_cmp)).astype(F32)) > 0.5
    s_c = jnp.where(vis_c, jnp.dot(kc_ref[...], qt, preferred_element_type=F32), NEG)
    e_c = jnp.where(vis_c, jnp.exp(s_c - jnp.max(s_c, axis=0, keepdims=True)), 0.0)
    p_c = e_c / jnp.maximum(jnp.sum(e_c, axis=0, keepdims=True), 1e-30)
    o_c = jnp.dot(vct_ref[...], p_c.astype(BF16), preferred_element_type=F32)

    p_sum = p_c[:, 0:tq]
    for r in range(1, rep):
        p_sum = p_sum + p_c[:, r * tq:(r + 1) * tq]
    p_hi = p_sum.astype(BF16)
    p_lo = (p_sum - p_hi.astype(F32)).astype(BF16)
    imp = (jnp.dot(ovt_ref[...], p_hi, preferred_element_type=F32)
           + jnp.dot(ovt_ref[...], p_lo, preferred_element_type=F32))
    n_blk = ovt_ref.shape[0]
    blk_i = lax.broadcasted_iota(jnp.int32, (n_blk, 1), 0)
    cur = lax.shift_right_logical(qpos, sel_shift)
    valid = blk_i <= cur
    forced = valid & ((blk_i == 0) | (blk_i > cur - SEL_LOCAL))
    imp = jnp.where(forced, jnp.inf, jnp.where(valid, imp, -jnp.inf))

    m0 = jnp.full((1, lanes), NEG, F32)
    acc0 = jnp.zeros((d + 8, lanes), F32)
    krow = lax.broadcasted_iota(jnp.int32, (tq, 1), 0)

    st = (m0, acc0)
    for n in range(WINDOW // tq + 1):
        j = i - n
        off = pl.multiple_of(jnp.maximum(j, 0) * tq, tq)
        kpos = off + krow + jnp.where(j < 0, 1 << 30, 0)
        mask = per_head(((kpos <= qpos) & (kpos > qpos - WINDOW)).astype(F32)) > 0.5
        st = _flash_step_t(qt, kw_ref[pl.ds(off, tq), :].astype(BF16), vw_ref[:, pl.ds(off, tq)].astype(BF16),
                           mask, *st)
    acc_w = st[1]

    blk_f = blk_i.astype(F32)
    sel = jnp.zeros((n_blk, tq), F32)
    for _ in range(min(SEL_TOP, n_blk)):
        top = jnp.max(imp, axis=0, keepdims=True)
        idx = jnp.min(jnp.where(imp == top, blk_f, float(n_blk)), axis=0, keepdims=True)
        hit = blk_f == idx
        sel = jnp.where(hit, 1.0, sel)
        imp = jnp.where(hit, -jnp.inf, imp)
    sel = sel.astype(BF16)

    exp_blk = lax.broadcasted_iota(jnp.int32, (tq, n_blk), 1)
    exp_key = lax.shift_right_logical(lax.broadcasted_iota(jnp.int32, (tq, n_blk), 0), sel_shift)

    def key_tile(n):
        return jnp.clip(i - n, 0, i)

    def scores(n):
        j = key_tile(n)
        expand = (exp_blk == j * (tq // SEL_BLOCK) + exp_key).astype(BF16)
        return (jnp.dot(ks_ref[pl.ds(pl.multiple_of(j * tq, tq), tq), :].astype(BF16), qt, preferred_element_type=F32),
                jnp.dot(expand, sel, preferred_element_type=F32))

    def weighted_values(n, slot):
        vt = vs_ref[:, pl.ds(pl.multiple_of(key_tile(n) * tq, tq), tq)].astype(BF16)
        return jnp.dot(jnp.concatenate([vt, jnp.ones((8, tq), BF16)], axis=0), p_ref[slot],
                       preferred_element_type=F32)

    s_ref[0], ch_ref[0] = scores(0)
    p_ref[0] = jnp.zeros(p_ref.shape[1:], BF16)

    def sel_step(n, rd, wr, st):
        m, alpha, acc = st
        pv = weighted_values(n - 1, rd)
        s_ref[wr], ch_ref[wr] = scores(n + 1)
        kpos = (i - n) * tq + krow + jnp.where(n > i, 1 << 30, 0)
        mask = per_head(jnp.where(kpos <= qpos, ch_ref[rd], 0.0)) > 0.5
        s = jnp.where(mask, s_ref[rd], NEG)
        m_new = jnp.maximum(m, jnp.max(s, axis=0, keepdims=True))
        p_ref[wr] = jnp.exp(s - m_new).astype(BF16)
        return m_new, jnp.exp(m - m_new), acc * alpha + pv

    def sel_pair(pair, st):
        return sel_step(2 * pair + 1, 1, 0, sel_step(2 * pair, 0, 1, st))

    _, _, acc_s = lax.fori_loop(0, (i + 3) >> 1, sel_pair, (m0, jnp.ones((1, lanes), F32), acc0))

    o_s = acc_s[:d] / acc_s[d:d + 1]
    o_w = acc_w[:d] / acc_w[d:d + 1]
    gate = _sigmoid(gt_ref[...])
    outs = []
    for r in range(rep):
        sl = slice(r * tq, (r + 1) * tq)
        outs.append(gate[3 * r:3 * r + 1] * o_c[:, sl] + gate[3 * r + 1:3 * r + 2] * o_s[:, sl]
                    + gate[3 * r + 2:3 * r + 3] * o_w[:, sl])
    o_ref[...] = jnp.concatenate(outs, axis=0).astype(o_ref.dtype)


def nsa_prompt_t(nsa_qt, nsa_g, nsa_kvt, cmp_pos, cmp_w1, cmp_w2, tq=128):
    b, _, t = nsa_qt.shape
    g, d, rep = NSA_KV_GROUPS, HEAD_DIM, NSA_REP
    assert t % tq == 0 and tq % SEL_BLOCK == 0 and WINDOW % tq == 0 and t % CMP_STRIDE == 0
    n_cmp = (t - CMP_BLOCK) // CMP_STRIDE + 1
    nr = t // CMP_STRIDE
    n_blk = t // SEL_BLOCK
    kinds = nsa_kvt.reshape(b, 6, g, d, t)
    rows16 = jnp.swapaxes(kinds[:, 0:2], 3, 4).reshape(b, 2, g, nr, CMP_STRIDE * d)
    kcvc = nsa_compress(rows16, cmp_pos, cmp_w1, cmp_w2)
    vct = jnp.swapaxes(kcvc[:, 1], 2, 3)
    overlap_t = _overlap_matrix(nr, n_cmp, n_blk).T
    keys = jnp.swapaxes(kinds[:, 2::2], 3, 4).astype(BF16)
    gates = jnp.transpose(nsa_g.reshape(b, t, g, 3 * rep), (0, 2, 3, 1))
    gates = jnp.pad(gates, ((0, 0), (0, 0), (0, 16 - 3 * rep), (0, 0)))
    values = lambda kind: pl.BlockSpec((None, d, t), lambda bi, gi, i: (bi, kind * g + gi, 0))
    rows = lambda kind: pl.BlockSpec((None, None, None, t, d), lambda bi, gi, i: (bi, kind, gi, 0, 0))
    out = pl.pallas_call(
        functools.partial(_nsa_prompt_t_kernel, tq=tq, n_cmp=n_cmp),
        grid=(b, g, t // tq),
        in_specs=[pl.BlockSpec((None, rep * d, tq), lambda bi, gi, i: (bi, gi, i)),
                  pl.BlockSpec((None, None, 16, tq), lambda bi, gi, i: (bi, gi, 0, i)),
                  pl.BlockSpec((None, None, None, nr, d), lambda bi, gi, i: (bi, 0, gi, 0, 0)),
                  pl.BlockSpec((None, None, d, nr), lambda bi, gi, i: (bi, gi, 0, 0)),
                  pl.BlockSpec((n_blk, nr), lambda bi, gi, i: (0, 0)),
                  rows(0), values(3), rows(1), values(5)],
        out_specs=pl.BlockSpec((None, rep * d, tq), lambda bi, gi, i: (bi, gi, i)),
        out_shape=jax.ShapeDtypeStruct((b, g * rep * d, t), BF16),
        scratch_shapes=[pltpu.VMEM((2, tq, rep * tq), F32), pltpu.VMEM((2, tq, tq), F32),
                        pltpu.VMEM((2, tq, rep * tq), BF16)],
        compiler_params=_params("parallel", "parallel", "arbitrary"),
        name="nsa_prompt",
    )(nsa_qt, gates, kcvc, vct, overlap_t, keys, nsa_kvt, keys, nsa_kvt)
    return jnp.swapaxes(out, 1, 2)


def _ssd_step_pre_kernel(x_ref, buf_ref, cw_ref, cb_ref, dt_ref, dtb_ref, alog_ref, exp_ref,
                         xs_ref, xdt_ref, bm_ref, cm_ref, dec_ref):
    conv = cb_ref[...] + cw_ref[SSD_CONV - 1:SSD_CONV, :] * x_ref[...]
    for j in range(SSD_CONV - 1):
        conv = conv + cw_ref[j:j + 1, :] * buf_ref[j]
    u = conv * _sigmoid(conv)
    xs = u[:, :SSD_INNER]
    dt = _softplus(dt_ref[...] + dtb_ref[...])
    xs_ref[...] = xs
    xdt_ref[...] = xs * jnp.dot(dt, exp_ref[...], precision=HI, preferred_element_type=F32)
    bm_ref[...] = u[:, SSD_INNER:SSD_INNER + SSD_GN]
    cm_ref[...] = u[:, SSD_INNER + SSD_GN:]
    dec_ref[...] = jnp.exp(dt * (-jnp.exp(alog_ref[...])))


def _ssd_step_state_kernel(h0_ref, xdt_ref, dec_ref, bm_ref, cm_ref, h_ref, y_ref):
    r = SSD_HEADS // SSD_GROUPS
    for b in range(h0_ref.shape[0]):
        for hd in range(SSD_HEADS):
            g = hd // r
            hn = dec_ref[b, hd] * h0_ref[b, hd] + xdt_ref[b, hd] * bm_ref[b, g]
            h_ref[b, hd] = hn
            y_ref[b, hd] = jnp.sum(hn * cm_ref[b, g], axis=-1, keepdims=True)


def _ssd_step_post_kernel(y_ref, xs_ref, z_ref, dfull_ref, nw_ref, o_ref):
    zz = z_ref[...]
    y = (y_ref[...] + dfull_ref[...] * xs_ref[...]) * (zz * _sigmoid(zz))
    o_ref[...] = _rms(y, nw_ref[...]).astype(o_ref.dtype)


def ssd_step(xbc, z, dt_raw, h0, conv_buf, conv_w, conv_b, dt_bias, a_log, d_skip, norm_w):
    b = xbc.shape[0]
    hds, p, n = SSD_HEADS, SSD_HEAD_DIM, SSD_STATE
    expand = (jnp.arange(SSD_INNER)[None, :] // p == jnp.arange(hds)[:, None]).astype(F32)
    d_full = jnp.repeat(d_skip, p).reshape(1, SSD_INNER)
    sds = lambda shape: jax.ShapeDtypeStruct(shape, F32)
    xs, xdt, bm, cm, dec = pl.pallas_call(
        _ssd_step_pre_kernel,
        out_shape=[sds((b, SSD_INNER)), sds((b, SSD_INNER)), sds((b, SSD_GN)), sds((b, SSD_GN)), sds((b, hds))],
        name="ssd_step_pre",
    )(xbc, jnp.swapaxes(conv_buf, 0, 1), conv_w, conv_b.reshape(1, -1), dt_raw, dt_bias.reshape(1, hds),
      a_log.reshape(1, hds), expand)
    n_seq = _pick(b, (SEQS_PER_STEP // 2, 1))
    per_b = lambda *dims: pl.BlockSpec((n_seq,) + dims, lambda bi: (bi,) + (0,) * len(dims))
    h_new, y_col = pl.pallas_call(
        _ssd_step_state_kernel,
        grid=(b // n_seq,),
        in_specs=[per_b(hds, p, n), per_b(hds, p, 1), per_b(hds, 1, 1), per_b(SSD_GROUPS, 1, n),
                  per_b(SSD_GROUPS, 1, n)],
        out_specs=[per_b(hds, p, n), per_b(hds, p, 1)],
        out_shape=[sds((b, hds, p, n)), sds((b, hds, p, 1))],
        compiler_params=_params("parallel"),
        name="ssd_step_state",
    )(h0, xdt.reshape(b, hds, p, 1), dec.reshape(b, hds, 1, 1), bm.reshape(b, SSD_GROUPS, 1, n),
      cm.reshape(b, SSD_GROUPS, 1, n))
    y = pl.pallas_call(
        _ssd_step_post_kernel,
        out_shape=jax.ShapeDtypeStruct((b, SSD_INNER), BF16),
        name="ssd_step_post",
    )(y_col.reshape(b, SSD_INNER), xs, z, d_full, norm_w.reshape(1, -1))
    return y, h_new


def _sb_decode_kernel(pt_ref, q_ref, *refs):
    page_refs, (o_ref, carry_ref, acc_ref) = refs[:PAGES_PER_STEP], refs[PAGES_PER_STEP:]
    p = pl.program_id(1)
    tk = page_refs[0].shape[-1]
    hds = SB_HEADS

    @pl.when(p == 0)
    def _():
        carry_ref[...] = jnp.zeros(carry_ref.shape, F32)
        acc_ref[...] = jnp.zeros(acc_ref.shape, F32)

    scale = HEAD_DIM ** -0.5
    qs = [q_ref[h] * scale for h in range(hds)]
    z = jnp.concatenate([jnp.sum(ref[0, h] * qs[h], axis=0, keepdims=True)
                         for ref in page_refs for h in range(hds)], axis=0)
    row = lax.broadcasted_iota(jnp.int32, (tk, tk), 0)
    col = lax.broadcasted_iota(jnp.int32, (tk, tk), 1)
    sp = _softplus(z)
    later = _split_dot(-sp, (row > col).astype(BF16))
    total = later[:, 0:1] - sp[:, 0:1]
    carry = carry_ref[...]
    for k, ref in enumerate(page_refs):
        rows = slice(k * hds, (k + 1) * hds)
        a = jnp.exp(z[rows] - sp[rows] + later[rows] + carry)
        carry = carry + total[rows]
        for h in range(hds):
            acc_ref[h] += ref[1, h] * a[h:h + 1, :]
    carry_ref[...] = carry

    @pl.when(p == pl.num_programs(1) - 1)
    def _():
        for h in range(hds):
            o_ref[h] = jnp.sum(acc_ref[h], axis=-1, keepdims=True)


def _page_specs(block, layer, n_pages, kind_block, descending):
    def spec(k):
        def index(bi, p, pt):
            pos = p * PAGES_PER_STEP + k
            pos = n_pages - 1 - pos if descending else pos
            return (layer, pt[bi, pos], kind_block) + (0,) * (len(block) - 3)
        return pl.BlockSpec(block, index)
    return [spec(k) for k in range(PAGES_PER_STEP)]


def sb_decode(q, pool_t, layer, page_table):
    b = q.shape[0]
    h, d = SB_HEADS, HEAD_DIM
    n_pages = page_table.shape[1]
    page = pool_t.shape[-1]
    assert n_pages % PAGES_PER_STEP == 0
    out = pl.pallas_call(
        _sb_decode_kernel,
        grid_spec=pltpu.PrefetchScalarGridSpec(
            num_scalar_prefetch=1,
            grid=(b, n_pages // PAGES_PER_STEP),
            in_specs=[pl.BlockSpec((None, h, d, 1), lambda bi, p, pt: (bi, 0, 0, 0))]
            + _page_specs((None, None, 2, h, d, page), layer, n_pages, 0, descending=True),
            out_specs=pl.BlockSpec((None, h, d, 1), lambda bi, p, pt: (bi, 0, 0, 0)),
            scratch_shapes=[pltpu.VMEM((h, 1), F32), pltpu.VMEM((h, d, page), F32)]),
        out_shape=jax.ShapeDtypeStruct((b, h, d, 1), F32),
        compiler_params=_params("parallel", "arbitrary"),
        name="sb_decode",
    )(page_table, q.reshape(b, h, d, 1), *([pool_t] * PAGES_PER_STEP))
    return out.reshape(b, h * d)


def _nsa_gather_kernel(pt_ref, *refs):
    page_refs, (o_ref, x_ref) = refs[:PAGES_PER_STEP], refs[PAGES_PER_STEP:]
    page = page_refs[0].shape[-1]
    grp, d = NSA_KV_GROUPS, HEAD_DIM
    n_out = page // CMP_STRIDE
    for k, ref in enumerate(page_refs):
        for kind in range(2):
            x = x_ref.at[2 * k + kind]
            x[...] = ref[kind].reshape(grp * d, page).T
            steps = [x[pl.ds(s, n_out, stride=CMP_STRIDE), :] for s in range(CMP_STRIDE)]
            for g in range(grp):
                o_ref[kind, g, k * n_out:(k + 1) * n_out, :] = jnp.concatenate(
                    [st[:, g * d:(g + 1) * d] for st in steps], axis=1)


def nsa_gather_rows16(pool_t, layer, page_table):
    b, n_pages = page_table.shape
    g, d = NSA_KV_GROUPS, HEAD_DIM
    page = pool_t.shape[-1]
    assert page % CMP_STRIDE == 0 and n_pages % PAGES_PER_STEP == 0
    n_out = PAGES_PER_STEP * (page // CMP_STRIDE)
    return pl.pallas_call(
        _nsa_gather_kernel,
        grid_spec=pltpu.PrefetchScalarGridSpec(
            num_scalar_prefetch=1,
            grid=(b, n_pages // PAGES_PER_STEP),
            in_specs=_page_specs((None, None, 2, g, d, page), layer, n_pages, 0, descending=False),
            out_specs=pl.BlockSpec((None, 2, g, n_out, CMP_STRIDE * d), lambda bi, p, pt: (bi, 0, 0, p, 0)),
            scratch_shapes=[pltpu.VMEM((2 * PAGES_PER_STEP, page, g * d), F32)]),
        out_shape=jax.ShapeDtypeStruct((b, 2, g, n_pages * (page // CMP_STRIDE), CMP_STRIDE * d), F32),
        compiler_params=_params("parallel", "arbitrary"),
        name="nsa_gather_rows16",
    )(page_table, *([pool_t] * PAGES_PER_STEP))


def _nsa_decode_select_kernel(q_ref, kc_ref, vc_ref, ov_ref, oc_ref, sel_ref, *, n_cmp, q_pos):
    rep, d, grp = NSA_REP, HEAD_DIM, NSA_KV_GROUPS
    n_seq = q_ref.shape[0]
    n_rows = kc_ref.shape[2]
    n_blk = ov_ref.shape[1]
    cidx = lax.broadcasted_iota(jnp.int32, (1, n_rows), 1)
    vis = (cidx * CMP_STRIDE + (CMP_BLOCK - 1) <= q_pos) & (cidx < n_cmp)
    imp = []
    for b in range(n_seq):
        q = (q_ref[b] * (d ** -0.5)).astype(BF16)
        o_c = []
        for g in range(grp):
            s = jnp.where(vis, _dot_nt(q[g * rep:(g + 1) * rep], kc_ref[b, g]), NEG)
            e = jnp.where(vis, jnp.exp(s - jnp.max(s, axis=-1, keepdims=True)), 0.0)
            p = e / jnp.maximum(jnp.sum(e, axis=-1, keepdims=True), 1e-30)
            o_c.append(jnp.dot(p.astype(BF16), vc_ref[b, g], preferred_element_type=F32))
            imp.append(_split_dot(jnp.sum(p, axis=0, keepdims=True), ov_ref[...]))
        oc_ref[b] = jnp.concatenate(o_c, axis=0)
    imp = jnp.concatenate(imp, axis=0)
    blk_i = lax.broadcasted_iota(jnp.int32, (1, n_blk), 1)
    forced = (blk_i == 0) | (blk_i > n_blk - SEL_LOCAL)
    sel = _top_blocks(jnp.where(forced, jnp.inf, imp), blk_i.astype(F32), min(SEL_TOP - 1, n_blk))
    pad = jnp.zeros((sel_ref.shape[1] - grp, n_blk), F32)
    for b in range(n_seq):
        sel_ref[b] = jnp.concatenate([sel[b * grp:(b + 1) * grp], pad], axis=0)


def _nsa_decode_attend_kernel(pt_ref, q_ref, sel_ref, *refs, win_skip):
    page_refs = refs[:PAGES_PER_STEP]
    new_ref, win_ref, oc_ref, gt_ref, o_ref, m_ref, acc_ref = refs[PAGES_PER_STEP:]
    p = pl.program_id(1)
    n_steps = pl.num_programs(1)
    rep, d, grp = NSA_REP, HEAD_DIM, NSA_KV_GROUPS
    tk = page_refs[0].shape[-1]
    lanes = PAGES_PER_STEP * tk
    scale = d ** -0.5
    qf = q_ref[...] * scale
    q = qf.astype(BF16)
    new = new_ref[...]
    new_row = lambda kind, g: new[kind * grp + g:kind * grp + g + 1, :]
    per_head = lambda f: jnp.concatenate([f(g) for g in range(grp)], axis=0)

    @pl.when(p == 0)
    def _():
        m_ref[...] = per_head(lambda g: jnp.sum(qf[g * rep:(g + 1) * rep] * new_row(2, g), axis=-1, keepdims=True))
        acc_ref[...] = per_head(lambda g: jnp.concatenate(
            [jnp.broadcast_to(new_row(3, g), (rep, d)), jnp.ones((rep, d), F32)], axis=1))

    n_blk = sel_ref.shape[1]
    lane = lax.broadcasted_iota(jnp.int32, (n_blk, lanes), 1)
    page_pos = (n_steps - p) * PAGES_PER_STEP - 1 - lax.shift_right_logical(lane, int(math.log2(tk)))
    blk_of_lane = page_pos * (tk // SEL_BLOCK) + lax.shift_right_logical(lane & (tk - 1), int(math.log2(SEL_BLOCK)))
    expand = (lax.broadcasted_iota(jnp.int32, (n_blk, lanes), 0) == blk_of_lane).astype(BF16)
    chosen = jnp.dot(sel_ref[...].astype(BF16), expand, preferred_element_type=F32)

    def scores(g):
        qg = q[g * rep:(g + 1) * rep]
        sg = jnp.concatenate([jnp.dot(qg, ref[0, g].astype(BF16), preferred_element_type=F32) for ref in page_refs],
                             axis=1)
        return jnp.where(chosen[g:g + 1, :] > 0.5, sg, NEG)

    s = per_head(scores)
    m_old = m_ref[...]
    m_new = jnp.maximum(m_old, jnp.max(s, axis=-1, keepdims=True))
    pr = jnp.exp(s - m_new).astype(BF16)

    def weighted_values(g):
        out = jnp.zeros((rep, 2 * d), F32)
        for k, ref in enumerate(page_refs):
            vt_ext = jnp.concatenate([ref[1, g].astype(BF16), jnp.ones((d, tk), BF16)], axis=0)
            out = out + _dot_nt(pr[g * rep:(g + 1) * rep, k * tk:(k + 1) * tk], vt_ext)
        return out

    acc_ref[...] = acc_ref[...] * jnp.exp(m_old - m_new) + per_head(weighted_values)
    m_ref[...] = m_new

    @pl.when(p == n_steps - 1)
    def _():
        acc = acc_ref[...]
        o_s = acc[:, :d] / acc[:, d:]
        wlen = win_ref.shape[-1]
        vis = lax.broadcasted_iota(jnp.int32, (1, wlen), 1) >= win_skip

        def window(g):
            qg = q[g * rep:(g + 1) * rep]
            s_w = jnp.where(vis, jnp.dot(qg, win_ref[0, g].astype(BF16), preferred_element_type=F32), NEG)
            s_n = jnp.sum(qf[g * rep:(g + 1) * rep] * new_row(4, g), axis=-1, keepdims=True)
            mx = jnp.maximum(jnp.max(s_w, axis=-1, keepdims=True), s_n)
            e_w = jnp.where(vis, jnp.exp(s_w - mx), 0.0)
            e_n = jnp.exp(s_n - mx)
            num = _dot_nt(e_w.astype(BF16), win_ref[1, g].astype(BF16)) + e_n * new_row(5, g)
            return num / (jnp.sum(e_w, axis=-1, keepdims=True) + e_n)

        o_w = per_head(window)
        gate = _sigmoid(gt_ref[...])
        o_ref[...] = gate[:, 0:1] * oc_ref[...] + gate[:, 1:2] * o_s + gate[:, 2:3] * o_w


def nsa_decode(nsa_q, nsa_g, nsa_kv_new, pool_t, win_t, layer, page_table, cmp_pos, cmp_w1, cmp_w2):
    b = nsa_q.shape[0]
    g, d, rep, hds = NSA_KV_GROUPS, HEAD_DIM, NSA_REP, NSA_HEADS
    n_pages = page_table.shape[1]
    page = pool_t.shape[-1]
    past = n_pages * page
    wlen = win_t.shape[-1]
    assert past % SEL_BLOCK == 0 and past % CMP_STRIDE == 0 and page % SEL_BLOCK == 0 and wlen <= past
    n_cmp = (past + 1 - CMP_BLOCK) // CMP_STRIDE + 1
    nr = past // CMP_STRIDE
    n_blk = past // SEL_BLOCK
    rows16 = nsa_gather_rows16(pool_t, layer, page_table)
    kcvc = nsa_compress(rows16, cmp_pos, cmp_w1, cmp_w2)
    overlap = _overlap_matrix(nr, n_cmp, n_blk)
    q3 = nsa_q.reshape(b, hds, d)
    n_seq = _pick(b, (SEQS_PER_STEP, 1))
    o_c, sel = pl.pallas_call(
        functools.partial(_nsa_decode_select_kernel, n_cmp=n_cmp, q_pos=past),
        grid=(b // n_seq,),
        in_specs=[pl.BlockSpec((n_seq, hds, d), lambda bi: (bi, 0, 0)),
                  pl.BlockSpec((n_seq, None, g, nr, d), lambda bi: (bi, 0, 0, 0, 0)),
                  pl.BlockSpec((n_seq, None, g, nr, d), lambda bi: (bi, 1, 0, 0, 0)),
                  pl.BlockSpec((nr, n_blk), lambda bi: (0, 0))],
        out_specs=[pl.BlockSpec((n_seq, hds, d), lambda bi: (bi, 0, 0)),
                   pl.BlockSpec((n_seq, 8, n_blk), lambda bi: (bi, 0, 0))],
        out_shape=[jax.ShapeDtypeStruct((b, hds, d), F32), jax.ShapeDtypeStruct((b, 8, n_blk), F32)],
        compiler_params=_params("parallel"),
        name="nsa_decode_select",
    )(q3, kcvc, kcvc, overlap)
    gates = jnp.pad(nsa_g.reshape(b, hds, 3), ((0, 0), (0, 0), (0, V7X_LANES - 3)))
    fixed = lambda *dims: pl.BlockSpec((None,) + dims, lambda bi, p, pt: (bi,) + (0,) * len(dims))
    out = pl.pallas_call(
        functools.partial(_nsa_decode_attend_kernel, win_skip=wlen - WINDOW + 1),
        grid_spec=pltpu.PrefetchScalarGridSpec(
            num_scalar_prefetch=1,
            grid=(b, n_pages // PAGES_PER_STEP),
            in_specs=[fixed(hds, d), fixed(8, n_blk)]
            + _page_specs((None, None, 2, g, d, page), layer, n_pages, 1, descending=True)
            + [fixed(6 * g, d),
               pl.BlockSpec((None, None, 2, g, d, wlen), lambda bi, p, pt: (layer, bi, 0, 0, 0, 0)),
               fixed(hds, d), fixed(hds, V7X_LANES)],
            out_specs=fixed(hds, d),
            scratch_shapes=[pltpu.VMEM((hds, 1), F32), pltpu.VMEM((hds, 2 * d), F32)]),
        out_shape=jax.ShapeDtypeStruct((b, hds, d), F32),
        compiler_params=_params("parallel", "arbitrary"),
        name="nsa_decode_attend",
    )(page_table, q3, sel, *([pool_t] * PAGES_PER_STEP), nsa_kv_new.reshape(b, 6 * g, d), win_t, o_c, gates)
    return out.reshape(b, hds * d)


def _col_offsets():
    offs, s = [], 0
    for n in IN_SPLITS:
        offs.append(s)
        s += n
    return offs


def _layer_weights(l, p):
    o = _col_offsets()
    w_in = p['w_in'][l]
    cols = lambda a, n: w_in[:, a:a + n]
    small = jnp.concatenate([cols(o[2], SSD_HEADS), cols(o[6], 3 * NSA_HEADS)], axis=1)
    small = jnp.pad(small, ((0, 0), (0, V7X_LANES - small.shape[1])))
    bf = lambda a: a.astype(BF16)
    return {
        'w_z': bf(cols(o[0], SSD_INNER)), 'w_xbc': bf(cols(o[1], SSD_CONV_DIM)), 'w_small': bf(small),
        'w_sbq': bf(cols(o[3], SB_WIDTH)), 'w_sbkv': bf(cols(o[3] + SB_WIDTH, 2 * SB_WIDTH)),
        'w_nq': bf(cols(o[4], NSA_WIDTH)), 'w_nkv': bf(cols(o[5], 6 * NSA_KV_WIDTH)),
        'w_brg': bf(cols(o[7], N_BRANCH * D_MODEL)),
        'mix_pre': p['norm_mix_pre'][l], 'mix_post': p['norm_mix_post'][l],
        'ffn_pre': p['norm_ffn_pre'][l], 'ffn_post': p['norm_ffn_post'][l],
        'conv_w': p['ssd_conv_w'][l], 'conv_b': p['ssd_conv_b'][l], 'dt_bias': p['ssd_dt_bias'][l],
        'a_log': p['ssd_a_log'][l], 'd_skip': p['ssd_d'][l], 'ssd_norm': p['ssd_norm'][l],
        'w_ssd_out': bf(p['w_ssd_out'][l]), 'w_sb_out': bf(p['w_sb_out'][l]), 'w_nsa_out': bf(p['w_nsa_out'][l]),
        'w_o': bf(p['w_o'][l]), 'w_ffn_gate': bf(p['w_ffn_gate'][l]), 'w_ffn_up': bf(p['w_ffn_up'][l]),
        'w_ffn_down': bf(p['w_ffn_down'][l]),
        'cmp_pos': p['nsa_cmp_pos'][l], 'cmp_w1': p['nsa_cmp_w1'][l], 'cmp_w2': p['nsa_cmp_w2'][l],
    }


def _trunk_tail(x, ssd_y, sb_o, nsa_o, br_g, lw):
    b, t, d = x.shape
    m = b * t
    x1 = merge_branches(x.reshape(m, d), ssd_y.reshape(m, -1), sb_o.reshape(m, -1), nsa_o.reshape(m, -1),
                        br_g.reshape(m, -1), lw['w_ssd_out'], lw['w_sb_out'], lw['w_nsa_out'], lw['w_o'],
                        lw['mix_post'])
    x2 = ffn(x1, lw['ffn_pre'], lw['ffn_post'], lw['w_ffn_gate'], lw['w_ffn_up'], lw['w_ffn_down'])
    return x2.reshape(b, t, d)


def _layer_prompt(x, lw):
    b, t, _ = x.shape
    g = lw['mix_pre']
    z = norm_matmul(x, g, lw['w_z'])
    xbc = norm_matmul(x, g, lw['w_xbc'])
    small = norm_matmul(x, g, lw['w_small'])
    br_g = norm_matmul(x, g, lw['w_brg'])
    sb_q = norm_matmul(x, g, lw['w_sbq'], out_dtype=BF16)
    nsa_qt = norm_matmul(x, g, lw['w_nq'].T, transposed=True)
    sb_kvt = norm_matmul(x, g, lw['w_sbkv'].T, transposed=True)
    nsa_kvt = norm_matmul(x, g, lw['w_nkv'].T, transposed=True)
    dt_raw = small[..., :SSD_HEADS]
    nsa_g = small[..., SSD_HEADS:SSD_HEADS + 3 * NSA_HEADS]

    h0 = jnp.zeros((b, SSD_HEADS, SSD_HEAD_DIM, SSD_STATE), F32)
    conv0 = jnp.zeros((b, SSD_CONV - 1, SSD_CONV_DIM), F32)
    ssd_y, h_new = ssd_prompt(xbc, z, dt_raw, h0, conv0, lw['conv_w'], lw['conv_b'], lw['dt_bias'], lw['a_log'],
                              lw['d_skip'], lw['ssd_norm'])
    conv_new = xbc[:, t - (SSD_CONV - 1):, :]
    sb_o = sb_prompt(sb_q, sb_kvt)

    nsa_o = nsa_prompt_t(nsa_qt, nsa_g, nsa_kvt, lw['cmp_pos'], lw['cmp_w1'], lw['cmp_w2'])

    y = _trunk_tail(x, ssd_y, sb_o, nsa_o, br_g, lw)
    sb_kv = jnp.moveaxis(sb_kvt.reshape(b, 2, SB_HEADS, HEAD_DIM, t), 4, 1)
    nsa_all = jnp.moveaxis(nsa_kvt.reshape(b, 6, NSA_KV_GROUPS, HEAD_DIM, t), 4, 1)
    keep = min(WINDOW, t)
    return y, sb_kv, nsa_all[:, :, 0:4], nsa_all[:, t - keep:, 4:6], h_new, conv_new


def _layer_sample(x, lw, layer, sb_pool_t, nsa_pool_t, win_t, h0, conv_buf, page_table):
    bsz, t = x.shape[:2]
    assert t == 1
    past = page_table.shape[1] * sb_pool_t.shape[-1]
    xr = x.reshape(1, bsz, D_MODEL)
    g = lw['mix_pre']
    pr = lambda w: norm_matmul(xr, g, w)[0]
    z, xbc, small, br_g = pr(lw['w_z']), pr(lw['w_xbc']), pr(lw['w_small']), pr(lw['w_brg'])
    sb_q, sb_kv, nsa_q, nsa_kv = pr(lw['w_sbq']), pr(lw['w_sbkv']), pr(lw['w_nq']), pr(lw['w_nkv'])
    dt_raw = small[:, :SSD_HEADS]
    nsa_g = small[:, SSD_HEADS:SSD_HEADS + 3 * NSA_HEADS]
    ssd_y, h_new = ssd_step(xbc, z, dt_raw, h0, conv_buf, lw['conv_w'], lw['conv_b'], lw['dt_bias'], lw['a_log'],
                            lw['d_skip'], lw['ssd_norm'])
    conv_new = jnp.concatenate([conv_buf[:, 1:], xbc[:, None, :]], axis=1)
    sb_o = sb_decode(sb_q, sb_pool_t, layer, page_table)
    nsa_o = nsa_decode(nsa_q, nsa_g, nsa_kv, nsa_pool_t, win_t, layer, page_table, lw['cmp_pos'], lw['cmp_w1'],
                       lw['cmp_w2'])
    y = _trunk_tail(x, ssd_y[:, None], sb_o.astype(BF16)[:, None], nsa_o.astype(BF16)[:, None], br_g[:, None], lw)
    kv_new = sb_kv.reshape(bsz, 1, 2, SB_HEADS, HEAD_DIM)
    nkv = nsa_kv.reshape(bsz, 1, 6, NSA_KV_GROUPS, HEAD_DIM)
    keep = min(WINDOW, past + 1)
    win_all_t = jnp.concatenate([win_t[layer], nkv[:, 0, 4:6][..., None]], axis=-1)
    win_new = jnp.moveaxis(win_all_t[..., win_all_t.shape[-1] - keep:], 4, 1)
    return y, kv_new, nkv[:, :, 0:4], win_new, h_new, conv_new


def kernel(x_prompt, x_sample, cache_sb_kv, cache_nsa_kv, cache_nsa_win, state_ssd, state_conv, page_table,
           norm_mix_pre, norm_mix_post, norm_ffn_pre, norm_ffn_post, w_in, ssd_conv_w, ssd_conv_b, ssd_dt_bias,
           ssd_a_log, ssd_d, ssd_norm, w_ssd_out, w_sb_out, nsa_cmp_pos, nsa_cmp_w1, nsa_cmp_w2, w_nsa_out, w_o,
           w_ffn_gate, w_ffn_up, w_ffn_down):
    p = dict(norm_mix_pre=norm_mix_pre, norm_mix_post=norm_mix_post, norm_ffn_pre=norm_ffn_pre,
             norm_ffn_post=norm_ffn_post, w_in=w_in, ssd_conv_w=ssd_conv_w, ssd_conv_b=ssd_conv_b,
             ssd_dt_bias=ssd_dt_bias, ssd_a_log=ssd_a_log, ssd_d=ssd_d, ssd_norm=ssd_norm, w_ssd_out=w_ssd_out,
             w_sb_out=w_sb_out, nsa_cmp_pos=nsa_cmp_pos, nsa_cmp_w1=nsa_cmp_w1, nsa_cmp_w2=nsa_cmp_w2,
             w_nsa_out=w_nsa_out, w_o=w_o, w_ffn_gate=w_ffn_gate, w_ffn_up=w_ffn_up, w_ffn_down=w_ffn_down)
    yp, ys = x_prompt, x_sample
    outs_p, outs_s = [], []
    time_minor = lambda a: jnp.transpose(a, (0, 1, 3, 4, 5, 2))
    sb_pool_t, nsa_pool_t, win_t = time_minor(cache_sb_kv), time_minor(cache_nsa_kv), time_minor(cache_nsa_win)
    for l in range(w_in.shape[0]):
        lw = _layer_weights(l, p)
        res = _layer_prompt(yp, lw)
        yp = res[0]
        outs_p.append(res[1:])
        res = _layer_sample(ys, lw, l, sb_pool_t, nsa_pool_t, win_t, state_ssd[l], state_conv[l], page_table)
        ys = res[0]
        outs_s.append(res[1:])
    st = lambda outs, i: jnp.stack([o[i] for o in outs])
    return (yp, ys, st(outs_p, 0), st(outs_s, 0), st(outs_p, 1), st(outs_s, 1), st(outs_p, 2), st(outs_s, 2),
            st(outs_p, 3), st(outs_s, 3), st(outs_p, 4), st(outs_s, 4))
```

```python
import functools
import math

import jax
import jax.numpy as jnp
from jax import lax
from jax.experimental import pallas as pl
from jax.experimental.pallas import tpu as pltpu

D_MODEL = 1024
HEAD_DIM = 64
SSD_INNER = D_MODEL
SSD_HEAD_DIM = 64
SSD_HEADS = SSD_INNER // SSD_HEAD_DIM
SSD_GROUPS = 2
SSD_STATE = 128
SSD_CONV = 4
SSD_GN = SSD_GROUPS * SSD_STATE
SSD_CONV_DIM = SSD_INNER + 2 * SSD_GN
SSD_CHUNK = 128
SB_HEADS = 8
SB_WIDTH = SB_HEADS * HEAD_DIM
NSA_HEADS = 8
NSA_KV_GROUPS = 2
NSA_REP = NSA_HEADS // NSA_KV_GROUPS
NSA_WIDTH = NSA_HEADS * HEAD_DIM
NSA_KV_WIDTH = NSA_KV_GROUPS * HEAD_DIM
CMP_BLOCK = 32
CMP_STRIDE = 16
CMP_HIDDEN = 128
SEL_BLOCK = 64
SEL_TOP = 16
SEL_LOCAL = 2
WINDOW = 512
Q_BLOCK = 128
N_BRANCH = 3
FFN_HIDDEN = ((8 * D_MODEL + 3 * 256 - 1) // (3 * 256)) * 256
RMS_EPS = 1e-6
IN_SPLITS = (SSD_INNER, SSD_CONV_DIM, SSD_HEADS, 3 * SB_WIDTH, NSA_WIDTH, 6 * NSA_KV_WIDTH, 3 * NSA_HEADS,
             N_BRANCH * D_MODEL)

V7X_LANES = 128
V7X_VMEM_LIMIT = 56 * 1024 * 1024
PAGES_PER_STEP = 16
SEQS_PER_STEP = 8
NEG = -1e30
BF16 = jnp.bfloat16
F32 = jnp.float32
HI = lax.Precision.HIGHEST


def _params(*sem):
    return pltpu.CompilerParams(dimension_semantics=sem, vmem_limit_bytes=V7X_VMEM_LIMIT)


def _pick(n, cands):
    for c in cands:
        if n % c == 0:
            return c
    return n


def _rms(x, w):
    return x * lax.rsqrt(jnp.mean(x * x, axis=-1, keepdims=True) + RMS_EPS) * w


def _softplus(x):
    return jnp.maximum(x, 0.0) + jnp.log1p(jnp.exp(-jnp.abs(x)))


def _sigmoid(x):
    return 1.0 / (1.0 + jnp.exp(-x))


def _dot_nt(a, b):
    return lax.dot_general(a, b, (((1,), (1,)), ((), ())), preferred_element_type=F32)


def _dot_tn(a, b):
    return lax.dot_general(a, b, (((0,), (0,)), ((), ())), preferred_element_type=F32)


def _norm_mm_kernel(x_ref, g_ref, w_ref, o_ref, h_ref, *, transposed):
    @pl.when(pl.program_id(2) == 0)
    def _():
        h_ref[...] = _rms(x_ref[...], g_ref[...]).astype(BF16)

    if transposed:
        o_ref[...] = _dot_nt(w_ref[...], h_ref[...]).astype(o_ref.dtype)
    else:
        o_ref[...] = jnp.dot(h_ref[...], w_ref[...], preferred_element_type=F32).astype(o_ref.dtype)


def norm_matmul(x, gain, w, out_dtype=F32, transposed=False):
    b, t, k = x.shape
    n = w.shape[0] if transposed else w.shape[1]
    tm = _pick(t, (1024, 512, 256, 128))
    tn = _pick(n, (1024, 768, 512, 256, 128))
    if transposed:
        w_spec = pl.BlockSpec((tn, k), lambda bi, i, j: (j, 0))
        o_spec = pl.BlockSpec((None, tn, tm), lambda bi, i, j: (bi, j, i))
        o_shape = (b, n, t)
    else:
        w_spec = pl.BlockSpec((k, tn), lambda bi, i, j: (0, j))
        o_spec = pl.BlockSpec((None, tm, tn), lambda bi, i, j: (bi, i, j))
        o_shape = (b, t, n)
    return pl.pallas_call(
        functools.partial(_norm_mm_kernel, transposed=transposed),
        grid=(b, t // tm, n // tn),
        in_specs=[pl.BlockSpec((None, tm, k), lambda bi, i, j: (bi, i, 0)),
                  pl.BlockSpec((1, k), lambda bi, i, j: (0, 0)),
                  w_spec],
        out_specs=o_spec,
        out_shape=jax.ShapeDtypeStruct(o_shape, out_dtype),
        scratch_shapes=[pltpu.VMEM((tm, k), BF16)],
        compiler_params=_params("parallel", "parallel", "arbitrary"),
        name="norm_matmul_t" if transposed else "norm_matmul",
    )(x, gain.reshape(1, k), w)


def _ssd_chunk_kernel(xbc_ref, z_ref, dt_ref, dtt_ref, h0_ref, c0_ref, cw_ref, cb_ref, dtb_ref, dtbt_ref,
                      alog_ref, alogt_ref, dfull_ref, nw_ref, exp_ref, y_ref, h_ref, xp_ref):
    q = SSD_CHUNK
    c = pl.program_id(1)

    @pl.when(c == 0)
    def _():
        h_ref[...] = h0_ref[...]
        xp_ref[5:8, :] = c0_ref[...]

    xp_ref[8:8 + q, :] = xbc_ref[...]
    conv = cb_ref[...]
    for j in range(SSD_CONV):
        conv = conv + cw_ref[j:j + 1, :] * xp_ref[5 + j:5 + j + q, :]
    xp_ref[5:8, :] = xp_ref[q + 5:q + 8, :]
    u = conv * _sigmoid(conv)
    xs = u[:, :SSD_INNER]
    bm = u[:, SSD_INNER:SSD_INNER + SSD_GN].astype(BF16)
    cm = u[:, SSD_INNER + SSD_GN:].astype(BF16)

    dt = _softplus(dt_ref[...] + dtb_ref[...])
    dtt = _softplus(dtt_ref[...] + dtbt_ref[...])
    dta = dt * (-jnp.exp(alog_ref[...]))
    dtat = dtt * (-jnp.exp(alogt_ref[...]))
    row = lax.broadcasted_iota(jnp.int32, (q, q), 0)
    col = lax.broadcasted_iota(jnp.int32, (q, q), 1)
    tril = row >= col
    acum = jnp.dot(tril.astype(F32), dta, precision=HI, preferred_element_type=F32)
    acumt = jnp.dot(dtat, (row <= col).astype(F32), precision=HI, preferred_element_type=F32)
    expand = exp_ref[...]
    dt_full = jnp.dot(dt, expand, precision=HI, preferred_element_type=F32)
    ea_full = jnp.dot(jnp.exp(acum), expand, precision=HI, preferred_element_type=F32)
    te_full = jnp.dot(jnp.exp(acum[q - 1:q, :] - acum), expand, precision=HI, preferred_element_type=F32)
    xdt = xs * dt_full
    xdt_b = xdt.astype(BF16)
    xw_b = (xdt * te_full).astype(BF16)

    r = SSD_HEADS // SSD_GROUPS
    gw = r * SSD_HEAD_DIM
    y_diag, y_off = [], []
    for g in range(SSD_GROUPS):
        cm_g = cm[:, g * SSD_STATE:(g + 1) * SSD_STATE]
        bm_g = bm[:, g * SSD_STATE:(g + 1) * SSD_STATE]
        cb = _dot_nt(cm_g, bm_g)
        h_g = h_ref[g * r:(g + 1) * r].reshape(gw, SSD_STATE)
        y_off.append(_dot_nt(cm_g, h_g.astype(BF16)))
        st = _dot_tn(xw_b[:, g * gw:(g + 1) * gw], bm_g)
        for hh in range(r):
            hd = g * r + hh
            seg = acum[:, hd:hd + 1] - acumt[hd:hd + 1, :]
            decay = jnp.exp(jnp.where(tril, seg, -jnp.inf))
            m = (cb * decay).astype(BF16)
            y_diag.append(jnp.dot(m, xdt_b[:, hd * SSD_HEAD_DIM:(hd + 1) * SSD_HEAD_DIM],
                                  preferred_element_type=F32))
            dec = jnp.exp(acumt[hd:hd + 1, q - 1:q])
            h_ref[hd] = dec * h_ref[hd] + st[hh * SSD_HEAD_DIM:(hh + 1) * SSD_HEAD_DIM, :]
    y = (jnp.concatenate(y_diag, axis=1) + jnp.concatenate(y_off, axis=1) * ea_full
         + dfull_ref[...] * xs)
    zz = z_ref[...]
    y = y * (zz * _sigmoid(zz))
    y_ref[...] = _rms(y, nw_ref[...]).astype(y_ref.dtype)


def ssd_prompt(xbc, z, dt_raw, h0, conv0, conv_w, conv_b, dt_bias, a_log, d_skip, norm_w):
    b, t, _ = xbc.shape
    q = SSD_CHUNK
    nc = t // q
    hds = SSD_HEADS
    expand = (jnp.arange(SSD_INNER)[None, :] // SSD_HEAD_DIM == jnp.arange(hds)[:, None]).astype(F32)
    d_full = jnp.repeat(d_skip, SSD_HEAD_DIM).reshape(1, SSD_INNER)
    dtt = jnp.swapaxes(dt_raw, 1, 2)
    full = lambda shape: pl.BlockSpec(shape, lambda bi, ci: (0,) * len(shape))
    y, h = pl.pallas_call(
        _ssd_chunk_kernel,
        grid=(b, nc),
        in_specs=[pl.BlockSpec((None, q, SSD_CONV_DIM), lambda bi, ci: (bi, ci, 0)),
                  pl.BlockSpec((None, q, SSD_INNER), lambda bi, ci: (bi, ci, 0)),
                  pl.BlockSpec((None, q, hds), lambda bi, ci: (bi, ci, 0)),
                  pl.BlockSpec((None, hds, q), lambda bi, ci: (bi, 0, ci)),
                  pl.BlockSpec((None, hds, SSD_HEAD_DIM, SSD_STATE), lambda bi, ci: (bi, 0, 0, 0)),
                  pl.BlockSpec((None, SSD_CONV - 1, SSD_CONV_DIM), lambda bi, ci: (bi, 0, 0)),
                  full((SSD_CONV, SSD_CONV_DIM)), full((1, SSD_CONV_DIM)),
                  full((1, hds)), full((hds, 1)), full((1, hds)), full((hds, 1)),
                  full((1, SSD_INNER)), full((1, SSD_INNER)), full((hds, SSD_INNER))],
        out_specs=[pl.BlockSpec((None, q, SSD_INNER), lambda bi, ci: (bi, ci, 0)),
                   pl.BlockSpec((None, hds, SSD_HEAD_DIM, SSD_STATE), lambda bi, ci: (bi, 0, 0, 0))],
        out_shape=[jax.ShapeDtypeStruct((b, t, SSD_INNER), BF16),
                   jax.ShapeDtypeStruct((b, hds, SSD_HEAD_DIM, SSD_STATE), F32)],
        scratch_shapes=[pltpu.VMEM((q + 8, SSD_CONV_DIM), F32)],
        compiler_params=_params("parallel", "arbitrary"),
        name="ssd_chunk_scan",
    )(xbc, z, dt_raw, dtt, h0, conv0, conv_w, conv_b.reshape(1, -1), dt_bias.reshape(1, hds),
      dt_bias.reshape(hds, 1), a_log.reshape(1, hds), a_log.reshape(hds, 1), d_full, norm_w.reshape(1, -1), expand)
    return y, h


def _sb_stage(z):
    sp = jnp.maximum(z, 0.0) + jnp.log(1.0 + jnp.exp(-jnp.abs(z)))
    hi = sp.astype(BF16)
    return hi, (sp - hi.astype(F32)).astype(BF16)


def _sb_prompt_kernel(q_ref, kt_ref, vt_ref, o_ref, z_ref, hl_ref, a_ref, *, tq, tk):
    i = pl.program_id(2)
    band = tq // tk
    assert band <= 2
    last = (i + 1) * band - 1
    q = (q_ref[...] * (HEAD_DIM ** -0.5)).astype(BF16)
    row = lax.broadcasted_iota(jnp.int32, (tk, tk), 0)
    col = lax.broadcasted_iota(jnp.int32, (tk, tk), 1)
    upper = jnp.where(row >= col, -1.0, 0.0).astype(BF16)
    upper2 = jnp.concatenate([upper, upper], axis=0)
    qpos = i * tq + lax.broadcasted_iota(jnp.int32, (tq, 1), 0)
    kcol = lax.broadcasted_iota(jnp.int32, (1, tk), 1)

    def tile(ref, s):
        off = pl.multiple_of(jnp.clip(last - s, 0, last) * tk, tk)
        return ref[:, pl.ds(off, tk)].astype(BF16)

    def masked_scores(s):
        kpos = jnp.clip(last - s, 0, last) * tk + kcol
        return jnp.where(kpos < qpos, jnp.dot(q, tile(kt_ref, s), preferred_element_type=F32), NEG)

    z0 = masked_scores(0)
    z_ref[0] = z0
    hl_ref[0, :, :tk], hl_ref[0, :, tk:] = _sb_stage(z0)
    z_ref[1] = masked_scores(1)
    a_ref[2] = jnp.zeros((tq, tk), BF16)

    def step(n, c, st):
        carry, acc = st
        nxt, prv = (c + 1) % 3, (c + 2) % 3
        later = jnp.dot(hl_ref[c], upper2, preferred_element_type=F32)
        acc = acc + _dot_nt(a_ref[prv], tile(vt_ref, n - 1))
        z_ref[prv] = jnp.dot(q, tile(kt_ref, n + 2), preferred_element_type=F32)
        hl_ref[nxt, :, :tk], hl_ref[nxt, :, tk:] = _sb_stage(z_ref[nxt])
        dead = jnp.where(n <= last, 0.0, NEG)
        a_ref[c] = jnp.exp(z_ref[c] + later + (carry + dead)).astype(BF16)
        return carry + later[:, 0:1], acc

    def body(trip, st):
        for c in range(3):
            st = step(3 * trip + c, c, st)
        return st

    st = (jnp.zeros((tq, 1), F32), jnp.zeros((tq, HEAD_DIM), F32))
    _, acc = lax.fori_loop(0, lax.div(last + 4, 3), body, st)
    o_ref[...] = acc.astype(o_ref.dtype)


def sb_prompt(q, kvt, tq=512, tk=256):
    b, t, _ = q.shape
    h = SB_HEADS
    tq, tk = min(tq, t), min(tk, t)
    assert t % tq == 0 and tq % tk == 0
    qh = jnp.swapaxes(q.reshape(b, t, h, HEAD_DIM), 1, 2)
    out = pl.pallas_call(
        functools.partial(_sb_prompt_kernel, tq=tq, tk=tk),
        grid=(b, h, t // tq),
        in_specs=[pl.BlockSpec((None, None, tq, HEAD_DIM), lambda bi, hi, i: (bi, hi, i, 0)),
                  pl.BlockSpec((None, HEAD_DIM, t), lambda bi, hi, i: (bi, hi, 0)),
                  pl.BlockSpec((None, HEAD_DIM, t), lambda bi, hi, i: (bi, h + hi, 0))],
        out_specs=pl.BlockSpec((None, None, tq, HEAD_DIM), lambda bi, hi, i: (bi, hi, i, 0)),
        out_shape=jax.ShapeDtypeStruct((b, h, t, HEAD_DIM), BF16),
        scratch_shapes=[pltpu.VMEM((3, tq, tk), F32), pltpu.VMEM((3, tq, 2 * tk), BF16),
                        pltpu.VMEM((3, tq, tk), BF16)],
        compiler_params=_params("parallel", "parallel", "arbitrary"),
        name="sb_prompt",
    )(qh, kvt, kvt)
    return jnp.swapaxes(out, 1, 2).reshape(b, t, h * HEAD_DIM)


def _merge_kernel(x_ref, ssd_ref, sb_ref, nsa_ref, gl_ref, wssd_ref, wsb_ref, wnsa_ref, wo_ref, nw_ref, o_ref):
    d = D_MODEL
    gl = gl_ref[...]
    merged = (_sigmoid(gl[:, :d]) * jnp.dot(ssd_ref[...], wssd_ref[...], preferred_element_type=F32)
              + _sigmoid(gl[:, d:2 * d]) * jnp.dot(sb_ref[...], wsb_ref[...], preferred_element_type=F32)
              + _sigmoid(gl[:, 2 * d:]) * jnp.dot(nsa_ref[...], wnsa_ref[...], preferred_element_type=F32))
    y = jnp.dot(merged.astype(BF16), wo_ref[...], preferred_element_type=F32)
    o_ref[...] = x_ref[...] + _rms(y, nw_ref[...])


def merge_branches(x, ssd_y, sb_o, nsa_o, gate_logits, w_ssd_out, w_sb_out, w_nsa_out, w_o, norm_w):
    m, d = x.shape
    tm = _pick(m, (512, 256, 128, 32))
    rows = lambda n: pl.BlockSpec((tm, n), lambda i: (i, 0))
    full = lambda a: pl.BlockSpec(a.shape, lambda i: (0, 0))
    nw = norm_w.reshape(1, d)
    return pl.pallas_call(
        _merge_kernel,
        grid=(m // tm,),
        in_specs=[rows(d), rows(ssd_y.shape[1]), rows(sb_o.shape[1]), rows(nsa_o.shape[1]), rows(N_BRANCH * d),
                  full(w_ssd_out), full(w_sb_out), full(w_nsa_out), full(w_o), full(nw)],
        out_specs=rows(d),
        out_shape=jax.ShapeDtypeStruct((m, d), F32),
        compiler_params=_params("parallel"),
        name="merge_branches",
    )(x, ssd_y, sb_o, nsa_o, gate_logits, w_ssd_out, w_sb_out, w_nsa_out, w_o, nw)


def _ffn_up_kernel(x_ref, g_ref, wg_ref, wu_ref, o_ref, h_ref):
    @pl.when(pl.program_id(1) == 0)
    def _():
        h_ref[...] = _rms(x_ref[...], g_ref[...]).astype(BF16)

    h = h_ref[...]
    a = jnp.dot(h, wg_ref[...], preferred_element_type=F32)
    u = jnp.dot(h, wu_ref[...], preferred_element_type=F32)
    o_ref[...] = (a * _sigmoid(a) * u).astype(o_ref.dtype)


def _ffn_down_kernel(a_ref, x_ref, wd_ref, nw_ref, o_ref):
    f = jnp.dot(a_ref[...], wd_ref[...], preferred_element_type=F32)
    o_ref[...] = x_ref[...] + _rms(f, nw_ref[...])


def ffn(x, pre_w, post_w, w_gate, w_up, w_down):
    m, d = x.shape
    f = w_gate.shape[1]
    tm = _pick(m, (1024, 512, 256, 128, 32))
    tn = _pick(f, (256, 128))
    act = pl.pallas_call(
        _ffn_up_kernel,
        grid=(m // tm, f // tn),
        in_specs=[pl.BlockSpec((tm, d), lambda i, j: (i, 0)),
                  pl.BlockSpec((1, d), lambda i, j: (0, 0)),
                  pl.BlockSpec((d, tn), lambda i, j: (0, j)),
                  pl.BlockSpec((d, tn), lambda i, j: (0, j))],
        out_specs=pl.BlockSpec((tm, tn), lambda i, j: (i, j)),
        out_shape=jax.ShapeDtypeStruct((m, f), BF16),
        scratch_shapes=[pltpu.VMEM((tm, d), BF16)],
        compiler_params=_params("parallel", "arbitrary"),
        name="ffn_up",
    )(x, pre_w.reshape(1, d), w_gate, w_up)
    tm2 = _pick(m, (512, 256, 128, 32))
    return pl.pallas_call(
        _ffn_down_kernel,
        grid=(m // tm2,),
        in_specs=[pl.BlockSpec((tm2, f), lambda i: (i, 0)),
                  pl.BlockSpec((tm2, d), lambda i: (i, 0)),
                  pl.BlockSpec((f, d), lambda i: (0, 0)),
                  pl.BlockSpec((1, d), lambda i: (0, 0))],
        out_specs=pl.BlockSpec((tm2, d), lambda i: (i, 0)),
        out_shape=jax.ShapeDtypeStruct((m, d), F32),
        compiler_params=_params("parallel"),
        name="ffn_down",
    )(act, x, w_down, post_w.reshape(1, d))


def _compress_rows(r, pos_ref, w1_ref, w2_ref, sh_ref):
    nr = r.shape[0]
    half = CMP_STRIDE * HEAD_DIM
    top = jnp.dot((r + pos_ref[0:1, :]).astype(BF16), w1_ref[:half, :], preferred_element_type=F32)
    bot = jnp.dot((r + pos_ref[1:2, :]).astype(BF16), w1_ref[half:, :], preferred_element_type=F32)
    sh_ref[0:nr, :] = bot
    sh_ref[nr:nr + 8, :] = jnp.zeros((8, CMP_HIDDEN), F32)
    pre = top + sh_ref[1:nr + 1, :]
    hid = pre * _sigmoid(pre)
    return jnp.dot(hid.astype(BF16), w2_ref[...], preferred_element_type=F32)


def _nsa_compress_kernel(r_ref, pos_ref, w1_ref, w2_ref, o_ref, sh_ref):
    o_ref[...] = _compress_rows(r_ref[...], pos_ref, w1_ref, w2_ref, sh_ref).astype(o_ref.dtype)


def nsa_compress(rows16, cmp_pos, cmp_w1, cmp_w2):
    assert CMP_BLOCK == 2 * CMP_STRIDE
    b, _, g, nr, w = rows16.shape
    pos = cmp_pos.reshape(2, 2, w)
    return pl.pallas_call(
        _nsa_compress_kernel,
        grid=(b, 2, g),
        in_specs=[pl.BlockSpec((None, None, None, nr, w), lambda bi, ki, gi: (bi, ki, gi, 0, 0)),
                  pl.BlockSpec((None, 2, w), lambda bi, ki, gi: (ki, 0, 0)),
                  pl.BlockSpec((None, 2 * w, CMP_HIDDEN), lambda bi, ki, gi: (ki, 0, 0)),
                  pl.BlockSpec((None, CMP_HIDDEN, HEAD_DIM), lambda bi, ki, gi: (ki, 0, 0))],
        out_specs=pl.BlockSpec((None, None, None, nr, HEAD_DIM), lambda bi, ki, gi: (bi, ki, gi, 0, 0)),
        out_shape=jax.ShapeDtypeStruct((b, 2, g, nr, HEAD_DIM), BF16),
        scratch_shapes=[pltpu.VMEM((nr + 8, CMP_HIDDEN), F32)],
        compiler_params=_params("parallel", "parallel", "parallel"),
        name="nsa_compress",
    )(rows16, pos, cmp_w1.astype(BF16), cmp_w2.astype(BF16))


def _overlap_matrix(n_cmp_rows, n_cmp, n_blk):
    c = jnp.arange(n_cmp_rows)[:, None]
    n = jnp.arange(n_blk)[None, :]
    c_start, c_end = c * CMP_STRIDE, c * CMP_STRIDE + CMP_BLOCK - 1
    return ((c_start < (n + 1) * SEL_BLOCK) & (c_end >= n * SEL_BLOCK) & (c < n_cmp)).astype(BF16)


def _split_dot(x, w):
    hi = x.astype(BF16)
    lo = (x - hi.astype(F32)).astype(BF16)
    return jnp.dot(hi, w, preferred_element_type=F32) + jnp.dot(lo, w, preferred_element_type=F32)


def _top_blocks(imp, blk, n_top):
    n_blk = imp.shape[1]
    sel = jnp.zeros(imp.shape, F32)
    for _ in range(n_top):
        m = jnp.max(imp, axis=-1, keepdims=True)
        idx = jnp.min(jnp.where(imp == m, blk, float(n_blk)), axis=-1, keepdims=True)
        hit = blk == idx
        sel = jnp.where(hit, 1.0, sel)
        imp = jnp.where(hit, -jnp.inf, imp)
    return sel


def _flash_step_t(qt, k, vt, mask, m, acc):
    s = jnp.where(mask, jnp.dot(k, qt, preferred_element_type=F32), NEG)
    m_new = jnp.maximum(m, jnp.max(s, axis=0, keepdims=True))
    p = jnp.exp(s - m_new).astype(BF16)
    vt_ext = jnp.concatenate([vt, jnp.ones((8, vt.shape[1]), BF16)], axis=0)
    return m_new, acc * jnp.exp(m - m_new) + jnp.dot(vt_ext, p, preferred_element_type=F32)


def _nsa_prompt_t_kernel(qt_ref, gt_ref, kc_ref, vct_ref, ovt_ref, ks_ref, vs_ref, kw_ref, vw_ref, o_ref,
                         s_ref, ch_ref, p_ref, *, tq, n_cmp):
    i = pl.program_id(2)
    rep, d = NSA_REP, HEAD_DIM
    lanes = rep * tq
    sel_shift = int(math.log2(SEL_BLOCK))
    qt_blk = qt_ref[...]
    qt = jnp.concatenate([qt_blk[r * d:(r + 1) * d, :] for r in range(rep)], axis=1)
    qt = (qt * (d ** -0.5)).astype(BF16)
    qpos = i * tq + lax.broadcasted_iota(jnp.int32, (1, tq), 1)
    per_head = lambda a: jnp.concatenate([a] * rep, axis=1)

    n_rows = kc_ref.shape[0]
    cidx = lax.broadcasted_iota(jnp.int32, (n_rows, 1), 0)
    vis_c = per_head(((cidx * CMP_STRIDE + (CMP_BLOCK - 1) <= qpos) & (cidx < n_cmp)).astype(F32)) > 0.5
    s_c = jnp.where(vis_c, jnp.dot(kc_ref[...], qt, preferred_element_type=F32), NEG)
    e_c = jnp.where(vis_c, jnp.exp(s_c - jnp.max(s_c, axis=0, keepdims=True)), 0.0)
    p_c = e_c / jnp.maximum(jnp.sum(e_c, axis=0, keepdims=True), 1e-30)
    o_c = jnp.dot(vct_ref[...], p_c.astype(BF16), preferred_element_type=F32)

    p_sum = p_c[:, 0:tq]
    for r in range(1, rep):
        p_sum = p_sum + p_c[:, r * tq:(r + 1) * tq]
    p_hi = p_sum.astype(BF16)
    p_lo = (p_sum - p_hi.astype(F32)).astype(BF16)
    imp = (jnp.dot(ovt_ref[...], p_hi, preferred_element_type=F32)
           + jnp.dot(ovt_ref[...], p_lo, preferred_element_type=F32))
    n_blk = ovt_ref.shape[0]
    blk_i = lax.broadcasted_iota(jnp.int32, (n_blk, 1), 0)
    cur = lax.shift_right_logical(qpos, sel_shift)
    valid = blk_i <= cur
    forced = valid & ((blk_i == 0) | (blk_i > cur - SEL_LOCAL))
    imp = jnp.where(forced, jnp.inf, jnp.where(valid, imp, -jnp.inf))

    m0 = jnp.full((1, lanes), NEG, F32)
    acc0 = jnp.zeros((d + 8, lanes), F32)
    krow = lax.broadcasted_iota(jnp.int32, (tq, 1), 0)

    st = (m0, acc0)
    for n in range(WINDOW // tq + 1):
        j = i - n
        off = pl.multiple_of(jnp.maximum(j, 0) * tq, tq)
        kpos = off + krow + jnp.where(j < 0, 1 << 30, 0)
        mask = per_head(((kpos <= qpos) & (kpos > qpos - WINDOW)).astype(F32)) > 0.5
        st = _flash_step_t(qt, kw_ref[pl.ds(off, tq), :].astype(BF16), vw_ref[:, pl.ds(off, tq)].astype(BF16),
                           mask, *st)
    acc_w = st[1]

    blk_f = blk_i.astype(F32)
    sel = jnp.zeros((n_blk, tq), F32)
    for _ in range(min(SEL_TOP, n_blk)):
        top = jnp.max(imp, axis=0, keepdims=True)
        idx = jnp.min(jnp.where(imp == top, blk_f, float(n_blk)), axis=0, keepdims=True)
        hit = blk_f == idx
        sel = jnp.where(hit, 1.0, sel)
        imp = jnp.where(hit, -jnp.inf, imp)
    sel = sel.astype(BF16)

    exp_blk = lax.broadcasted_iota(jnp.int32, (tq, n_blk), 1)
    exp_key = lax.shift_right_logical(lax.broadcasted_iota(jnp.int32, (tq, n_blk), 0), sel_shift)

    def key_tile(n):
        return jnp.clip(i - n, 0, i)

    def scores(n):
        j = key_tile(n)
        expand = (exp_blk == j * (tq // SEL_BLOCK) + exp_key).astype(BF16)
        return (jnp.dot(ks_ref[pl.ds(pl.multiple_of(j * tq, tq), tq), :].astype(BF16), qt, preferred_element_type=F32),
                jnp.dot(expand, sel, preferred_element_type=F32))

    def weighted_values(n, slot):
        vt = vs_ref[:, pl.ds(pl.multiple_of(key_tile(n) * tq, tq), tq)].astype(BF16)
        return jnp.dot(jnp.concatenate([vt, jnp.ones((8, tq), BF16)], axis=0), p_ref[slot],
                       preferred_element_type=F32)

    s_ref[0], ch_ref[0] = scores(0)
    p_ref[0] = jnp.zeros(p_ref.shape[1:], BF16)

    def sel_step(n, rd, wr, st):
        m, alpha, acc = st
        pv = weighted_values(n - 1, rd)
        s_ref[wr], ch_ref[wr] = scores(n + 1)
        kpos = (i - n) * tq + krow + jnp.where(n > i, 1 << 30, 0)
        mask = per_head(jnp.where(kpos <= qpos, ch_ref[rd], 0.0)) > 0.5
        s = jnp.where(mask, s_ref[rd], NEG)
        m_new = jnp.maximum(m, jnp.max(s, axis=0, keepdims=True))
        p_ref[wr] = jnp.exp(s - m_new).astype(BF16)
        return m_new, jnp.exp(m - m_new), acc * alpha + pv

    def sel_pair(pair, st):
        return sel_step(2 * pair + 1, 1, 0, sel_step(2 * pair, 0, 1, st))

    _, _, acc_s = lax.fori_loop(0, (i + 3) >> 1, sel_pair, (m0, jnp.ones((1, lanes), F32), acc0))

    o_s = acc_s[:d] / acc_s[d:d + 1]
    o_w = acc_w[:d] / acc_w[d:d + 1]
    gate = _sigmoid(gt_ref[...])
    outs = []
    for r in range(rep):
        sl = slice(r * tq, (r + 1) * tq)
        outs.append(gate[3 * r:3 * r + 1] * o_c[:, sl] + gate[3 * r + 1:3 * r + 2] * o_s[:, sl]
                    + gate[3 * r + 2:3 * r + 3] * o_w[:, sl])
    o_ref[...] = jnp.concatenate(outs, axis=0).astype(o_ref.dtype)


def nsa_prompt_t(nsa_qt, nsa_g, nsa_kvt, cmp_pos, cmp_w1, cmp_w2, tq=128):
    b, _, t = nsa_qt.shape
    g, d, rep = NSA_KV_GROUPS, HEAD_DIM, NSA_REP
    assert t % tq == 0 and tq % SEL_BLOCK == 0 and WINDOW % tq == 0 and t % CMP_STRIDE == 0
    n_cmp = (t - CMP_BLOCK) // CMP_STRIDE + 1
    nr = t // CMP_STRIDE
    n_blk = t // SEL_BLOCK
    kinds = nsa_kvt.reshape(b, 6, g, d, t)
    rows16 = jnp.swapaxes(kinds[:, 0:2], 3, 4).reshape(b, 2, g, nr, CMP_STRIDE * d)
    kcvc = nsa_compress(rows16, cmp_pos, cmp_w1, cmp_w2)
    vct = jnp.swapaxes(kcvc[:, 1], 2, 3)
    overlap_t = _overlap_matrix(nr, n_cmp, n_blk).T
    keys = jnp.swapaxes(kinds[:, 2::2], 3, 4).astype(BF16)
    gates = jnp.transpose(nsa_g.reshape(b, t, g, 3 * rep), (0, 2, 3, 1))
    gates = jnp.pad(gates, ((0, 0), (0, 0), (0, 16 - 3 * rep), (0, 0)))
    values = lambda kind: pl.BlockSpec((None, d, t), lambda bi, gi, i: (bi, kind * g + gi, 0))
    rows = lambda kind: pl.BlockSpec((None, None, None, t, d), lambda bi, gi, i: (bi, kind, gi, 0, 0))
    out = pl.pallas_call(
        functools.partial(_nsa_prompt_t_kernel, tq=tq, n_cmp=n_cmp),
        grid=(b, g, t // tq),
        in_specs=[pl.BlockSpec((None, rep * d, tq), lambda bi, gi, i: (bi, gi, i)),
                  pl.BlockSpec((None, None, 16, tq), lambda bi, gi, i: (bi, gi, 0, i)),
                  pl.BlockSpec((None, None, None, nr, d), lambda bi, gi, i: (bi, 0, gi, 0, 0)),
                  pl.BlockSpec((None, None, d, nr), lambda bi, gi, i: (bi, gi, 0, 0)),
                  pl.BlockSpec((n_blk, nr), lambda bi, gi, i: (0, 0)),
                  rows(0), values(3), rows(1), values(5)],
        out_specs=pl.BlockSpec((None, rep * d, tq), lambda bi, gi, i: (bi, gi, i)),
        out_shape=jax.ShapeDtypeStruct((b, g * rep * d, t), BF16),
        scratch_shapes=[pltpu.VMEM((2, tq, rep * tq), F32), pltpu.VMEM((2, tq, tq), F32),
                        pltpu.VMEM((2, tq, rep * tq), BF16)],
        compiler_params=_params("parallel", "parallel", "arbitrary"),
        name="nsa_prompt",
    )(nsa_qt, gates, kcvc, vct, overlap_t, keys, nsa_kvt, keys, nsa_kvt)
    return jnp.swapaxes(out, 1, 2)


def _ssd_step_pre_kernel(x_ref, buf_ref, cw_ref, cb_ref, dt_ref, dtb_ref, alog_ref, exp_ref,
                         xs_ref, xdt_ref, bm_ref, cm_ref, dec_ref):
    conv = cb_ref[...] + cw_ref[SSD_CONV - 1:SSD_CONV, :] * x_ref[...]
    for j in range(SSD_CONV - 1):
        conv = conv + cw_ref[j:j + 1, :] * buf_ref[j]
    u = conv * _sigmoid(conv)
    xs = u[:, :SSD_INNER]
    dt = _softplus(dt_ref[...] + dtb_ref[...])
    xs_ref[...] = xs
    xdt_ref[...] = xs * jnp.dot(dt, exp_ref[...], precision=HI, preferred_element_type=F32)
    bm_ref[...] = u[:, SSD_INNER:SSD_INNER + SSD_GN]
    cm_ref[...] = u[:, SSD_INNER + SSD_GN:]
    dec_ref[...] = jnp.exp(dt * (-jnp.exp(alog_ref[...])))


def _ssd_step_state_kernel(h0_ref, xdt_ref, dec_ref, bm_ref, cm_ref, h_ref, y_ref):
    r = SSD_HEADS // SSD_GROUPS
    for b in range(h0_ref.shape[0]):
        for hd in range(SSD_HEADS):
            g = hd // r
            hn = dec_ref[b, hd] * h0_ref[b, hd] + xdt_ref[b, hd] * bm_ref[b, g]
            h_ref[b, hd] = hn
            y_ref[b, hd] = jnp.sum(hn * cm_ref[b, g], axis=-1, keepdims=True)


def _ssd_step_post_kernel(y_ref, xs_ref, z_ref, dfull_ref, nw_ref, o_ref):
    zz = z_ref[...]
    y = (y_ref[...] + dfull_ref[...] * xs_ref[...]) * (zz * _sigmoid(zz))
    o_ref[...] = _rms(y, nw_ref[...]).astype(o_ref.dtype)


def ssd_step(xbc, z, dt_raw, h0, conv_buf, conv_w, conv_b, dt_bias, a_log, d_skip, norm_w):
    b = xbc.shape[0]
    hds, p, n = SSD_HEADS, SSD_HEAD_DIM, SSD_STATE
    expand = (jnp.arange(SSD_INNER)[None, :] // p == jnp.arange(hds)[:, None]).astype(F32)
    d_full = jnp.repeat(d_skip, p).reshape(1, SSD_INNER)
    sds = lambda shape: jax.ShapeDtypeStruct(shape, F32)
    xs, xdt, bm, cm, dec = pl.pallas_call(
        _ssd_step_pre_kernel,
        out_shape=[sds((b, SSD_INNER)), sds((b, SSD_INNER)), sds((b, SSD_GN)), sds((b, SSD_GN)), sds((b, hds))],
        name="ssd_step_pre",
    )(xbc, jnp.swapaxes(conv_buf, 0, 1), conv_w, conv_b.reshape(1, -1), dt_raw, dt_bias.reshape(1, hds),
      a_log.reshape(1, hds), expand)
    n_seq = 1
    per_b = lambda *dims: pl.BlockSpec((n_seq,) + dims, lambda bi: (bi,) + (0,) * len(dims))
    h_new, y_col = pl.pallas_call(
        _ssd_step_state_kernel,
        grid=(b // n_seq,),
        in_specs=[per_b(hds, p, n), per_b(hds, p, 1), per_b(hds, 1, 1), per_b(SSD_GROUPS, 1, n),
                  per_b(SSD_GROUPS, 1, n)],
        out_specs=[per_b(hds, p, n), per_b(hds, p, 1)],
        out_shape=[sds((b, hds, p, n)), sds((b, hds, p, 1))],
        compiler_params=_params("parallel"),
        name="ssd_step_state",
    )(h0, xdt.reshape(b, hds, p, 1), dec.reshape(b, hds, 1, 1), bm.reshape(b, SSD_GROUPS, 1, n),
      cm.reshape(b, SSD_GROUPS, 1, n))
    y = pl.pallas_call(
        _ssd_step_post_kernel,
        out_shape=jax.ShapeDtypeStruct((b, SSD_INNER), BF16),
        name="ssd_step_post",
    )(y_col.reshape(b, SSD_INNER), xs, z, d_full, norm_w.reshape(1, -1))
    return y, h_new


def _sb_decode_kernel(pt_ref, q_ref, *refs):
    page_refs, (o_ref, carry_ref, acc_ref) = refs[:PAGES_PER_STEP], refs[PAGES_PER_STEP:]
    p = pl.program_id(1)
    tk = page_refs[0].shape[-1]
    hds = SB_HEADS

    @pl.when(p == 0)
    def _():
        carry_ref[...] = jnp.zeros(carry_ref.shape, F32)
        acc_ref[...] = jnp.zeros(acc_ref.shape, F32)

    scale = HEAD_DIM ** -0.5
    qs = [q_ref[h] * scale for h in range(hds)]
    z = jnp.concatenate([jnp.sum(ref[0, h] * qs[h], axis=0, keepdims=True)
                         for ref in page_refs for h in range(hds)], axis=0)
    row = lax.broadcasted_iota(jnp.int32, (tk, tk), 0)
    col = lax.broadcasted_iota(jnp.int32, (tk, tk), 1)
    sp = _softplus(z)
    later = _split_dot(-sp, (row > col).astype(BF16))
    total = later[:, 0:1] - sp[:, 0:1]
    carry = carry_ref[...]
    for k, ref in enumerate(page_refs):
        rows = slice(k * hds, (k + 1) * hds)
        a = jnp.exp(z[rows] - sp[rows] + later[rows] + carry)
        carry = carry + total[rows]
        for h in range(hds):
            acc_ref[h] += ref[1, h] * a[h:h + 1, :]
    carry_ref[...] = carry

    @pl.when(p == pl.num_programs(1) - 1)
    def _():
        for h in range(hds):
            o_ref[h] = jnp.sum(acc_ref[h], axis=-1, keepdims=True)


def _page_specs(block, layer, n_pages, kind_block, descending):
    def spec(k):
        def index(bi, p, pt):
            pos = p * PAGES_PER_STEP + k
            pos = n_pages - 1 - pos if descending else pos
            return (layer, pt[bi, pos], kind_block) + (0,) * (len(block) - 3)
        return pl.BlockSpec(block, index)
    return [spec(k) for k in range(PAGES_PER_STEP)]


def sb_decode(q, pool_t, layer, page_table):
    b = q.shape[0]
    h, d = SB_HEADS, HEAD_DIM
    n_pages = page_table.shape[1]
    page = pool_t.shape[-1]
    assert n_pages % PAGES_PER_STEP == 0
    out = pl.pallas_call(
        _sb_decode_kernel,
        grid_spec=pltpu.PrefetchScalarGridSpec(
            num_scalar_prefetch=1,
            grid=(b, n_pages // PAGES_PER_STEP),
            in_specs=[pl.BlockSpec((None, h, d, 1), lambda bi, p, pt: (bi, 0, 0, 0))]
            + _page_specs((None, None, 2, h, d, page), layer, n_pages, 0, descending=True),
            out_specs=pl.BlockSpec((None, h, d, 1), lambda bi, p, pt: (bi, 0, 0, 0)),
            scratch_shapes=[pltpu.VMEM((h, 1), F32), pltpu.VMEM((h, d, page), F32)]),
        out_shape=jax.ShapeDtypeStruct((b, h, d, 1), F32),
        compiler_params=_params("parallel", "arbitrary"),
        name="sb_decode",
    )(page_table, q.reshape(b, h, d, 1), *([pool_t] * PAGES_PER_STEP))
    return out.reshape(b, h * d)


def _nsa_gather_compress_kernel(pt_ref, *refs):
    page_refs = refs[:PAGES_PER_STEP]
    pos_ref, w1_ref, w2_ref, o_ref, x_ref, r_ref, sh_ref = refs[PAGES_PER_STEP:]
    p = pl.program_id(1)
    page = page_refs[0].shape[-1]
    grp, d = NSA_KV_GROUPS, HEAD_DIM
    n_out = page // CMP_STRIDE
    for k, ref in enumerate(page_refs):
        row0 = pl.multiple_of((p * PAGES_PER_STEP + k) * n_out, n_out)
        for kind in range(2):
            x = x_ref.at[2 * k + kind]
            x[...] = ref[kind].reshape(grp * d, page).T
            steps = [x[pl.ds(s, n_out, stride=CMP_STRIDE), :] for s in range(CMP_STRIDE)]
            for g in range(grp):
                r_ref[kind, g, pl.ds(row0, n_out), :] = jnp.concatenate(
                    [st[:, g * d:(g + 1) * d] for st in steps], axis=1)

    @pl.when(p == pl.num_programs(1) - 1)
    def _():
        for kind in range(2):
            for g in range(grp):
                o_ref[kind, g] = _compress_rows(r_ref[kind, g], pos_ref.at[kind], w1_ref.at[kind], w2_ref.at[kind],
                                                sh_ref).astype(o_ref.dtype)


def nsa_gather_compress(pool_t, layer, page_table, cmp_pos, cmp_w1, cmp_w2):
    b, n_pages = page_table.shape
    g, d = NSA_KV_GROUPS, HEAD_DIM
    page = pool_t.shape[-1]
    assert page % CMP_STRIDE == 0 and n_pages % PAGES_PER_STEP == 0 and CMP_BLOCK == 2 * CMP_STRIDE
    nr = n_pages * (page // CMP_STRIDE)
    w = CMP_STRIDE * d
    whole = lambda shape: pl.BlockSpec(shape, lambda bi, p, pt: (0,) * len(shape))
    return pl.pallas_call(
        _nsa_gather_compress_kernel,
        grid_spec=pltpu.PrefetchScalarGridSpec(
            num_scalar_prefetch=1,
            grid=(b, n_pages // PAGES_PER_STEP),
            in_specs=_page_specs((None, None, 2, g, d, page), layer, n_pages, 0, descending=False)
            + [whole((2, 2, w)), whole((2, 2 * w, CMP_HIDDEN)), whole((2, CMP_HIDDEN, d))],
            out_specs=pl.BlockSpec((None, 2, g, nr, d), lambda bi, p, pt: (bi, 0, 0, 0, 0)),
            scratch_shapes=[pltpu.VMEM((2 * PAGES_PER_STEP, page, g * d), F32), pltpu.VMEM((2, g, nr, w), F32),
                            pltpu.VMEM((nr + 8, CMP_HIDDEN), F32)]),
        out_shape=jax.ShapeDtypeStruct((b, 2, g, nr, d), BF16),
        compiler_params=_params("parallel", "arbitrary"),
        name="nsa_gather_compress",
    )(page_table, *([pool_t] * PAGES_PER_STEP), cmp_pos.reshape(2, 2, w), cmp_w1.astype(BF16), cmp_w2.astype(BF16))


def _nsa_decode_select_kernel(q_ref, kc_ref, vc_ref, ov_ref, oc_ref, sel_ref, *, n_cmp, q_pos):
    rep, d, grp = NSA_REP, HEAD_DIM, NSA_KV_GROUPS
    n_seq = q_ref.shape[0]
    n_rows = kc_ref.shape[2]
    n_blk = ov_ref.shape[1]
    cidx = lax.broadcasted_iota(jnp.int32, (1, n_rows), 1)
    vis = (cidx * CMP_STRIDE + (CMP_BLOCK - 1) <= q_pos) & (cidx < n_cmp)
    imp = []
    for b in range(n_seq):
        q = (q_ref[b] * (d ** -0.5)).astype(BF16)
        o_c = []
        for g in range(grp):
            s = jnp.where(vis, _dot_nt(q[g * rep:(g + 1) * rep], kc_ref[b, g]), NEG)
            e = jnp.where(vis, jnp.exp(s - jnp.max(s, axis=-1, keepdims=True)), 0.0)
            p = e / jnp.maximum(jnp.sum(e, axis=-1, keepdims=True), 1e-30)
            o_c.append(jnp.dot(p.astype(BF16), vc_ref[b, g], preferred_element_type=F32))
            imp.append(_split_dot(jnp.sum(p, axis=0, keepdims=True), ov_ref[...]))
        oc_ref[b] = jnp.concatenate(o_c, axis=0)
    imp = jnp.concatenate(imp, axis=0)
    blk_i = lax.broadcasted_iota(jnp.int32, (1, n_blk), 1)
    forced = (blk_i == 0) | (blk_i > n_blk - SEL_LOCAL)
    sel = _top_blocks(jnp.where(forced, jnp.inf, imp), blk_i.astype(F32), min(SEL_TOP - 1, n_blk))
    pad = jnp.zeros((sel_ref.shape[1] - grp, n_blk), F32)
    for b in range(n_seq):
        sel_ref[b] = jnp.concatenate([sel[b * grp:(b + 1) * grp], pad], axis=0)


def _nsa_decode_attend_kernel(pt_ref, q_ref, sel_ref, *refs, win_skip):
    page_refs = refs[:PAGES_PER_STEP]
    new_ref, win_ref, oc_ref, gt_ref, o_ref, m_ref, acc_ref = refs[PAGES_PER_STEP:]
    p = pl.program_id(1)
    n_steps = pl.num_programs(1)
    rep, d, grp = NSA_REP, HEAD_DIM, NSA_KV_GROUPS
    tk = page_refs[0].shape[-1]
    lanes = PAGES_PER_STEP * tk
    scale = d ** -0.5
    qf = q_ref[...] * scale
    q = qf.astype(BF16)
    new = new_ref[...]
    new_row = lambda kind, g: new[kind * grp + g:kind * grp + g + 1, :]
    per_head = lambda f: jnp.concatenate([f(g) for g in range(grp)], axis=0)

    @pl.when(p == 0)
    def _():
        m_ref[...] = per_head(lambda g: jnp.sum(qf[g * rep:(g + 1) * rep] * new_row(2, g), axis=-1, keepdims=True))
        acc_ref[...] = per_head(lambda g: jnp.concatenate(
            [jnp.broadcast_to(new_row(3, g), (rep, d)), jnp.ones((rep, d), F32)], axis=1))

    n_blk = sel_ref.shape[1]
    lane = lax.broadcasted_iota(jnp.int32, (n_blk, lanes), 1)
    page_pos = (n_steps - p) * PAGES_PER_STEP - 1 - lax.shift_right_logical(lane, int(math.log2(tk)))
    blk_of_lane = page_pos * (tk // SEL_BLOCK) + lax.shift_right_logical(lane & (tk - 1), int(math.log2(SEL_BLOCK)))
    expand = (lax.broadcasted_iota(jnp.int32, (n_blk, lanes), 0) == blk_of_lane).astype(BF16)
    chosen = jnp.dot(sel_ref[...].astype(BF16), expand, preferred_element_type=F32)

    def scores(g):
        qg = q[g * rep:(g + 1) * rep]
        sg = jnp.concatenate([jnp.dot(qg, ref[0, g].astype(BF16), preferred_element_type=F32) for ref in page_refs],
                             axis=1)
        return jnp.where(chosen[g:g + 1, :] > 0.5, sg, NEG)

    s = per_head(scores)
    m_old = m_ref[...]
    m_new = jnp.maximum(m_old, jnp.max(s, axis=-1, keepdims=True))
    pr = jnp.exp(s - m_new).astype(BF16)

    def weighted_values(g):
        out = jnp.zeros((rep, 2 * d), F32)
        for k, ref in enumerate(page_refs):
            vt_ext = jnp.concatenate([ref[1, g].astype(BF16), jnp.ones((d, tk), BF16)], axis=0)
            out = out + _dot_nt(pr[g * rep:(g + 1) * rep, k * tk:(k + 1) * tk], vt_ext)
        return out

    acc_ref[...] = acc_ref[...] * jnp.exp(m_old - m_new) + per_head(weighted_values)
    m_ref[...] = m_new

    @pl.when(p == n_steps - 1)
    def _():
        acc = acc_ref[...]
        o_s = acc[:, :d] / acc[:, d:]
        wlen = win_ref.shape[-1]
        vis = lax.broadcasted_iota(jnp.int32, (1, wlen), 1) >= win_skip

        def window(g):
            qg = q[g * rep:(g + 1) * rep]
            s_w = jnp.where(vis, jnp.dot(qg, win_ref[0, g].astype(BF16), preferred_element_type=F32), NEG)
            s_n = jnp.sum(qf[g * rep:(g + 1) * rep] * new_row(4, g), axis=-1, keepdims=True)
            mx = jnp.maximum(jnp.max(s_w, axis=-1, keepdims=True), s_n)
            e_w = jnp.where(vis, jnp.exp(s_w - mx), 0.0)
            e_n = jnp.exp(s_n - mx)
            num = _dot_nt(e_w.astype(BF16), win_ref[1, g].astype(BF16)) + e_n * new_row(5, g)
            return num / (jnp.sum(e_w, axis=-1, keepdims=True) + e_n)

        o_w = per_head(window)
        gate = _sigmoid(gt_ref[...])
        o_ref[...] = gate[:, 0:1] * oc_ref[...] + gate[:, 1:2] * o_s + gate[:, 2:3] * o_w


def nsa_decode(nsa_q, nsa_g, nsa_kv_new, pool_t, win_t, layer, page_table, cmp_pos, cmp_w1, cmp_w2):
    b = nsa_q.shape[0]
    g, d, rep, hds = NSA_KV_GROUPS, HEAD_DIM, NSA_REP, NSA_HEADS
    n_pages = page_table.shape[1]
    page = pool_t.shape[-1]
    past = n_pages * page
    wlen = win_t.shape[-1]
    assert past % SEL_BLOCK == 0 and past % CMP_STRIDE == 0 and page % SEL_BLOCK == 0 and wlen <= past
    n_cmp = (past + 1 - CMP_BLOCK) // CMP_STRIDE + 1
    nr = past // CMP_STRIDE
    n_blk = past // SEL_BLOCK
    kcvc = nsa_gather_compress(pool_t, layer, page_table, cmp_pos, cmp_w1, cmp_w2)
    overlap = _overlap_matrix(nr, n_cmp, n_blk)
    q3 = nsa_q.reshape(b, hds, d)
    n_seq = _pick(b, (SEQS_PER_STEP, 1))
    o_c, sel = pl.pallas_call(
        functools.partial(_nsa_decode_select_kernel, n_cmp=n_cmp, q_pos=past),
        grid=(b // n_seq,),
        in_specs=[pl.BlockSpec((n_seq, hds, d), lambda bi: (bi, 0, 0)),
                  pl.BlockSpec((n_seq, None, g, nr, d), lambda bi: (bi, 0, 0, 0, 0)),
                  pl.BlockSpec((n_seq, None, g, nr, d), lambda bi: (bi, 1, 0, 0, 0)),
                  pl.BlockSpec((nr, n_blk), lambda bi: (0, 0))],
        out_specs=[pl.BlockSpec((n_seq, hds, d), lambda bi: (bi, 0, 0)),
                   pl.BlockSpec((n_seq, 8, n_blk), lambda bi: (bi, 0, 0))],
        out_shape=[jax.ShapeDtypeStruct((b, hds, d), F32), jax.ShapeDtypeStruct((b, 8, n_blk), F32)],
        compiler_params=_params("parallel"),
        name="nsa_decode_select",
    )(q3, kcvc, kcvc, overlap)
    gates = jnp.pad(nsa_g.reshape(b, hds, 3), ((0, 0), (0, 0), (0, V7X_LANES - 3)))
    fixed = lambda *dims: pl.BlockSpec((None,) + dims, lambda bi, p, pt: (bi,) + (0,) * len(dims))
    out = pl.pallas_call(
        functools.partial(_nsa_decode_attend_kernel, win_skip=wlen - WINDOW + 1),
        grid_spec=pltpu.PrefetchScalarGridSpec(
            num_scalar_prefetch=1,
            grid=(b, n_pages // PAGES_PER_STEP),
            in_specs=[fixed(hds, d), fixed(8, n_blk)]
            + _page_specs((None, None, 2, g, d, page), layer, n_pages, 1, descending=True)
            + [fixed(6 * g, d),
               pl.BlockSpec((None, None, 2, g, d, wlen), lambda bi, p, pt: (layer, bi, 0, 0, 0, 0)),
               fixed(hds, d), fixed(hds, V7X_LANES)],
            out_specs=fixed(hds, d),
            scratch_shapes=[pltpu.VMEM((hds, 1), F32), pltpu.VMEM((hds, 2 * d), F32)]),
        out_shape=jax.ShapeDtypeStruct((b, hds, d), F32),
        compiler_params=_params("parallel", "arbitrary"),
        name="nsa_decode_attend",
    )(page_table, q3, sel, *([pool_t] * PAGES_PER_STEP), nsa_kv_new.reshape(b, 6 * g, d), win_t, o_c, gates)
    return out.reshape(b, hds * d)


def _col_offsets():
    offs, s = [], 0
    for n in IN_SPLITS:
        offs.append(s)
        s += n
    return offs


def _layer_weights(l, p):
    o = _col_offsets()
    w_in = p['w_in'][l]
    cols = lambda a, n: w_in[:, a:a + n]
    small = jnp.concatenate([cols(o[2], SSD_HEADS), cols(o[6], 3 * NSA_HEADS)], axis=1)
    small = jnp.pad(small, ((0, 0), (0, V7X_LANES - small.shape[1])))
    bf = lambda a: a.astype(BF16)
    return {
        'w_z': bf(cols(o[0], SSD_INNER)), 'w_xbc': bf(cols(o[1], SSD_CONV_DIM)), 'w_small': bf(small),
        'w_sbq': bf(cols(o[3], SB_WIDTH)), 'w_sbkv': bf(cols(o[3] + SB_WIDTH, 2 * SB_WIDTH)),
        'w_nq': bf(cols(o[4], NSA_WIDTH)), 'w_nkv': bf(cols(o[5], 6 * NSA_KV_WIDTH)),
        'w_brg': bf(cols(o[7], N_BRANCH * D_MODEL)),
        'mix_pre': p['norm_mix_pre'][l], 'mix_post': p['norm_mix_post'][l],
        'ffn_pre': p['norm_ffn_pre'][l], 'ffn_post': p['norm_ffn_post'][l],
        'conv_w': p['ssd_conv_w'][l], 'conv_b': p['ssd_conv_b'][l], 'dt_bias': p['ssd_dt_bias'][l],
        'a_log': p['ssd_a_log'][l], 'd_skip': p['ssd_d'][l], 'ssd_norm': p['ssd_norm'][l],
        'w_ssd_out': bf(p['w_ssd_out'][l]), 'w_sb_out': bf(p['w_sb_out'][l]), 'w_nsa_out': bf(p['w_nsa_out'][l]),
        'w_o': bf(p['w_o'][l]), 'w_ffn_gate': bf(p['w_ffn_gate'][l]), 'w_ffn_up': bf(p['w_ffn_up'][l]),
        'w_ffn_down': bf(p['w_ffn_down'][l]),
        'cmp_pos': p['nsa_cmp_pos'][l], 'cmp_w1': p['nsa_cmp_w1'][l], 'cmp_w2': p['nsa_cmp_w2'][l],
    }


def _trunk_tail(x, ssd_y, sb_o, nsa_o, br_g, lw):
    b, t, d = x.shape
    m = b * t
    x1 = merge_branches(x.reshape(m, d), ssd_y.reshape(m, -1), sb_o.reshape(m, -1), nsa_o.reshape(m, -1),
                        br_g.reshape(m, -1), lw['w_ssd_out'], lw['w_sb_out'], lw['w_nsa_out'], lw['w_o'],
                        lw['mix_post'])
    x2 = ffn(x1, lw['ffn_pre'], lw['ffn_post'], lw['w_ffn_gate'], lw['w_ffn_up'], lw['w_ffn_down'])
    return x2.reshape(b, t, d)


def _layer_prompt(x, lw):
    b, t, _ = x.shape
    g = lw['mix_pre']
    z = norm_matmul(x, g, lw['w_z'])
    xbc = norm_matmul(x, g, lw['w_xbc'])
    small = norm_matmul(x, g, lw['w_small'])
    br_g = norm_matmul(x, g, lw['w_brg'])
    sb_q = norm_matmul(x, g, lw['w_sbq'], out_dtype=BF16)
    nsa_qt = norm_matmul(x, g, lw['w_nq'].T, transposed=True)
    sb_kvt = norm_matmul(x, g, lw['w_sbkv'].T, transposed=True)
    nsa_kvt = norm_matmul(x, g, lw['w_nkv'].T, transposed=True)
    dt_raw = small[..., :SSD_HEADS]
    nsa_g = small[..., SSD_HEADS:SSD_HEADS + 3 * NSA_HEADS]

    h0 = jnp.zeros((b, SSD_HEADS, SSD_HEAD_DIM, SSD_STATE), F32)
    conv0 = jnp.zeros((b, SSD_CONV - 1, SSD_CONV_DIM), F32)
    ssd_y, h_new = ssd_prompt(xbc, z, dt_raw, h0, conv0, lw['conv_w'], lw['conv_b'], lw['dt_bias'], lw['a_log'],
                              lw['d_skip'], lw['ssd_norm'])
    conv_new = xbc[:, t - (SSD_CONV - 1):, :]
    sb_o = sb_prompt(sb_q, sb_kvt)

    nsa_o = nsa_prompt_t(nsa_qt, nsa_g, nsa_kvt, lw['cmp_pos'], lw['cmp_w1'], lw['cmp_w2'])

    y = _trunk_tail(x, ssd_y, sb_o, nsa_o, br_g, lw)
    sb_kv = jnp.moveaxis(sb_kvt.reshape(b, 2, SB_HEADS, HEAD_DIM, t), 4, 1)
    nsa_all = jnp.moveaxis(nsa_kvt.reshape(b, 6, NSA_KV_GROUPS, HEAD_DIM, t), 4, 1)
    keep = min(WINDOW, t)
    return y, sb_kv, nsa_all[:, :, 0:4], nsa_all[:, t - keep:, 4:6], h_new, conv_new


def _layer_sample(x, lw, layer, sb_pool_t, nsa_pool_t, win_t, h0, conv_buf, page_table):
    bsz, t = x.shape[:2]
    assert t == 1
    past = page_table.shape[1] * sb_pool_t.shape[-1]
    xr = x.reshape(1, bsz, D_MODEL)
    g = lw['mix_pre']
    pr = lambda w: norm_matmul(xr, g, w)[0]
    z, xbc, small, br_g = pr(lw['w_z']), pr(lw['w_xbc']), pr(lw['w_small']), pr(lw['w_brg'])
    sb_q, sb_kv, nsa_q, nsa_kv = pr(lw['w_sbq']), pr(lw['w_sbkv']), pr(lw['w_nq']), pr(lw['w_nkv'])
    dt_raw = small[:, :SSD_HEADS]
    nsa_g = small[:, SSD_HEADS:SSD_HEADS + 3 * NSA_HEADS]
    ssd_y, h_new = ssd_step(xbc, z, dt_raw, h0, conv_buf, lw['conv_w'], lw['conv_b'], lw['dt_bias'], lw['a_log'],
                            lw['d_skip'], lw['ssd_norm'])
    conv_new = jnp.concatenate([conv_buf[:, 1:], xbc[:, None, :]], axis=1)
    sb_o = sb_decode(sb_q, sb_pool_t, layer, page_table)
    nsa_o = nsa_decode(nsa_q, nsa_g, nsa_kv, nsa_pool_t, win_t, layer, page_table, lw['cmp_pos'], lw['cmp_w1'],
                       lw['cmp_w2'])
    y = _trunk_tail(x, ssd_y[:, None], sb_o.astype(BF16)[:, None], nsa_o.astype(BF16)[:, None], br_g[:, None], lw)
    kv_new = sb_kv.reshape(bsz, 1, 2, SB_HEADS, HEAD_DIM)
    nkv = nsa_kv.reshape(bsz, 1, 6, NSA_KV_GROUPS, HEAD_DIM)
    keep = min(WINDOW, past + 1)
    win_all_t = jnp.concatenate([win_t[layer], nkv[:, 0, 4:6][..., None]], axis=-1)
    win_new = jnp.moveaxis(win_all_t[..., win_all_t.shape[-1] - keep:], 4, 1)
    return y, kv_new, nkv[:, :, 0:4], win_new, h_new, conv_new


def kernel(x_prompt, x_sample, cache_sb_kv, cache_nsa_kv, cache_nsa_win, state_ssd, state_conv, page_table,
           norm_mix_pre, norm_mix_post, norm_ffn_pre, norm_ffn_post, w_in, ssd_conv_w, ssd_conv_b, ssd_dt_bias,
           ssd_a_log, ssd_d, ssd_norm, w_ssd_out, w_sb_out, nsa_cmp_pos, nsa_cmp_w1, nsa_cmp_w2, w_nsa_out, w_o,
           w_ffn_gate, w_ffn_up, w_ffn_down):
    p = dict(norm_mix_pre=norm_mix_pre, norm_mix_post=norm_mix_post, norm_ffn_pre=norm_ffn_pre,
             norm_ffn_post=norm_ffn_post, w_in=w_in, ssd_conv_w=ssd_conv_w, ssd_conv_b=ssd_conv_b,
             ssd_dt_bias=ssd_dt_bias, ssd_a_log=ssd_a_log, ssd_d=ssd_d, ssd_norm=ssd_norm, w_ssd_out=w_ssd_out,
             w_sb_out=w_sb_out, nsa_cmp_pos=nsa_cmp_pos, nsa_cmp_w1=nsa_cmp_w1, nsa_cmp_w2=nsa_cmp_w2,
             w_nsa_out=w_nsa_out, w_o=w_o, w_ffn_gate=w_ffn_gate, w_ffn_up=w_ffn_up, w_ffn_down=w_ffn_down)
    yp, ys = x_prompt, x_sample
    outs_p, outs_s = [], []
    time_minor = lambda a: jnp.transpose(a, (0, 1, 3, 4, 5, 2))
    sb_pool_t, nsa_pool_t, win_t = time_minor(cache_sb_kv), time_minor(cache_nsa_kv), time_minor(cache_nsa_win)
    for l in range(w_in.shape[0]):
        lw = _layer_weights(l, p)
        res = _layer_prompt(yp, lw)
        yp = res[0]
        outs_p.append(res[1:])
        res = _layer_sample(ys, lw, l, sb_pool_t, nsa_pool_t, win_t, state_ssd[l], state_conv[l], page_table)
        ys = res[0]
        outs_s.append(res[1:])
    st = lambda outs, i: jnp.stack([o[i] for o in outs])
    return (yp, ys, st(outs_p, 0), st(outs_s, 0), st(outs_p, 1), st(outs_s, 1), st(outs_p, 2), st(outs_s, 2),
            st(outs_p, 3), st(outs_s, 3), st(outs_p, 4), st(outs_s, 4))
```

```python
import functools
import math

import jax
import jax.numpy as jnp
from jax import lax
from jax.experimental import pallas as pl
from jax.experimental.pallas import tpu as pltpu

D_MODEL = 1024
HEAD_DIM = 64
SSD_INNER = D_MODEL
SSD_HEAD_DIM = 64
SSD_HEADS = SSD_INNER // SSD_HEAD_DIM
SSD_GROUPS = 2
SSD_STATE = 128
SSD_CONV = 4
SSD_GN = SSD_GROUPS * SSD_STATE
SSD_CONV_DIM = SSD_INNER + 2 * SSD_GN
SSD_CHUNK = 128
SB_HEADS = 8
SB_WIDTH = SB_HEADS * HEAD_DIM
NSA_HEADS = 8
NSA_KV_GROUPS = 2
NSA_REP = NSA_HEADS // NSA_KV_GROUPS
NSA_WIDTH = NSA_HEADS * HEAD_DIM
NSA_KV_WIDTH = NSA_KV_GROUPS * HEAD_DIM
CMP_BLOCK = 32
CMP_STRIDE = 16
CMP_HIDDEN = 128
SEL_BLOCK = 64
SEL_TOP = 16
SEL_LOCAL = 2
WINDOW = 512
Q_BLOCK = 128
N_BRANCH = 3
FFN_HIDDEN = ((8 * D_MODEL + 3 * 256 - 1) // (3 * 256)) * 256
RMS_EPS = 1e-6
IN_SPLITS = (SSD_INNER, SSD_CONV_DIM, SSD_HEADS, 3 * SB_WIDTH, NSA_WIDTH, 6 * NSA_KV_WIDTH, 3 * NSA_HEADS,
             N_BRANCH * D_MODEL)

V7X_LANES = 128
V7X_VMEM_LIMIT = 56 * 1024 * 1024
PAGES_PER_STEP = 16
SEQS_PER_STEP = 8
NEG = -1e30
BF16 = jnp.bfloat16
F32 = jnp.float32
HI = lax.Precision.HIGHEST


def _params(*sem):
    return pltpu.CompilerParams(dimension_semantics=sem, vmem_limit_bytes=V7X_VMEM_LIMIT)


def _pick(n, cands):
    for c in cands:
        if n % c == 0:
            return c
    return n


def _rms(x, w):
    return x * lax.rsqrt(jnp.mean(x * x, axis=-1, keepdims=True) + RMS_EPS) * w


def _softplus(x):
    return jnp.maximum(x, 0.0) + jnp.log1p(jnp.exp(-jnp.abs(x)))


def _sigmoid(x):
    return 1.0 / (1.0 + jnp.exp(-x))


def _dot_nt(a, b):
    return lax.dot_general(a, b, (((1,), (1,)), ((), ())), preferred_element_type=F32)


def _dot_tn(a, b):
    return lax.dot_general(a, b, (((0,), (0,)), ((), ())), preferred_element_type=F32)


def _norm_mm_kernel(x_ref, g_ref, w_ref, *rest, transposed):
    o_ref, h_ref = rest[-2:]

    @pl.when(pl.program_id(2) == 0)
    def _():
        h_ref[...] = _rms(x_ref[...], g_ref[...]).astype(BF16)

    if transposed:
        o_ref[...] = _dot_nt(w_ref[...], h_ref[...]).astype(o_ref.dtype)
    else:
        o_ref[...] = jnp.dot(h_ref[...], w_ref[...], preferred_element_type=F32).astype(o_ref.dtype)


def norm_matmul(x, gain, w, out_dtype=F32, transposed=False, stack=None):
    b, t, k = x.shape
    n = w.shape[0] if transposed else w.shape[1]
    tm = _pick(t, (1024, 512, 256, 128))
    tn = _pick(n, (1024, 768, 512, 256, 128))
    operands, extra_specs, aliases = [x, gain.reshape(1, k), w], [], {}
    if stack is not None:
        assert transposed
        buf, layer, n_layers = stack
        w_spec = pl.BlockSpec((tn, k), lambda bi, i, j: (j, 0))
        o_spec = pl.BlockSpec((None, None, tn, tm), lambda bi, i, j: (layer, bi, j, i))
        o_shape = (n_layers, b, n, t)
        if buf is not None:
            operands.append(buf)
            extra_specs.append(pl.BlockSpec(memory_space=pl.ANY))
            aliases = {3: 0}
    elif transposed:
        w_spec = pl.BlockSpec((tn, k), lambda bi, i, j: (j, 0))
        o_spec = pl.BlockSpec((None, tn, tm), lambda bi, i, j: (bi, j, i))
        o_shape = (b, n, t)
    else:
        w_spec = pl.BlockSpec((k, tn), lambda bi, i, j: (0, j))
        o_spec = pl.BlockSpec((None, tm, tn), lambda bi, i, j: (bi, i, j))
        o_shape = (b, t, n)
    return pl.pallas_call(
        functools.partial(_norm_mm_kernel, transposed=transposed),
        grid=(b, t // tm, n // tn),
        in_specs=[pl.BlockSpec((None, tm, k), lambda bi, i, j: (bi, i, 0)),
                  pl.BlockSpec((1, k), lambda bi, i, j: (0, 0)),
                  w_spec] + extra_specs,
        out_specs=o_spec,
        out_shape=jax.ShapeDtypeStruct(o_shape, out_dtype),
        scratch_shapes=[pltpu.VMEM((tm, k), BF16)],
        input_output_aliases=aliases,
        compiler_params=_params("parallel", "parallel", "arbitrary"),
        name="norm_matmul_t" if transposed else "norm_matmul",
    )(*operands)


def _ssd_chunk_kernel(xbc_ref, z_ref, dt_ref, dtt_ref, h0_ref, c0_ref, cw_ref, cb_ref, dtb_ref, dtbt_ref,
                      alog_ref, alogt_ref, dfull_ref, nw_ref, exp_ref, y_ref, h_ref, xp_ref):
    q = SSD_CHUNK
    c = pl.program_id(1)

    @pl.when(c == 0)
    def _():
        h_ref[...] = h0_ref[...]
        xp_ref[5:8, :] = c0_ref[...]

    xp_ref[8:8 + q, :] = xbc_ref[...]
    conv = cb_ref[...]
    for j in range(SSD_CONV):
        conv = conv + cw_ref[j:j + 1, :] * xp_ref[5 + j:5 + j + q, :]
    xp_ref[5:8, :] = xp_ref[q + 5:q + 8, :]
    u = conv * _sigmoid(conv)
    xs = u[:, :SSD_INNER]
    bm = u[:, SSD_INNER:SSD_INNER + SSD_GN].astype(BF16)
    cm = u[:, SSD_INNER + SSD_GN:].astype(BF16)

    dt = _softplus(dt_ref[...] + dtb_ref[...])
    dtt = _softplus(dtt_ref[...] + dtbt_ref[...])
    dta = dt * (-jnp.exp(alog_ref[...]))
    dtat = dtt * (-jnp.exp(alogt_ref[...]))
    row = lax.broadcasted_iota(jnp.int32, (q, q), 0)
    col = lax.broadcasted_iota(jnp.int32, (q, q), 1)
    tril = row >= col
    acum = jnp.dot(tril.astype(F32), dta, precision=HI, preferred_element_type=F32)
    acumt = jnp.dot(dtat, (row <= col).astype(F32), precision=HI, preferred_element_type=F32)
    expand = exp_ref[...]
    dt_full = jnp.dot(dt, expand, precision=HI, preferred_element_type=F32)
    ea_full = jnp.dot(jnp.exp(acum), expand, precision=HI, preferred_element_type=F32)
    te_full = jnp.dot(jnp.exp(acum[q - 1:q, :] - acum), expand, precision=HI, preferred_element_type=F32)
    xdt = xs * dt_full
    xdt_b = xdt.astype(BF16)
    xw_b = (xdt * te_full).astype(BF16)

    r = SSD_HEADS // SSD_GROUPS
    gw = r * SSD_HEAD_DIM
    y_diag, y_off = [], []
    for g in range(SSD_GROUPS):
        cm_g = cm[:, g * SSD_STATE:(g + 1) * SSD_STATE]
        bm_g = bm[:, g * SSD_STATE:(g + 1) * SSD_STATE]
        cb = _dot_nt(cm_g, bm_g)
        h_g = h_ref[g * r:(g + 1) * r].reshape(gw, SSD_STATE)
        y_off.append(_dot_nt(cm_g, h_g.astype(BF16)))
        st = _dot_tn(xw_b[:, g * gw:(g + 1) * gw], bm_g)
        for hh in range(r):
            hd = g * r + hh
            seg = acum[:, hd:hd + 1] - acumt[hd:hd + 1, :]
            decay = jnp.exp(jnp.where(tril, seg, -jnp.inf))
            m = (cb * decay).astype(BF16)
            y_diag.append(jnp.dot(m, xdt_b[:, hd * SSD_HEAD_DIM:(hd + 1) * SSD_HEAD_DIM],
                                  preferred_element_type=F32))
            dec = jnp.exp(acumt[hd:hd + 1, q - 1:q])
            h_ref[hd] = dec * h_ref[hd] + st[hh * SSD_HEAD_DIM:(hh + 1) * SSD_HEAD_DIM, :]
    y = (jnp.concatenate(y_diag, axis=1) + jnp.concatenate(y_off, axis=1) * ea_full
         + dfull_ref[...] * xs)
    zz = z_ref[...]
    y = y * (zz * _sigmoid(zz))
    y_ref[...] = _rms(y, nw_ref[...]).astype(y_ref.dtype)


def ssd_prompt(xbc, z, dt_raw, h0, conv0, conv_w, conv_b, dt_bias, a_log, d_skip, norm_w):
    b, t, _ = xbc.shape
    q = SSD_CHUNK
    nc = t // q
    hds = SSD_HEADS
    expand = (jnp.arange(SSD_INNER)[None, :] // SSD_HEAD_DIM == jnp.arange(hds)[:, None]).astype(F32)
    d_full = jnp.repeat(d_skip, SSD_HEAD_DIM).reshape(1, SSD_INNER)
    dtt = jnp.swapaxes(dt_raw, 1, 2)
    full = lambda shape: pl.BlockSpec(shape, lambda bi, ci: (0,) * len(shape))
    y, h = pl.pallas_call(
        _ssd_chunk_kernel,
        grid=(b, nc),
        in_specs=[pl.BlockSpec((None, q, SSD_CONV_DIM), lambda bi, ci: (bi, ci, 0)),
                  pl.BlockSpec((None, q, SSD_INNER), lambda bi, ci: (bi, ci, 0)),
                  pl.BlockSpec((None, q, hds), lambda bi, ci: (bi, ci, 0)),
                  pl.BlockSpec((None, hds, q), lambda bi, ci: (bi, 0, ci)),
                  pl.BlockSpec((None, hds, SSD_HEAD_DIM, SSD_STATE), lambda bi, ci: (bi, 0, 0, 0)),
                  pl.BlockSpec((None, SSD_CONV - 1, SSD_CONV_DIM), lambda bi, ci: (bi, 0, 0)),
                  full((SSD_CONV, SSD_CONV_DIM)), full((1, SSD_CONV_DIM)),
                  full((1, hds)), full((hds, 1)), full((1, hds)), full((hds, 1)),
                  full((1, SSD_INNER)), full((1, SSD_INNER)), full((hds, SSD_INNER))],
        out_specs=[pl.BlockSpec((None, q, SSD_INNER), lambda bi, ci: (bi, ci, 0)),
                   pl.BlockSpec((None, hds, SSD_HEAD_DIM, SSD_STATE), lambda bi, ci: (bi, 0, 0, 0))],
        out_shape=[jax.ShapeDtypeStruct((b, t, SSD_INNER), BF16),
                   jax.ShapeDtypeStruct((b, hds, SSD_HEAD_DIM, SSD_STATE), F32)],
        scratch_shapes=[pltpu.VMEM((q + 8, SSD_CONV_DIM), F32)],
        compiler_params=_params("parallel", "arbitrary"),
        name="ssd_chunk_scan",
    )(xbc, z, dt_raw, dtt, h0, conv0, conv_w, conv_b.reshape(1, -1), dt_bias.reshape(1, hds),
      dt_bias.reshape(hds, 1), a_log.reshape(1, hds), a_log.reshape(hds, 1), d_full, norm_w.reshape(1, -1), expand)
    return y, h


def _sb_stage(z):
    sp = jnp.maximum(z, 0.0) + jnp.log(1.0 + jnp.exp(-jnp.abs(z)))
    hi = sp.astype(BF16)
    return hi, (sp - hi.astype(F32)).astype(BF16)


def _sb_prompt_kernel(q_ref, kt_ref, vt_ref, o_ref, z_ref, hl_ref, a_ref, *, tq, tk):
    i = pl.program_id(2)
    band = tq // tk
    assert band <= 2
    last = (i + 1) * band - 1
    q = (q_ref[...] * (HEAD_DIM ** -0.5)).astype(BF16)
    row = lax.broadcasted_iota(jnp.int32, (tk, tk), 0)
    col = lax.broadcasted_iota(jnp.int32, (tk, tk), 1)
    upper = jnp.where(row >= col, -1.0, 0.0).astype(BF16)
    upper2 = jnp.concatenate([upper, upper], axis=0)
    qpos = i * tq + lax.broadcasted_iota(jnp.int32, (tq, 1), 0)
    kcol = lax.broadcasted_iota(jnp.int32, (1, tk), 1)

    def tile(ref, s):
        off = pl.multiple_of(jnp.clip(last - s, 0, last) * tk, tk)
        return ref[:, pl.ds(off, tk)].astype(BF16)

    def masked_scores(s):
        kpos = jnp.clip(last - s, 0, last) * tk + kcol
        return jnp.where(kpos < qpos, jnp.dot(q, tile(kt_ref, s), preferred_element_type=F32), NEG)

    z0 = masked_scores(0)
    z_ref[0] = z0
    hl_ref[0, :, :tk], hl_ref[0, :, tk:] = _sb_stage(z0)
    z_ref[1] = masked_scores(1)
    a_ref[2] = jnp.zeros((tq, tk), BF16)

    def step(n, c, st):
        carry, acc = st
        nxt, prv = (c + 1) % 3, (c + 2) % 3
        later = jnp.dot(hl_ref[c], upper2, preferred_element_type=F32)
        acc = acc + _dot_nt(a_ref[prv], tile(vt_ref, n - 1))
        z_ref[prv] = jnp.dot(q, tile(kt_ref, n + 2), preferred_element_type=F32)
        hl_ref[nxt, :, :tk], hl_ref[nxt, :, tk:] = _sb_stage(z_ref[nxt])
        dead = jnp.where(n <= last, 0.0, NEG)
        a_ref[c] = jnp.exp(z_ref[c] + later + (carry + dead)).astype(BF16)
        return carry + later[:, 0:1], acc

    def body(trip, st):
        for c in range(3):
            st = step(3 * trip + c, c, st)
        return st

    st = (jnp.zeros((tq, 1), F32), jnp.zeros((tq, HEAD_DIM), F32))
    _, acc = lax.fori_loop(0, lax.div(last + 4, 3), body, st)
    o_ref[...] = acc.astype(o_ref.dtype)


def sb_prompt(q, kvt, layer=0, tq=512, tk=256):
    b, t, _ = q.shape
    h = SB_HEADS
    tq, tk = min(tq, t), min(tk, t)
    assert t % tq == 0 and tq % tk == 0
    qh = jnp.swapaxes(q.reshape(b, t, h, HEAD_DIM), 1, 2)
    out = pl.pallas_call(
        functools.partial(_sb_prompt_kernel, tq=tq, tk=tk),
        grid=(b, h, t // tq),
        in_specs=[pl.BlockSpec((None, None, tq, HEAD_DIM), lambda bi, hi, i: (bi, hi, i, 0)),
                  pl.BlockSpec((None, None, HEAD_DIM, t), lambda bi, hi, i: (layer, bi, hi, 0)),
                  pl.BlockSpec((None, None, HEAD_DIM, t), lambda bi, hi, i: (layer, bi, h + hi, 0))],
        out_specs=pl.BlockSpec((None, None, tq, HEAD_DIM), lambda bi, hi, i: (bi, hi, i, 0)),
        out_shape=jax.ShapeDtypeStruct((b, h, t, HEAD_DIM), BF16),
        scratch_shapes=[pltpu.VMEM((3, tq, tk), F32), pltpu.VMEM((3, tq, 2 * tk), BF16),
                        pltpu.VMEM((3, tq, tk), BF16)],
        compiler_params=_params("parallel", "parallel", "arbitrary"),
        name="sb_prompt",
    )(qh, kvt, kvt)
    return jnp.swapaxes(out, 1, 2).reshape(b, t, h * HEAD_DIM)


def _merge_kernel(x_ref, ssd_ref, sb_ref, nsa_ref, gl_ref, wssd_ref, wsb_ref, wnsa_ref, wo_ref, nw_ref, o_ref):
    d = D_MODEL
    gl = gl_ref[...]
    merged = (_sigmoid(gl[:, :d]) * jnp.dot(ssd_ref[...], wssd_ref[...], preferred_element_type=F32)
              + _sigmoid(gl[:, d:2 * d]) * jnp.dot(sb_ref[...], wsb_ref[...], preferred_element_type=F32)
              + _sigmoid(gl[:, 2 * d:]) * jnp.dot(nsa_ref[...], wnsa_ref[...], preferred_element_type=F32))
    y = jnp.dot(merged.astype(BF16), wo_ref[...], preferred_element_type=F32)
    o_ref[...] = x_ref[...] + _rms(y, nw_ref[...])


def merge_branches(x, ssd_y, sb_o, nsa_o, gate_logits, w_ssd_out, w_sb_out, w_nsa_out, w_o, norm_w):
    m, d = x.shape
    tm = _pick(m, (512, 256, 128, 32))
    rows = lambda n: pl.BlockSpec((tm, n), lambda i: (i, 0))
    full = lambda a: pl.BlockSpec(a.shape, lambda i: (0, 0))
    nw = norm_w.reshape(1, d)
    return pl.pallas_call(
        _merge_kernel,
        grid=(m // tm,),
        in_specs=[rows(d), rows(ssd_y.shape[1]), rows(sb_o.shape[1]), rows(nsa_o.shape[1]), rows(N_BRANCH * d),
                  full(w_ssd_out), full(w_sb_out), full(w_nsa_out), full(w_o), full(nw)],
        out_specs=rows(d),
        out_shape=jax.ShapeDtypeStruct((m, d), F32),
        compiler_params=_params("parallel"),
        name="merge_branches",
    )(x, ssd_y, sb_o, nsa_o, gate_logits, w_ssd_out, w_sb_out, w_nsa_out, w_o, nw)


def _ffn_up_kernel(x_ref, g_ref, wg_ref, wu_ref, o_ref, h_ref):
    @pl.when(pl.program_id(1) == 0)
    def _():
        h_ref[...] = _rms(x_ref[...], g_ref[...]).astype(BF16)

    h = h_ref[...]
    a = jnp.dot(h, wg_ref[...], preferred_element_type=F32)
    u = jnp.dot(h, wu_ref[...], preferred_element_type=F32)
    o_ref[...] = (a * _sigmoid(a) * u).astype(o_ref.dtype)


def _ffn_down_kernel(a_ref, x_ref, wd_ref, nw_ref, o_ref):
    f = jnp.dot(a_ref[...], wd_ref[...], preferred_element_type=F32)
    o_ref[...] = x_ref[...] + _rms(f, nw_ref[...])


def ffn(x, pre_w, post_w, w_gate, w_up, w_down):
    m, d = x.shape
    f = w_gate.shape[1]
    tm = _pick(m, (1024, 512, 256, 128, 32))
    tn = _pick(f, (256, 128))
    act = pl.pallas_call(
        _ffn_up_kernel,
        grid=(m // tm, f // tn),
        in_specs=[pl.BlockSpec((tm, d), lambda i, j: (i, 0)),
                  pl.BlockSpec((1, d), lambda i, j: (0, 0)),
                  pl.BlockSpec((d, tn), lambda i, j: (0, j)),
                  pl.BlockSpec((d, tn), lambda i, j: (0, j))],
        out_specs=pl.BlockSpec((tm, tn), lambda i, j: (i, j)),
        out_shape=jax.ShapeDtypeStruct((m, f), BF16),
        scratch_shapes=[pltpu.VMEM((tm, d), BF16)],
        compiler_params=_params("parallel", "arbitrary"),
        name="ffn_up",
    )(x, pre_w.reshape(1, d), w_gate, w_up)
    tm2 = _pick(m, (512, 256, 128, 32))
    return pl.pallas_call(
        _ffn_down_kernel,
        grid=(m // tm2,),
        in_specs=[pl.BlockSpec((tm2, f), lambda i: (i, 0)),
                  pl.BlockSpec((tm2, d), lambda i: (i, 0)),
                  pl.BlockSpec((f, d), lambda i: (0, 0)),
                  pl.BlockSpec((1, d), lambda i: (0, 0))],
        out_specs=pl.BlockSpec((tm2, d), lambda i: (i, 0)),
        out_shape=jax.ShapeDtypeStruct((m, d), F32),
        compiler_params=_params("parallel"),
        name="ffn_down",
    )(act, x, w_down, post_w.reshape(1, d))


def _compress_rows(r, pos_ref, w1_ref, w2_ref, sh_ref):
    nr = r.shape[0]
    half = CMP_STRIDE * HEAD_DIM
    top = jnp.dot((r + pos_ref[0:1, :]).astype(BF16), w1_ref[:half, :], preferred_element_type=F32)
    bot = jnp.dot((r + pos_ref[1:2, :]).astype(BF16), w1_ref[half:, :], preferred_element_type=F32)
    sh_ref[0:nr, :] = bot
    sh_ref[nr:nr + 8, :] = jnp.zeros((8, CMP_HIDDEN), F32)
    pre = top + sh_ref[1:nr + 1, :]
    hid = pre * _sigmoid(pre)
    return jnp.dot(hid.astype(BF16), w2_ref[...], preferred_element_type=F32)


def _nsa_compress_kernel(r_ref, pos_ref, w1_ref, w2_ref, o_ref, sh_ref):
    o_ref[...] = _compress_rows(r_ref[...], pos_ref, w1_ref, w2_ref, sh_ref).astype(o_ref.dtype)


def nsa_compress(rows16, cmp_pos, cmp_w1, cmp_w2):
    assert CMP_BLOCK == 2 * CMP_STRIDE
    b, _, g, nr, w = rows16.shape
    pos = cmp_pos.reshape(2, 2, w)
    return pl.pallas_call(
        _nsa_compress_kernel,
        grid=(b, 2, g),
        in_specs=[pl.BlockSpec((None, None, None, nr, w), lambda bi, ki, gi: (bi, ki, gi, 0, 0)),
                  pl.BlockSpec((None, 2, w), lambda bi, ki, gi: (ki, 0, 0)),
                  pl.BlockSpec((None, 2 * w, CMP_HIDDEN), lambda bi, ki, gi: (ki, 0, 0)),
                  pl.BlockSpec((None, CMP_HIDDEN, HEAD_DIM), lambda bi, ki, gi: (ki, 0, 0))],
        out_specs=pl.BlockSpec((None, None, None, nr, HEAD_DIM), lambda bi, ki, gi: (bi, ki, gi, 0, 0)),
        out_shape=jax.ShapeDtypeStruct((b, 2, g, nr, HEAD_DIM), BF16),
        scratch_shapes=[pltpu.VMEM((nr + 8, CMP_HIDDEN), F32)],
        compiler_params=_params("parallel", "parallel", "parallel"),
        name="nsa_compress",
    )(rows16, pos, cmp_w1.astype(BF16), cmp_w2.astype(BF16))


def _overlap_matrix(n_cmp_rows, n_cmp, n_blk):
    c = jnp.arange(n_cmp_rows)[:, None]
    n = jnp.arange(n_blk)[None, :]
    c_start, c_end = c * CMP_STRIDE, c * CMP_STRIDE + CMP_BLOCK - 1
    return ((c_start < (n + 1) * SEL_BLOCK) & (c_end >= n * SEL_BLOCK) & (c < n_cmp)).astype(BF16)


def _split_dot(x, w):
    hi = x.astype(BF16)
    lo = (x - hi.astype(F32)).astype(BF16)
    return jnp.dot(hi, w, preferred_element_type=F32) + jnp.dot(lo, w, preferred_element_type=F32)


def _top_blocks(imp, blk, n_top):
    n_blk = imp.shape[1]
    sel = jnp.zeros(imp.shape, F32)
    for _ in range(n_top):
        m = jnp.max(imp, axis=-1, keepdims=True)
        idx = jnp.min(jnp.where(imp == m, blk, float(n_blk)), axis=-1, keepdims=True)
        hit = blk == idx
        sel = jnp.where(hit, 1.0, sel)
        imp = jnp.where(hit, -jnp.inf, imp)
    return sel


def _flash_step_t(qt, k, vt, mask, m, acc):
    s = jnp.where(mask, jnp.dot(k, qt, preferred_element_type=F32), NEG)
    m_new = jnp.maximum(m, jnp.max(s, axis=0, keepdims=True))
    p = jnp.exp(s - m_new).astype(BF16)
    vt_ext = jnp.concatenate([vt, jnp.ones((8, vt.shape[1]), BF16)], axis=0)
    return m_new, acc * jnp.exp(m - m_new) + jnp.dot(vt_ext, p, preferred_element_type=F32)


def _nsa_prompt_t_kernel(qt_ref, gt_ref, kc_ref, vct_ref, ovt_ref, ks_ref, vs_ref, kw_ref, vw_ref, o_ref,
                         s_ref, ch_ref, p_ref, *, tq, n_cmp):
    i = pl.program_id(2)
    rep, d = NSA_REP, HEAD_DIM
    lanes = rep * tq
    sel_shift = int(math.log2(SEL_BLOCK))
    qt_blk = qt_ref[...]
    qt = jnp.concatenate([qt_blk[r * d:(r + 1) * d, :] for r in range(rep)], axis=1)
    qt = (qt * (d ** -0.5)).astype(BF16)
    qpos = i * tq + lax.broadcasted_iota(jnp.int32, (1, tq), 1)
    per_head = lambda a: jnp.concatenate([a] * rep, axis=1)

    n_rows = kc_ref.shape[0]
    cidx = lax.broadcasted_iota(jnp.int32, (n_rows, 1), 0)
    vis_c = per_head(((cidx * CMP_STRIDE + (CMP_BLOCK - 1) <= qpos) & (cidx < n_cmp)).astype(F32)) > 0.5
    s_c = jnp.where(vis_c, jnp.dot(kc_ref[...], qt, preferred_element_type=F32), NEG)
    e_c = jnp.where(vis_c, jnp.exp(s_c - jnp.max(s_c, axis=0, keepdims=True)), 0.0)
    p_c = e_c / jnp.maximum(jnp.sum(e_c, axis=0, keepdims=True), 1e-30)
    o_c = jnp.dot(vct_ref[...], p_c.astype(BF16), preferred_element_type=F32)

    p_sum = p_c[:, 0:tq]
    for r in range(1, rep):
        p_sum = p_sum + p_c[:, r * tq:(r + 1) * tq]
    p_hi = p_sum.astype(BF16)
    p_lo = (p_sum - p_hi.astype(F32)).astype(BF16)
    imp = (jnp.dot(ovt_ref[...], p_hi, preferred_element_type=F32)
           + jnp.dot(ovt_ref[...], p_lo, preferred_element_type=F32))
    n_blk = ovt_ref.shape[0]
    blk_i = lax.broadcasted_iota(jnp.int32, (n_blk, 1), 0)
    cur = lax.shift_right_logical(qpos, sel_shift)
    valid = blk_i <= cur
    forced = valid & ((blk_i == 0) | (blk_i > cur - SEL_LOCAL))
    imp = jnp.where(forced, jnp.inf, jnp.where(valid, imp, -jnp.inf))

    m0 = jnp.full((1, lanes), NEG, F32)
    acc0 = jnp.zeros((d + 8, lanes), F32)
    krow = lax.broadcasted_iota(jnp.int32, (tq, 1), 0)

    st = (m0, acc0)
    for n in range(WINDOW // tq + 1):
        j = i - n
        off = pl.multiple_of(jnp.maximum(j, 0) * tq, tq)
        kpos = off + krow + jnp.where(j < 0, 1 << 30, 0)
        mask = per_head(((kpos <= qpos) & (kpos > qpos - WINDOW)).astype(F32)) > 0.5
        st = _flash_step_t(qt, kw_ref[pl.ds(off, tq), :].astype(BF16), vw_ref[:, pl.ds(off, tq)].astype(BF16),
                           mask, *st)
    acc_w = st[1]

    blk_f = blk_i.astype(F32)
    sel = jnp.zeros((n_blk, tq), F32)
    for _ in range(min(SEL_TOP, n_blk)):
        top = jnp.max(imp, axis=0, keepdims=True)
        idx = jnp.min(jnp.where(imp == top, blk_f, float(n_blk)), axis=0, keepdims=True)
        hit = blk_f == idx
        sel = jnp.where(hit, 1.0, sel)
        imp = jnp.where(hit, -jnp.inf, imp)
    sel = sel.astype(BF16)

    exp_blk = lax.broadcasted_iota(jnp.int32, (tq, n_blk), 1)
    exp_key = lax.shift_right_logical(lax.broadcasted_iota(jnp.int32, (tq, n_blk), 0), sel_shift)

    def key_tile(n):
        return jnp.clip(i - n, 0, i)

    def scores(n):
        j = key_tile(n)
        expand = (exp_blk == j * (tq // SEL_BLOCK) + exp_key).astype(BF16)
        return (jnp.dot(ks_ref[pl.ds(pl.multiple_of(j * tq, tq), tq), :].astype(BF16), qt, preferred_element_type=F32),
                jnp.dot(expand, sel, preferred_element_type=F32))

    def weighted_values(n, slot):
        vt = vs_ref[:, pl.ds(pl.multiple_of(key_tile(n) * tq, tq), tq)].astype(BF16)
        return jnp.dot(jnp.concatenate([vt, jnp.ones((8, tq), BF16)], axis=0), p_ref[slot],
                       preferred_element_type=F32)

    s_ref[0], ch_ref[0] = scores(0)
    p_ref[0] = jnp.zeros(p_ref.shape[1:], BF16)

    def sel_step(n, rd, wr, st):
        m, alpha, acc = st
        pv = weighted_values(n - 1, rd)
        s_ref[wr], ch_ref[wr] = scores(n + 1)
        kpos = (i - n) * tq + krow + jnp.where(n > i, 1 << 30, 0)
        mask = per_head(jnp.where(kpos <= qpos, ch_ref[rd], 0.0)) > 0.5
        s = jnp.where(mask, s_ref[rd], NEG)
        m_new = jnp.maximum(m, jnp.max(s, axis=0, keepdims=True))
        p_ref[wr] = jnp.exp(s - m_new).astype(BF16)
        return m_new, jnp.exp(m - m_new), acc * alpha + pv

    def sel_pair(pair, st):
        return sel_step(2 * pair + 1, 1, 0, sel_step(2 * pair, 0, 1, st))

    _, _, acc_s = lax.fori_loop(0, (i + 3) >> 1, sel_pair, (m0, jnp.ones((1, lanes), F32), acc0))

    o_s = acc_s[:d] / acc_s[d:d + 1]
    o_w = acc_w[:d] / acc_w[d:d + 1]
    gate = _sigmoid(gt_ref[...])
    outs = []
    for r in range(rep):
        sl = slice(r * tq, (r + 1) * tq)
        outs.append(gate[3 * r:3 * r + 1] * o_c[:, sl] + gate[3 * r + 1:3 * r + 2] * o_s[:, sl]
                    + gate[3 * r + 2:3 * r + 3] * o_w[:, sl])
    o_ref[...] = jnp.concatenate(outs, axis=0).astype(o_ref.dtype)


def nsa_prompt_t(nsa_qt, nsa_g, nsa_kvt, cmp_pos, cmp_w1, cmp_w2, tq=128):
    b, _, t = nsa_qt.shape
    g, d, rep = NSA_KV_GROUPS, HEAD_DIM, NSA_REP
    assert t % tq == 0 and tq % SEL_BLOCK == 0 and WINDOW % tq == 0 and t % CMP_STRIDE == 0
    n_cmp = (t - CMP_BLOCK) // CMP_STRIDE + 1
    nr = t // CMP_STRIDE
    n_blk = t // SEL_BLOCK
    kinds = nsa_kvt.reshape(b, 6, g, d, t)
    rows16 = jnp.swapaxes(kinds[:, 0:2], 3, 4).reshape(b, 2, g, nr, CMP_STRIDE * d)
    kcvc = nsa_compress(rows16, cmp_pos, cmp_w1, cmp_w2)
    vct = jnp.swapaxes(kcvc[:, 1], 2, 3)
    overlap_t = _overlap_matrix(nr, n_cmp, n_blk).T
    keys = jnp.swapaxes(kinds[:, 2::2], 3, 4).astype(BF16)
    gates = jnp.transpose(nsa_g.reshape(b, t, g, 3 * rep), (0, 2, 3, 1))
    gates = jnp.pad(gates, ((0, 0), (0, 0), (0, 16 - 3 * rep), (0, 0)))
    values = lambda kind: pl.BlockSpec((None, d, t), lambda bi, gi, i: (bi, kind * g + gi, 0))
    rows = lambda kind: pl.BlockSpec((None, None, None, t, d), lambda bi, gi, i: (bi, kind, gi, 0, 0))
    out = pl.pallas_call(
        functools.partial(_nsa_prompt_t_kernel, tq=tq, n_cmp=n_cmp),
        grid=(b, g, t // tq),
        in_specs=[pl.BlockSpec((None, rep * d, tq), lambda bi, gi, i: (bi, gi, i)),
                  pl.BlockSpec((None, None, 16, tq), lambda bi, gi, i: (bi, gi, 0, i)),
                  pl.BlockSpec((None, None, None, nr, d), lambda bi, gi, i: (bi, 0, gi, 0, 0)),
                  pl.BlockSpec((None, None, d, nr), lambda bi, gi, i: (bi, gi, 0, 0)),
                  pl.BlockSpec((n_blk, nr), lambda bi, gi, i: (0, 0)),
                  rows(0), values(3), rows(1), values(5)],
        out_specs=pl.BlockSpec((None, rep * d, tq), lambda bi, gi, i: (bi, gi, i)),
        out_shape=jax.ShapeDtypeStruct((b, g * rep * d, t), BF16),
        scratch_shapes=[pltpu.VMEM((2, tq, rep * tq), F32), pltpu.VMEM((2, tq, tq), F32),
                        pltpu.VMEM((2, tq, rep * tq), BF16)],
        compiler_params=_params("parallel", "parallel", "arbitrary"),
        name="nsa_prompt",
    )(nsa_qt, gates, kcvc, vct, overlap_t, keys, nsa_kvt, keys, nsa_kvt)
    return jnp.swapaxes(out, 1, 2)


def _ssd_step_pre_kernel(x_ref, buf_ref, cw_ref, cb_ref, dt_ref, dtb_ref, alog_ref, exp_ref,
                         xs_ref, xdt_ref, bm_ref, cm_ref, dec_ref):
    conv = cb_ref[...] + cw_ref[SSD_CONV - 1:SSD_CONV, :] * x_ref[...]
    for j in range(SSD_CONV - 1):
        conv = conv + cw_ref[j:j + 1, :] * buf_ref[j]
    u = conv * _sigmoid(conv)
    xs = u[:, :SSD_INNER]
    dt = _softplus(dt_ref[...] + dtb_ref[...])
    xs_ref[...] = xs
    xdt_ref[...] = xs * jnp.dot(dt, exp_ref[...], precision=HI, preferred_element_type=F32)
    bm_ref[...] = u[:, SSD_INNER:SSD_INNER + SSD_GN]
    cm_ref[...] = u[:, SSD_INNER + SSD_GN:]
    dec_ref[...] = jnp.exp(dt * (-jnp.exp(alog_ref[...])))


def _ssd_step_state_kernel(h0_ref, xdt_ref, dec_ref, bm_ref, cm_ref, h_ref, y_ref):
    r = SSD_HEADS // SSD_GROUPS
    for b in range(h0_ref.shape[0]):
        for hd in range(SSD_HEADS):
            g = hd // r
            hn = dec_ref[b, hd] * h0_ref[b, hd] + xdt_ref[b, hd] * bm_ref[b, g]
            h_ref[b, hd] = hn
            y_ref[b, hd] = jnp.sum(hn * cm_ref[b, g], axis=-1, keepdims=True)


def _ssd_step_post_kernel(y_ref, xs_ref, z_ref, dfull_ref, nw_ref, o_ref):
    zz = z_ref[...]
    y = (y_ref[...] + dfull_ref[...] * xs_ref[...]) * (zz * _sigmoid(zz))
    o_ref[...] = _rms(y, nw_ref[...]).astype(o_ref.dtype)


def ssd_step(xbc, z, dt_raw, h0, conv_buf, conv_w, conv_b, dt_bias, a_log, d_skip, norm_w):
    b = xbc.shape[0]
    hds, p, n = SSD_HEADS, SSD_HEAD_DIM, SSD_STATE
    expand = (jnp.arange(SSD_INNER)[None, :] // p == jnp.arange(hds)[:, None]).astype(F32)
    d_full = jnp.repeat(d_skip, p).reshape(1, SSD_INNER)
    sds = lambda shape: jax.ShapeDtypeStruct(shape, F32)
    xs, xdt, bm, cm, dec = pl.pallas_call(
        _ssd_step_pre_kernel,
        out_shape=[sds((b, SSD_INNER)), sds((b, SSD_INNER)), sds((b, SSD_GN)), sds((b, SSD_GN)), sds((b, hds))],
        name="ssd_step_pre",
    )(xbc, jnp.swapaxes(conv_buf, 0, 1), conv_w, conv_b.reshape(1, -1), dt_raw, dt_bias.reshape(1, hds),
      a_log.reshape(1, hds), expand)
    n_seq = 1
    per_b = lambda *dims: pl.BlockSpec((n_seq,) + dims, lambda bi: (bi,) + (0,) * len(dims))
    h_new, y_col = pl.pallas_call(
        _ssd_step_state_kernel,
        grid=(b // n_seq,),
        in_specs=[per_b(hds, p, n), per_b(hds, p, 1), per_b(hds, 1, 1), per_b(SSD_GROUPS, 1, n),
                  per_b(SSD_GROUPS, 1, n)],
        out_specs=[per_b(hds, p, n), per_b(hds, p, 1)],
        out_shape=[sds((b, hds, p, n)), sds((b, hds, p, 1))],
        compiler_params=_params("parallel"),
        name="ssd_step_state",
    )(h0, xdt.reshape(b, hds, p, 1), dec.reshape(b, hds, 1, 1), bm.reshape(b, SSD_GROUPS, 1, n),
      cm.reshape(b, SSD_GROUPS, 1, n))
    y = pl.pallas_call(
        _ssd_step_post_kernel,
        out_shape=jax.ShapeDtypeStruct((b, SSD_INNER), BF16),
        name="ssd_step_post",
    )(y_col.reshape(b, SSD_INNER), xs, z, d_full, norm_w.reshape(1, -1))
    return y, h_new


def _sb_decode_kernel(pt_ref, q_ref, *refs):
    page_refs, (o_ref, carry_ref, acc_ref) = refs[:PAGES_PER_STEP], refs[PAGES_PER_STEP:]
    p = pl.program_id(1)
    tk = page_refs[0].shape[-1]
    hds = SB_HEADS

    @pl.when(p == 0)
    def _():
        carry_ref[...] = jnp.zeros(carry_ref.shape, F32)
        acc_ref[...] = jnp.zeros(acc_ref.shape, F32)

    scale = HEAD_DIM ** -0.5
    qs = [q_ref[h] * scale for h in range(hds)]
    z = jnp.concatenate([jnp.sum(ref[0, h] * qs[h], axis=0, keepdims=True)
                         for ref in page_refs for h in range(hds)], axis=0)
    row = lax.broadcasted_iota(jnp.int32, (tk, tk), 0)
    col = lax.broadcasted_iota(jnp.int32, (tk, tk), 1)
    sp = _softplus(z)
    later = _split_dot(-sp, (row > col).astype(BF16))
    total = later[:, 0:1] - sp[:, 0:1]
    carry = carry_ref[...]
    for k, ref in enumerate(page_refs):
        rows = slice(k * hds, (k + 1) * hds)
        a = jnp.exp(z[rows] - sp[rows] + later[rows] + carry)
        carry = carry + total[rows]
        for h in range(hds):
            acc_ref[h] += ref[1, h] * a[h:h + 1, :]
    carry_ref[...] = carry

    @pl.when(p == pl.num_programs(1) - 1)
    def _():
        for h in range(hds):
            o_ref[h] = jnp.sum(acc_ref[h], axis=-1, keepdims=True)


def _page_specs(block, layer, n_pages, kind_block, descending):
    def spec(k):
        def index(bi, p, pt):
            pos = p * PAGES_PER_STEP + k
            pos = n_pages - 1 - pos if descending else pos
            return (layer, pt[bi, pos], kind_block) + (0,) * (len(block) - 3)
        return pl.BlockSpec(block, index)
    return [spec(k) for k in range(PAGES_PER_STEP)]


def sb_decode(q, pool_t, layer, page_table):
    b = q.shape[0]
    h, d = SB_HEADS, HEAD_DIM
    n_pages = page_table.shape[1]
    page = pool_t.shape[-1]
    assert n_pages % PAGES_PER_STEP == 0
    out = pl.pallas_call(
        _sb_decode_kernel,
        grid_spec=pltpu.PrefetchScalarGridSpec(
            num_scalar_prefetch=1,
            grid=(b, n_pages // PAGES_PER_STEP),
            in_specs=[pl.BlockSpec((None, h, d, 1), lambda bi, p, pt: (bi, 0, 0, 0))]
            + _page_specs((None, None, 2, h, d, page), layer, n_pages, 0, descending=True),
            out_specs=pl.BlockSpec((None, h, d, 1), lambda bi, p, pt: (bi, 0, 0, 0)),
            scratch_shapes=[pltpu.VMEM((h, 1), F32), pltpu.VMEM((h, d, page), F32)]),
        out_shape=jax.ShapeDtypeStruct((b, h, d, 1), F32),
        compiler_params=_params("parallel", "arbitrary"),
        name="sb_decode",
    )(page_table, q.reshape(b, h, d, 1), *([pool_t] * PAGES_PER_STEP))
    return out.reshape(b, h * d)


def _nsa_gather_compress_kernel(pt_ref, *refs):
    page_refs = refs[:PAGES_PER_STEP]
    pos_ref, w1_ref, w2_ref, o_ref, x_ref, r_ref, sh_ref = refs[PAGES_PER_STEP:]
    p = pl.program_id(1)
    page = page_refs[0].shape[-1]
    grp, d = NSA_KV_GROUPS, HEAD_DIM
    n_out = page // CMP_STRIDE
    for k, ref in enumerate(page_refs):
        row0 = pl.multiple_of((p * PAGES_PER_STEP + k) * n_out, n_out)
        for kind in range(2):
            x = x_ref.at[2 * k + kind]
            x[...] = ref[kind].reshape(grp * d, page).T
            steps = [x[pl.ds(s, n_out, stride=CMP_STRIDE), :] for s in range(CMP_STRIDE)]
            for g in range(grp):
                r_ref[kind, g, pl.ds(row0, n_out), :] = jnp.concatenate(
                    [st[:, g * d:(g + 1) * d] for st in steps], axis=1)

    @pl.when(p == pl.num_programs(1) - 1)
    def _():
        for kind in range(2):
            for g in range(grp):
                o_ref[kind, g] = _compress_rows(r_ref[kind, g], pos_ref.at[kind], w1_ref.at[kind], w2_ref.at[kind],
                                                sh_ref).astype(o_ref.dtype)


def nsa_gather_compress(pool_t, layer, page_table, cmp_pos, cmp_w1, cmp_w2):
    b, n_pages = page_table.shape
    g, d = NSA_KV_GROUPS, HEAD_DIM
    page = pool_t.shape[-1]
    assert page % CMP_STRIDE == 0 and n_pages % PAGES_PER_STEP == 0 and CMP_BLOCK == 2 * CMP_STRIDE
    nr = n_pages * (page // CMP_STRIDE)
    w = CMP_STRIDE * d
    whole = lambda shape: pl.BlockSpec(shape, lambda bi, p, pt: (0,) * len(shape))
    return pl.pallas_call(
        _nsa_gather_compress_kernel,
        grid_spec=pltpu.PrefetchScalarGridSpec(
            num_scalar_prefetch=1,
            grid=(b, n_pages // PAGES_PER_STEP),
            in_specs=_page_specs((None, None, 2, g, d, page), layer, n_pages, 0, descending=False)
            + [whole((2, 2, w)), whole((2, 2 * w, CMP_HIDDEN)), whole((2, CMP_HIDDEN, d))],
            out_specs=pl.BlockSpec((None, 2, g, nr, d), lambda bi, p, pt: (bi, 0, 0, 0, 0)),
            scratch_shapes=[pltpu.VMEM((2 * PAGES_PER_STEP, page, g * d), F32), pltpu.VMEM((2, g, nr, w), F32),
                            pltpu.VMEM((nr + 8, CMP_HIDDEN), F32)]),
        out_shape=jax.ShapeDtypeStruct((b, 2, g, nr, d), BF16),
        compiler_params=_params("parallel", "arbitrary"),
        name="nsa_gather_compress",
    )(page_table, *([pool_t] * PAGES_PER_STEP), cmp_pos.reshape(2, 2, w), cmp_w1.astype(BF16), cmp_w2.astype(BF16))


def _nsa_decode_select_kernel(q_ref, kc_ref, vc_ref, ov_ref, oc_ref, sel_ref, *, n_cmp, q_pos):
    rep, d, grp = NSA_REP, HEAD_DIM, NSA_KV_GROUPS
    n_seq = q_ref.shape[0]
    n_rows = kc_ref.shape[2]
    n_blk = ov_ref.shape[1]
    cidx = lax.broadcasted_iota(jnp.int32, (1, n_rows), 1)
    vis = (cidx * CMP_STRIDE + (CMP_BLOCK - 1) <= q_pos) & (cidx < n_cmp)
    imp = []
    for b in range(n_seq):
        q = (q_ref[b] * (d ** -0.5)).astype(BF16)
        o_c = []
        for g in range(grp):
            s = jnp.where(vis, _dot_nt(q[g * rep:(g + 1) * rep], kc_ref[b, g]), NEG)
            e = jnp.where(vis, jnp.exp(s - jnp.max(s, axis=-1, keepdims=True)), 0.0)
            p = e / jnp.maximum(jnp.sum(e, axis=-1, keepdims=True), 1e-30)
            o_c.append(jnp.dot(p.astype(BF16), vc_ref[b, g], preferred_element_type=F32))
            imp.append(_split_dot(jnp.sum(p, axis=0, keepdims=True), ov_ref[...]))
        oc_ref[b] = jnp.concatenate(o_c, axis=0)
    imp = jnp.concatenate(imp, axis=0)
    blk_i = lax.broadcasted_iota(jnp.int32, (1, n_blk), 1)
    forced = (blk_i == 0) | (blk_i > n_blk - SEL_LOCAL)
    sel = _top_blocks(jnp.where(forced, jnp.inf, imp), blk_i.astype(F32), min(SEL_TOP - 1, n_blk))
    pad = jnp.zeros((sel_ref.shape[1] - grp, n_blk), F32)
    for b in range(n_seq):
        sel_ref[b] = jnp.concatenate([sel[b * grp:(b + 1) * grp], pad], axis=0)


def _nsa_decode_attend_kernel(pt_ref, q_ref, sel_ref, *refs, win_skip):
    page_refs = refs[:PAGES_PER_STEP]
    new_ref, win_ref, oc_ref, gt_ref, o_ref, m_ref, acc_ref = refs[PAGES_PER_STEP:]
    p = pl.program_id(1)
    n_steps = pl.num_programs(1)
    rep, d, grp = NSA_REP, HEAD_DIM, NSA_KV_GROUPS
    tk = page_refs[0].shape[-1]
    lanes = PAGES_PER_STEP * tk
    scale = d ** -0.5
    qf = q_ref[...] * scale
    q = qf.astype(BF16)
    new = new_ref[...]
    new_row = lambda kind, g: new[kind * grp + g:kind * grp + g + 1, :]
    per_head = lambda f: jnp.concatenate([f(g) for g in range(grp)], axis=0)

    @pl.when(p == 0)
    def _():
        m_ref[...] = per_head(lambda g: jnp.sum(qf[g * rep:(g + 1) * rep] * new_row(2, g), axis=-1, keepdims=True))
        acc_ref[...] = per_head(lambda g: jnp.concatenate(
            [jnp.broadcast_to(new_row(3, g), (rep, d)), jnp.ones((rep, d), F32)], axis=1))

    n_blk = sel_ref.shape[1]
    lane = lax.broadcasted_iota(jnp.int32, (n_blk, lanes), 1)
    page_pos = (n_steps - p) * PAGES_PER_STEP - 1 - lax.shift_right_logical(lane, int(math.log2(tk)))
    blk_of_lane = page_pos * (tk // SEL_BLOCK) + lax.shift_right_logical(lane & (tk - 1), int(math.log2(SEL_BLOCK)))
    expand = (lax.broadcasted_iota(jnp.int32, (n_blk, lanes), 0) == blk_of_lane).astype(BF16)
    chosen = jnp.dot(sel_ref[...].astype(BF16), expand, preferred_element_type=F32)

    def scores(g):
        qg = q[g * rep:(g + 1) * rep]
        sg = jnp.concatenate([jnp.dot(qg, ref[0, g].astype(BF16), preferred_element_type=F32) for ref in page_refs],
                             axis=1)
        return jnp.where(chosen[g:g + 1, :] > 0.5, sg, NEG)

    s = per_head(scores)
    m_old = m_ref[...]
    m_new = jnp.maximum(m_old, jnp.max(s, axis=-1, keepdims=True))
    pr = jnp.exp(s - m_new).astype(BF16)

    def weighted_values(g):
        out = jnp.zeros((rep, 2 * d), F32)
        for k, ref in enumerate(page_refs):
            vt_ext = jnp.concatenate([ref[1, g].astype(BF16), jnp.ones((d, tk), BF16)], axis=0)
            out = out + _dot_nt(pr[g * rep:(g + 1) * rep, k * tk:(k + 1) * tk], vt_ext)
        return out

    acc_ref[...] = acc_ref[...] * jnp.exp(m_old - m_new) + per_head(weighted_values)
    m_ref[...] = m_new

    @pl.when(p == n_steps - 1)
    def _():
        acc = acc_ref[...]
        o_s = acc[:, :d] / acc[:, d:]
        wlen = win_ref.shape[-1]
        vis = lax.broadcasted_iota(jnp.int32, (1, wlen), 1) >= win_skip

        def window(g):
            qg = q[g * rep:(g + 1) * rep]
            s_w = jnp.where(vis, jnp.dot(qg, win_ref[0, g].astype(BF16), preferred_element_type=F32), NEG)
            s_n = jnp.sum(qf[g * rep:(g + 1) * rep] * new_row(4, g), axis=-1, keepdims=True)
            mx = jnp.maximum(jnp.max(s_w, axis=-1, keepdims=True), s_n)
            e_w = jnp.where(vis, jnp.exp(s_w - mx), 0.0)
            e_n = jnp.exp(s_n - mx)
            num = _dot_nt(e_w.astype(BF16), win_ref[1, g].astype(BF16)) + e_n * new_row(5, g)
            return num / (jnp.sum(e_w, axis=-1, keepdims=True) + e_n)

        o_w = per_head(window)
        gate = _sigmoid(gt_ref[...])
        o_ref[...] = gate[:, 0:1] * oc_ref[...] + gate[:, 1:2] * o_s + gate[:, 2:3] * o_w


def nsa_decode(nsa_q, nsa_g, nsa_kv_new, pool_t, win_t, layer, page_table, cmp_pos, cmp_w1, cmp_w2):
    b = nsa_q.shape[0]
    g, d, rep, hds = NSA_KV_GROUPS, HEAD_DIM, NSA_REP, NSA_HEADS
    n_pages = page_table.shape[1]
    page = pool_t.shape[-1]
    past = n_pages * page
    wlen = win_t.shape[-1]
    assert past % SEL_BLOCK == 0 and past % CMP_STRIDE == 0 and page % SEL_BLOCK == 0 and wlen <= past
    n_cmp = (past + 1 - CMP_BLOCK) // CMP_STRIDE + 1
    nr = past // CMP_STRIDE
    n_blk = past // SEL_BLOCK
    kcvc = nsa_gather_compress(pool_t, layer, page_table, cmp_pos, cmp_w1, cmp_w2)
    overlap = _overlap_matrix(nr, n_cmp, n_blk)
    q3 = nsa_q.reshape(b, hds, d)
    n_seq = _pick(b, (SEQS_PER_STEP, 1))
    o_c, sel = pl.pallas_call(
        functools.partial(_nsa_decode_select_kernel, n_cmp=n_cmp, q_pos=past),
        grid=(b // n_seq,),
        in_specs=[pl.BlockSpec((n_seq, hds, d), lambda bi: (bi, 0, 0)),
                  pl.BlockSpec((n_seq, None, g, nr, d), lambda bi: (bi, 0, 0, 0, 0)),
                  pl.BlockSpec((n_seq, None, g, nr, d), lambda bi: (bi, 1, 0, 0, 0)),
                  pl.BlockSpec((nr, n_blk), lambda bi: (0, 0))],
        out_specs=[pl.BlockSpec((n_seq, hds, d), lambda bi: (bi, 0, 0)),
                   pl.BlockSpec((n_seq, 8, n_blk), lambda bi: (bi, 0, 0))],
        out_shape=[jax.ShapeDtypeStruct((b, hds, d), F32), jax.ShapeDtypeStruct((b, 8, n_blk), F32)],
        compiler_params=_params("parallel"),
        name="nsa_decode_select",
    )(q3, kcvc, kcvc, overlap)
    gates = jnp.pad(nsa_g.reshape(b, hds, 3), ((0, 0), (0, 0), (0, V7X_LANES - 3)))
    fixed = lambda *dims: pl.BlockSpec((None,) + dims, lambda bi, p, pt: (bi,) + (0,) * len(dims))
    out = pl.pallas_call(
        functools.partial(_nsa_decode_attend_kernel, win_skip=wlen - WINDOW + 1),
        grid_spec=pltpu.PrefetchScalarGridSpec(
            num_scalar_prefetch=1,
            grid=(b, n_pages // PAGES_PER_STEP),
            in_specs=[fixed(hds, d), fixed(8, n_blk)]
            + _page_specs((None, None, 2, g, d, page), layer, n_pages, 1, descending=True)
            + [fixed(6 * g, d),
               pl.BlockSpec((None, None, 2, g, d, wlen), lambda bi, p, pt: (layer, bi, 0, 0, 0, 0)),
               fixed(hds, d), fixed(hds, V7X_LANES)],
            out_specs=fixed(hds, d),
            scratch_shapes=[pltpu.VMEM((hds, 1), F32), pltpu.VMEM((hds, 2 * d), F32)]),
        out_shape=jax.ShapeDtypeStruct((b, hds, d), F32),
        compiler_params=_params("parallel", "arbitrary"),
        name="nsa_decode_attend",
    )(page_table, q3, sel, *([pool_t] * PAGES_PER_STEP), nsa_kv_new.reshape(b, 6 * g, d), win_t, o_c, gates)
    return out.reshape(b, hds * d)


def _col_offsets():
    offs, s = [], 0
    for n in IN_SPLITS:
        offs.append(s)
        s += n
    return offs


def _layer_weights(l, p):
    o = _col_offsets()
    w_in = p['w_in'][l]
    cols = lambda a, n: w_in[:, a:a + n]
    small = jnp.concatenate([cols(o[2], SSD_HEADS), cols(o[6], 3 * NSA_HEADS)], axis=1)
    small = jnp.pad(small, ((0, 0), (0, V7X_LANES - small.shape[1])))
    bf = lambda a: a.astype(BF16)
    return {
        'w_z': bf(cols(o[0], SSD_INNER)), 'w_xbc': bf(cols(o[1], SSD_CONV_DIM)), 'w_small': bf(small),
        'w_sbq': bf(cols(o[3], SB_WIDTH)), 'w_sbkv': bf(cols(o[3] + SB_WIDTH, 2 * SB_WIDTH)),
        'w_nq': bf(cols(o[4], NSA_WIDTH)), 'w_nkv': bf(cols(o[5], 6 * NSA_KV_WIDTH)),
        'w_brg': bf(cols(o[7], N_BRANCH * D_MODEL)),
        'mix_pre': p['norm_mix_pre'][l], 'mix_post': p['norm_mix_post'][l],
        'ffn_pre': p['norm_ffn_pre'][l], 'ffn_post': p['norm_ffn_post'][l],
        'conv_w': p['ssd_conv_w'][l], 'conv_b': p['ssd_conv_b'][l], 'dt_bias': p['ssd_dt_bias'][l],
        'a_log': p['ssd_a_log'][l], 'd_skip': p['ssd_d'][l], 'ssd_norm': p['ssd_norm'][l],
        'w_ssd_out': bf(p['w_ssd_out'][l]), 'w_sb_out': bf(p['w_sb_out'][l]), 'w_nsa_out': bf(p['w_nsa_out'][l]),
        'w_o': bf(p['w_o'][l]), 'w_ffn_gate': bf(p['w_ffn_gate'][l]), 'w_ffn_up': bf(p['w_ffn_up'][l]),
        'w_ffn_down': bf(p['w_ffn_down'][l]),
        'cmp_pos': p['nsa_cmp_pos'][l], 'cmp_w1': p['nsa_cmp_w1'][l], 'cmp_w2': p['nsa_cmp_w2'][l],
    }


def _trunk_tail(x, ssd_y, sb_o, nsa_o, br_g, lw):
    b, t, d = x.shape
    m = b * t
    x1 = merge_branches(x.reshape(m, d), ssd_y.reshape(m, -1), sb_o.reshape(m, -1), nsa_o.reshape(m, -1),
                        br_g.reshape(m, -1), lw['w_ssd_out'], lw['w_sb_out'], lw['w_nsa_out'], lw['w_o'],
                        lw['mix_post'])
    x2 = ffn(x1, lw['ffn_pre'], lw['ffn_post'], lw['w_ffn_gate'], lw['w_ffn_up'], lw['w_ffn_down'])
    return x2.reshape(b, t, d)


def _layer_prompt(x, lw, layer, n_layers, sb_kv_all):
    b, t, _ = x.shape
    g = lw['mix_pre']
    z = norm_matmul(x, g, lw['w_z'])
    xbc = norm_matmul(x, g, lw['w_xbc'])
    small = norm_matmul(x, g, lw['w_small'])
    br_g = norm_matmul(x, g, lw['w_brg'])
    sb_q = norm_matmul(x, g, lw['w_sbq'], out_dtype=BF16)
    nsa_qt = norm_matmul(x, g, lw['w_nq'].T, transposed=True)
    sb_kv_all = norm_matmul(x, g, lw['w_sbkv'].T, transposed=True, stack=(sb_kv_all, layer, n_layers))
    nsa_kvt = norm_matmul(x, g, lw['w_nkv'].T, transposed=True)
    dt_raw = small[..., :SSD_HEADS]
    nsa_g = small[..., SSD_HEADS:SSD_HEADS + 3 * NSA_HEADS]

    h0 = jnp.zeros((b, SSD_HEADS, SSD_HEAD_DIM, SSD_STATE), F32)
    conv0 = jnp.zeros((b, SSD_CONV - 1, SSD_CONV_DIM), F32)
    ssd_y, h_new = ssd_prompt(xbc, z, dt_raw, h0, conv0, lw['conv_w'], lw['conv_b'], lw['dt_bias'], lw['a_log'],
                              lw['d_skip'], lw['ssd_norm'])
    conv_new = xbc[:, t - (SSD_CONV - 1):, :]
    sb_o = sb_prompt(sb_q, sb_kv_all, layer)

    nsa_o = nsa_prompt_t(nsa_qt, nsa_g, nsa_kvt, lw['cmp_pos'], lw['cmp_w1'], lw['cmp_w2'])

    y = _trunk_tail(x, ssd_y, sb_o, nsa_o, br_g, lw)
    nsa_all = jnp.moveaxis(nsa_kvt.reshape(b, 6, NSA_KV_GROUPS, HEAD_DIM, t), 4, 1)
    keep = min(WINDOW, t)
    return y, sb_kv_all, nsa_all[:, :, 0:4], nsa_all[:, t - keep:, 4:6], h_new, conv_new


def _layer_sample(x, lw, layer, sb_pool_t, nsa_pool_t, win_t, h0, conv_buf, page_table):
    bsz, t = x.shape[:2]
    assert t == 1
    past = page_table.shape[1] * sb_pool_t.shape[-1]
    xr = x.reshape(1, bsz, D_MODEL)
    g = lw['mix_pre']
    pr = lambda w: norm_matmul(xr, g, w)[0]
    z, xbc, small, br_g = pr(lw['w_z']), pr(lw['w_xbc']), pr(lw['w_small']), pr(lw['w_brg'])
    sb_q, sb_kv, nsa_q, nsa_kv = pr(lw['w_sbq']), pr(lw['w_sbkv']), pr(lw['w_nq']), pr(lw['w_nkv'])
    dt_raw = small[:, :SSD_HEADS]
    nsa_g = small[:, SSD_HEADS:SSD_HEADS + 3 * NSA_HEADS]
    ssd_y, h_new = ssd_step(xbc, z, dt_raw, h0, conv_buf, lw['conv_w'], lw['conv_b'], lw['dt_bias'], lw['a_log'],
                            lw['d_skip'], lw['ssd_norm'])
    conv_new = jnp.concatenate([conv_buf[:, 1:], xbc[:, None, :]], axis=1)
    sb_o = sb_decode(sb_q, sb_pool_t, layer, page_table)
    nsa_o = nsa_decode(nsa_q, nsa_g, nsa_kv, nsa_pool_t, win_t, layer, page_table, lw['cmp_pos'], lw['cmp_w1'],
                       lw['cmp_w2'])
    y = _trunk_tail(x, ssd_y[:, None], sb_o.astype(BF16)[:, None], nsa_o.astype(BF16)[:, None], br_g[:, None], lw)
    kv_new = sb_kv.reshape(bsz, 1, 2, SB_HEADS, HEAD_DIM)
    nkv = nsa_kv.reshape(bsz, 1, 6, NSA_KV_GROUPS, HEAD_DIM)
    keep = min(WINDOW, past + 1)
    win_all_t = jnp.concatenate([win_t[layer], nkv[:, 0, 4:6][..., None]], axis=-1)
    win_new = jnp.moveaxis(win_all_t[..., win_all_t.shape[-1] - keep:], 4, 1)
    return y, kv_new, nkv[:, :, 0:4], win_new, h_new, conv_new


def kernel(x_prompt, x_sample, cache_sb_kv, cache_nsa_kv, cache_nsa_win, state_ssd, state_conv, page_table,
           norm_mix_pre, norm_mix_post, norm_ffn_pre, norm_ffn_post, w_in, ssd_conv_w, ssd_conv_b, ssd_dt_bias,
           ssd_a_log, ssd_d, ssd_norm, w_ssd_out, w_sb_out, nsa_cmp_pos, nsa_cmp_w1, nsa_cmp_w2, w_nsa_out, w_o,
           w_ffn_gate, w_ffn_up, w_ffn_down):
    p = dict(norm_mix_pre=norm_mix_pre, norm_mix_post=norm_mix_post, norm_ffn_pre=norm_ffn_pre,
             norm_ffn_post=norm_ffn_post, w_in=w_in, ssd_conv_w=ssd_conv_w, ssd_conv_b=ssd_conv_b,
             ssd_dt_bias=ssd_dt_bias, ssd_a_log=ssd_a_log, ssd_d=ssd_d, ssd_norm=ssd_norm, w_ssd_out=w_ssd_out,
             w_sb_out=w_sb_out, nsa_cmp_pos=nsa_cmp_pos, nsa_cmp_w1=nsa_cmp_w1, nsa_cmp_w2=nsa_cmp_w2,
             w_nsa_out=w_nsa_out, w_o=w_o, w_ffn_gate=w_ffn_gate, w_ffn_up=w_ffn_up, w_ffn_down=w_ffn_down)
    yp, ys = x_prompt, x_sample
    outs_p, outs_s = [], []
    time_minor = lambda a: jnp.transpose(a, (0, 1, 3, 4, 5, 2))
    sb_pool_t, nsa_pool_t, win_t = time_minor(cache_sb_kv), time_minor(cache_nsa_kv), time_minor(cache_nsa_win)
    n_layers = w_in.shape[0]
    sb_kv_all = None
    for l in range(n_layers):
        lw = _layer_weights(l, p)
        res = _layer_prompt(yp, lw, l, n_layers, sb_kv_all)
        yp, sb_kv_all = res[0], res[1]
        outs_p.append(res[1:])
        res = _layer_sample(ys, lw, l, sb_pool_t, nsa_pool_t, win_t, state_ssd[l], state_conv[l], page_table)
        ys = res[0]
        outs_s.append(res[1:])
    st = lambda outs, i: jnp.stack([o[i] for o in outs])
    bp, tp = x_prompt.shape[:2]
    sb_kv_p = jnp.moveaxis(sb_kv_all.reshape(n_layers, bp, 2, SB_HEADS, HEAD_DIM, tp), 5, 2)
    return (yp, ys, sb_kv_p, st(outs_s, 0), st(outs_p, 1), st(outs_s, 1), st(outs_p, 2), st(outs_s, 2),
            st(outs_p, 3), st(outs_s, 3), st(outs_p, 4), st(outs_s, 4))
```

```python
import functools
import math

import jax
import jax.numpy as jnp
from jax import lax
from jax.experimental import pallas as pl
from jax.experimental.pallas import tpu as pltpu

D_MODEL = 1024
HEAD_DIM = 64
SSD_INNER = D_MODEL
SSD_HEAD_DIM = 64
SSD_HEADS = SSD_INNER // SSD_HEAD_DIM
SSD_GROUPS = 2
SSD_STATE = 128
SSD_CONV = 4
SSD_GN = SSD_GROUPS * SSD_STATE
SSD_CONV_DIM = SSD_INNER + 2 * SSD_GN
SSD_CHUNK = 128
SB_HEADS = 8
SB_WIDTH = SB_HEADS * HEAD_DIM
NSA_HEADS = 8
NSA_KV_GROUPS = 2
NSA_REP = NSA_HEADS // NSA_KV_GROUPS
NSA_WIDTH = NSA_HEADS * HEAD_DIM
NSA_KV_WIDTH = NSA_KV_GROUPS * HEAD_DIM
CMP_BLOCK = 32
CMP_STRIDE = 16
CMP_HIDDEN = 128
SEL_BLOCK = 64
SEL_TOP = 16
SEL_LOCAL = 2
WINDOW = 512
Q_BLOCK = 128
N_BRANCH = 3
FFN_HIDDEN = ((8 * D_MODEL + 3 * 256 - 1) // (3 * 256)) * 256
RMS_EPS = 1e-6
IN_SPLITS = (SSD_INNER, SSD_CONV_DIM, SSD_HEADS, 3 * SB_WIDTH, NSA_WIDTH, 6 * NSA_KV_WIDTH, 3 * NSA_HEADS,
             N_BRANCH * D_MODEL)

V7X_LANES = 128
V7X_VMEM_LIMIT = 56 * 1024 * 1024
PAGES_PER_STEP = 16
SEQS_PER_STEP = 8
NEG = -1e30
BF16 = jnp.bfloat16
F32 = jnp.float32
HI = lax.Precision.HIGHEST


def _params(*sem):
    return pltpu.CompilerParams(dimension_semantics=sem, vmem_limit_bytes=V7X_VMEM_LIMIT)


def _pick(n, cands):
    for c in cands:
        if n % c == 0:
            return c
    return n


def _rms(x, w):
    return x * lax.rsqrt(jnp.mean(x * x, axis=-1, keepdims=True) + RMS_EPS) * w


def _softplus(x):
    return jnp.maximum(x, 0.0) + jnp.log1p(jnp.exp(-jnp.abs(x)))


def _sigmoid(x):
    return 1.0 / (1.0 + jnp.exp(-x))


def _dot_nt(a, b):
    return lax.dot_general(a, b, (((1,), (1,)), ((), ())), preferred_element_type=F32)


def _dot_tn(a, b):
    return lax.dot_general(a, b, (((0,), (0,)), ((), ())), preferred_element_type=F32)


def _norm_mm_kernel(x_ref, g_ref, w_ref, *rest, transposed):
    o_ref, h_ref = rest[-2:]

    if x_ref.dtype == BF16:
        h = x_ref[...]
    else:
        @pl.when(pl.program_id(2) == 0)
        def _():
            h_ref[...] = _rms(x_ref[...], g_ref[...]).astype(BF16)

        h = h_ref[...]
    if transposed:
        o_ref[...] = _dot_nt(w_ref[...], h).astype(o_ref.dtype)
    else:
        o_ref[...] = jnp.dot(h, w_ref[...], preferred_element_type=F32).astype(o_ref.dtype)


def _rms_cast_kernel(x_ref, g_ref, o_ref):
    o_ref[...] = _rms(x_ref[...], g_ref[...]).astype(o_ref.dtype)


def rms_cast(x, gain):
    b, t, k = x.shape
    tm = _pick(t, (1024, 512, 256, 128))
    return pl.pallas_call(
        _rms_cast_kernel,
        grid=(b, t // tm),
        in_specs=[pl.BlockSpec((None, tm, k), lambda bi, i: (bi, i, 0)), pl.BlockSpec((1, k), lambda bi, i: (0, 0))],
        out_specs=pl.BlockSpec((None, tm, k), lambda bi, i: (bi, i, 0)),
        out_shape=jax.ShapeDtypeStruct((b, t, k), BF16),
        compiler_params=_params("parallel", "parallel"),
        name="rms_cast",
    )(x, gain.reshape(1, k))


def norm_matmul(x, gain, w, out_dtype=F32, transposed=False, stack=None):
    b, t, k = x.shape
    n = w.shape[0] if transposed else w.shape[1]
    tm = _pick(t, (1024, 512, 256, 128))
    tn = _pick(n, (1024, 768, 512, 256, 128))
    assert (gain is None) == (x.dtype == BF16)
    gain = jnp.ones((k,), F32) if gain is None else gain
    operands, extra_specs, aliases = [x, gain.reshape(1, k), w], [], {}
    if stack is not None:
        assert transposed
        buf, layer, n_layers = stack
        w_spec = pl.BlockSpec((tn, k), lambda bi, i, j: (j, 0))
        o_spec = pl.BlockSpec((None, None, tn, tm), lambda bi, i, j: (layer, bi, j, i))
        o_shape = (n_layers, b, n, t)
        if buf is not None:
            operands.append(buf)
            extra_specs.append(pl.BlockSpec(memory_space=pl.ANY))
            aliases = {3: 0}
    elif transposed:
        w_spec = pl.BlockSpec((tn, k), lambda bi, i, j: (j, 0))
        o_spec = pl.BlockSpec((None, tn, tm), lambda bi, i, j: (bi, j, i))
        o_shape = (b, n, t)
    else:
        w_spec = pl.BlockSpec((k, tn), lambda bi, i, j: (0, j))
        o_spec = pl.BlockSpec((None, tm, tn), lambda bi, i, j: (bi, i, j))
        o_shape = (b, t, n)
    return pl.pallas_call(
        functools.partial(_norm_mm_kernel, transposed=transposed),
        grid=(b, t // tm, n // tn),
        in_specs=[pl.BlockSpec((None, tm, k), lambda bi, i, j: (bi, i, 0)),
                  pl.BlockSpec((1, k), lambda bi, i, j: (0, 0)),
                  w_spec] + extra_specs,
        out_specs=o_spec,
        out_shape=jax.ShapeDtypeStruct(o_shape, out_dtype),
        scratch_shapes=[pltpu.VMEM((tm, k), BF16)],
        input_output_aliases=aliases,
        compiler_params=_params("parallel", "parallel", "arbitrary"),
        name="norm_matmul_t" if transposed else "norm_matmul",
    )(*operands)


def _ssd_chunk_kernel(xbc_ref, z_ref, dt_ref, dtt_ref, h0_ref, c0_ref, cw_ref, cb_ref, dtb_ref, dtbt_ref,
                      alog_ref, alogt_ref, dfull_ref, nw_ref, exp_ref, y_ref, h_ref, xp_ref):
    q = SSD_CHUNK
    c = pl.program_id(1)

    @pl.when(c == 0)
    def _():
        h_ref[...] = h0_ref[...]
        xp_ref[5:8, :] = c0_ref[...]

    xp_ref[8:8 + q, :] = xbc_ref[...]
    conv = cb_ref[...]
    for j in range(SSD_CONV):
        conv = conv + cw_ref[j:j + 1, :] * xp_ref[5 + j:5 + j + q, :]
    xp_ref[5:8, :] = xp_ref[q + 5:q + 8, :]
    u = conv * _sigmoid(conv)
    xs = u[:, :SSD_INNER]
    bm = u[:, SSD_INNER:SSD_INNER + SSD_GN].astype(BF16)
    cm = u[:, SSD_INNER + SSD_GN:].astype(BF16)

    dt = _softplus(dt_ref[...] + dtb_ref[...])
    dtt = _softplus(dtt_ref[...] + dtbt_ref[...])
    dta = dt * (-jnp.exp(alog_ref[...]))
    dtat = dtt * (-jnp.exp(alogt_ref[...]))
    row = lax.broadcasted_iota(jnp.int32, (q, q), 0)
    col = lax.broadcasted_iota(jnp.int32, (q, q), 1)
    tril = row >= col
    acum = jnp.dot(tril.astype(F32), dta, precision=HI, preferred_element_type=F32)
    acumt = jnp.dot(dtat, (row <= col).astype(F32), precision=HI, preferred_element_type=F32)
    expand = exp_ref[...]
    dt_full = jnp.dot(dt, expand, precision=HI, preferred_element_type=F32)
    ea_full = jnp.dot(jnp.exp(acum), expand, precision=HI, preferred_element_type=F32)
    te_full = jnp.dot(jnp.exp(acum[q - 1:q, :] - acum), expand, precision=HI, preferred_element_type=F32)
    xdt = xs * dt_full
    xdt_b = xdt.astype(BF16)
    xw_b = (xdt * te_full).astype(BF16)

    r = SSD_HEADS // SSD_GROUPS
    gw = r * SSD_HEAD_DIM
    y_diag, y_off = [], []
    for g in range(SSD_GROUPS):
        cm_g = cm[:, g * SSD_STATE:(g + 1) * SSD_STATE]
        bm_g = bm[:, g * SSD_STATE:(g + 1) * SSD_STATE]
        cb = _dot_nt(cm_g, bm_g)
        h_g = h_ref[g * r:(g + 1) * r].reshape(gw, SSD_STATE)
        y_off.append(_dot_nt(cm_g, h_g.astype(BF16)))
        st = _dot_tn(xw_b[:, g * gw:(g + 1) * gw], bm_g)
        for hh in range(r):
            hd = g * r + hh
            seg = acum[:, hd:hd + 1] - acumt[hd:hd + 1, :]
            decay = jnp.exp(jnp.where(tril, seg, -jnp.inf))
            m = (cb * decay).astype(BF16)
            y_diag.append(jnp.dot(m, xdt_b[:, hd * SSD_HEAD_DIM:(hd + 1) * SSD_HEAD_DIM],
                                  preferred_element_type=F32))
            dec = jnp.exp(acumt[hd:hd + 1, q - 1:q])
            h_ref[hd] = dec * h_ref[hd] + st[hh * SSD_HEAD_DIM:(hh + 1) * SSD_HEAD_DIM, :]
    y = (jnp.concatenate(y_diag, axis=1) + jnp.concatenate(y_off, axis=1) * ea_full
         + dfull_ref[...] * xs)
    zz = z_ref[...]
    y = y * (zz * _sigmoid(zz))
    y_ref[...] = _rms(y, nw_ref[...]).astype(y_ref.dtype)


def ssd_prompt(xbc, z, dt_raw, h0, conv0, conv_w, conv_b, dt_bias, a_log, d_skip, norm_w):
    b, t, _ = xbc.shape
    q = SSD_CHUNK
    nc = t // q
    hds = SSD_HEADS
    expand = (jnp.arange(SSD_INNER)[None, :] // SSD_HEAD_DIM == jnp.arange(hds)[:, None]).astype(F32)
    d_full = jnp.repeat(d_skip, SSD_HEAD_DIM).reshape(1, SSD_INNER)
    dtt = jnp.swapaxes(dt_raw, 1, 2)
    full = lambda shape: pl.BlockSpec(shape, lambda bi, ci: (0,) * len(shape))
    y, h = pl.pallas_call(
        _ssd_chunk_kernel,
        grid=(b, nc),
        in_specs=[pl.BlockSpec((None, q, SSD_CONV_DIM), lambda bi, ci: (bi, ci, 0)),
                  pl.BlockSpec((None, q, SSD_INNER), lambda bi, ci: (bi, ci, 0)),
                  pl.BlockSpec((None, q, hds), lambda bi, ci: (bi, ci, 0)),
                  pl.BlockSpec((None, hds, q), lambda bi, ci: (bi, 0, ci)),
                  pl.BlockSpec((None, hds, SSD_HEAD_DIM, SSD_STATE), lambda bi, ci: (bi, 0, 0, 0)),
                  pl.BlockSpec((None, SSD_CONV - 1, SSD_CONV_DIM), lambda bi, ci: (bi, 0, 0)),
                  full((SSD_CONV, SSD_CONV_DIM)), full((1, SSD_CONV_DIM)),
                  full((1, hds)), full((hds, 1)), full((1, hds)), full((hds, 1)),
                  full((1, SSD_INNER)), full((1, SSD_INNER)), full((hds, SSD_INNER))],
        out_specs=[pl.BlockSpec((None, q, SSD_INNER), lambda bi, ci: (bi, ci, 0)),
                   pl.BlockSpec((None, hds, SSD_HEAD_DIM, SSD_STATE), lambda bi, ci: (bi, 0, 0, 0))],
        out_shape=[jax.ShapeDtypeStruct((b, t, SSD_INNER), BF16),
                   jax.ShapeDtypeStruct((b, hds, SSD_HEAD_DIM, SSD_STATE), F32)],
        scratch_shapes=[pltpu.VMEM((q + 8, SSD_CONV_DIM), F32)],
        compiler_params=_params("parallel", "arbitrary"),
        name="ssd_chunk_scan",
    )(xbc, z, dt_raw, dtt, h0, conv0, conv_w, conv_b.reshape(1, -1), dt_bias.reshape(1, hds),
      dt_bias.reshape(hds, 1), a_log.reshape(1, hds), a_log.reshape(hds, 1), d_full, norm_w.reshape(1, -1), expand)
    return y, h


def _sb_stage(z):
    sp = jnp.maximum(z, 0.0) + jnp.log(1.0 + jnp.exp(-jnp.abs(z)))
    hi = sp.astype(BF16)
    return hi, (sp - hi.astype(F32)).astype(BF16)


def _sb_prompt_kernel(q_ref, kt_ref, vt_ref, o_ref, z_ref, hl_ref, a_ref, *, tq, tk):
    i = pl.program_id(2)
    band = tq // tk
    assert band <= 2
    last = (i + 1) * band - 1
    q = (q_ref[...] * (HEAD_DIM ** -0.5)).astype(BF16)
    row = lax.broadcasted_iota(jnp.int32, (tk, tk), 0)
    col = lax.broadcasted_iota(jnp.int32, (tk, tk), 1)
    upper = jnp.where(row >= col, -1.0, 0.0).astype(BF16)
    upper2 = jnp.concatenate([upper, upper], axis=0)
    qpos = i * tq + lax.broadcasted_iota(jnp.int32, (tq, 1), 0)
    kcol = lax.broadcasted_iota(jnp.int32, (1, tk), 1)

    def tile(ref, s):
        off = pl.multiple_of(jnp.clip(last - s, 0, last) * tk, tk)
        return ref[:, pl.ds(off, tk)].astype(BF16)

    def masked_scores(s):
        kpos = jnp.clip(last - s, 0, last) * tk + kcol
        return jnp.where(kpos < qpos, jnp.dot(q, tile(kt_ref, s), preferred_element_type=F32), NEG)

    z0 = masked_scores(0)
    z_ref[0] = z0
    hl_ref[0, :, :tk], hl_ref[0, :, tk:] = _sb_stage(z0)
    z_ref[1] = masked_scores(1)
    a_ref[2] = jnp.zeros((tq, tk), BF16)

    def step(n, c, st):
        carry, acc = st
        nxt, prv = (c + 1) % 3, (c + 2) % 3
        later = jnp.dot(hl_ref[c], upper2, preferred_element_type=F32)
        acc = acc + _dot_nt(a_ref[prv], tile(vt_ref, n - 1))
        z_ref[prv] = jnp.dot(q, tile(kt_ref, n + 2), preferred_element_type=F32)
        hl_ref[nxt, :, :tk], hl_ref[nxt, :, tk:] = _sb_stage(z_ref[nxt])
        dead = jnp.where(n <= last, 0.0, NEG)
        a_ref[c] = jnp.exp(z_ref[c] + later + (carry + dead)).astype(BF16)
        return carry + later[:, 0:1], acc

    def body(trip, st):
        for c in range(3):
            st = step(3 * trip + c, c, st)
        return st

    st = (jnp.zeros((tq, 1), F32), jnp.zeros((tq, HEAD_DIM), F32))
    _, acc = lax.fori_loop(0, lax.div(last + 4, 3), body, st)
    o_ref[...] = acc.astype(o_ref.dtype)


def sb_prompt(q, kvt, layer=0, tq=512, tk=256):
    b, t, _ = q.shape
    h = SB_HEADS
    tq, tk = min(tq, t), min(tk, t)
    assert t % tq == 0 and tq % tk == 0
    qh = jnp.swapaxes(q.reshape(b, t, h, HEAD_DIM), 1, 2)
    out = pl.pallas_call(
        functools.partial(_sb_prompt_kernel, tq=tq, tk=tk),
        grid=(b, h, t // tq),
        in_specs=[pl.BlockSpec((None, None, tq, HEAD_DIM), lambda bi, hi, i: (bi, hi, i, 0)),
                  pl.BlockSpec((None, None, HEAD_DIM, t), lambda bi, hi, i: (layer, bi, hi, 0)),
                  pl.BlockSpec((None, None, HEAD_DIM, t), lambda bi, hi, i: (layer, bi, h + hi, 0))],
        out_specs=pl.BlockSpec((None, None, tq, HEAD_DIM), lambda bi, hi, i: (bi, hi, i, 0)),
        out_shape=jax.ShapeDtypeStruct((b, h, t, HEAD_DIM), BF16),
        scratch_shapes=[pltpu.VMEM((3, tq, tk), F32), pltpu.VMEM((3, tq, 2 * tk), BF16),
                        pltpu.VMEM((3, tq, tk), BF16)],
        compiler_params=_params("parallel", "parallel", "arbitrary"),
        name="sb_prompt",
    )(qh, kvt, kvt)
    return jnp.swapaxes(out, 1, 2).reshape(b, t, h * HEAD_DIM)


def _merge_kernel(x_ref, ssd_ref, sb_ref, nsa_ref, gl_ref, wssd_ref, wsb_ref, wnsa_ref, wo_ref, nw_ref, o_ref):
    d = D_MODEL
    gl = gl_ref[...]
    merged = (_sigmoid(gl[:, :d]) * jnp.dot(ssd_ref[...], wssd_ref[...], preferred_element_type=F32)
              + _sigmoid(gl[:, d:2 * d]) * jnp.dot(sb_ref[...], wsb_ref[...], preferred_element_type=F32)
              + _sigmoid(gl[:, 2 * d:]) * jnp.dot(nsa_ref[...], wnsa_ref[...], preferred_element_type=F32))
    y = jnp.dot(merged.astype(BF16), wo_ref[...], preferred_element_type=F32)
    o_ref[...] = x_ref[...] + _rms(y, nw_ref[...])


def merge_branches(x, ssd_y, sb_o, nsa_o, gate_logits, w_ssd_out, w_sb_out, w_nsa_out, w_o, norm_w):
    m, d = x.shape
    tm = _pick(m, (512, 256, 128, 32))
    rows = lambda n: pl.BlockSpec((tm, n), lambda i: (i, 0))
    full = lambda a: pl.BlockSpec(a.shape, lambda i: (0, 0))
    nw = norm_w.reshape(1, d)
    return pl.pallas_call(
        _merge_kernel,
        grid=(m // tm,),
        in_specs=[rows(d), rows(ssd_y.shape[1]), rows(sb_o.shape[1]), rows(nsa_o.shape[1]), rows(N_BRANCH * d),
                  full(w_ssd_out), full(w_sb_out), full(w_nsa_out), full(w_o), full(nw)],
        out_specs=rows(d),
        out_shape=jax.ShapeDtypeStruct((m, d), F32),
        compiler_params=_params("parallel"),
        name="merge_branches",
    )(x, ssd_y, sb_o, nsa_o, gate_logits, w_ssd_out, w_sb_out, w_nsa_out, w_o, nw)


def _ffn_up_kernel(x_ref, g_ref, wg_ref, wu_ref, o_ref, h_ref):
    @pl.when(pl.program_id(1) == 0)
    def _():
        h_ref[...] = _rms(x_ref[...], g_ref[...]).astype(BF16)

    h = h_ref[...]
    a = jnp.dot(h, wg_ref[...], preferred_element_type=F32)
    u = jnp.dot(h, wu_ref[...], preferred_element_type=F32)
    o_ref[...] = (a * _sigmoid(a) * u).astype(o_ref.dtype)


def _ffn_down_kernel(a_ref, x_ref, wd_ref, nw_ref, o_ref):
    f = jnp.dot(a_ref[...], wd_ref[...], preferred_element_type=F32)
    o_ref[...] = x_ref[...] + _rms(f, nw_ref[...])


def ffn(x, pre_w, post_w, w_gate, w_up, w_down):
    m, d = x.shape
    f = w_gate.shape[1]
    tm = _pick(m, (1024, 512, 256, 128, 32))
    tn = _pick(f, (256, 128))
    act = pl.pallas_call(
        _ffn_up_kernel,
        grid=(m // tm, f // tn),
        in_specs=[pl.BlockSpec((tm, d), lambda i, j: (i, 0)),
                  pl.BlockSpec((1, d), lambda i, j: (0, 0)),
                  pl.BlockSpec((d, tn), lambda i, j: (0, j)),
                  pl.BlockSpec((d, tn), lambda i, j: (0, j))],
        out_specs=pl.BlockSpec((tm, tn), lambda i, j: (i, j)),
        out_shape=jax.ShapeDtypeStruct((m, f), BF16),
        scratch_shapes=[pltpu.VMEM((tm, d), BF16)],
        compiler_params=_params("parallel", "arbitrary"),
        name="ffn_up",
    )(x, pre_w.reshape(1, d), w_gate, w_up)
    tm2 = _pick(m, (512, 256, 128, 32))
    return pl.pallas_call(
        _ffn_down_kernel,
        grid=(m // tm2,),
        in_specs=[pl.BlockSpec((tm2, f), lambda i: (i, 0)),
                  pl.BlockSpec((tm2, d), lambda i: (i, 0)),
                  pl.BlockSpec((f, d), lambda i: (0, 0)),
                  pl.BlockSpec((1, d), lambda i: (0, 0))],
        out_specs=pl.BlockSpec((tm2, d), lambda i: (i, 0)),
        out_shape=jax.ShapeDtypeStruct((m, d), F32),
        compiler_params=_params("parallel"),
        name="ffn_down",
    )(act, x, w_down, post_w.reshape(1, d))


def _compress_rows(r, pos_ref, w1_ref, w2_ref, sh_ref):
    nr = r.shape[0]
    half = CMP_STRIDE * HEAD_DIM
    top = jnp.dot((r + pos_ref[0:1, :]).astype(BF16), w1_ref[:half, :], preferred_element_type=F32)
    bot = jnp.dot((r + pos_ref[1:2, :]).astype(BF16), w1_ref[half:, :], preferred_element_type=F32)
    sh_ref[0:nr, :] = bot
    sh_ref[nr:nr + 8, :] = jnp.zeros((8, CMP_HIDDEN), F32)
    pre = top + sh_ref[1:nr + 1, :]
    hid = pre * _sigmoid(pre)
    return jnp.dot(hid.astype(BF16), w2_ref[...], preferred_element_type=F32)


def _nsa_compress_kernel(r_ref, pos_ref, w1_ref, w2_ref, o_ref, sh_ref):
    o_ref[...] = _compress_rows(r_ref[...], pos_ref, w1_ref, w2_ref, sh_ref).astype(o_ref.dtype)


def nsa_compress(rows16, cmp_pos, cmp_w1, cmp_w2):
    assert CMP_BLOCK == 2 * CMP_STRIDE
    b, _, g, nr, w = rows16.shape
    pos = cmp_pos.reshape(2, 2, w)
    return pl.pallas_call(
        _nsa_compress_kernel,
        grid=(b, 2, g),
        in_specs=[pl.BlockSpec((None, None, None, nr, w), lambda bi, ki, gi: (bi, ki, gi, 0, 0)),
                  pl.BlockSpec((None, 2, w), lambda bi, ki, gi: (ki, 0, 0)),
                  pl.BlockSpec((None, 2 * w, CMP_HIDDEN), lambda bi, ki, gi: (ki, 0, 0)),
                  pl.BlockSpec((None, CMP_HIDDEN, HEAD_DIM), lambda bi, ki, gi: (ki, 0, 0))],
        out_specs=pl.BlockSpec((None, None, None, nr, HEAD_DIM), lambda bi, ki, gi: (bi, ki, gi, 0, 0)),
        out_shape=jax.ShapeDtypeStruct((b, 2, g, nr, HEAD_DIM), BF16),
        scratch_shapes=[pltpu.VMEM((nr + 8, CMP_HIDDEN), F32)],
        compiler_params=_params("parallel", "parallel", "parallel"),
        name="nsa_compress",
    )(rows16, pos, cmp_w1.astype(BF16), cmp_w2.astype(BF16))


def _overlap_matrix(n_cmp_rows, n_cmp, n_blk):
    c = jnp.arange(n_cmp_rows)[:, None]
    n = jnp.arange(n_blk)[None, :]
    c_start, c_end = c * CMP_STRIDE, c * CMP_STRIDE + CMP_BLOCK - 1
    return ((c_start < (n + 1) * SEL_BLOCK) & (c_end >= n * SEL_BLOCK) & (c < n_cmp)).astype(BF16)


def _split_dot(x, w):
    hi = x.astype(BF16)
    lo = (x - hi.astype(F32)).astype(BF16)
    return jnp.dot(hi, w, preferred_element_type=F32) + jnp.dot(lo, w, preferred_element_type=F32)


def _top_blocks(imp, blk, n_top):
    n_blk = imp.shape[1]
    sel = jnp.zeros(imp.shape, F32)
    for _ in range(n_top):
        m = jnp.max(imp, axis=-1, keepdims=True)
        idx = jnp.min(jnp.where(imp == m, blk, float(n_blk)), axis=-1, keepdims=True)
        hit = blk == idx
        sel = jnp.where(hit, 1.0, sel)
        imp = jnp.where(hit, -jnp.inf, imp)
    return sel


def _flash_step_t(qt, k, vt, mask, m, acc):
    s = jnp.where(mask, jnp.dot(k, qt, preferred_element_type=F32), NEG)
    m_new = jnp.maximum(m, jnp.max(s, axis=0, keepdims=True))
    p = jnp.exp(s - m_new).astype(BF16)
    vt_ext = jnp.concatenate([vt, jnp.ones((8, vt.shape[1]), BF16)], axis=0)
    return m_new, acc * jnp.exp(m - m_new) + jnp.dot(vt_ext, p, preferred_element_type=F32)


def _nsa_prompt_t_kernel(qt_ref, gt_ref, kc_ref, vct_ref, ovt_ref, ks_ref, vs_ref, kw_ref, vw_ref, o_ref,
                         s_ref, ch_ref, p_ref, *, tq, n_cmp):
    i = pl.program_id(2)
    rep, d = NSA_REP, HEAD_DIM
    lanes = rep * tq
    sel_shift = int(math.log2(SEL_BLOCK))
    qt_blk = qt_ref[...]
    qt = jnp.concatenate([qt_blk[r * d:(r + 1) * d, :] for r in range(rep)], axis=1)
    qt = (qt * (d ** -0.5)).astype(BF16)
    qpos = i * tq + lax.broadcasted_iota(jnp.int32, (1, tq), 1)
    per_head = lambda a: jnp.concatenate([a] * rep, axis=1)

    n_rows = kc_ref.shape[0]
    cidx = lax.broadcasted_iota(jnp.int32, (n_rows, 1), 0)
    vis_c = per_head(((cidx * CMP_STRIDE + (CMP_BLOCK - 1) <= qpos) & (cidx < n_cmp)).astype(F32)) > 0.5
    s_c = jnp.where(vis_c, jnp.dot(kc_ref[...], qt, preferred_element_type=F32), NEG)
    e_c = jnp.where(vis_c, jnp.exp(s_c - jnp.max(s_c, axis=0, keepdims=True)), 0.0)
    p_c = e_c / jnp.maximum(jnp.sum(e_c, axis=0, keepdims=True), 1e-30)
    o_c = jnp.dot(vct_ref[...], p_c.astype(BF16), preferred_element_type=F32)

    p_sum = p_c[:, 0:tq]
    for r in range(1, rep):
        p_sum = p_sum + p_c[:, r * tq:(r + 1) * tq]
    p_hi = p_sum.astype(BF16)
    p_lo = (p_sum - p_hi.astype(F32)).astype(BF16)
    imp = (jnp.dot(ovt_ref[...], p_hi, preferred_element_type=F32)
           + jnp.dot(ovt_ref[...], p_lo, preferred_element_type=F32))
    n_blk = ovt_ref.shape[0]
    blk_i = lax.broadcasted_iota(jnp.int32, (n_blk, 1), 0)
    cur = lax.shift_right_logical(qpos, sel_shift)
    valid = blk_i <= cur
    forced = valid & ((blk_i == 0) | (blk_i > cur - SEL_LOCAL))
    imp = jnp.where(forced, jnp.inf, jnp.where(valid, imp, -jnp.inf))

    m0 = jnp.full((1, lanes), NEG, F32)
    acc0 = jnp.zeros((d + 8, lanes), F32)
    krow = lax.broadcasted_iota(jnp.int32, (tq, 1), 0)

    st = (m0, acc0)
    for n in range(WINDOW // tq + 1):
        j = i - n
        off = pl.multiple_of(jnp.maximum(j, 0) * tq, tq)
        kpos = off + krow + jnp.where(j < 0, 1 << 30, 0)
        mask = per_head(((kpos <= qpos) & (kpos > qpos - WINDOW)).astype(F32)) > 0.5
        st = _flash_step_t(qt, kw_ref[pl.ds(off, tq), :].astype(BF16), vw_ref[:, pl.ds(off, tq)].astype(BF16),
                           mask, *st)
    acc_w = st[1]

    blk_f = blk_i.astype(F32)
    sel = jnp.zeros((n_blk, tq), F32)
    for _ in range(min(SEL_TOP, n_blk)):
        top = jnp.max(imp, axis=0, keepdims=True)
        idx = jnp.min(jnp.where(imp == top, blk_f, float(n_blk)), axis=0, keepdims=True)
        hit = blk_f == idx
        sel = jnp.where(hit, 1.0, sel)
        imp = jnp.where(hit, -jnp.inf, imp)
    sel = sel.astype(BF16)

    exp_blk = lax.broadcasted_iota(jnp.int32, (tq, n_blk), 1)
    exp_key = lax.shift_right_logical(lax.broadcasted_iota(jnp.int32, (tq, n_blk), 0), sel_shift)

    def key_tile(n):
        return jnp.clip(i - n, 0, i)

    def scores(n):
        j = key_tile(n)
        expand = (exp_blk == j * (tq // SEL_BLOCK) + exp_key).astype(BF16)
        return (jnp.dot(ks_ref[pl.ds(pl.multiple_of(j * tq, tq), tq), :].astype(BF16), qt, preferred_element_type=F32),
                jnp.dot(expand, sel, preferred_element_type=F32))

    def weighted_values(n, slot):
        vt = vs_ref[:, pl.ds(pl.multiple_of(key_tile(n) * tq, tq), tq)].astype(BF16)
        return jnp.dot(jnp.concatenate([vt, jnp.ones((8, tq), BF16)], axis=0), p_ref[slot],
                       preferred_element_type=F32)

    s_ref[0], ch_ref[0] = scores(0)
    p_ref[0] = jnp.zeros(p_ref.shape[1:], BF16)

    def sel_step(n, rd, wr, st):
        m, alpha, acc = st
        pv = weighted_values(n - 1, rd)
        s_ref[wr], ch_ref[wr] = scores(n + 1)
        kpos = (i - n) * tq + krow + jnp.where(n > i, 1 << 30, 0)
        mask = per_head(jnp.where(kpos <= qpos, ch_ref[rd], 0.0)) > 0.5
        s = jnp.where(mask, s_ref[rd], NEG)
        m_new = jnp.maximum(m, jnp.max(s, axis=0, keepdims=True))
        p_ref[wr] = jnp.exp(s - m_new).astype(BF16)
        return m_new, jnp.exp(m - m_new), acc * alpha + pv

    def sel_pair(pair, st):
        return sel_step(2 * pair + 1, 1, 0, sel_step(2 * pair, 0, 1, st))

    _, _, acc_s = lax.fori_loop(0, (i + 3) >> 1, sel_pair, (m0, jnp.ones((1, lanes), F32), acc0))

    o_s = acc_s[:d] / acc_s[d:d + 1]
    o_w = acc_w[:d] / acc_w[d:d + 1]
    gate = _sigmoid(gt_ref[...])
    outs = []
    for r in range(rep):
        sl = slice(r * tq, (r + 1) * tq)
        outs.append(gate[3 * r:3 * r + 1] * o_c[:, sl] + gate[3 * r + 1:3 * r + 2] * o_s[:, sl]
                    + gate[3 * r + 2:3 * r + 3] * o_w[:, sl])
    o_ref[...] = jnp.concatenate(outs, axis=0).astype(o_ref.dtype)


def nsa_prompt_t(nsa_qt, nsa_g, nsa_kvt, cmp_pos, cmp_w1, cmp_w2, tq=128):
    b, _, t = nsa_qt.shape
    g, d, rep = NSA_KV_GROUPS, HEAD_DIM, NSA_REP
    assert t % tq == 0 and tq % SEL_BLOCK == 0 and WINDOW % tq == 0 and t % CMP_STRIDE == 0
    n_cmp = (t - CMP_BLOCK) // CMP_STRIDE + 1
    nr = t // CMP_STRIDE
    n_blk = t // SEL_BLOCK
    kinds = nsa_kvt.reshape(b, 6, g, d, t)
    rows16 = jnp.swapaxes(kinds[:, 0:2], 3, 4).reshape(b, 2, g, nr, CMP_STRIDE * d)
    kcvc = nsa_compress(rows16, cmp_pos, cmp_w1, cmp_w2)
    vct = jnp.swapaxes(kcvc[:, 1], 2, 3)
    overlap_t = _overlap_matrix(nr, n_cmp, n_blk).T
    keys = jnp.swapaxes(kinds[:, 2::2], 3, 4).astype(BF16)
    gates = jnp.transpose(nsa_g.reshape(b, t, g, 3 * rep), (0, 2, 3, 1))
    gates = jnp.pad(gates, ((0, 0), (0, 0), (0, 16 - 3 * rep), (0, 0)))
    values = lambda kind: pl.BlockSpec((None, d, t), lambda bi, gi, i: (bi, kind * g + gi, 0))
    rows = lambda kind: pl.BlockSpec((None, None, None, t, d), lambda bi, gi, i: (bi, kind, gi, 0, 0))
    out = pl.pallas_call(
        functools.partial(_nsa_prompt_t_kernel, tq=tq, n_cmp=n_cmp),
        grid=(b, g, t // tq),
        in_specs=[pl.BlockSpec((None, rep * d, tq), lambda bi, gi, i: (bi, gi, i)),
                  pl.BlockSpec((None, None, 16, tq), lambda bi, gi, i: (bi, gi, 0, i)),
                  pl.BlockSpec((None, None, None, nr, d), lambda bi, gi, i: (bi, 0, gi, 0, 0)),
                  pl.BlockSpec((None, None, d, nr), lambda bi, gi, i: (bi, gi, 0, 0)),
                  pl.BlockSpec((n_blk, nr), lambda bi, gi, i: (0, 0)),
                  rows(0), values(3), rows(1), values(5)],
        out_specs=pl.BlockSpec((None, rep * d, tq), lambda bi, gi, i: (bi, gi, i)),
        out_shape=jax.ShapeDtypeStruct((b, g * rep * d, t), BF16),
        scratch_shapes=[pltpu.VMEM((2, tq, rep * tq), F32), pltpu.VMEM((2, tq, tq), F32),
                        pltpu.VMEM((2, tq, rep * tq), BF16)],
        compiler_params=_params("parallel", "parallel", "arbitrary"),
        name="nsa_prompt",
    )(nsa_qt, gates, kcvc, vct, overlap_t, keys, nsa_kvt, keys, nsa_kvt)
    return jnp.swapaxes(out, 1, 2)


def _ssd_step_pre_kernel(x_ref, buf_ref, cw_ref, cb_ref, dt_ref, dtb_ref, alog_ref, exp_ref,
                         xs_ref, xdt_ref, bm_ref, cm_ref, dec_ref):
    conv = cb_ref[...] + cw_ref[SSD_CONV - 1:SSD_CONV, :] * x_ref[...]
    for j in range(SSD_CONV - 1):
        conv = conv + cw_ref[j:j + 1, :] * buf_ref[j]
    u = conv * _sigmoid(conv)
    xs = u[:, :SSD_INNER]
    dt = _softplus(dt_ref[...] + dtb_ref[...])
    xs_ref[...] = xs
    xdt_ref[...] = xs * jnp.dot(dt, exp_ref[...], precision=HI, preferred_element_type=F32)
    bm_ref[...] = u[:, SSD_INNER:SSD_INNER + SSD_GN]
    cm_ref[...] = u[:, SSD_INNER + SSD_GN:]
    dec_ref[...] = jnp.exp(dt * (-jnp.exp(alog_ref[...])))


def _ssd_step_state_kernel(h0_ref, xdt_ref, dec_ref, bm_ref, cm_ref, h_ref, y_ref):
    r = SSD_HEADS // SSD_GROUPS
    for b in range(h0_ref.shape[0]):
        for hd in range(SSD_HEADS):
            g = hd // r
            hn = dec_ref[b, hd] * h0_ref[b, hd] + xdt_ref[b, hd] * bm_ref[b, g]
            h_ref[b, hd] = hn
            y_ref[b, hd] = jnp.sum(hn * cm_ref[b, g], axis=-1, keepdims=True)


def _ssd_step_post_kernel(y_ref, xs_ref, z_ref, dfull_ref, nw_ref, o_ref):
    zz = z_ref[...]
    y = (y_ref[...] + dfull_ref[...] * xs_ref[...]) * (zz * _sigmoid(zz))
    o_ref[...] = _rms(y, nw_ref[...]).astype(o_ref.dtype)


def ssd_step(xbc, z, dt_raw, h0, conv_buf, conv_w, conv_b, dt_bias, a_log, d_skip, norm_w):
    b = xbc.shape[0]
    hds, p, n = SSD_HEADS, SSD_HEAD_DIM, SSD_STATE
    expand = (jnp.arange(SSD_INNER)[None, :] // p == jnp.arange(hds)[:, None]).astype(F32)
    d_full = jnp.repeat(d_skip, p).reshape(1, SSD_INNER)
    sds = lambda shape: jax.ShapeDtypeStruct(shape, F32)
    xs, xdt, bm, cm, dec = pl.pallas_call(
        _ssd_step_pre_kernel,
        out_shape=[sds((b, SSD_INNER)), sds((b, SSD_INNER)), sds((b, SSD_GN)), sds((b, SSD_GN)), sds((b, hds))],
        name="ssd_step_pre",
    )(xbc, jnp.swapaxes(conv_buf, 0, 1), conv_w, conv_b.reshape(1, -1), dt_raw, dt_bias.reshape(1, hds),
      a_log.reshape(1, hds), expand)
    n_seq = 1
    per_b = lambda *dims: pl.BlockSpec((n_seq,) + dims, lambda bi: (bi,) + (0,) * len(dims))
    h_new, y_col = pl.pallas_call(
        _ssd_step_state_kernel,
        grid=(b // n_seq,),
        in_specs=[per_b(hds, p, n), per_b(hds, p, 1), per_b(hds, 1, 1), per_b(SSD_GROUPS, 1, n),
                  per_b(SSD_GROUPS, 1, n)],
        out_specs=[per_b(hds, p, n), per_b(hds, p, 1)],
        out_shape=[sds((b, hds, p, n)), sds((b, hds, p, 1))],
        compiler_params=_params("parallel"),
        name="ssd_step_state",
    )(h0, xdt.reshape(b, hds, p, 1), dec.reshape(b, hds, 1, 1), bm.reshape(b, SSD_GROUPS, 1, n),
      cm.reshape(b, SSD_GROUPS, 1, n))
    y = pl.pallas_call(
        _ssd_step_post_kernel,
        out_shape=jax.ShapeDtypeStruct((b, SSD_INNER), BF16),
        name="ssd_step_post",
    )(y_col.reshape(b, SSD_INNER), xs, z, d_full, norm_w.reshape(1, -1))
    return y, h_new


def _sb_decode_kernel(pt_ref, q_ref, *refs):
    page_refs, (o_ref, carry_ref, acc_ref) = refs[:PAGES_PER_STEP], refs[PAGES_PER_STEP:]
    p = pl.program_id(1)
    tk = page_refs[0].shape[-1]
    hds = SB_HEADS

    @pl.when(p == 0)
    def _():
        carry_ref[...] = jnp.zeros(carry_ref.shape, F32)
        acc_ref[...] = jnp.zeros(acc_ref.shape, F32)

    scale = HEAD_DIM ** -0.5
    qs = [q_ref[h] * scale for h in range(hds)]
    z = jnp.concatenate([jnp.sum(ref[0, h] * qs[h], axis=0, keepdims=True)
                         for ref in page_refs for h in range(hds)], axis=0)
    row = lax.broadcasted_iota(jnp.int32, (tk, tk), 0)
    col = lax.broadcasted_iota(jnp.int32, (tk, tk), 1)
    sp = _softplus(z)
    later = _split_dot(-sp, (row > col).astype(BF16))
    total = later[:, 0:1] - sp[:, 0:1]
    carry = carry_ref[...]
    for k, ref in enumerate(page_refs):
        rows = slice(k * hds, (k + 1) * hds)
        a = jnp.exp(z[rows] - sp[rows] + later[rows] + carry)
        carry = carry + total[rows]
        for h in range(hds):
            acc_ref[h] += ref[1, h] * a[h:h + 1, :]
    carry_ref[...] = carry

    @pl.when(p == pl.num_programs(1) - 1)
    def _():
        for h in range(hds):
            o_ref[h] = jnp.sum(acc_ref[h], axis=-1, keepdims=True)


def _page_specs(block, layer, n_pages, kind_block, descending):
    def spec(k):
        def index(bi, p, pt):
            pos = p * PAGES_PER_STEP + k
            pos = n_pages - 1 - pos if descending else pos
            return (layer, pt[bi, pos], kind_block) + (0,) * (len(block) - 3)
        return pl.BlockSpec(block, index)
    return [spec(k) for k in range(PAGES_PER_STEP)]


def sb_decode(q, pool_t, layer, page_table):
    b = q.shape[0]
    h, d = SB_HEADS, HEAD_DIM
    n_pages = page_table.shape[1]
    page = pool_t.shape[-1]
    assert n_pages % PAGES_PER_STEP == 0
    out = pl.pallas_call(
        _sb_decode_kernel,
        grid_spec=pltpu.PrefetchScalarGridSpec(
            num_scalar_prefetch=1,
            grid=(b, n_pages // PAGES_PER_STEP),
            in_specs=[pl.BlockSpec((None, h, d, 1), lambda bi, p, pt: (bi, 0, 0, 0))]
            + _page_specs((None, None, 2, h, d, page), layer, n_pages, 0, descending=True),
            out_specs=pl.BlockSpec((None, h, d, 1), lambda bi, p, pt: (bi, 0, 0, 0)),
            scratch_shapes=[pltpu.VMEM((h, 1), F32), pltpu.VMEM((h, d, page), F32)]),
        out_shape=jax.ShapeDtypeStruct((b, h, d, 1), F32),
        compiler_params=_params("parallel", "arbitrary"),
        name="sb_decode",
    )(page_table, q.reshape(b, h, d, 1), *([pool_t] * PAGES_PER_STEP))
    return out.reshape(b, h * d)


def _nsa_gather_compress_kernel(pt_ref, *refs):
    page_refs = refs[:PAGES_PER_STEP]
    pos_ref, w1_ref, w2_ref, o_ref, x_ref, r_ref, sh_ref = refs[PAGES_PER_STEP:]
    p = pl.program_id(1)
    page = page_refs[0].shape[-1]
    grp, d = NSA_KV_GROUPS, HEAD_DIM
    n_out = page // CMP_STRIDE
    for k, ref in enumerate(page_refs):
        row0 = pl.multiple_of((p * PAGES_PER_STEP + k) * n_out, n_out)
        for kind in range(2):
            x = x_ref.at[2 * k + kind]
            x[...] = ref[kind].reshape(grp * d, page).T
            steps = [x[pl.ds(s, n_out, stride=CMP_STRIDE), :] for s in range(CMP_STRIDE)]
            for g in range(grp):
                r_ref[kind, g, pl.ds(row0, n_out), :] = jnp.concatenate(
                    [st[:, g * d:(g + 1) * d] for st in steps], axis=1)

    @pl.when(p == pl.num_programs(1) - 1)
    def _():
        for kind in range(2):
            for g in range(grp):
                o_ref[kind, g] = _compress_rows(r_ref[kind, g], pos_ref.at[kind], w1_ref.at[kind], w2_ref.at[kind],
                                                sh_ref).astype(o_ref.dtype)


def nsa_gather_compress(pool_t, layer, page_table, cmp_pos, cmp_w1, cmp_w2):
    b, n_pages = page_table.shape
    g, d = NSA_KV_GROUPS, HEAD_DIM
    page = pool_t.shape[-1]
    assert page % CMP_STRIDE == 0 and n_pages % PAGES_PER_STEP == 0 and CMP_BLOCK == 2 * CMP_STRIDE
    nr = n_pages * (page // CMP_STRIDE)
    w = CMP_STRIDE * d
    whole = lambda shape: pl.BlockSpec(shape, lambda bi, p, pt: (0,) * len(shape))
    return pl.pallas_call(
        _nsa_gather_compress_kernel,
        grid_spec=pltpu.PrefetchScalarGridSpec(
            num_scalar_prefetch=1,
            grid=(b, n_pages // PAGES_PER_STEP),
            in_specs=_page_specs((None, None, 2, g, d, page), layer, n_pages, 0, descending=False)
            + [whole((2, 2, w)), whole((2, 2 * w, CMP_HIDDEN)), whole((2, CMP_HIDDEN, d))],
            out_specs=pl.BlockSpec((None, 2, g, nr, d), lambda bi, p, pt: (bi, 0, 0, 0, 0)),
            scratch_shapes=[pltpu.VMEM((2 * PAGES_PER_STEP, page, g * d), F32), pltpu.VMEM((2, g, nr, w), F32),
                            pltpu.VMEM((nr + 8, CMP_HIDDEN), F32)]),
        out_shape=jax.ShapeDtypeStruct((b, 2, g, nr, d), BF16),
        compiler_params=_params("parallel", "arbitrary"),
        name="nsa_gather_compress",
    )(page_table, *([pool_t] * PAGES_PER_STEP), cmp_pos.reshape(2, 2, w), cmp_w1.astype(BF16), cmp_w2.astype(BF16))


def _nsa_decode_select_kernel(q_ref, kc_ref, vc_ref, ov_ref, oc_ref, sel_ref, *, n_cmp, q_pos):
    rep, d, grp = NSA_REP, HEAD_DIM, NSA_KV_GROUPS
    n_seq = q_ref.shape[0]
    n_rows = kc_ref.shape[2]
    n_blk = ov_ref.shape[1]
    cidx = lax.broadcasted_iota(jnp.int32, (1, n_rows), 1)
    vis = (cidx * CMP_STRIDE + (CMP_BLOCK - 1) <= q_pos) & (cidx < n_cmp)
    imp = []
    for b in range(n_seq):
        q = (q_ref[b] * (d ** -0.5)).astype(BF16)
        o_c = []
        for g in range(grp):
            s = jnp.where(vis, _dot_nt(q[g * rep:(g + 1) * rep], kc_ref[b, g]), NEG)
            e = jnp.where(vis, jnp.exp(s - jnp.max(s, axis=-1, keepdims=True)), 0.0)
            p = e / jnp.maximum(jnp.sum(e, axis=-1, keepdims=True), 1e-30)
            o_c.append(jnp.dot(p.astype(BF16), vc_ref[b, g], preferred_element_type=F32))
            imp.append(_split_dot(jnp.sum(p, axis=0, keepdims=True), ov_ref[...]))
        oc_ref[b] = jnp.concatenate(o_c, axis=0)
    imp = jnp.concatenate(imp, axis=0)
    blk_i = lax.broadcasted_iota(jnp.int32, (1, n_blk), 1)
    forced = (blk_i == 0) | (blk_i > n_blk - SEL_LOCAL)
    sel = _top_blocks(jnp.where(forced, jnp.inf, imp), blk_i.astype(F32), min(SEL_TOP - 1, n_blk))
    pad = jnp.zeros((sel_ref.shape[1] - grp, n_blk), F32)
    for b in range(n_seq):
        sel_ref[b] = jnp.concatenate([sel[b * grp:(b + 1) * grp], pad], axis=0)


def _nsa_decode_attend_kernel(pt_ref, q_ref, sel_ref, *refs, win_skip):
    page_refs = refs[:PAGES_PER_STEP]
    new_ref, win_ref, oc_ref, gt_ref, o_ref, m_ref, acc_ref = refs[PAGES_PER_STEP:]
    p = pl.program_id(1)
    n_steps = pl.num_programs(1)
    rep, d, grp = NSA_REP, HEAD_DIM, NSA_KV_GROUPS
    tk = page_refs[0].shape[-1]
    lanes = PAGES_PER_STEP * tk
    scale = d ** -0.5
    qf = q_ref[...] * scale
    q = qf.astype(BF16)
    new = new_ref[...]
    new_row = lambda kind, g: new[kind * grp + g:kind * grp + g + 1, :]
    per_head = lambda f: jnp.concatenate([f(g) for g in range(grp)], axis=0)

    @pl.when(p == 0)
    def _():
        m_ref[...] = per_head(lambda g: jnp.sum(qf[g * rep:(g + 1) * rep] * new_row(2, g), axis=-1, keepdims=True))
        acc_ref[...] = per_head(lambda g: jnp.concatenate(
            [jnp.broadcast_to(new_row(3, g), (rep, d)), jnp.ones((rep, d), F32)], axis=1))

    n_blk = sel_ref.shape[1]
    lane = lax.broadcasted_iota(jnp.int32, (n_blk, lanes), 1)
    page_pos = (n_steps - p) * PAGES_PER_STEP - 1 - lax.shift_right_logical(lane, int(math.log2(tk)))
    blk_of_lane = page_pos * (tk // SEL_BLOCK) + lax.shift_right_logical(lane & (tk - 1), int(math.log2(SEL_BLOCK)))
    expand = (lax.broadcasted_iota(jnp.int32, (n_blk, lanes), 0) == blk_of_lane).astype(BF16)
    chosen = jnp.dot(sel_ref[...].astype(BF16), expand, preferred_element_type=F32)

    def scores(g):
        qg = q[g * rep:(g + 1) * rep]
        sg = jnp.concatenate([jnp.dot(qg, ref[0, g].astype(BF16), preferred_element_type=F32) for ref in page_refs],
                             axis=1)
        return jnp.where(chosen[g:g + 1, :] > 0.5, sg, NEG)

    s = per_head(scores)
    m_old = m_ref[...]
    m_new = jnp.maximum(m_old, jnp.max(s, axis=-1, keepdims=True))
    pr = jnp.exp(s - m_new).astype(BF16)

    def weighted_values(g):
        out = jnp.zeros((rep, 2 * d), F32)
        for k, ref in enumerate(page_refs):
            vt_ext = jnp.concatenate([ref[1, g].astype(BF16), jnp.ones((d, tk), BF16)], axis=0)
            out = out + _dot_nt(pr[g * rep:(g + 1) * rep, k * tk:(k + 1) * tk], vt_ext)
        return out

    acc_ref[...] = acc_ref[...] * jnp.exp(m_old - m_new) + per_head(weighted_values)
    m_ref[...] = m_new

    @pl.when(p == n_steps - 1)
    def _():
        acc = acc_ref[...]
        o_s = acc[:, :d] / acc[:, d:]
        wlen = win_ref.shape[-1]
        vis = lax.broadcasted_iota(jnp.int32, (1, wlen), 1) >= win_skip

        def window(g):
            qg = q[g * rep:(g + 1) * rep]
            s_w = jnp.where(vis, jnp.dot(qg, win_ref[0, g].astype(BF16), preferred_element_type=F32), NEG)
            s_n = jnp.sum(qf[g * rep:(g + 1) * rep] * new_row(4, g), axis=-1, keepdims=True)
            mx = jnp.maximum(jnp.max(s_w, axis=-1, keepdims=True), s_n)
            e_w = jnp.where(vis, jnp.exp(s_w - mx), 0.0)
            e_n = jnp.exp(s_n - mx)
            num = _dot_nt(e_w.astype(BF16), win_ref[1, g].astype(BF16)) + e_n * new_row(5, g)
            return num / (jnp.sum(e_w, axis=-1, keepdims=True) + e_n)

        o_w = per_head(window)
        gate = _sigmoid(gt_ref[...])
        o_ref[...] = gate[:, 0:1] * oc_ref[...] + gate[:, 1:2] * o_s + gate[:, 2:3] * o_w


def nsa_decode(nsa_q, nsa_g, nsa_kv_new, pool_t, win_t, layer, page_table, cmp_pos, cmp_w1, cmp_w2):
    b = nsa_q.shape[0]
    g, d, rep, hds = NSA_KV_GROUPS, HEAD_DIM, NSA_REP, NSA_HEADS
    n_pages = page_table.shape[1]
    page = pool_t.shape[-1]
    past = n_pages * page
    wlen = win_t.shape[-1]
    assert past % SEL_BLOCK == 0 and past % CMP_STRIDE == 0 and page % SEL_BLOCK == 0 and wlen <= past
    n_cmp = (past + 1 - CMP_BLOCK) // CMP_STRIDE + 1
    nr = past // CMP_STRIDE
    n_blk = past // SEL_BLOCK
    kcvc = nsa_gather_compress(pool_t, layer, page_table, cmp_pos, cmp_w1, cmp_w2)
    overlap = _overlap_matrix(nr, n_cmp, n_blk)
    q3 = nsa_q.reshape(b, hds, d)
    n_seq = _pick(b, (SEQS_PER_STEP, 1))
    o_c, sel = pl.pallas_call(
        functools.partial(_nsa_decode_select_kernel, n_cmp=n_cmp, q_pos=past),
        grid=(b // n_seq,),
        in_specs=[pl.BlockSpec((n_seq, hds, d), lambda bi: (bi, 0, 0)),
                  pl.BlockSpec((n_seq, None, g, nr, d), lambda bi: (bi, 0, 0, 0, 0)),
                  pl.BlockSpec((n_seq, None, g, nr, d), lambda bi: (bi, 1, 0, 0, 0)),
                  pl.BlockSpec((nr, n_blk), lambda bi: (0, 0))],
        out_specs=[pl.BlockSpec((n_seq, hds, d), lambda bi: (bi, 0, 0)),
                   pl.BlockSpec((n_seq, 8, n_blk), lambda bi: (bi, 0, 0))],
        out_shape=[jax.ShapeDtypeStruct((b, hds, d), F32), jax.ShapeDtypeStruct((b, 8, n_blk), F32)],
        compiler_params=_params("parallel"),
        name="nsa_decode_select",
    )(q3, kcvc, kcvc, overlap)
    gates = jnp.pad(nsa_g.reshape(b, hds, 3), ((0, 0), (0, 0), (0, V7X_LANES - 3)))
    fixed = lambda *dims: pl.BlockSpec((None,) + dims, lambda bi, p, pt: (bi,) + (0,) * len(dims))
    out = pl.pallas_call(
        functools.partial(_nsa_decode_attend_kernel, win_skip=wlen - WINDOW + 1),
        grid_spec=pltpu.PrefetchScalarGridSpec(
            num_scalar_prefetch=1,
            grid=(b, n_pages // PAGES_PER_STEP),
            in_specs=[fixed(hds, d), fixed(8, n_blk)]
            + _page_specs((None, None, 2, g, d, page), layer, n_pages, 1, descending=True)
            + [fixed(6 * g, d),
               pl.BlockSpec((None, None, 2, g, d, wlen), lambda bi, p, pt: (layer, bi, 0, 0, 0, 0)),
               fixed(hds, d), fixed(hds, V7X_LANES)],
            out_specs=fixed(hds, d),
            scratch_shapes=[pltpu.VMEM((hds, 1), F32), pltpu.VMEM((hds, 2 * d), F32)]),
        out_shape=jax.ShapeDtypeStruct((b, hds, d), F32),
        compiler_params=_params("parallel", "arbitrary"),
        name="nsa_decode_attend",
    )(page_table, q3, sel, *([pool_t] * PAGES_PER_STEP), nsa_kv_new.reshape(b, 6 * g, d), win_t, o_c, gates)
    return out.reshape(b, hds * d)


def _col_offsets():
    offs, s = [], 0
    for n in IN_SPLITS:
        offs.append(s)
        s += n
    return offs


def _layer_weights(l, p):
    o = _col_offsets()
    w_in = p['w_in'][l]
    cols = lambda a, n: w_in[:, a:a + n]
    small = jnp.concatenate([cols(o[2], SSD_HEADS), cols(o[6], 3 * NSA_HEADS)], axis=1)
    small = jnp.pad(small, ((0, 0), (0, V7X_LANES - small.shape[1])))
    bf = lambda a: a.astype(BF16)
    return {
        'w_z': bf(cols(o[0], SSD_INNER)), 'w_xbc': bf(cols(o[1], SSD_CONV_DIM)), 'w_small': bf(small),
        'w_sbq': bf(cols(o[3], SB_WIDTH)), 'w_sbkv': bf(cols(o[3] + SB_WIDTH, 2 * SB_WIDTH)),
        'w_nq': bf(cols(o[4], NSA_WIDTH)), 'w_nkv': bf(cols(o[5], 6 * NSA_KV_WIDTH)),
        'w_brg': bf(cols(o[7], N_BRANCH * D_MODEL)),
        'mix_pre': p['norm_mix_pre'][l], 'mix_post': p['norm_mix_post'][l],
        'ffn_pre': p['norm_ffn_pre'][l], 'ffn_post': p['norm_ffn_post'][l],
        'conv_w': p['ssd_conv_w'][l], 'conv_b': p['ssd_conv_b'][l], 'dt_bias': p['ssd_dt_bias'][l],
        'a_log': p['ssd_a_log'][l], 'd_skip': p['ssd_d'][l], 'ssd_norm': p['ssd_norm'][l],
        'w_ssd_out': bf(p['w_ssd_out'][l]), 'w_sb_out': bf(p['w_sb_out'][l]), 'w_nsa_out': bf(p['w_nsa_out'][l]),
        'w_o': bf(p['w_o'][l]), 'w_ffn_gate': bf(p['w_ffn_gate'][l]), 'w_ffn_up': bf(p['w_ffn_up'][l]),
        'w_ffn_down': bf(p['w_ffn_down'][l]),
        'cmp_pos': p['nsa_cmp_pos'][l], 'cmp_w1': p['nsa_cmp_w1'][l], 'cmp_w2': p['nsa_cmp_w2'][l],
    }


def _trunk_tail(x, ssd_y, sb_o, nsa_o, br_g, lw):
    b, t, d = x.shape
    m = b * t
    x1 = merge_branches(x.reshape(m, d), ssd_y.reshape(m, -1), sb_o.reshape(m, -1), nsa_o.reshape(m, -1),
                        br_g.reshape(m, -1), lw['w_ssd_out'], lw['w_sb_out'], lw['w_nsa_out'], lw['w_o'],
                        lw['mix_post'])
    x2 = ffn(x1, lw['ffn_pre'], lw['ffn_post'], lw['w_ffn_gate'], lw['w_ffn_up'], lw['w_ffn_down'])
    return x2.reshape(b, t, d)


def _layer_prompt(x, lw, layer, n_layers, sb_kv_all):
    b, t, _ = x.shape
    h = rms_cast(x, lw['mix_pre'])
    z = norm_matmul(h, None, lw['w_z'])
    xbc = norm_matmul(h, None, lw['w_xbc'])
    small = norm_matmul(h, None, lw['w_small'])
    br_g = norm_matmul(h, None, lw['w_brg'])
    sb_q = norm_matmul(h, None, lw['w_sbq'], out_dtype=BF16)
    nsa_qt = norm_matmul(h, None, lw['w_nq'].T, transposed=True)
    sb_kv_all = norm_matmul(h, None, lw['w_sbkv'].T, transposed=True, stack=(sb_kv_all, layer, n_layers))
    nsa_kvt = norm_matmul(h, None, lw['w_nkv'].T, transposed=True)
    dt_raw = small[..., :SSD_HEADS]
    nsa_g = small[..., SSD_HEADS:SSD_HEADS + 3 * NSA_HEADS]

    h0 = jnp.zeros((b, SSD_HEADS, SSD_HEAD_DIM, SSD_STATE), F32)
    conv0 = jnp.zeros((b, SSD_CONV - 1, SSD_CONV_DIM), F32)
    ssd_y, h_new = ssd_prompt(xbc, z, dt_raw, h0, conv0, lw['conv_w'], lw['conv_b'], lw['dt_bias'], lw['a_log'],
                              lw['d_skip'], lw['ssd_norm'])
    conv_new = xbc[:, t - (SSD_CONV - 1):, :]
    sb_o = sb_prompt(sb_q, sb_kv_all, layer)

    nsa_o = nsa_prompt_t(nsa_qt, nsa_g, nsa_kvt, lw['cmp_pos'], lw['cmp_w1'], lw['cmp_w2'])

    y = _trunk_tail(x, ssd_y, sb_o, nsa_o, br_g, lw)
    nsa_all = jnp.moveaxis(nsa_kvt.reshape(b, 6, NSA_KV_GROUPS, HEAD_DIM, t), 4, 1)
    keep = min(WINDOW, t)
    return y, sb_kv_all, nsa_all[:, :, 0:4], nsa_all[:, t - keep:, 4:6], h_new, conv_new


def _layer_sample(x, lw, layer, sb_pool_t, nsa_pool_t, win_t, h0, conv_buf, page_table):
    bsz, t = x.shape[:2]
    assert t == 1
    past = page_table.shape[1] * sb_pool_t.shape[-1]
    xr = x.reshape(1, bsz, D_MODEL)
    g = lw['mix_pre']
    pr = lambda w: norm_matmul(xr, g, w)[0]
    z, xbc, small, br_g = pr(lw['w_z']), pr(lw['w_xbc']), pr(lw['w_small']), pr(lw['w_brg'])
    sb_q, sb_kv, nsa_q, nsa_kv = pr(lw['w_sbq']), pr(lw['w_sbkv']), pr(lw['w_nq']), pr(lw['w_nkv'])
    dt_raw = small[:, :SSD_HEADS]
    nsa_g = small[:, SSD_HEADS:SSD_HEADS + 3 * NSA_HEADS]
    ssd_y, h_new = ssd_step(xbc, z, dt_raw, h0, conv_buf, lw['conv_w'], lw['conv_b'], lw['dt_bias'], lw['a_log'],
                            lw['d_skip'], lw['ssd_norm'])
    conv_new = jnp.concatenate([conv_buf[:, 1:], xbc[:, None, :]], axis=1)
    sb_o = sb_decode(sb_q, sb_pool_t, layer, page_table)
    nsa_o = nsa_decode(nsa_q, nsa_g, nsa_kv, nsa_pool_t, win_t, layer, page_table, lw['cmp_pos'], lw['cmp_w1'],
                       lw['cmp_w2'])
    y = _trunk_tail(x, ssd_y[:, None], sb_o.astype(BF16)[:, None], nsa_o.astype(BF16)[:, None], br_g[:, None], lw)
    kv_new = sb_kv.reshape(bsz, 1, 2, SB_HEADS, HEAD_DIM)
    nkv = nsa_kv.reshape(bsz, 1, 6, NSA_KV_GROUPS, HEAD_DIM)
    keep = min(WINDOW, past + 1)
    win_all_t = jnp.concatenate([win_t[layer], nkv[:, 0, 4:6][..., None]], axis=-1)
    win_new = jnp.moveaxis(win_all_t[..., win_all_t.shape[-1] - keep:], 4, 1)
    return y, kv_new, nkv[:, :, 0:4], win_new, h_new, conv_new


def kernel(x_prompt, x_sample, cache_sb_kv, cache_nsa_kv, cache_nsa_win, state_ssd, state_conv, page_table,
           norm_mix_pre, norm_mix_post, norm_ffn_pre, norm_ffn_post, w_in, ssd_conv_w, ssd_conv_b, ssd_dt_bias,
           ssd_a_log, ssd_d, ssd_norm, w_ssd_out, w_sb_out, nsa_cmp_pos, nsa_cmp_w1, nsa_cmp_w2, w_nsa_out, w_o,
           w_ffn_gate, w_ffn_up, w_ffn_down):
    p = dict(norm_mix_pre=norm_mix_pre, norm_mix_post=norm_mix_post, norm_ffn_pre=norm_ffn_pre,
             norm_ffn_post=norm_ffn_post, w_in=w_in, ssd_conv_w=ssd_conv_w, ssd_conv_b=ssd_conv_b,
             ssd_dt_bias=ssd_dt_bias, ssd_a_log=ssd_a_log, ssd_d=ssd_d, ssd_norm=ssd_norm, w_ssd_out=w_ssd_out,
             w_sb_out=w_sb_out, nsa_cmp_pos=nsa_cmp_pos, nsa_cmp_w1=nsa_cmp_w1, nsa_cmp_w2=nsa_cmp_w2,
             w_nsa_out=w_nsa_out, w_o=w_o, w_ffn_gate=w_ffn_gate, w_ffn_up=w_ffn_up, w_ffn_down=w_ffn_down)
    yp, ys = x_prompt, x_sample
    outs_p, outs_s = [], []
    time_minor = lambda a: jnp.transpose(a, (0, 1, 3, 4, 5, 2))
    sb_pool_t, nsa_pool_t, win_t = time_minor(cache_sb_kv), time_minor(cache_nsa_kv), time_minor(cache_nsa_win)
    n_layers = w_in.shape[0]
    sb_kv_all = None
    for l in range(n_layers):
        lw = _layer_weights(l, p)
        res = _layer_prompt(yp, lw, l, n_layers, sb_kv_all)
        yp, sb_kv_all = res[0], res[1]
        outs_p.append(res[1:])
        res = _layer_sample(ys, lw, l, sb_pool_t, nsa_pool_t, win_t, state_ssd[l], state_conv[l], page_table)
        ys = res[0]
        outs_s.append(res[1:])
    st = lambda outs, i: jnp.stack([o[i] for o in outs])
    bp, tp = x_prompt.shape[:2]
    sb_kv_p = jnp.moveaxis(sb_kv_all.reshape(n_layers, bp, 2, SB_HEADS, HEAD_DIM, tp), 5, 2)
    return (yp, ys, sb_kv_p, st(outs_s, 0), st(outs_p, 1), st(outs_s, 1), st(outs_p, 2), st(outs_s, 2),
            st(outs_p, 3), st(outs_s, 3), st(outs_p, 4), st(outs_s, 4))
```

```python
import functools
import math

import jax
import jax.numpy as jnp
from jax import lax
from jax.experimental import pallas as pl
from jax.experimental.pallas import tpu as pltpu

D_MODEL = 1024
HEAD_DIM = 64
SSD_INNER = D_MODEL
SSD_HEAD_DIM = 64
SSD_HEADS = SSD_INNER // SSD_HEAD_DIM
SSD_GROUPS = 2
SSD_STATE = 128
SSD_CONV = 4
SSD_GN = SSD_GROUPS * SSD_STATE
SSD_CONV_DIM = SSD_INNER + 2 * SSD_GN
SSD_CHUNK = 128
SB_HEADS = 8
SB_WIDTH = SB_HEADS * HEAD_DIM
NSA_HEADS = 8
NSA_KV_GROUPS = 2
NSA_REP = NSA_HEADS // NSA_KV_GROUPS
NSA_WIDTH = NSA_HEADS * HEAD_DIM
NSA_KV_WIDTH = NSA_KV_GROUPS * HEAD_DIM
CMP_BLOCK = 32
CMP_STRIDE = 16
CMP_HIDDEN = 128
SEL_BLOCK = 64
SEL_TOP = 16
SEL_LOCAL = 2
WINDOW = 512
Q_BLOCK = 128
N_BRANCH = 3
FFN_HIDDEN = ((8 * D_MODEL + 3 * 256 - 1) // (3 * 256)) * 256
RMS_EPS = 1e-6
IN_SPLITS = (SSD_INNER, SSD_CONV_DIM, SSD_HEADS, 3 * SB_WIDTH, NSA_WIDTH, 6 * NSA_KV_WIDTH, 3 * NSA_HEADS,
             N_BRANCH * D_MODEL)

V7X_LANES = 128
V7X_VMEM_LIMIT = 56 * 1024 * 1024
PAGES_PER_STEP = 32
SEQS_PER_STEP = 8
NEG = -1e30
BF16 = jnp.bfloat16
F32 = jnp.float32
HI = lax.Precision.HIGHEST


def _params(*sem):
    return pltpu.CompilerParams(dimension_semantics=sem, vmem_limit_bytes=V7X_VMEM_LIMIT)


def _pick(n, cands):
    for c in cands:
        if n % c == 0:
            return c
    return n


def _rms(x, w):
    return x * lax.rsqrt(jnp.mean(x * x, axis=-1, keepdims=True) + RMS_EPS) * w


def _softplus(x):
    return jnp.maximum(x, 0.0) + jnp.log1p(jnp.exp(-jnp.abs(x)))


def _sigmoid(x):
    return 1.0 / (1.0 + jnp.exp(-x))


def _dot_nt(a, b):
    return lax.dot_general(a, b, (((1,), (1,)), ((), ())), preferred_element_type=F32)


def _dot_tn(a, b):
    return lax.dot_general(a, b, (((0,), (0,)), ((), ())), preferred_element_type=F32)


def _norm_mm_kernel(x_ref, g_ref, w_ref, *rest, transposed):
    o_ref, h_ref = rest[-2:]

    if x_ref.dtype == BF16:
        h = x_ref[...]
    else:
        @pl.when(pl.program_id(2) == 0)
        def _():
            h_ref[...] = _rms(x_ref[...], g_ref[...]).astype(BF16)

        h = h_ref[...]
    if transposed:
        o_ref[...] = _dot_nt(w_ref[...], h).astype(o_ref.dtype)
    else:
        o_ref[...] = jnp.dot(h, w_ref[...], preferred_element_type=F32).astype(o_ref.dtype)


def _rms_cast_kernel(x_ref, g_ref, o_ref):
    o_ref[...] = _rms(x_ref[...], g_ref[...]).astype(o_ref.dtype)


def rms_cast(x, gain):
    b, t, k = x.shape
    tm = _pick(t, (1024, 512, 256, 128))
    return pl.pallas_call(
        _rms_cast_kernel,
        grid=(b, t // tm),
        in_specs=[pl.BlockSpec((None, tm, k), lambda bi, i: (bi, i, 0)), pl.BlockSpec((1, k), lambda bi, i: (0, 0))],
        out_specs=pl.BlockSpec((None, tm, k), lambda bi, i: (bi, i, 0)),
        out_shape=jax.ShapeDtypeStruct((b, t, k), BF16),
        compiler_params=_params("parallel", "parallel"),
        name="rms_cast",
    )(x, gain.reshape(1, k))


def norm_matmul(x, gain, w, out_dtype=F32, transposed=False, stack=None):
    b, t, k = x.shape
    n = w.shape[0] if transposed else w.shape[1]
    tm = _pick(t, (1024, 512, 256, 128))
    tn = _pick(n, (1024, 768, 512, 256, 128))
    assert (gain is None) == (x.dtype == BF16)
    gain = jnp.ones((k,), F32) if gain is None else gain
    operands, extra_specs, aliases = [x, gain.reshape(1, k), w], [], {}
    if stack is not None:
        assert transposed
        buf, layer, n_layers = stack
        w_spec = pl.BlockSpec((tn, k), lambda bi, i, j: (j, 0))
        o_spec = pl.BlockSpec((None, None, tn, tm), lambda bi, i, j: (layer, bi, j, i))
        o_shape = (n_layers, b, n, t)
        if buf is not None:
            operands.append(buf)
            extra_specs.append(pl.BlockSpec(memory_space=pl.ANY))
            aliases = {3: 0}
    elif transposed:
        w_spec = pl.BlockSpec((tn, k), lambda bi, i, j: (j, 0))
        o_spec = pl.BlockSpec((None, tn, tm), lambda bi, i, j: (bi, j, i))
        o_shape = (b, n, t)
    else:
        w_spec = pl.BlockSpec((k, tn), lambda bi, i, j: (0, j))
        o_spec = pl.BlockSpec((None, tm, tn), lambda bi, i, j: (bi, i, j))
        o_shape = (b, t, n)
    return pl.pallas_call(
        functools.partial(_norm_mm_kernel, transposed=transposed),
        grid=(b, t // tm, n // tn),
        in_specs=[pl.BlockSpec((None, tm, k), lambda bi, i, j: (bi, i, 0)),
                  pl.BlockSpec((1, k), lambda bi, i, j: (0, 0)),
                  w_spec] + extra_specs,
        out_specs=o_spec,
        out_shape=jax.ShapeDtypeStruct(o_shape, out_dtype),
        scratch_shapes=[pltpu.VMEM((tm, k), BF16)],
        input_output_aliases=aliases,
        compiler_params=_params("parallel", "parallel", "arbitrary"),
        name="norm_matmul_t" if transposed else "norm_matmul",
    )(*operands)


def _ssd_chunk_kernel(xbc_ref, z_ref, dt_ref, dtt_ref, h0_ref, c0_ref, cw_ref, cb_ref, dtb_ref, dtbt_ref,
                      alog_ref, alogt_ref, dfull_ref, nw_ref, exp_ref, y_ref, h_ref, xp_ref):
    q = SSD_CHUNK
    c = pl.program_id(1)

    @pl.when(c == 0)
    def _():
        h_ref[...] = h0_ref[...]
        xp_ref[5:8, :] = c0_ref[...]

    xp_ref[8:8 + q, :] = xbc_ref[...]
    conv = cb_ref[...]
    for j in range(SSD_CONV):
        conv = conv + cw_ref[j:j + 1, :] * xp_ref[5 + j:5 + j + q, :]
    xp_ref[5:8, :] = xp_ref[q + 5:q + 8, :]
    u = conv * _sigmoid(conv)
    xs = u[:, :SSD_INNER]
    bm = u[:, SSD_INNER:SSD_INNER + SSD_GN].astype(BF16)
    cm = u[:, SSD_INNER + SSD_GN:].astype(BF16)

    dt = _softplus(dt_ref[...] + dtb_ref[...])
    dtt = _softplus(dtt_ref[...] + dtbt_ref[...])
    dta = dt * (-jnp.exp(alog_ref[...]))
    dtat = dtt * (-jnp.exp(alogt_ref[...]))
    row = lax.broadcasted_iota(jnp.int32, (q, q), 0)
    col = lax.broadcasted_iota(jnp.int32, (q, q), 1)
    tril = row >= col
    acum = jnp.dot(tril.astype(F32), dta, precision=HI, preferred_element_type=F32)
    acumt = jnp.dot(dtat, (row <= col).astype(F32), precision=HI, preferred_element_type=F32)
    expand = exp_ref[...]
    dt_full = jnp.dot(dt, expand, precision=HI, preferred_element_type=F32)
    ea_full = jnp.dot(jnp.exp(acum), expand, precision=HI, preferred_element_type=F32)
    te_full = jnp.dot(jnp.exp(acum[q - 1:q, :] - acum), expand, precision=HI, preferred_element_type=F32)
    xdt = xs * dt_full
    xdt_b = xdt.astype(BF16)
    xw_b = (xdt * te_full).astype(BF16)

    r = SSD_HEADS // SSD_GROUPS
    gw = r * SSD_HEAD_DIM
    y_diag, y_off = [], []
    for g in range(SSD_GROUPS):
        cm_g = cm[:, g * SSD_STATE:(g + 1) * SSD_STATE]
        bm_g = bm[:, g * SSD_STATE:(g + 1) * SSD_STATE]
        cb = _dot_nt(cm_g, bm_g)
        h_g = h_ref[g * r:(g + 1) * r].reshape(gw, SSD_STATE)
        y_off.append(_dot_nt(cm_g, h_g.astype(BF16)))
        st = _dot_tn(xw_b[:, g * gw:(g + 1) * gw], bm_g)
        for hh in range(r):
            hd = g * r + hh
            seg = acum[:, hd:hd + 1] - acumt[hd:hd + 1, :]
            decay = jnp.exp(jnp.where(tril, seg, -jnp.inf))
            m = (cb * decay).astype(BF16)
            y_diag.append(jnp.dot(m, xdt_b[:, hd * SSD_HEAD_DIM:(hd + 1) * SSD_HEAD_DIM],
                                  preferred_element_type=F32))
            dec = jnp.exp(acumt[hd:hd + 1, q - 1:q])
            h_ref[hd] = dec * h_ref[hd] + st[hh * SSD_HEAD_DIM:(hh + 1) * SSD_HEAD_DIM, :]
    y = (jnp.concatenate(y_diag, axis=1) + jnp.concatenate(y_off, axis=1) * ea_full
         + dfull_ref[...] * xs)
    zz = z_ref[...]
    y = y * (zz * _sigmoid(zz))
    y_ref[...] = _rms(y, nw_ref[...]).astype(y_ref.dtype)


def ssd_prompt(xbc, z, dt_raw, h0, conv0, conv_w, conv_b, dt_bias, a_log, d_skip, norm_w):
    b, t, _ = xbc.shape
    q = SSD_CHUNK
    nc = t // q
    hds = SSD_HEADS
    expand = (jnp.arange(SSD_INNER)[None, :] // SSD_HEAD_DIM == jnp.arange(hds)[:, None]).astype(F32)
    d_full = jnp.repeat(d_skip, SSD_HEAD_DIM).reshape(1, SSD_INNER)
    dtt = jnp.swapaxes(dt_raw, 1, 2)
    full = lambda shape: pl.BlockSpec(shape, lambda bi, ci: (0,) * len(shape))
    y, h = pl.pallas_call(
        _ssd_chunk_kernel,
        grid=(b, nc),
        in_specs=[pl.BlockSpec((None, q, SSD_CONV_DIM), lambda bi, ci: (bi, ci, 0)),
                  pl.BlockSpec((None, q, SSD_INNER), lambda bi, ci: (bi, ci, 0)),
                  pl.BlockSpec((None, q, hds), lambda bi, ci: (bi, ci, 0)),
                  pl.BlockSpec((None, hds, q), lambda bi, ci: (bi, 0, ci)),
                  pl.BlockSpec((None, hds, SSD_HEAD_DIM, SSD_STATE), lambda bi, ci: (bi, 0, 0, 0)),
                  pl.BlockSpec((None, SSD_CONV - 1, SSD_CONV_DIM), lambda bi, ci: (bi, 0, 0)),
                  full((SSD_CONV, SSD_CONV_DIM)), full((1, SSD_CONV_DIM)),
                  full((1, hds)), full((hds, 1)), full((1, hds)), full((hds, 1)),
                  full((1, SSD_INNER)), full((1, SSD_INNER)), full((hds, SSD_INNER))],
        out_specs=[pl.BlockSpec((None, q, SSD_INNER), lambda bi, ci: (bi, ci, 0)),
                   pl.BlockSpec((None, hds, SSD_HEAD_DIM, SSD_STATE), lambda bi, ci: (bi, 0, 0, 0))],
        out_shape=[jax.ShapeDtypeStruct((b, t, SSD_INNER), BF16),
                   jax.ShapeDtypeStruct((b, hds, SSD_HEAD_DIM, SSD_STATE), F32)],
        scratch_shapes=[pltpu.VMEM((q + 8, SSD_CONV_DIM), F32)],
        compiler_params=_params("parallel", "arbitrary"),
        name="ssd_chunk_scan",
    )(xbc, z, dt_raw, dtt, h0, conv0, conv_w, conv_b.reshape(1, -1), dt_bias.reshape(1, hds),
      dt_bias.reshape(hds, 1), a_log.reshape(1, hds), a_log.reshape(hds, 1), d_full, norm_w.reshape(1, -1), expand)
    return y, h


def _sb_stage(z):
    sp = jnp.maximum(z, 0.0) + jnp.log(1.0 + jnp.exp(-jnp.abs(z)))
    hi = sp.astype(BF16)
    return hi, (sp - hi.astype(F32)).astype(BF16)


def _sb_prompt_kernel(q_ref, kt_ref, vt_ref, o_ref, z_ref, hl_ref, a_ref, *, tq, tk):
    i = pl.program_id(2)
    band = tq // tk
    assert band <= 2
    last = (i + 1) * band - 1
    q = (q_ref[...] * (HEAD_DIM ** -0.5)).astype(BF16)
    row = lax.broadcasted_iota(jnp.int32, (tk, tk), 0)
    col = lax.broadcasted_iota(jnp.int32, (tk, tk), 1)
    upper = jnp.where(row >= col, -1.0, 0.0).astype(BF16)
    upper2 = jnp.concatenate([upper, upper], axis=0)
    qpos = i * tq + lax.broadcasted_iota(jnp.int32, (tq, 1), 0)
    kcol = lax.broadcasted_iota(jnp.int32, (1, tk), 1)

    def tile(ref, s):
        off = pl.multiple_of(jnp.clip(last - s, 0, last) * tk, tk)
        return ref[:, pl.ds(off, tk)].astype(BF16)

    def masked_scores(s):
        kpos = jnp.clip(last - s, 0, last) * tk + kcol
        return jnp.where(kpos < qpos, jnp.dot(q, tile(kt_ref, s), preferred_element_type=F32), NEG)

    z0 = masked_scores(0)
    z_ref[0] = z0
    hl_ref[0, :, :tk], hl_ref[0, :, tk:] = _sb_stage(z0)
    z_ref[1] = masked_scores(1)
    a_ref[2] = jnp.zeros((tq, tk), BF16)

    def step(n, c, st):
        carry, acc = st
        nxt, prv = (c + 1) % 3, (c + 2) % 3
        later = jnp.dot(hl_ref[c], upper2, preferred_element_type=F32)
        acc = acc + _dot_nt(a_ref[prv], tile(vt_ref, n - 1))
        z_ref[prv] = jnp.dot(q, tile(kt_ref, n + 2), preferred_element_type=F32)
        hl_ref[nxt, :, :tk], hl_ref[nxt, :, tk:] = _sb_stage(z_ref[nxt])
        dead = jnp.where(n <= last, 0.0, NEG)
        a_ref[c] = jnp.exp(z_ref[c] + later + (carry + dead)).astype(BF16)
        return carry + later[:, 0:1], acc

    def body(trip, st):
        for c in range(3):
            st = step(3 * trip + c, c, st)
        return st

    st = (jnp.zeros((tq, 1), F32), jnp.zeros((tq, HEAD_DIM), F32))
    _, acc = lax.fori_loop(0, lax.div(last + 4, 3), body, st)
    o_ref[...] = acc.astype(o_ref.dtype)


def sb_prompt(q, kvt, layer=0, tq=512, tk=256):
    b, t, _ = q.shape
    h = SB_HEADS
    tq, tk = min(tq, t), min(tk, t)
    assert t % tq == 0 and tq % tk == 0
    qh = jnp.swapaxes(q.reshape(b, t, h, HEAD_DIM), 1, 2)
    out = pl.pallas_call(
        functools.partial(_sb_prompt_kernel, tq=tq, tk=tk),
        grid=(b, h, t // tq),
        in_specs=[pl.BlockSpec((None, None, tq, HEAD_DIM), lambda bi, hi, i: (bi, hi, i, 0)),
                  pl.BlockSpec((None, None, HEAD_DIM, t), lambda bi, hi, i: (layer, bi, hi, 0)),
                  pl.BlockSpec((None, None, HEAD_DIM, t), lambda bi, hi, i: (layer, bi, h + hi, 0))],
        out_specs=pl.BlockSpec((None, None, tq, HEAD_DIM), lambda bi, hi, i: (bi, hi, i, 0)),
        out_shape=jax.ShapeDtypeStruct((b, h, t, HEAD_DIM), BF16),
        scratch_shapes=[pltpu.VMEM((3, tq, tk), F32), pltpu.VMEM((3, tq, 2 * tk), BF16),
                        pltpu.VMEM((3, tq, tk), BF16)],
        compiler_params=_params("parallel", "parallel", "arbitrary"),
        name="sb_prompt",
    )(qh, kvt, kvt)
    return jnp.swapaxes(out, 1, 2).reshape(b, t, h * HEAD_DIM)


def _merge_kernel(x_ref, ssd_ref, sb_ref, nsa_ref, gl_ref, wssd_ref, wsb_ref, wnsa_ref, wo_ref, nw_ref, o_ref):
    d = D_MODEL
    gl = gl_ref[...]
    merged = (_sigmoid(gl[:, :d]) * jnp.dot(ssd_ref[...], wssd_ref[...], preferred_element_type=F32)
              + _sigmoid(gl[:, d:2 * d]) * jnp.dot(sb_ref[...], wsb_ref[...], preferred_element_type=F32)
              + _sigmoid(gl[:, 2 * d:]) * jnp.dot(nsa_ref[...], wnsa_ref[...], preferred_element_type=F32))
    y = jnp.dot(merged.astype(BF16), wo_ref[...], preferred_element_type=F32)
    o_ref[...] = x_ref[...] + _rms(y, nw_ref[...])


def merge_branches(x, ssd_y, sb_o, nsa_o, gate_logits, w_ssd_out, w_sb_out, w_nsa_out, w_o, norm_w):
    m, d = x.shape
    tm = _pick(m, (512, 256, 128, 32))
    rows = lambda n: pl.BlockSpec((tm, n), lambda i: (i, 0))
    full = lambda a: pl.BlockSpec(a.shape, lambda i: (0, 0))
    nw = norm_w.reshape(1, d)
    return pl.pallas_call(
        _merge_kernel,
        grid=(m // tm,),
        in_specs=[rows(d), rows(ssd_y.shape[1]), rows(sb_o.shape[1]), rows(nsa_o.shape[1]), rows(N_BRANCH * d),
                  full(w_ssd_out), full(w_sb_out), full(w_nsa_out), full(w_o), full(nw)],
        out_specs=rows(d),
        out_shape=jax.ShapeDtypeStruct((m, d), F32),
        compiler_params=_params("parallel"),
        name="merge_branches",
    )(x, ssd_y, sb_o, nsa_o, gate_logits, w_ssd_out, w_sb_out, w_nsa_out, w_o, nw)


def _ffn_up_kernel(x_ref, g_ref, wg_ref, wu_ref, o_ref, h_ref):
    @pl.when(pl.program_id(1) == 0)
    def _():
        h_ref[...] = _rms(x_ref[...], g_ref[...]).astype(BF16)

    h = h_ref[...]
    a = jnp.dot(h, wg_ref[...], preferred_element_type=F32)
    u = jnp.dot(h, wu_ref[...], preferred_element_type=F32)
    o_ref[...] = (a * _sigmoid(a) * u).astype(o_ref.dtype)


def _ffn_down_kernel(a_ref, x_ref, wd_ref, nw_ref, o_ref):
    f = jnp.dot(a_ref[...], wd_ref[...], preferred_element_type=F32)
    o_ref[...] = x_ref[...] + _rms(f, nw_ref[...])


def ffn(x, pre_w, post_w, w_gate, w_up, w_down):
    m, d = x.shape
    f = w_gate.shape[1]
    tm = _pick(m, (1024, 512, 256, 128, 32))
    tn = _pick(f, (256, 128))
    act = pl.pallas_call(
        _ffn_up_kernel,
        grid=(m // tm, f // tn),
        in_specs=[pl.BlockSpec((tm, d), lambda i, j: (i, 0)),
                  pl.BlockSpec((1, d), lambda i, j: (0, 0)),
                  pl.BlockSpec((d, tn), lambda i, j: (0, j)),
                  pl.BlockSpec((d, tn), lambda i, j: (0, j))],
        out_specs=pl.BlockSpec((tm, tn), lambda i, j: (i, j)),
        out_shape=jax.ShapeDtypeStruct((m, f), BF16),
        scratch_shapes=[pltpu.VMEM((tm, d), BF16)],
        compiler_params=_params("parallel", "arbitrary"),
        name="ffn_up",
    )(x, pre_w.reshape(1, d), w_gate, w_up)
    tm2 = _pick(m, (512, 256, 128, 32))
    return pl.pallas_call(
        _ffn_down_kernel,
        grid=(m // tm2,),
        in_specs=[pl.BlockSpec((tm2, f), lambda i: (i, 0)),
                  pl.BlockSpec((tm2, d), lambda i: (i, 0)),
                  pl.BlockSpec((f, d), lambda i: (0, 0)),
                  pl.BlockSpec((1, d), lambda i: (0, 0))],
        out_specs=pl.BlockSpec((tm2, d), lambda i: (i, 0)),
        out_shape=jax.ShapeDtypeStruct((m, d), F32),
        compiler_params=_params("parallel"),
        name="ffn_down",
    )(act, x, w_down, post_w.reshape(1, d))


def _compress_rows(r, pos_ref, w1_ref, w2_ref, sh_ref):
    nr = r.shape[0]
    half = CMP_STRIDE * HEAD_DIM
    top = jnp.dot((r + pos_ref[0:1, :]).astype(BF16), w1_ref[:half, :], preferred_element_type=F32)
    bot = jnp.dot((r + pos_ref[1:2, :]).astype(BF16), w1_ref[half:, :], preferred_element_type=F32)
    sh_ref[0:nr, :] = bot
    sh_ref[nr:nr + 8, :] = jnp.zeros((8, CMP_HIDDEN), F32)
    pre = top + sh_ref[1:nr + 1, :]
    hid = pre * _sigmoid(pre)
    return jnp.dot(hid.astype(BF16), w2_ref[...], preferred_element_type=F32)


def _nsa_compress_kernel(r_ref, pos_ref, w1_ref, w2_ref, o_ref, sh_ref):
    o_ref[...] = _compress_rows(r_ref[...], pos_ref, w1_ref, w2_ref, sh_ref).astype(o_ref.dtype)


def nsa_compress(rows16, cmp_pos, cmp_w1, cmp_w2):
    assert CMP_BLOCK == 2 * CMP_STRIDE
    b, _, g, nr, w = rows16.shape
    pos = cmp_pos.reshape(2, 2, w)
    return pl.pallas_call(
        _nsa_compress_kernel,
        grid=(b, 2, g),
        in_specs=[pl.BlockSpec((None, None, None, nr, w), lambda bi, ki, gi: (bi, ki, gi, 0, 0)),
                  pl.BlockSpec((None, 2, w), lambda bi, ki, gi: (ki, 0, 0)),
                  pl.BlockSpec((None, 2 * w, CMP_HIDDEN), lambda bi, ki, gi: (ki, 0, 0)),
                  pl.BlockSpec((None, CMP_HIDDEN, HEAD_DIM), lambda bi, ki, gi: (ki, 0, 0))],
        out_specs=pl.BlockSpec((None, None, None, nr, HEAD_DIM), lambda bi, ki, gi: (bi, ki, gi, 0, 0)),
        out_shape=jax.ShapeDtypeStruct((b, 2, g, nr, HEAD_DIM), BF16),
        scratch_shapes=[pltpu.VMEM((nr + 8, CMP_HIDDEN), F32)],
        compiler_params=_params("parallel", "parallel", "parallel"),
        name="nsa_compress",
    )(rows16, pos, cmp_w1.astype(BF16), cmp_w2.astype(BF16))


def _overlap_matrix(n_cmp_rows, n_cmp, n_blk):
    c = jnp.arange(n_cmp_rows)[:, None]
    n = jnp.arange(n_blk)[None, :]
    c_start, c_end = c * CMP_STRIDE, c * CMP_STRIDE + CMP_BLOCK - 1
    return ((c_start < (n + 1) * SEL_BLOCK) & (c_end >= n * SEL_BLOCK) & (c < n_cmp)).astype(BF16)


def _split_dot(x, w):
    hi = x.astype(BF16)
    lo = (x - hi.astype(F32)).astype(BF16)
    return jnp.dot(hi, w, preferred_element_type=F32) + jnp.dot(lo, w, preferred_element_type=F32)


def _top_blocks(imp, blk, n_top):
    n_blk = imp.shape[1]
    sel = jnp.zeros(imp.shape, F32)
    for _ in range(n_top):
        m = jnp.max(imp, axis=-1, keepdims=True)
        idx = jnp.min(jnp.where(imp == m, blk, float(n_blk)), axis=-1, keepdims=True)
        hit = blk == idx
        sel = jnp.where(hit, 1.0, sel)
        imp = jnp.where(hit, -jnp.inf, imp)
    return sel


def _flash_step_t(qt, k, vt, mask, m, acc):
    s = jnp.where(mask, jnp.dot(k, qt, preferred_element_type=F32), NEG)
    m_new = jnp.maximum(m, jnp.max(s, axis=0, keepdims=True))
    p = jnp.exp(s - m_new).astype(BF16)
    vt_ext = jnp.concatenate([vt, jnp.ones((8, vt.shape[1]), BF16)], axis=0)
    return m_new, acc * jnp.exp(m - m_new) + jnp.dot(vt_ext, p, preferred_element_type=F32)


def _nsa_prompt_t_kernel(qt_ref, gt_ref, kc_ref, vct_ref, ovt_ref, ks_ref, vs_ref, kw_ref, vw_ref, o_ref,
                         s_ref, ch_ref, p_ref, *, tq, n_cmp):
    i = pl.program_id(2)
    rep, d = NSA_REP, HEAD_DIM
    lanes = rep * tq
    sel_shift = int(math.log2(SEL_BLOCK))
    qt_blk = qt_ref[...]
    qt = jnp.concatenate([qt_blk[r * d:(r + 1) * d, :] for r in range(rep)], axis=1)
    qt = (qt * (d ** -0.5)).astype(BF16)
    qpos = i * tq + lax.broadcasted_iota(jnp.int32, (1, tq), 1)
    per_head = lambda a: jnp.concatenate([a] * rep, axis=1)

    n_rows = kc_ref.shape[0]
    cidx = lax.broadcasted_iota(jnp.int32, (n_rows, 1), 0)
    vis_c = per_head(((cidx * CMP_STRIDE + (CMP_BLOCK - 1) <= qpos) & (cidx < n_cmp)).astype(F32)) > 0.5
    s_c = jnp.where(vis_c, jnp.dot(kc_ref[...], qt, preferred_element_type=F32), NEG)
    e_c = jnp.where(vis_c, jnp.exp(s_c - jnp.max(s_c, axis=0, keepdims=True)), 0.0)
    p_c = e_c / jnp.maximum(jnp.sum(e_c, axis=0, keepdims=True), 1e-30)
    o_c = jnp.dot(vct_ref[...], p_c.astype(BF16), preferred_element_type=F32)

    p_sum = p_c[:, 0:tq]
    for r in range(1, rep):
        p_sum = p_sum + p_c[:, r * tq:(r + 1) * tq]
    p_hi = p_sum.astype(BF16)
    p_lo = (p_sum - p_hi.astype(F32)).astype(BF16)
    imp = (jnp.dot(ovt_ref[...], p_hi, preferred_element_type=F32)
           + jnp.dot(ovt_ref[...], p_lo, preferred_element_type=F32))
    n_blk = ovt_ref.shape[0]
    blk_i = lax.broadcasted_iota(jnp.int32, (n_blk, 1), 0)
    cur = lax.shift_right_logical(qpos, sel_shift)
    valid = blk_i <= cur
    forced = valid & ((blk_i == 0) | (blk_i > cur - SEL_LOCAL))
    imp = jnp.where(forced, jnp.inf, jnp.where(valid, imp, -jnp.inf))

    m0 = jnp.full((1, lanes), NEG, F32)
    acc0 = jnp.zeros((d + 8, lanes), F32)
    krow = lax.broadcasted_iota(jnp.int32, (tq, 1), 0)

    st = (m0, acc0)
    for n in range(WINDOW // tq + 1):
        j = i - n
        off = pl.multiple_of(jnp.maximum(j, 0) * tq, tq)
        kpos = off + krow + jnp.where(j < 0, 1 << 30, 0)
        mask = per_head(((kpos <= qpos) & (kpos > qpos - WINDOW)).astype(F32)) > 0.5
        st = _flash_step_t(qt, kw_ref[pl.ds(off, tq), :].astype(BF16), vw_ref[:, pl.ds(off, tq)].astype(BF16),
                           mask, *st)
    acc_w = st[1]

    blk_f = blk_i.astype(F32)
    sel = jnp.zeros((n_blk, tq), F32)
    for _ in range(min(SEL_TOP, n_blk)):
        top = jnp.max(imp, axis=0, keepdims=True)
        idx = jnp.min(jnp.where(imp == top, blk_f, float(n_blk)), axis=0, keepdims=True)
        hit = blk_f == idx
        sel = jnp.where(hit, 1.0, sel)
        imp = jnp.where(hit, -jnp.inf, imp)
    sel = sel.astype(BF16)

    exp_blk = lax.broadcasted_iota(jnp.int32, (tq, n_blk), 1)
    exp_key = lax.shift_right_logical(lax.broadcasted_iota(jnp.int32, (tq, n_blk), 0), sel_shift)

    def key_tile(n):
        return jnp.clip(i - n, 0, i)

    def scores(n):
        j = key_tile(n)
        expand = (exp_blk == j * (tq // SEL_BLOCK) + exp_key).astype(BF16)
        return (jnp.dot(ks_ref[pl.ds(pl.multiple_of(j * tq, tq), tq), :].astype(BF16), qt, preferred_element_type=F32),
                jnp.dot(expand, sel, preferred_element_type=F32))

    def weighted_values(n, slot):
        vt = vs_ref[:, pl.ds(pl.multiple_of(key_tile(n) * tq, tq), tq)].astype(BF16)
        return jnp.dot(jnp.concatenate([vt, jnp.ones((8, tq), BF16)], axis=0), p_ref[slot],
                       preferred_element_type=F32)

    s_ref[0], ch_ref[0] = scores(0)
    p_ref[0] = jnp.zeros(p_ref.shape[1:], BF16)

    def sel_step(n, rd, wr, st):
        m, alpha, acc = st
        pv = weighted_values(n - 1, rd)
        s_ref[wr], ch_ref[wr] = scores(n + 1)
        kpos = (i - n) * tq + krow + jnp.where(n > i, 1 << 30, 0)
        mask = per_head(jnp.where(kpos <= qpos, ch_ref[rd], 0.0)) > 0.5
        s = jnp.where(mask, s_ref[rd], NEG)
        m_new = jnp.maximum(m, jnp.max(s, axis=0, keepdims=True))
        p_ref[wr] = jnp.exp(s - m_new).astype(BF16)
        return m_new, jnp.exp(m - m_new), acc * alpha + pv

    def sel_pair(pair, st):
        return sel_step(2 * pair + 1, 1, 0, sel_step(2 * pair, 0, 1, st))

    _, _, acc_s = lax.fori_loop(0, (i + 3) >> 1, sel_pair, (m0, jnp.ones((1, lanes), F32), acc0))

    o_s = acc_s[:d] / acc_s[d:d + 1]
    o_w = acc_w[:d] / acc_w[d:d + 1]
    gate = _sigmoid(gt_ref[...])
    outs = []
    for r in range(rep):
        sl = slice(r * tq, (r + 1) * tq)
        outs.append(gate[3 * r:3 * r + 1] * o_c[:, sl] + gate[3 * r + 1:3 * r + 2] * o_s[:, sl]
                    + gate[3 * r + 2:3 * r + 3] * o_w[:, sl])
    o_ref[...] = jnp.concatenate(outs, axis=0).astype(o_ref.dtype)


def nsa_prompt_t(nsa_qt, nsa_g, nsa_kvt, cmp_pos, cmp_w1, cmp_w2, tq=128):
    b, _, t = nsa_qt.shape
    g, d, rep = NSA_KV_GROUPS, HEAD_DIM, NSA_REP
    assert t % tq == 0 and tq % SEL_BLOCK == 0 and WINDOW % tq == 0 and t % CMP_STRIDE == 0
    n_cmp = (t - CMP_BLOCK) // CMP_STRIDE + 1
    nr = t // CMP_STRIDE
    n_blk = t // SEL_BLOCK
    kinds = nsa_kvt.reshape(b, 6, g, d, t)
    rows16 = jnp.swapaxes(kinds[:, 0:2], 3, 4).reshape(b, 2, g, nr, CMP_STRIDE * d)
    kcvc = nsa_compress(rows16, cmp_pos, cmp_w1, cmp_w2)
    vct = jnp.swapaxes(kcvc[:, 1], 2, 3)
    overlap_t = _overlap_matrix(nr, n_cmp, n_blk).T
    keys = jnp.swapaxes(kinds[:, 2::2], 3, 4).astype(BF16)
    gates = jnp.transpose(nsa_g.reshape(b, t, g, 3 * rep), (0, 2, 3, 1))
    gates = jnp.pad(gates, ((0, 0), (0, 0), (0, 16 - 3 * rep), (0, 0)))
    values = lambda kind: pl.BlockSpec((None, d, t), lambda bi, gi, i: (bi, kind * g + gi, 0))
    rows = lambda kind: pl.BlockSpec((None, None, None, t, d), lambda bi, gi, i: (bi, kind, gi, 0, 0))
    out = pl.pallas_call(
        functools.partial(_nsa_prompt_t_kernel, tq=tq, n_cmp=n_cmp),
        grid=(b, g, t // tq),
        in_specs=[pl.BlockSpec((None, rep * d, tq), lambda bi, gi, i: (bi, gi, i)),
                  pl.BlockSpec((None, None, 16, tq), lambda bi, gi, i: (bi, gi, 0, i)),
                  pl.BlockSpec((None, None, None, nr, d), lambda bi, gi, i: (bi, 0, gi, 0, 0)),
                  pl.BlockSpec((None, None, d, nr), lambda bi, gi, i: (bi, gi, 0, 0)),
                  pl.BlockSpec((n_blk, nr), lambda bi, gi, i: (0, 0)),
                  rows(0), values(3), rows(1), values(5)],
        out_specs=pl.BlockSpec((None, rep * d, tq), lambda bi, gi, i: (bi, gi, i)),
        out_shape=jax.ShapeDtypeStruct((b, g * rep * d, t), BF16),
        scratch_shapes=[pltpu.VMEM((2, tq, rep * tq), F32), pltpu.VMEM((2, tq, tq), F32),
                        pltpu.VMEM((2, tq, rep * tq), BF16)],
        compiler_params=_params("parallel", "parallel", "arbitrary"),
        name="nsa_prompt",
    )(nsa_qt, gates, kcvc, vct, overlap_t, keys, nsa_kvt, keys, nsa_kvt)
    return jnp.swapaxes(out, 1, 2)


def _ssd_step_pre_kernel(x_ref, buf_ref, cw_ref, cb_ref, dt_ref, dtb_ref, alog_ref, exp_ref,
                         xs_ref, xdt_ref, bm_ref, cm_ref, dec_ref):
    conv = cb_ref[...] + cw_ref[SSD_CONV - 1:SSD_CONV, :] * x_ref[...]
    for j in range(SSD_CONV - 1):
        conv = conv + cw_ref[j:j + 1, :] * buf_ref[j]
    u = conv * _sigmoid(conv)
    xs = u[:, :SSD_INNER]
    dt = _softplus(dt_ref[...] + dtb_ref[...])
    xs_ref[...] = xs
    xdt_ref[...] = xs * jnp.dot(dt, exp_ref[...], precision=HI, preferred_element_type=F32)
    bm_ref[...] = u[:, SSD_INNER:SSD_INNER + SSD_GN]
    cm_ref[...] = u[:, SSD_INNER + SSD_GN:]
    dec_ref[...] = jnp.exp(dt * (-jnp.exp(alog_ref[...])))


def _ssd_step_state_kernel(h0_ref, xdt_ref, dec_ref, bm_ref, cm_ref, h_ref, y_ref):
    r = SSD_HEADS // SSD_GROUPS
    for b in range(h0_ref.shape[0]):
        for hd in range(SSD_HEADS):
            g = hd // r
            hn = dec_ref[b, hd] * h0_ref[b, hd] + xdt_ref[b, hd] * bm_ref[b, g]
            h_ref[b, hd] = hn
            y_ref[b, hd] = jnp.sum(hn * cm_ref[b, g], axis=-1, keepdims=True)


def _ssd_step_post_kernel(y_ref, xs_ref, z_ref, dfull_ref, nw_ref, o_ref):
    zz = z_ref[...]
    y = (y_ref[...] + dfull_ref[...] * xs_ref[...]) * (zz * _sigmoid(zz))
    o_ref[...] = _rms(y, nw_ref[...]).astype(o_ref.dtype)


def ssd_step(xbc, z, dt_raw, h0, conv_buf, conv_w, conv_b, dt_bias, a_log, d_skip, norm_w):
    b = xbc.shape[0]
    hds, p, n = SSD_HEADS, SSD_HEAD_DIM, SSD_STATE
    expand = (jnp.arange(SSD_INNER)[None, :] // p == jnp.arange(hds)[:, None]).astype(F32)
    d_full = jnp.repeat(d_skip, p).reshape(1, SSD_INNER)
    sds = lambda shape: jax.ShapeDtypeStruct(shape, F32)
    xs, xdt, bm, cm, dec = pl.pallas_call(
        _ssd_step_pre_kernel,
        out_shape=[sds((b, SSD_INNER)), sds((b, SSD_INNER)), sds((b, SSD_GN)), sds((b, SSD_GN)), sds((b, hds))],
        name="ssd_step_pre",
    )(xbc, jnp.swapaxes(conv_buf, 0, 1), conv_w, conv_b.reshape(1, -1), dt_raw, dt_bias.reshape(1, hds),
      a_log.reshape(1, hds), expand)
    n_seq = 1
    per_b = lambda *dims: pl.BlockSpec((n_seq,) + dims, lambda bi: (bi,) + (0,) * len(dims))
    h_new, y_col = pl.pallas_call(
        _ssd_step_state_kernel,
        grid=(b // n_seq,),
        in_specs=[per_b(hds, p, n), per_b(hds, p, 1), per_b(hds, 1, 1), per_b(SSD_GROUPS, 1, n),
                  per_b(SSD_GROUPS, 1, n)],
        out_specs=[per_b(hds, p, n), per_b(hds, p, 1)],
        out_shape=[sds((b, hds, p, n)), sds((b, hds, p, 1))],
        compiler_params=_params("parallel"),
        name="ssd_step_state",
    )(h0, xdt.reshape(b, hds, p, 1), dec.reshape(b, hds, 1, 1), bm.reshape(b, SSD_GROUPS, 1, n),
      cm.reshape(b, SSD_GROUPS, 1, n))
    y = pl.pallas_call(
        _ssd_step_post_kernel,
        out_shape=jax.ShapeDtypeStruct((b, SSD_INNER), BF16),
        name="ssd_step_post",
    )(y_col.reshape(b, SSD_INNER), xs, z, d_full, norm_w.reshape(1, -1))
    return y, h_new


def _sb_decode_kernel(pt_ref, q_ref, *refs):
    page_refs, (o_ref, carry_ref, acc_ref) = refs[:PAGES_PER_STEP], refs[PAGES_PER_STEP:]
    p = pl.program_id(1)
    tk = page_refs[0].shape[-1]
    hds = SB_HEADS

    @pl.when(p == 0)
    def _():
        carry_ref[...] = jnp.zeros(carry_ref.shape, F32)
        acc_ref[...] = jnp.zeros(acc_ref.shape, F32)

    scale = HEAD_DIM ** -0.5
    qs = [q_ref[h] * scale for h in range(hds)]
    z = jnp.concatenate([jnp.sum(ref[0, h] * qs[h], axis=0, keepdims=True)
                         for ref in page_refs for h in range(hds)], axis=0)
    row = lax.broadcasted_iota(jnp.int32, (tk, tk), 0)
    col = lax.broadcasted_iota(jnp.int32, (tk, tk), 1)
    sp = _softplus(z)
    later = _split_dot(-sp, (row > col).astype(BF16))
    total = later[:, 0:1] - sp[:, 0:1]
    carry = carry_ref[...]
    for k, ref in enumerate(page_refs):
        rows = slice(k * hds, (k + 1) * hds)
        a = jnp.exp(z[rows] - sp[rows] + later[rows] + carry)
        carry = carry + total[rows]
        for h in range(hds):
            acc_ref[h] += ref[1, h] * a[h:h + 1, :]
    carry_ref[...] = carry

    @pl.when(p == pl.num_programs(1) - 1)
    def _():
        for h in range(hds):
            o_ref[h] = jnp.sum(acc_ref[h], axis=-1, keepdims=True)


def _page_specs(block, layer, n_pages, kind_block, descending):
    def spec(k):
        def index(bi, p, pt):
            pos = p * PAGES_PER_STEP + k
            pos = n_pages - 1 - pos if descending else pos
            return (layer, pt[bi, pos], kind_block) + (0,) * (len(block) - 3)
        return pl.BlockSpec(block, index)
    return [spec(k) for k in range(PAGES_PER_STEP)]


def sb_decode(q, pool_t, layer, page_table):
    b = q.shape[0]
    h, d = SB_HEADS, HEAD_DIM
    n_pages = page_table.shape[1]
    page = pool_t.shape[-1]
    assert n_pages % PAGES_PER_STEP == 0
    out = pl.pallas_call(
        _sb_decode_kernel,
        grid_spec=pltpu.PrefetchScalarGridSpec(
            num_scalar_prefetch=1,
            grid=(b, n_pages // PAGES_PER_STEP),
            in_specs=[pl.BlockSpec((None, h, d, 1), lambda bi, p, pt: (bi, 0, 0, 0))]
            + _page_specs((None, None, 2, h, d, page), layer, n_pages, 0, descending=True),
            out_specs=pl.BlockSpec((None, h, d, 1), lambda bi, p, pt: (bi, 0, 0, 0)),
            scratch_shapes=[pltpu.VMEM((h, 1), F32), pltpu.VMEM((h, d, page), F32)]),
        out_shape=jax.ShapeDtypeStruct((b, h, d, 1), F32),
        compiler_params=_params("parallel", "arbitrary"),
        name="sb_decode",
    )(page_table, q.reshape(b, h, d, 1), *([pool_t] * PAGES_PER_STEP))
    return out.reshape(b, h * d)


def _nsa_gather_compress_kernel(pt_ref, *refs):
    page_refs = refs[:PAGES_PER_STEP]
    pos_ref, w1_ref, w2_ref, o_ref, x_ref, r_ref, sh_ref = refs[PAGES_PER_STEP:]
    p = pl.program_id(1)
    page = page_refs[0].shape[-1]
    grp, d = NSA_KV_GROUPS, HEAD_DIM
    n_out = page // CMP_STRIDE
    for k, ref in enumerate(page_refs):
        row0 = pl.multiple_of((p * PAGES_PER_STEP + k) * n_out, n_out)
        for kind in range(2):
            x = x_ref.at[2 * k + kind]
            x[...] = ref[kind].reshape(grp * d, page).T
            steps = [x[pl.ds(s, n_out, stride=CMP_STRIDE), :] for s in range(CMP_STRIDE)]
            for g in range(grp):
                r_ref[kind, g, pl.ds(row0, n_out), :] = jnp.concatenate(
                    [st[:, g * d:(g + 1) * d] for st in steps], axis=1)

    @pl.when(p == pl.num_programs(1) - 1)
    def _():
        for kind in range(2):
            for g in range(grp):
                o_ref[kind, g] = _compress_rows(r_ref[kind, g], pos_ref.at[kind], w1_ref.at[kind], w2_ref.at[kind],
                                                sh_ref).astype(o_ref.dtype)


def nsa_gather_compress(pool_t, layer, page_table, cmp_pos, cmp_w1, cmp_w2):
    b, n_pages = page_table.shape
    g, d = NSA_KV_GROUPS, HEAD_DIM
    page = pool_t.shape[-1]
    assert page % CMP_STRIDE == 0 and n_pages % PAGES_PER_STEP == 0 and CMP_BLOCK == 2 * CMP_STRIDE
    nr = n_pages * (page // CMP_STRIDE)
    w = CMP_STRIDE * d
    whole = lambda shape: pl.BlockSpec(shape, lambda bi, p, pt: (0,) * len(shape))
    return pl.pallas_call(
        _nsa_gather_compress_kernel,
        grid_spec=pltpu.PrefetchScalarGridSpec(
            num_scalar_prefetch=1,
            grid=(b, n_pages // PAGES_PER_STEP),
            in_specs=_page_specs((None, None, 2, g, d, page), layer, n_pages, 0, descending=False)
            + [whole((2, 2, w)), whole((2, 2 * w, CMP_HIDDEN)), whole((2, CMP_HIDDEN, d))],
            out_specs=pl.BlockSpec((None, 2, g, nr, d), lambda bi, p, pt: (bi, 0, 0, 0, 0)),
            scratch_shapes=[pltpu.VMEM((2 * PAGES_PER_STEP, page, g * d), F32), pltpu.VMEM((2, g, nr, w), F32),
                            pltpu.VMEM((nr + 8, CMP_HIDDEN), F32)]),
        out_shape=jax.ShapeDtypeStruct((b, 2, g, nr, d), BF16),
        compiler_params=_params("parallel", "arbitrary"),
        name="nsa_gather_compress",
    )(page_table, *([pool_t] * PAGES_PER_STEP), cmp_pos.reshape(2, 2, w), cmp_w1.astype(BF16), cmp_w2.astype(BF16))


def _nsa_decode_select_kernel(q_ref, kc_ref, vc_ref, ov_ref, oc_ref, sel_ref, *, n_cmp, q_pos):
    rep, d, grp = NSA_REP, HEAD_DIM, NSA_KV_GROUPS
    n_seq = q_ref.shape[0]
    n_rows = kc_ref.shape[2]
    n_blk = ov_ref.shape[1]
    cidx = lax.broadcasted_iota(jnp.int32, (1, n_rows), 1)
    vis = (cidx * CMP_STRIDE + (CMP_BLOCK - 1) <= q_pos) & (cidx < n_cmp)
    imp = []
    for b in range(n_seq):
        q = (q_ref[b] * (d ** -0.5)).astype(BF16)
        o_c = []
        for g in range(grp):
            s = jnp.where(vis, _dot_nt(q[g * rep:(g + 1) * rep], kc_ref[b, g]), NEG)
            e = jnp.where(vis, jnp.exp(s - jnp.max(s, axis=-1, keepdims=True)), 0.0)
            p = e / jnp.maximum(jnp.sum(e, axis=-1, keepdims=True), 1e-30)
            o_c.append(jnp.dot(p.astype(BF16), vc_ref[b, g], preferred_element_type=F32))
            imp.append(_split_dot(jnp.sum(p, axis=0, keepdims=True), ov_ref[...]))
        oc_ref[b] = jnp.concatenate(o_c, axis=0)
    imp = jnp.concatenate(imp, axis=0)
    blk_i = lax.broadcasted_iota(jnp.int32, (1, n_blk), 1)
    forced = (blk_i == 0) | (blk_i > n_blk - SEL_LOCAL)
    sel = _top_blocks(jnp.where(forced, jnp.inf, imp), blk_i.astype(F32), min(SEL_TOP - 1, n_blk))
    pad = jnp.zeros((sel_ref.shape[1] - grp, n_blk), F32)
    for b in range(n_seq):
        sel_ref[b] = jnp.concatenate([sel[b * grp:(b + 1) * grp], pad], axis=0)


def _nsa_decode_attend_kernel(pt_ref, q_ref, sel_ref, *refs, win_skip):
    page_refs = refs[:PAGES_PER_STEP]
    new_ref, win_ref, oc_ref, gt_ref, o_ref, m_ref, acc_ref = refs[PAGES_PER_STEP:]
    p = pl.program_id(1)
    n_steps = pl.num_programs(1)
    rep, d, grp = NSA_REP, HEAD_DIM, NSA_KV_GROUPS
    tk = page_refs[0].shape[-1]
    lanes = PAGES_PER_STEP * tk
    scale = d ** -0.5
    qf = q_ref[...] * scale
    q = qf.astype(BF16)
    new = new_ref[...]
    new_row = lambda kind, g: new[kind * grp + g:kind * grp + g + 1, :]
    per_head = lambda f: jnp.concatenate([f(g) for g in range(grp)], axis=0)

    @pl.when(p == 0)
    def _():
        m_ref[...] = per_head(lambda g: jnp.sum(qf[g * rep:(g + 1) * rep] * new_row(2, g), axis=-1, keepdims=True))
        acc_ref[...] = per_head(lambda g: jnp.concatenate(
            [jnp.broadcast_to(new_row(3, g), (rep, d)), jnp.ones((rep, d), F32)], axis=1))

    n_blk = sel_ref.shape[1]
    lane = lax.broadcasted_iota(jnp.int32, (n_blk, lanes), 1)
    page_pos = (n_steps - p) * PAGES_PER_STEP - 1 - lax.shift_right_logical(lane, int(math.log2(tk)))
    blk_of_lane = page_pos * (tk // SEL_BLOCK) + lax.shift_right_logical(lane & (tk - 1), int(math.log2(SEL_BLOCK)))
    expand = (lax.broadcasted_iota(jnp.int32, (n_blk, lanes), 0) == blk_of_lane).astype(BF16)
    chosen = jnp.dot(sel_ref[...].astype(BF16), expand, preferred_element_type=F32)

    def scores(g):
        qg = q[g * rep:(g + 1) * rep]
        sg = jnp.concatenate([jnp.dot(qg, ref[0, g].astype(BF16), preferred_element_type=F32) for ref in page_refs],
                             axis=1)
        return jnp.where(chosen[g:g + 1, :] > 0.5, sg, NEG)

    s = per_head(scores)
    m_old = m_ref[...]
    m_new = jnp.maximum(m_old, jnp.max(s, axis=-1, keepdims=True))
    pr = jnp.exp(s - m_new).astype(BF16)

    def weighted_values(g):
        out = jnp.zeros((rep, 2 * d), F32)
        for k, ref in enumerate(page_refs):
            vt_ext = jnp.concatenate([ref[1, g].astype(BF16), jnp.ones((d, tk), BF16)], axis=0)
            out = out + _dot_nt(pr[g * rep:(g + 1) * rep, k * tk:(k + 1) * tk], vt_ext)
        return out

    acc_ref[...] = acc_ref[...] * jnp.exp(m_old - m_new) + per_head(weighted_values)
    m_ref[...] = m_new

    @pl.when(p == n_steps - 1)
    def _():
        acc = acc_ref[...]
        o_s = acc[:, :d] / acc[:, d:]
        wlen = win_ref.shape[-1]
        vis = lax.broadcasted_iota(jnp.int32, (1, wlen), 1) >= win_skip

        def window(g):
            qg = q[g * rep:(g + 1) * rep]
            s_w = jnp.where(vis, jnp.dot(qg, win_ref[0, g].astype(BF16), preferred_element_type=F32), NEG)
            s_n = jnp.sum(qf[g * rep:(g + 1) * rep] * new_row(4, g), axis=-1, keepdims=True)
            mx = jnp.maximum(jnp.max(s_w, axis=-1, keepdims=True), s_n)
            e_w = jnp.where(vis, jnp.exp(s_w - mx), 0.0)
            e_n = jnp.exp(s_n - mx)
            num = _dot_nt(e_w.astype(BF16), win_ref[1, g].astype(BF16)) + e_n * new_row(5, g)
            return num / (jnp.sum(e_w, axis=-1, keepdims=True) + e_n)

        o_w = per_head(window)
        gate = _sigmoid(gt_ref[...])
        o_ref[...] = gate[:, 0:1] * oc_ref[...] + gate[:, 1:2] * o_s + gate[:, 2:3] * o_w


def nsa_decode(nsa_q, nsa_g, nsa_kv_new, pool_t, win_t, layer, page_table, cmp_pos, cmp_w1, cmp_w2):
    b = nsa_q.shape[0]
    g, d, rep, hds = NSA_KV_GROUPS, HEAD_DIM, NSA_REP, NSA_HEADS
    n_pages = page_table.shape[1]
    page = pool_t.shape[-1]
    past = n_pages * page
    wlen = win_t.shape[-1]
    assert past % SEL_BLOCK == 0 and past % CMP_STRIDE == 0 and page % SEL_BLOCK == 0 and wlen <= past
    n_cmp = (past + 1 - CMP_BLOCK) // CMP_STRIDE + 1
    nr = past // CMP_STRIDE
    n_blk = past // SEL_BLOCK
    kcvc = nsa_gather_compress(pool_t, layer, page_table, cmp_pos, cmp_w1, cmp_w2)
    overlap = _overlap_matrix(nr, n_cmp, n_blk)
    q3 = nsa_q.reshape(b, hds, d)
    n_seq = _pick(b, (SEQS_PER_STEP, 1))
    o_c, sel = pl.pallas_call(
        functools.partial(_nsa_decode_select_kernel, n_cmp=n_cmp, q_pos=past),
        grid=(b // n_seq,),
        in_specs=[pl.BlockSpec((n_seq, hds, d), lambda bi: (bi, 0, 0)),
                  pl.BlockSpec((n_seq, None, g, nr, d), lambda bi: (bi, 0, 0, 0, 0)),
                  pl.BlockSpec((n_seq, None, g, nr, d), lambda bi: (bi, 1, 0, 0, 0)),
                  pl.BlockSpec((nr, n_blk), lambda bi: (0, 0))],
        out_specs=[pl.BlockSpec((n_seq, hds, d), lambda bi: (bi, 0, 0)),
                   pl.BlockSpec((n_seq, 8, n_blk), lambda bi: (bi, 0, 0))],
        out_shape=[jax.ShapeDtypeStruct((b, hds, d), F32), jax.ShapeDtypeStruct((b, 8, n_blk), F32)],
        compiler_params=_params("parallel"),
        name="nsa_decode_select",
    )(q3, kcvc, kcvc, overlap)
    gates = jnp.pad(nsa_g.reshape(b, hds, 3), ((0, 0), (0, 0), (0, V7X_LANES - 3)))
    fixed = lambda *dims: pl.BlockSpec((None,) + dims, lambda bi, p, pt: (bi,) + (0,) * len(dims))
    out = pl.pallas_call(
        functools.partial(_nsa_decode_attend_kernel, win_skip=wlen - WINDOW + 1),
        grid_spec=pltpu.PrefetchScalarGridSpec(
            num_scalar_prefetch=1,
            grid=(b, n_pages // PAGES_PER_STEP),
            in_specs=[fixed(hds, d), fixed(8, n_blk)]
            + _page_specs((None, None, 2, g, d, page), layer, n_pages, 1, descending=True)
            + [fixed(6 * g, d),
               pl.BlockSpec((None, None, 2, g, d, wlen), lambda bi, p, pt: (layer, bi, 0, 0, 0, 0)),
               fixed(hds, d), fixed(hds, V7X_LANES)],
            out_specs=fixed(hds, d),
            scratch_shapes=[pltpu.VMEM((hds, 1), F32), pltpu.VMEM((hds, 2 * d), F32)]),
        out_shape=jax.ShapeDtypeStruct((b, hds, d), F32),
        compiler_params=_params("parallel", "arbitrary"),
        name="nsa_decode_attend",
    )(page_table, q3, sel, *([pool_t] * PAGES_PER_STEP), nsa_kv_new.reshape(b, 6 * g, d), win_t, o_c, gates)
    return out.reshape(b, hds * d)


def _col_offsets():
    offs, s = [], 0
    for n in IN_SPLITS:
        offs.append(s)
        s += n
    return offs


def _layer_weights(l, p):
    o = _col_offsets()
    w_in = p['w_in'][l]
    cols = lambda a, n: w_in[:, a:a + n]
    small = jnp.concatenate([cols(o[2], SSD_HEADS), cols(o[6], 3 * NSA_HEADS)], axis=1)
    small = jnp.pad(small, ((0, 0), (0, V7X_LANES - small.shape[1])))
    bf = lambda a: a.astype(BF16)
    return {
        'w_z': bf(cols(o[0], SSD_INNER)), 'w_xbc': bf(cols(o[1], SSD_CONV_DIM)), 'w_small': bf(small),
        'w_sbq': bf(cols(o[3], SB_WIDTH)), 'w_sbkv': bf(cols(o[3] + SB_WIDTH, 2 * SB_WIDTH)),
        'w_nq': bf(cols(o[4], NSA_WIDTH)), 'w_nkv': bf(cols(o[5], 6 * NSA_KV_WIDTH)),
        'w_brg': bf(cols(o[7], N_BRANCH * D_MODEL)),
        'mix_pre': p['norm_mix_pre'][l], 'mix_post': p['norm_mix_post'][l],
        'ffn_pre': p['norm_ffn_pre'][l], 'ffn_post': p['norm_ffn_post'][l],
        'conv_w': p['ssd_conv_w'][l], 'conv_b': p['ssd_conv_b'][l], 'dt_bias': p['ssd_dt_bias'][l],
        'a_log': p['ssd_a_log'][l], 'd_skip': p['ssd_d'][l], 'ssd_norm': p['ssd_norm'][l],
        'w_ssd_out': bf(p['w_ssd_out'][l]), 'w_sb_out': bf(p['w_sb_out'][l]), 'w_nsa_out': bf(p['w_nsa_out'][l]),
        'w_o': bf(p['w_o'][l]), 'w_ffn_gate': bf(p['w_ffn_gate'][l]), 'w_ffn_up': bf(p['w_ffn_up'][l]),
        'w_ffn_down': bf(p['w_ffn_down'][l]),
        'cmp_pos': p['nsa_cmp_pos'][l], 'cmp_w1': p['nsa_cmp_w1'][l], 'cmp_w2': p['nsa_cmp_w2'][l],
    }


def _trunk_tail(x, ssd_y, sb_o, nsa_o, br_g, lw):
    b, t, d = x.shape
    m = b * t
    x1 = merge_branches(x.reshape(m, d), ssd_y.reshape(m, -1), sb_o.reshape(m, -1), nsa_o.reshape(m, -1),
                        br_g.reshape(m, -1), lw['w_ssd_out'], lw['w_sb_out'], lw['w_nsa_out'], lw['w_o'],
                        lw['mix_post'])
    x2 = ffn(x1, lw['ffn_pre'], lw['ffn_post'], lw['w_ffn_gate'], lw['w_ffn_up'], lw['w_ffn_down'])
    return x2.reshape(b, t, d)


def _layer_prompt(x, lw, layer, n_layers, sb_kv_all):
    b, t, _ = x.shape
    h = rms_cast(x, lw['mix_pre'])
    z = norm_matmul(h, None, lw['w_z'])
    xbc = norm_matmul(h, None, lw['w_xbc'])
    small = norm_matmul(h, None, lw['w_small'])
    br_g = norm_matmul(h, None, lw['w_brg'])
    sb_q = norm_matmul(h, None, lw['w_sbq'], out_dtype=BF16)
    nsa_qt = norm_matmul(h, None, lw['w_nq'].T, transposed=True)
    sb_kv_all = norm_matmul(h, None, lw['w_sbkv'].T, transposed=True, stack=(sb_kv_all, layer, n_layers))
    nsa_kvt = norm_matmul(h, None, lw['w_nkv'].T, transposed=True)
    dt_raw = small[..., :SSD_HEADS]
    nsa_g = small[..., SSD_HEADS:SSD_HEADS + 3 * NSA_HEADS]

    h0 = jnp.zeros((b, SSD_HEADS, SSD_HEAD_DIM, SSD_STATE), F32)
    conv0 = jnp.zeros((b, SSD_CONV - 1, SSD_CONV_DIM), F32)
    ssd_y, h_new = ssd_prompt(xbc, z, dt_raw, h0, conv0, lw['conv_w'], lw['conv_b'], lw['dt_bias'], lw['a_log'],
                              lw['d_skip'], lw['ssd_norm'])
    conv_new = xbc[:, t - (SSD_CONV - 1):, :]
    sb_o = sb_prompt(sb_q, sb_kv_all, layer)

    nsa_o = nsa_prompt_t(nsa_qt, nsa_g, nsa_kvt, lw['cmp_pos'], lw['cmp_w1'], lw['cmp_w2'])

    y = _trunk_tail(x, ssd_y, sb_o, nsa_o, br_g, lw)
    nsa_all = jnp.moveaxis(nsa_kvt.reshape(b, 6, NSA_KV_GROUPS, HEAD_DIM, t), 4, 1)
    keep = min(WINDOW, t)
    return y, sb_kv_all, nsa_all[:, :, 0:4], nsa_all[:, t - keep:, 4:6], h_new, conv_new


def _layer_sample(x, lw, layer, sb_pool_t, nsa_pool_t, win_t, h0, conv_buf, page_table):
    bsz, t = x.shape[:2]
    assert t == 1
    past = page_table.shape[1] * sb_pool_t.shape[-1]
    xr = x.reshape(1, bsz, D_MODEL)
    g = lw['mix_pre']
    pr = lambda w: norm_matmul(xr, g, w)[0]
    z, xbc, small, br_g = pr(lw['w_z']), pr(lw['w_xbc']), pr(lw['w_small']), pr(lw['w_brg'])
    sb_q, sb_kv, nsa_q, nsa_kv = pr(lw['w_sbq']), pr(lw['w_sbkv']), pr(lw['w_nq']), pr(lw['w_nkv'])
    dt_raw = small[:, :SSD_HEADS]
    nsa_g = small[:, SSD_HEADS:SSD_HEADS + 3 * NSA_HEADS]
    ssd_y, h_new = ssd_step(xbc, z, dt_raw, h0, conv_buf, lw['conv_w'], lw['conv_b'], lw['dt_bias'], lw['a_log'],
                            lw['d_skip'], lw['ssd_norm'])
    conv_new = jnp.concatenate([conv_buf[:, 1:], xbc[:, None, :]], axis=1)
    sb_o = sb_decode(sb_q, sb_pool_t, layer, page_table)
    nsa_o = nsa_decode(nsa_q, nsa_g, nsa_kv, nsa_pool_t, win_t, layer, page_table, lw['cmp_pos'], lw['cmp_w1'],
                       lw['cmp_w2'])
    y = _trunk_tail(x, ssd_y[:, None], sb_o.astype(BF16)[:, None], nsa_o.astype(BF16)[:, None], br_g[:, None], lw)
    kv_new = sb_kv.reshape(bsz, 1, 2, SB_HEADS, HEAD_DIM)
    nkv = nsa_kv.reshape(bsz, 1, 6, NSA_KV_GROUPS, HEAD_DIM)
    keep = min(WINDOW, past + 1)
    win_all_t = jnp.concatenate([win_t[layer], nkv[:, 0, 4:6][..., None]], axis=-1)
    win_new = jnp.moveaxis(win_all_t[..., win_all_t.shape[-1] - keep:], 4, 1)
    return y, kv_new, nkv[:, :, 0:4], win_new, h_new, conv_new


def kernel(x_prompt, x_sample, cache_sb_kv, cache_nsa_kv, cache_nsa_win, state_ssd, state_conv, page_table,
           norm_mix_pre, norm_mix_post, norm_ffn_pre, norm_ffn_post, w_in, ssd_conv_w, ssd_conv_b, ssd_dt_bias,
           ssd_a_log, ssd_d, ssd_norm, w_ssd_out, w_sb_out, nsa_cmp_pos, nsa_cmp_w1, nsa_cmp_w2, w_nsa_out, w_o,
           w_ffn_gate, w_ffn_up, w_ffn_down):
    p = dict(norm_mix_pre=norm_mix_pre, norm_mix_post=norm_mix_post, norm_ffn_pre=norm_ffn_pre,
             norm_ffn_post=norm_ffn_post, w_in=w_in, ssd_conv_w=ssd_conv_w, ssd_conv_b=ssd_conv_b,
             ssd_dt_bias=ssd_dt_bias, ssd_a_log=ssd_a_log, ssd_d=ssd_d, ssd_norm=ssd_norm, w_ssd_out=w_ssd_out,
             w_sb_out=w_sb_out, nsa_cmp_pos=nsa_cmp_pos, nsa_cmp_w1=nsa_cmp_w1, nsa_cmp_w2=nsa_cmp_w2,
             w_nsa_out=w_nsa_out, w_o=w_o, w_ffn_gate=w_ffn_gate, w_ffn_up=w_ffn_up, w_ffn_down=w_ffn_down)
    yp, ys = x_prompt, x_sample
    outs_p, outs_s = [], []
    time_minor = lambda a: jnp.transpose(a, (0, 1, 3, 4, 5, 2))
    sb_pool_t, nsa_pool_t, win_t = time_minor(cache_sb_kv), time_minor(cache_nsa_kv), time_minor(cache_nsa_win)
    n_layers = w_in.shape[0]
    sb_kv_all = None
    for l in range(n_layers):
        lw = _layer_weights(l, p)
        res = _layer_prompt(yp, lw, l, n_layers, sb_kv_all)
        yp, sb_kv_all = res[0], res[1]
        outs_p.append(res[1:])
        res = _layer_sample(ys, lw, l, sb_pool_t, nsa_pool_t, win_t, state_ssd[l], state_conv[l], page_table)
        ys = res[0]
        outs_s.append(res[1:])
    st = lambda outs, i: jnp.stack([o[i] for o in outs])
    bp, tp = x_prompt.shape[:2]
    sb_kv_p = jnp.moveaxis(sb_kv_all.reshape(n_layers, bp, 2, SB_HEADS, HEAD_DIM, tp), 5, 2)
    return (yp, ys, sb_kv_p, st(outs_s, 0), st(outs_p, 1), st(outs_s, 1), st(outs_p, 2), st(outs_s, 2),
            st(outs_p, 3), st(outs_s, 3), st(outs_p, 4), st(outs_s, 4))
```
